```python
import math
import jax, jax.numpy as jnp
from jax import lax
import numpy as np

D_MODEL = 1024
BATCH = 16
SEQ = 2048
DEPTH = 2

CHUNK = 64
Q_BLOCK = 128
NORM_EPS = 1e-6

POOL_WIDTH = D_MODEL // 4
POOL_GROUPS = 4
POOL_GROUP_DIM = POOL_WIDTH // POOL_GROUPS
POOL_WINDOWS = (2, 4, 8, 16)
CONV_WIDTH = D_MODEL // 4
CONV_K = 3
SGU_WIDTH = D_MODEL // 4
SGU_GROUPS = 4
SGU_GROUP_DIM = SGU_WIDTH // SGU_GROUPS
SGU_SEG = 128
DIFF_HEADS = 4
DIFF_QK_DIM = 64
DIFF_V_DIM = 2 * DIFF_QK_DIM
ATTN_WIDTH = DIFF_HEADS * DIFF_V_DIM
REL_BUCKETS = 32
REL_MAX_DIST = 128
N_SOFTMAX_MAPS = 2 * DIFF_HEADS
N_BRANCH = 4
D_FF = 2816
N_EXPERTS = 8
TOP_K = 2
D_FF_EXPERT = 3584
N_DENSE = (DEPTH + 1) // 2
N_MOE = DEPTH // 2

OFF_POOL = 0
OFF_CONV = OFF_POOL + POOL_WIDTH
OFF_SGU = OFF_CONV + 3 * CONV_WIDTH
OFF_ATTN = OFF_SGU + 2 * SGU_WIDTH
OFF_GATE = OFF_ATTN + 3 * ATTN_WIDTH
IN_COLS = OFF_GATE + N_BRANCH * D_MODEL

kernel_name = "hybrid_pool_conv_sgu_diffattn_moe"


def rms_norm(x, g):
    xf = x.astype(jnp.float32)
    y = xf * lax.rsqrt(jnp.mean(xf * xf, axis=-1, keepdims=True) + NORM_EPS)
    return (y * g.astype(jnp.float32)).astype(x.dtype)


def pool_mixer(a, pool_w, pool_scale):
    b_, s_, _ = a.shape
    af = a.astype(jnp.float32).reshape(b_, s_, POOL_GROUPS, POOL_GROUP_DIM)
    cs = jnp.pad(jnp.cumsum(af, axis=1), ((0, 0), (1, 0), (0, 0), (0, 0)))
    pos = jnp.arange(s_)
    outs = []
    for g, w in enumerate(POOL_WINDOWS):
        upper = cs[:, 1:, g]
        lower = jnp.pad(cs[:, :s_ + 1 - w, g], ((0, 0), (w - 1, 0), (0, 0)))
        count = jnp.minimum(pos + 1, w).astype(jnp.float32)[None, :, None]
        outs.append((upper - lower) / count - af[:, :, g])
    pooled = jnp.stack(outs, axis=2).astype(a.dtype)
    mixed = jnp.einsum('bsgc,gcd->bsgd', pooled, pool_w).reshape(b_, s_, POOL_WIDTH)
    return mixed * pool_scale


def short_conv_mixer(b_gate, c_gate, hin, conv_w):
    z = c_gate * hin
    y = lax.conv_general_dilated(
        z, conv_w[:, None, :].astype(z.dtype), window_strides=(1,),
        padding=((CONV_K - 1, 0),), dimension_numbers=('NWC', 'WIO', 'NWC'),
        feature_group_count=CONV_WIDTH)
    return b_gate * y


def spatial_gating_mixer(u, v, ln_g, sgu_w, sgu_b):
    b_, s_, _ = v.shape
    vf = v.astype(jnp.float32)
    mu = jnp.mean(vf, axis=-1, keepdims=True)
    var = jnp.mean(jnp.square(vf - mu), axis=-1, keepdims=True)
    vn = ((vf - mu) * lax.rsqrt(var + NORM_EPS) * ln_g.astype(jnp.float32)).astype(v.dtype)
    vn = vn.reshape(b_, s_ // SGU_SEG, SGU_SEG, SGU_GROUPS, SGU_GROUP_DIM)
    tri = jnp.tril(jnp.ones((SGU_SEG, SGU_SEG), dtype=bool))
    w = jnp.where(tri[None], sgu_w, 0.0)
    s = jnp.einsum('gpq,bnqgc->bnpgc', w, vn) + sgu_b.T[None, None, :, :, None]
    return u * s.reshape(b_, s_, SGU_WIDTH)


def rel_buckets(q_pos, k_pos):
    rel = k_pos[None, :] - q_pos[:, None]
    nb = REL_BUCKETS // 2
    max_exact = nb // 2
    n = jnp.abs(rel)
    nf = jnp.maximum(n, 1).astype(jnp.float32)
    large = max_exact + (jnp.log(nf / max_exact) / math.log(REL_MAX_DIST / max_exact)
                         * (nb - max_exact)).astype(jnp.int32)
    large = jnp.minimum(large, nb - 1)
    return jnp.where(rel > 0, nb, 0) + jnp.where(n < max_exact, n, large)


def diff_attention(q, k, v, rel_bias, q_g, k_g, lam, subln_g, lam_init):
    b_, s_ = q.shape[0], q.shape[1]
    q = rms_norm(q, q_g).transpose(0, 2, 3, 1, 4)
    k = rms_norm(k, k_g).transpose(0, 2, 3, 1, 4)
    v = v.transpose(0, 2, 1, 3)
    scale = DIFF_QK_DIM ** -0.5
    pos = jnp.arange(s_)
    outs = []
    for i in range(s_ // Q_BLOCK):
        q0 = i * Q_BLOCK
        k_end = q0 + Q_BLOCK
        qb = q[:, :, :, q0:k_end]
        kb = k[:, :, :, :k_end]
        vb = v[:, :, :k_end]
        qp = pos[q0:k_end]
        kp = pos[:k_end]
        bias = rel_bias[rel_buckets(qp, kp)].astype(jnp.float32)
        bias = bias.reshape(Q_BLOCK, k_end, DIFF_HEADS, 2).transpose(2, 3, 0, 1)
        mask = (qp[:, None] // CHUNK) >= (kp[None, :] // CHUNK)
        logits = jnp.einsum('bhmqd,bhmkd->bhmqk', qb, kb).astype(jnp.float32) * scale + bias
        logits = jnp.where(mask, logits, -jnp.inf)
        p = jax.nn.softmax(logits, axis=-1)
        a = (p[:, :, 0] - lam * p[:, :, 1]).astype(vb.dtype)
        outs.append(jnp.einsum('bhqk,bhkd->bhqd', a, vb))
    o = jnp.concatenate(outs, axis=2)
    o = rms_norm(o, subln_g) * (1.0 - lam_init)
    return o.transpose(0, 2, 1, 3).reshape(b_, s_, ATTN_WIDTH)


def swiglu(x, w_gate_up, w_down):
    g, u = jnp.split(x @ w_gate_up, 2, axis=-1)
    return (jax.nn.silu(g) * u) @ w_down


def moe_swiglu(xn, router_w, w_gate_up, w_down):
    b_, s_, d_ = xn.shape
    xf = xn.reshape(-1, d_)
    logits = (xf @ router_w).astype(jnp.float32)
    top_v, top_i = lax.top_k(logits, TOP_K)
    top_w = jax.nn.softmax(top_v, axis=-1)
    combine = jnp.sum(jax.nn.one_hot(top_i, N_EXPERTS, dtype=jnp.float32) * top_w[..., None],
                      axis=1).astype(xn.dtype)
    out = jnp.zeros_like(xf)
    for e in range(N_EXPERTS):
        out = out + combine[:, e:e + 1] * swiglu(xf, w_gate_up[e], w_down[e])
    return out.reshape(b_, s_, d_)


def token_mixer(xn, layer, lam_init, rel_bias, w_in, pool_w, pool_scale, conv_w, sgu_ln_g,
                sgu_w, sgu_b, q_norm_g, k_norm_g, diff_lambda, subln_g, w_branch_pool,
                w_branch_conv, w_branch_sgu, w_branch_attn, w_out):
    b_, s_, _ = xn.shape
    z = xn @ w_in[layer]
    y_a = pool_mixer(z[..., OFF_POOL:OFF_CONV], pool_w[layer], pool_scale[layer])
    b_gate, c_gate, hin = jnp.split(z[..., OFF_CONV:OFF_SGU], 3, axis=-1)
    y_b = short_conv_mixer(b_gate, c_gate, hin, conv_w[layer])
    u, v = jnp.split(jax.nn.gelu(z[..., OFF_SGU:OFF_ATTN], approximate=False), 2, axis=-1)
    y_c = spatial_gating_mixer(u, v, sgu_ln_g[layer], sgu_w[layer], sgu_b[layer])
    q, k, va = jnp.split(z[..., OFF_ATTN:OFF_GATE], 3, axis=-1)
    q = q.reshape(b_, s_, DIFF_HEADS, 2, DIFF_QK_DIM)
    k = k.reshape(b_, s_, DIFF_HEADS, 2, DIFF_QK_DIM)
    va = va.reshape(b_, s_, DIFF_HEADS, DIFF_V_DIM)
    lp = diff_lambda[layer].astype(jnp.float32)
    lam = jnp.exp(jnp.sum(lp[0] * lp[1])) - jnp.exp(jnp.sum(lp[2] * lp[3])) + lam_init
    y_d = diff_attention(q, k, va, rel_bias, q_norm_g[layer], k_norm_g[layer], lam,
                         subln_g[layer], lam_init)
    gates = jax.nn.sigmoid(z[..., OFF_GATE:].reshape(b_, s_, N_BRANCH, D_MODEL))
    merged = (gates[:, :, 0] * (y_a @ w_branch_pool[layer])
              + gates[:, :, 1] * (y_b @ w_branch_conv[layer])
              + gates[:, :, 2] * (y_c @ w_branch_sgu[layer])
              + gates[:, :, 3] * (y_d @ w_branch_attn[layer]))
    return merged @ w_out[layer]


def setup_inputs(seed: int = 0) -> dict:
    key = jax.random.key(seed)
    ks = jax.random.split(key, 32)
    f32 = jnp.float32
    L = DEPTH

    def nrm(k, shape, scale):
        return jax.random.normal(k, shape, f32) * scale

    def gain(k, shape):
        return 1.0 + 0.02 * jax.random.normal(k, shape, f32)

    return {
        "x": jax.random.normal(ks[0], (BATCH, SEQ, D_MODEL), f32),
        "rel_bias": nrm(ks[1], (REL_BUCKETS, N_SOFTMAX_MAPS), 0.5),
        "norm1_g": gain(ks[2], (L, D_MODEL)),
        "w_in": nrm(ks[3], (L, D_MODEL, IN_COLS), D_MODEL ** -0.5),
        "pool_w": nrm(ks[4], (L, POOL_GROUPS, POOL_GROUP_DIM, POOL_GROUP_DIM), POOL_GROUP_DIM ** -0.5),
        "pool_scale": 1.0 + 0.1 * jax.random.normal(ks[5], (L, POOL_WIDTH), f32),
        "conv_w": nrm(ks[6], (L, CONV_K, CONV_WIDTH), CONV_K ** -0.5),
        "sgu_ln_g": gain(ks[7], (L, SGU_WIDTH)),
        "sgu_w": nrm(ks[8], (L, SGU_GROUPS, SGU_SEG, SGU_SEG), SGU_SEG ** -0.5),
        "sgu_b": 1.0 + 0.1 * jax.random.normal(ks[9], (L, SGU_GROUPS, SGU_SEG), f32),
        "q_norm_g": gain(ks[10], (L, DIFF_QK_DIM)),
        "k_norm_g": gain(ks[11], (L, DIFF_QK_DIM)),
        "diff_lambda": nrm(ks[12], (L, 4, DIFF_QK_DIM), 0.1),
        "subln_g": gain(ks[13], (L, DIFF_V_DIM)),
        "w_branch_pool": nrm(ks[14], (L, POOL_WIDTH, D_MODEL), POOL_WIDTH ** -0.5),
        "w_branch_conv": nrm(ks[15], (L, CONV_WIDTH, D_MODEL), CONV_WIDTH ** -0.5),
        "w_branch_sgu": nrm(ks[16], (L, SGU_WIDTH, D_MODEL), SGU_WIDTH ** -0.5),
        "w_branch_attn": nrm(ks[17], (L, ATTN_WIDTH, D_MODEL), ATTN_WIDTH ** -0.5),
        "w_out": nrm(ks[18], (L, D_MODEL, D_MODEL), D_MODEL ** -0.5),
        "norm2_g": gain(ks[19], (L, D_MODEL)),
        "ffn_w_gate_up": nrm(ks[20], (N_DENSE, D_MODEL, 2 * D_FF), D_MODEL ** -0.5),
        "ffn_w_down": nrm(ks[21], (N_DENSE, D_FF, D_MODEL), D_FF ** -0.5),
        "router_w": nrm(ks[22], (N_MOE, D_MODEL, N_EXPERTS), D_MODEL ** -0.5),
        "moe_w_gate_up": nrm(ks[23], (N_MOE, N_EXPERTS, D_MODEL, 2 * D_FF_EXPERT), D_MODEL ** -0.5),
        "moe_w_down": nrm(ks[24], (N_MOE, N_EXPERTS, D_FF_EXPERT, D_MODEL), D_FF_EXPERT ** -0.5),
    }


def reference(x, rel_bias, norm1_g, w_in, pool_w, pool_scale, conv_w, sgu_ln_g, sgu_w, sgu_b,
              q_norm_g, k_norm_g, diff_lambda, subln_g, w_branch_pool, w_branch_conv,
              w_branch_sgu, w_branch_attn, w_out, norm2_g, ffn_w_gate_up, ffn_w_down,
              router_w, moe_w_gate_up, moe_w_down):
    h = x
    for layer in range(DEPTH):
        lam_init = 0.8 - 0.6 * math.exp(-0.3 * layer)
        xn = rms_norm(h, norm1_g[layer])
        h = h + token_mixer(xn, layer, lam_init, rel_bias, w_in, pool_w, pool_scale, conv_w,
                            sgu_ln_g, sgu_w, sgu_b, q_norm_g, k_norm_g, diff_lambda, subln_g,
                            w_branch_pool, w_branch_conv, w_branch_sgu, w_branch_attn, w_out)
        hn = rms_norm(h, norm2_g[layer])
        if layer % 2 == 0:
            h = h + swiglu(hn, ffn_w_gate_up[layer // 2], ffn_w_down[layer // 2])
        else:
            h = h + moe_swiglu(hn, router_w[layer // 2], moe_w_gate_up[layer // 2],
                               moe_w_down[layer // 2])
    return h
```

```python
import functools
import math

import jax
import jax.numpy as jnp
import numpy as np
from jax import lax
from jax.experimental import pallas as pl
from jax.experimental.pallas import tpu as pltpu

F32 = jnp.float32
BF16 = jnp.bfloat16

NORM_EPS = 1e-6
CHUNK = 64
POOL_WINDOWS = (2, 4, 8, 16)
POOL_GROUPS = 4
CONV_K = 3
SGU_GROUPS = 4
SGU_SEG = 128
DIFF_HEADS = 4
DIFF_QK_DIM = 64
DIFF_V_DIM = 128
REL_BUCKETS = 32
REL_MAX_DIST = 128
N_EXPERTS = 8
LANES = 128
V7X_VMEM_BYTES = 64 * 1024 * 1024
VMEM_LIMIT = V7X_VMEM_BYTES - 8 * 1024 * 1024
NEG_BIG = -1e30

HALO = 16
ATT_T = 256
FF_CHUNK = 256


def _rms(x, g):
    return x * lax.rsqrt(jnp.mean(x * x, axis=-1, keepdims=True) + NORM_EPS) * g


def _resident(shape):
    nd = len(shape)
    return pl.BlockSpec(shape, lambda *_: (0,) * nd, pipeline_mode=pl.Buffered(1))


def _params(sem):
    return pltpu.CompilerParams(dimension_semantics=sem, vmem_limit_bytes=VMEM_LIMIT)


def _in_proj_kernel(x_ref, g_ref, w_ref, o_ref, *, n_chunk):
    xn = _rms(x_ref[...], g_ref[...]).astype(BF16)
    n = o_ref.shape[1]
    for j in range(n // n_chunk):
        sl = slice(j * n_chunk, (j + 1) * n_chunk)
        o_ref[:, sl] = jnp.dot(xn, w_ref[:, sl], preferred_element_type=F32).astype(o_ref.dtype)


def _in_proj(h, g, w, tm):
    t, d = h.shape
    n = w.shape[1]
    return pl.pallas_call(
        functools.partial(_in_proj_kernel, n_chunk=512),
        grid=(t // tm,),
        in_specs=[pl.BlockSpec((tm, d), lambda i: (i, 0)), _resident((1, d)), _resident((d, n))],
        out_specs=pl.BlockSpec((tm, n), lambda i: (i, 0)),
        out_shape=jax.ShapeDtypeStruct((t, n), BF16),
        compiler_params=_params(("parallel",)),
        name="in_proj",
    )(h, g, w)


def _local_mix_kernel(z_ref, halo_ref, poolw_ref, pscale_ref, convw_ref, lng_ref, sguw_ref,
                      sgub_ref, o_ref, *, pw, cw):
    ts = z_ref.shape[0]
    i = pl.program_id(1)
    z = z_ref[...].astype(F32)
    halo = halo_ref[...].astype(F32)
    halo = jnp.where(i > 0, halo, 0.0)
    ext = jnp.concatenate([halo[:, :pw + 3 * cw], z[:, :pw + 3 * cw]], axis=0)
    rows = ext.shape[0]

    def back(x, k):
        return pltpu.roll(x, k, axis=0)

    a = ext[:, :pw]
    s2 = a + back(a, 1)
    s4 = s2 + back(s2, 2)
    s8 = s4 + back(s4, 4)
    s16 = s8 + back(s8, 8)
    lane = lax.broadcasted_iota(jnp.int32, (rows, pw), 1)
    grp = lane // (pw // POOL_GROUPS)
    win_sum = jnp.where(grp == 0, s2, jnp.where(grp == 1, s4, jnp.where(grp == 2, s8, s16)))
    win = jnp.where(grp == 0, 2, jnp.where(grp == 1, 4, jnp.where(grp == 2, 8, 16)))
    pos = i * ts - HALO + lax.broadcasted_iota(jnp.int32, (rows, pw), 0)
    count = jnp.minimum(pos + 1, win).astype(F32)
    pooled = (win_sum / jnp.maximum(count, 1.0) - a)[HALO:]
    y_a = jnp.dot(pooled.astype(BF16), poolw_ref[...], preferred_element_type=F32) * pscale_ref[...]
    o_ref[:, 0:pw] = y_a.astype(o_ref.dtype)

    b_gate = z[:, pw:pw + cw]
    zc = ext[:, pw + cw:pw + 2 * cw] * ext[:, pw + 2 * cw:pw + 3 * cw]
    conv = (convw_ref[0:1, :] * back(zc, 2) + convw_ref[1:2, :] * back(zc, 1)
            + convw_ref[2:3, :] * zc)[HALO:]
    o_ref[:, pw:pw + cw] = (b_gate * conv).astype(o_ref.dtype)

    sw = (z.shape[1] - pw - 3 * cw) // 2
    zc_uv = z[:, pw + 3 * cw:]
    uv = 0.5 * zc_uv * (1.0 + lax.erf(zc_uv * math.sqrt(0.5)))
    u = uv[:, :sw]
    v = uv[:, sw:]
    mu = jnp.mean(v, axis=-1, keepdims=True)
    var = jnp.mean(jnp.square(v - mu), axis=-1, keepdims=True)
    vn = (v - mu) * lax.rsqrt(var + NORM_EPS) * lng_ref[...]
    glane = lax.broadcasted_iota(jnp.int32, (SGU_SEG, sw), 1) // (sw // SGU_GROUPS)
    wcat = sguw_ref[...]
    bias = sgub_ref[...]
    for n in range(ts // SGU_SEG):
        seg = vn[n * SGU_SEG:(n + 1) * SGU_SEG]
        rhs = jnp.concatenate(
            [jnp.where(glane == g, seg, 0.0) for g in range(SGU_GROUPS)], axis=0).astype(BF16)
        s = jnp.dot(wcat, rhs, preferred_element_type=F32) + bias
        o_ref[n * SGU_SEG:(n + 1) * SGU_SEG, pw + cw:pw + cw + sw] = (
            u[n * SGU_SEG:(n + 1) * SGU_SEG] * s).astype(o_ref.dtype)


def _local_mix(z3, poolw_bd, pscale, convw, lng, sguw_cat, sgub_full, ts, pw, cw, sw):
    b, s, _ = z3.shape
    cols = pw + 3 * cw + 2 * sw
    hb = ts // HALO
    return pl.pallas_call(
        functools.partial(_local_mix_kernel, pw=pw, cw=cw),
        grid=(b, s // ts),
        in_specs=[
            pl.BlockSpec((None, ts, cols), lambda bi, i: (bi, i, 0)),
            pl.BlockSpec((None, HALO, cols), lambda bi, i: (bi, jnp.maximum(i * hb - 1, 0), 0)),
            _resident(poolw_bd.shape), _resident(pscale.shape), _resident(convw.shape),
            _resident(lng.shape), _resident(sguw_cat.shape), _resident(sgub_full.shape),
        ],
        out_specs=pl.BlockSpec((None, ts, pw + cw + sw), lambda bi, i: (bi, i, 0)),
        out_shape=jax.ShapeDtypeStruct((b, s, pw + cw + sw), BF16),
        compiler_params=_params(("parallel", "parallel")),
        name="local_mix",
    )(z3, z3, poolw_bd, pscale, convw, lng, sguw_cat, sgub_full)


def _diff_attn_kernel(q_ref, k_ref, v_ref, bias_ref, qg_ref, kg_ref, lam_ref, sg_ref, o_ref,
                      kn_ref, qs_ref, m_ref, l_ref, acc_ref, *, lam_init):
    tq = q_ref.shape[0]
    i = pl.program_id(2)
    half = lax.broadcasted_iota(jnp.int32, (1, 2 * DIFF_QK_DIM), 1) < DIFF_QK_DIM

    def qk_norm(x, g):
        sq = x * x
        ss0 = jnp.sum(jnp.where(half, sq, 0.0), axis=-1, keepdims=True)
        ss1 = jnp.sum(jnp.where(half, 0.0, sq), axis=-1, keepdims=True)
        r0 = lax.rsqrt(ss0 * (1.0 / DIFF_QK_DIM) + NORM_EPS)
        r1 = lax.rsqrt(ss1 * (1.0 / DIFF_QK_DIM) + NORM_EPS)
        return x * jnp.where(half, r0, r1) * g

    @pl.when(i == 0)
    def _():
        kn_ref[...] = qk_norm(k_ref[...].astype(F32), kg_ref[...]).astype(BF16)

    qn = qk_norm(q_ref[...].astype(F32), qg_ref[...]) * (DIFF_QK_DIM ** -0.5)
    qs_ref[0:tq, :] = jnp.where(half, qn, 0.0).astype(BF16)
    qs_ref[tq:2 * tq, :] = jnp.where(half, 0.0, qn).astype(BF16)
    m_ref[...] = jnp.full(m_ref.shape, NEG_BIG, F32)
    l_ref[...] = jnp.zeros(l_ref.shape, F32)
    acc_ref[...] = jnp.zeros(acc_ref.shape, F32)

    def tile(j, bias_idx):
        start = pl.multiple_of(j * tq, tq)
        kt = kn_ref[pl.ds(start, tq), :]
        vt = v_ref[pl.ds(start, tq), :]
        s = lax.dot_general(qs_ref[...], kt, (((1,), (1,)), ((), ())),
                            preferred_element_type=F32)
        for mi in range(2):
            sm = s[mi * tq:(mi + 1) * tq]
            if bias_idx is not None:
                sm = sm + bias_ref[mi, bias_idx]
            m_old = m_ref[mi]
            m_new = jnp.maximum(m_old, jnp.max(sm, axis=-1, keepdims=True))
            alpha = jnp.exp(m_old - m_new)
            p = jnp.exp(sm - m_new)
            l_ref[mi] = alpha * l_ref[mi] + jnp.sum(p, axis=-1, keepdims=True)
            acc_ref[mi] = alpha * acc_ref[mi] + jnp.dot(p.astype(BF16), vt,
                                                        preferred_element_type=F32)
            m_ref[mi] = m_new

    def far(j, c):
        tile(j, None)
        return c

    lax.fori_loop(0, jnp.maximum(i - 1, 0), far, 0)

    @pl.when(i > 0)
    def _():
        tile(i - 1, 0)

    tile(i, 1)

    lp = lam_ref[...]
    lam = (jnp.exp(jnp.sum(lp[0:1] * lp[1:2], axis=-1, keepdims=True))
           - jnp.exp(jnp.sum(lp[2:3] * lp[3:4], axis=-1, keepdims=True)) + lam_init)
    o = acc_ref[0] / l_ref[0] - lam * (acc_ref[1] / l_ref[1])
    o_ref[...] = (_rms(o, sg_ref[...]) * (1.0 - lam_init)).astype(o_ref.dtype)


def _diff_attn(z3, bias_near, qg2, kg2, lam_p, subln_g, lam_init, q_col, k_col, v_col):
    b, s, _ = z3.shape
    tq = ATT_T
    hw = 2 * DIFF_QK_DIM
    return pl.pallas_call(
        functools.partial(_diff_attn_kernel, lam_init=lam_init),
        grid=(b, DIFF_HEADS, s // tq),
        in_specs=[
            pl.BlockSpec((None, tq, hw), lambda bi, h, i: (bi, i, q_col + h)),
            pl.BlockSpec((None, s, hw), lambda bi, h, i: (bi, 0, k_col + h)),
            pl.BlockSpec((None, s, DIFF_V_DIM), lambda bi, h, i: (bi, 0, v_col + h)),
            pl.BlockSpec((None, 2, 2, tq, tq), lambda bi, h, i: (h, 0, 0, 0, 0)),
            _resident(qg2.shape), _resident(kg2.shape), _resident(lam_p.shape),
            _resident(subln_g.shape),
        ],
        out_specs=pl.BlockSpec((None, tq, DIFF_V_DIM), lambda bi, h, i: (bi, i, h)),
        out_shape=jax.ShapeDtypeStruct((b, s, DIFF_HEADS * DIFF_V_DIM), BF16),
        scratch_shapes=[
            pltpu.VMEM((s, hw), BF16),
            pltpu.VMEM((2 * tq, hw), BF16),
            pltpu.VMEM((2, tq, 1), F32),
            pltpu.VMEM((2, tq, 1), F32),
            pltpu.VMEM((2, tq, DIFF_V_DIM), F32),
        ],
        compiler_params=_params(("parallel", "parallel", "arbitrary")),
        name="diff_attn",
    )(z3, z3, z3, bias_near, qg2, kg2, lam_p, subln_g)


def _merge_kernel(h_ref, yabc_ref, yd_ref, g_ref, wg_ref, wb_ref, wo_ref, o_ref, *, widths):
    h = h_ref[...]
    d = h.shape[1]
    xn = _rms(h, g_ref[...]).astype(BF16)
    merged = None
    off = 0
    yoff = 0
    for bi, w in enumerate(widths):
        gate = jax.nn.sigmoid(jnp.dot(xn, wg_ref[:, bi * d:(bi + 1) * d],
                                      preferred_element_type=F32))
        if bi < len(widths) - 1:
            y = yabc_ref[:, yoff:yoff + w]
            yoff += w
        else:
            y = yd_ref[...]
        proj = jnp.dot(y, wb_ref[off:off + w, :], preferred_element_type=F32)
        off += w
        merged = gate * proj if merged is None else merged + gate * proj
    o_ref[...] = h + jnp.dot(merged.astype(BF16), wo_ref[...], preferred_element_type=F32)


def _merge(h, y_abc, y_d, g, wg, wb, wo, tm, widths):
    t, d = h.shape
    return pl.pallas_call(
        functools.partial(_merge_kernel, widths=widths),
        grid=(t // tm,),
        in_specs=[
            pl.BlockSpec((tm, d), lambda i: (i, 0)),
            pl.BlockSpec((tm, y_abc.shape[1]), lambda i: (i, 0)),
            pl.BlockSpec((tm, y_d.shape[1]), lambda i: (i, 0)),
            _resident(g.shape), _resident(wg.shape), _resident(wb.shape), _resident(wo.shape),
        ],
        out_specs=pl.BlockSpec((tm, d), lambda i: (i, 0)),
        out_shape=jax.ShapeDtypeStruct((t, d), F32),
        compiler_params=_params(("parallel",)),
        name="merge",
    )(h, y_abc, y_d, g, wg, wb, wo)


def _swiglu_acc(xn, wgu_ref, wd_ref, d_ff):
    acc = None
    for c in range(d_ff // FF_CHUNK):
        lo = c * FF_CHUNK
        g = jnp.dot(xn, wgu_ref[:, lo:lo + FF_CHUNK], preferred_element_type=F32)
        u = jnp.dot(xn, wgu_ref[:, d_ff + lo:d_ff + lo + FF_CHUNK], preferred_element_type=F32)
        act = (g * jax.nn.sigmoid(g) * u).astype(BF16)
        part = jnp.dot(act, wd_ref[lo:lo + FF_CHUNK, :], preferred_element_type=F32)
        acc = part if acc is None else acc + part
    return acc


def _ffn_kernel(h_ref, g_ref, wgu_ref, wd_ref, o_ref):
    h = h_ref[...]
    xn = _rms(h, g_ref[...]).astype(BF16)
    o_ref[...] = h + _swiglu_acc(xn, wgu_ref, wd_ref, wd_ref.shape[0])


def _ffn(h, g, wgu, wd, tm):
    t, d = h.shape
    return pl.pallas_call(
        _ffn_kernel,
        grid=(t // tm,),
        in_specs=[pl.BlockSpec((tm, d), lambda i: (i, 0)), _resident(g.shape),
                  _resident(wgu.shape), _resident(wd.shape)],
        out_specs=pl.BlockSpec((tm, d), lambda i: (i, 0)),
        out_shape=jax.ShapeDtypeStruct((t, d), F32),
        compiler_params=_params(("parallel",)),
        name="ffn",
    )(h, g, wgu, wd)


def _router_kernel(h_ref, g_ref, rw_ref, o_ref, tot_ref, carry_ref):
    tm = h_ref.shape[0]

    @pl.when(pl.program_id(0) == 0)
    def _():
        carry_ref[...] = jnp.zeros(carry_ref.shape, F32)

    hn = _rms(h_ref[...], g_ref[...])
    logits = jnp.dot(hn, rw_ref[...], preferred_element_type=F32, precision=lax.Precision.HIGHEST)
    lane = lax.broadcasted_iota(jnp.int32, (tm, LANES), 1)
    logits = jnp.where(lane < N_EXPERTS, logits, NEG_BIG)
    v1 = jnp.max(logits, axis=-1, keepdims=True)
    i1 = jnp.min(jnp.where(logits == v1, lane, LANES), axis=-1, keepdims=True)
    rest = jnp.where(lane == i1, NEG_BIG, logits)
    v2 = jnp.max(rest, axis=-1, keepdims=True)
    i2 = jnp.min(jnp.where(rest == v2, lane, LANES), axis=-1, keepdims=True)
    e = jnp.exp(v2 - v1)
    w1 = 1.0 / (1.0 + e)
    w2 = e / (1.0 + e)
    cnt = jnp.where((lane == i1) | (lane == i2), 1.0, 0.0)
    r = lax.broadcasted_iota(jnp.int32, (tm, tm), 0)
    c = lax.broadcasted_iota(jnp.int32, (tm, tm), 1)
    tri = jnp.where(c < r, 1.0, 0.0).astype(BF16)
    excl = jnp.dot(tri, cnt.astype(BF16), preferred_element_type=F32) + carry_ref[...]
    rank1 = jnp.sum(jnp.where(lane == i1, excl, 0.0), axis=-1, keepdims=True)
    rank2 = jnp.sum(jnp.where(lane == i2, excl, 0.0), axis=-1, keepdims=True)
    carry_ref[...] = carry_ref[...] + jnp.sum(cnt, axis=0, keepdims=True)
    tot_ref[...] = carry_ref[...]
    packed = jnp.where(lane == 0, i1.astype(F32), jnp.where(lane == 1, i2.astype(F32),
             jnp.where(lane == 2, w1, jnp.where(lane == 3, w2,
             jnp.where(lane == 4, rank1, jnp.where(lane == 5, rank2, 0.0))))))
    o_ref[...] = packed


def _router(h, g, rw_pad, tm):
    t, d = h.shape
    return pl.pallas_call(
        _router_kernel,
        grid=(t // tm,),
        in_specs=[pl.BlockSpec((tm, d), lambda i: (i, 0)), _resident(g.shape),
                  _resident(rw_pad.shape)],
        out_specs=[pl.BlockSpec((tm, LANES), lambda i: (i, 0)),
                   pl.BlockSpec((1, LANES), lambda i: (0, 0))],
        out_shape=[jax.ShapeDtypeStruct((t, LANES), F32), jax.ShapeDtypeStruct((1, LANES), F32)],
        scratch_shapes=[pltpu.VMEM((1, LANES), F32)],
        compiler_params=_params(("arbitrary",)),
        name="router",
    )(h, g, rw_pad)


def _row_copy(src, dst, s, d, sem):
    return pltpu.make_async_copy(src.at[pl.ds(s, 1)], dst.at[pl.ds(d, 1)], sem)


def _dispatch_kernel(dest_ref, h_ref, xs_in_ref, xs_ref, sem, *, ch):
    del xs_in_ref
    base = pl.program_id(0) * ch

    def issue(t, c):
        _row_copy(h_ref, xs_ref, base + t, dest_ref[0, t], sem).start()
        _row_copy(h_ref, xs_ref, base + t, dest_ref[1, t], sem).start()
        return c

    lax.fori_loop(0, ch, issue, 0, unroll=8)

    def drain(t, c):
        _row_copy(h_ref, xs_ref, base + t, dest_ref[0, t], sem).wait()
        _row_copy(h_ref, xs_ref, base + t, dest_ref[1, t], sem).wait()
        return c

    lax.fori_loop(0, ch, drain, 0, unroll=8)


def _dispatch(dest, h, xs_zero, ch):
    nc = dest.shape[0]
    return pl.pallas_call(
        functools.partial(_dispatch_kernel, ch=ch),
        grid=(nc,),
        in_specs=[pl.BlockSpec((None, 2, ch), lambda c: (c, 0, 0), memory_space=pltpu.SMEM),
                  pl.BlockSpec(memory_space=pl.ANY), pl.BlockSpec(memory_space=pl.ANY)],
        out_specs=pl.BlockSpec(memory_space=pl.ANY),
        out_shape=jax.ShapeDtypeStruct(xs_zero.shape, xs_zero.dtype),
        scratch_shapes=[pltpu.SemaphoreType.DMA(())],
        input_output_aliases={2: 0},
        compiler_params=pltpu.CompilerParams(dimension_semantics=("arbitrary",),
                                             has_side_effects=True),
        name="moe_dispatch",
    )(dest, h, xs_zero)


def _expert_kernel(te_ref, nu_ref, x_ref, g_ref, wgu_ref, wd_ref, y_ref):
    del te_ref

    @pl.when(pl.program_id(0) < nu_ref[0])
    def _():
        xn = _rms(x_ref[...], g_ref[...]).astype(BF16)
        y_ref[...] = _swiglu_acc(xn, wgu_ref, wd_ref, wd_ref.shape[0])

    @pl.when(pl.program_id(0) >= nu_ref[0])
    def _():
        y_ref[...] = jnp.zeros(y_ref.shape, y_ref.dtype)


def _experts(tile_expert, n_used, xs, g, wgu, wd, tm):
    r, d = xs.shape
    d_ff = wd.shape[1]
    grid_spec = pltpu.PrefetchScalarGridSpec(
        num_scalar_prefetch=2,
        grid=(r // tm,),
        in_specs=[
            pl.BlockSpec((tm, d), lambda i, te, nu: (i, 0)),
            pl.BlockSpec(g.shape, lambda i, te, nu: (0, 0), pipeline_mode=pl.Buffered(1)),
            pl.BlockSpec((None, d, 2 * d_ff), lambda i, te, nu: (te[i], 0, 0),
                         pipeline_mode=pl.Buffered(1)),
            pl.BlockSpec((None, d_ff, d), lambda i, te, nu: (te[i], 0, 0),
                         pipeline_mode=pl.Buffered(1)),
        ],
        out_specs=pl.BlockSpec((tm, d), lambda i, te, nu: (i, 0)),
    )
    return pl.pallas_call(
        _expert_kernel,
        grid_spec=grid_spec,
        out_shape=jax.ShapeDtypeStruct((r, d), F32),
        compiler_params=_params(("arbitrary",)),
        name="moe_experts",
    )(tile_expert, n_used, xs, g, wgu, wd)


def _combine_kernel(dest_ref, h_ref, pk_ref, y_ref, o_ref, buf_ref, sem):
    tm = h_ref.shape[0]

    def issue(t, c):
        _row_copy(y_ref, buf_ref.at[0], dest_ref[0, t], t, sem).start()
        _row_copy(y_ref, buf_ref.at[1], dest_ref[1, t], t, sem).start()
        return c

    lax.fori_loop(0, tm, issue, 0, unroll=8)

    def drain(t, c):
        _row_copy(y_ref, buf_ref.at[0], dest_ref[0, t], t, sem).wait()
        _row_copy(y_ref, buf_ref.at[1], dest_ref[1, t], t, sem).wait()
        return c

    lax.fori_loop(0, tm, drain, 0, unroll=8)
    pk = pk_ref[...]
    o_ref[...] = h_ref[...] + pk[:, 2:3] * buf_ref[0] + pk[:, 3:4] * buf_ref[1]


def _combine(dest, h, packed, y, tm):
    t, d = h.shape
    return pl.pallas_call(
        _combine_kernel,
        grid=(t // tm,),
        in_specs=[pl.BlockSpec((None, 2, tm), lambda i: (i, 0, 0), memory_space=pltpu.SMEM),
                  pl.BlockSpec((tm, d), lambda i: (i, 0)),
                  pl.BlockSpec((tm, LANES), lambda i: (i, 0)),
                  pl.BlockSpec(memory_space=pl.ANY)],
        out_specs=pl.BlockSpec((tm, d), lambda i: (i, 0)),
        out_shape=jax.ShapeDtypeStruct((t, d), F32),
        scratch_shapes=[pltpu.VMEM((2, tm, d), F32), pltpu.SemaphoreType.DMA(())],
        compiler_params=_params(("arbitrary",)),
        name="moe_combine",
    )(dest, h, packed, y)


def _rel_bucket(rel):
    nb = REL_BUCKETS // 2
    max_exact = nb // 2
    n = jnp.abs(rel)
    nf = jnp.maximum(n, 1).astype(F32)
    large = max_exact + (jnp.log(nf / max_exact) / math.log(REL_MAX_DIST / max_exact)
                         * (nb - max_exact)).astype(jnp.int32)
    large = jnp.minimum(large, nb - 1)
    return jnp.where(rel > 0, nb, 0) + jnp.where(n < max_exact, n, large)


def _near_bias(rel_bias):
    t = ATT_T
    qp = jnp.arange(t)[:, None]
    kp = jnp.arange(t)[None, :]
    far = rel_bias[_rel_bucket(jnp.full((), -(2 * t), jnp.int32))]
    prev = rel_bias[_rel_bucket(kp - t - qp)] - far
    diag = rel_bias[_rel_bucket(kp - qp)] - far
    diag = jnp.where(((kp // CHUNK) <= (qp // CHUNK))[:, :, None], diag, NEG_BIG)
    both = jnp.stack([prev, diag], axis=0)
    return both.reshape(2, t, t, DIFF_HEADS, 2).transpose(3, 4, 0, 1, 2).astype(F32)


def kernel(x, rel_bias, norm1_g, w_in, pool_w, pool_scale, conv_w, sgu_ln_g, sgu_w, sgu_b, q_norm_g, k_norm_g, diff_lambda, subln_g, w_branch_pool, w_branch_conv, w_branch_sgu, w_branch_attn, w_out, norm2_g, ffn_w_gate_up, ffn_w_down, router_w, moe_w_gate_up, moe_w_down):
    b, s, d = x.shape
    t = b * s
    depth = w_in.shape[0]
    pw = pool_scale.shape[1]
    cw = conv_w.shape[2]
    sw = sgu_ln_g.shape[1]
    aw = w_branch_attn.shape[1]
    mix_cols = pw + 3 * cw + 2 * sw + 3 * aw
    nb = REL_BUCKETS // 2
    assert nb // 2 + int(math.log((ATT_T + 1) / (nb // 2)) / math.log(REL_MAX_DIST / (nb // 2))
                         * (nb - nb // 2)) >= nb - 1
    q_col = (pw + 3 * cw + 2 * sw) // LANES
    k_col = q_col + aw // LANES
    v_col = k_col + aw // LANES
    tm = min(512, t)
    ts = min(512, s)

    bias_near = _near_bias(rel_bias)
    tri = jnp.tril(jnp.ones((SGU_SEG, SGU_SEG), bool))
    gd = pw // POOL_GROUPS

    h = x.reshape(t, d)
    for layer in range(depth):
        lam_init = 0.8 - 0.6 * math.exp(-0.3 * layer)
        w_mix = w_in[layer, :, :mix_cols].astype(BF16)
        w_gate = w_in[layer, :, mix_cols:].astype(BF16)
        poolw_bd = jnp.zeros((pw, pw), F32)
        for g in range(POOL_GROUPS):
            poolw_bd = poolw_bd.at[g * gd:(g + 1) * gd, g * gd:(g + 1) * gd].set(pool_w[layer, g])
        sguw_cat = jnp.where(tri[None], sgu_w[layer], 0.0).transpose(1, 0, 2).reshape(
            SGU_SEG, SGU_GROUPS * SGU_SEG).astype(BF16)
        sgub_full = jnp.repeat(sgu_b[layer].T, sw // SGU_GROUPS, axis=1)
        wb = jnp.concatenate([w_branch_pool[layer], w_branch_conv[layer], w_branch_sgu[layer],
                              w_branch_attn[layer]], axis=0).astype(BF16)

        z = _in_proj(h, norm1_g[layer][None], w_mix, tm)
        z3 = z.reshape(b, s, mix_cols)
        y_abc = _local_mix(z3, poolw_bd.astype(BF16), pool_scale[layer][None], conv_w[layer],
                           sgu_ln_g[layer][None], sguw_cat, sgub_full, ts, pw, cw, sw)
        y_d = _diff_attn(z3, bias_near, jnp.tile(q_norm_g[layer], 2)[None],
                         jnp.tile(k_norm_g[layer], 2)[None], diff_lambda[layer],
                         subln_g[layer][None], lam_init, q_col, k_col, v_col)
        h = _merge(h, y_abc.reshape(t, -1), y_d.reshape(t, -1), norm1_g[layer][None], w_gate, wb,
                   w_out[layer].astype(BF16), tm, (pw, cw, sw, aw))

        g2 = norm2_g[layer][None]
        if layer % 2 == 0:
            h = _ffn(h, g2, ffn_w_gate_up[layer // 2].astype(BF16),
                     ffn_w_down[layer // 2].astype(BF16), tm)
        else:
            li = layer // 2
            rw_pad = jnp.zeros((d, LANES), F32).at[:, :N_EXPERTS].set(router_w[li])
            packed, totals = _router(h, g2, rw_pad, tm)
            n_e = totals[0, :N_EXPERTS].astype(jnp.int32)
            n_pad = ((n_e + tm - 1) // tm) * tm
            ends = jnp.cumsum(n_pad)
            starts = ends - n_pad
            e1 = packed[:, 0].astype(jnp.int32)
            e2 = packed[:, 1].astype(jnp.int32)
            dest1 = starts[e1] + packed[:, 4].astype(jnp.int32)
            dest2 = starts[e2] + packed[:, 5].astype(jnp.int32)
            rows = 2 * t + N_EXPERTS * tm
            n_tiles = rows // tm
            tile_expert = jnp.minimum(
                jnp.sum((jnp.arange(n_tiles)[:, None] * tm) >= ends[None, :], axis=1),
                N_EXPERTS - 1).astype(jnp.int32)
            n_used = (ends[-1] // tm).astype(jnp.int32)[None]
            ch = min(2048, t)
            dest_d = jnp.stack([dest1.reshape(t // ch, ch), dest2.reshape(t // ch, ch)], axis=1)
            xs = _dispatch(dest_d, h, jnp.zeros((rows, d), F32), ch)
            y = _experts(tile_expert, n_used, xs, g2, moe_w_gate_up[li].astype(BF16),
                         moe_w_down[li].astype(BF16), tm)
            tc = min(256, t)
            dest_c = jnp.stack([dest1.reshape(t // tc, tc), dest2.reshape(t // tc, tc)], axis=1)
            h = _combine(dest_c, h, packed, y, tc)
    return h.reshape(b, s, d)
```

```python
import functools
import math

import jax
import jax.numpy as jnp
import numpy as np
from jax import lax
from jax.experimental import pallas as pl
from jax.experimental.pallas import tpu as pltpu

F32 = jnp.float32
BF16 = jnp.bfloat16

NORM_EPS = 1e-6
CHUNK = 64
POOL_WINDOWS = (2, 4, 8, 16)
POOL_GROUPS = 4
CONV_K = 3
SGU_GROUPS = 4
SGU_SEG = 128
DIFF_HEADS = 4
DIFF_QK_DIM = 64
DIFF_V_DIM = 128
REL_BUCKETS = 32
REL_MAX_DIST = 128
N_EXPERTS = 8
LANES = 128
V7X_VMEM_BYTES = 64 * 1024 * 1024
VMEM_LIMIT = V7X_VMEM_BYTES - 8 * 1024 * 1024
NEG_BIG = -1e30

HALO = 16
ATT_T = 256
FF_CHUNK = 256


def _rms(x, g):
    return x * lax.rsqrt(jnp.mean(x * x, axis=-1, keepdims=True) + NORM_EPS) * g


def _resident(shape):
    nd = len(shape)
    return pl.BlockSpec(shape, lambda *_: (0,) * nd, pipeline_mode=pl.Buffered(1))


def _params(sem):
    return pltpu.CompilerParams(dimension_semantics=sem, vmem_limit_bytes=VMEM_LIMIT)


def _in_proj_kernel(x_ref, g_ref, w_ref, o_ref, *, n_chunk):
    xn = _rms(x_ref[...], g_ref[...]).astype(BF16)
    n = o_ref.shape[1]
    for j in range(n // n_chunk):
        sl = slice(j * n_chunk, (j + 1) * n_chunk)
        o_ref[:, sl] = jnp.dot(xn, w_ref[:, sl], preferred_element_type=F32).astype(o_ref.dtype)


def _in_proj(h, g, w, tm):
    t, d = h.shape
    n = w.shape[1]
    return pl.pallas_call(
        functools.partial(_in_proj_kernel, n_chunk=512),
        grid=(t // tm,),
        in_specs=[pl.BlockSpec((tm, d), lambda i: (i, 0)), _resident((1, d)), _resident((d, n))],
        out_specs=pl.BlockSpec((tm, n), lambda i: (i, 0)),
        out_shape=jax.ShapeDtypeStruct((t, n), BF16),
        compiler_params=_params(("parallel",)),
        name="in_proj",
    )(h, g, w)


def _local_mix_kernel(z_ref, halo_ref, poolw_ref, pscale_ref, convw_ref, lng_ref, sguw_ref,
                      sgub_ref, o_ref, *, pw, cw):
    ts = z_ref.shape[0]
    i = pl.program_id(1)
    z = z_ref[...].astype(F32)
    halo = halo_ref[...].astype(F32)
    halo = jnp.where(i > 0, halo, 0.0)
    ext = jnp.concatenate([halo[:, :pw + 3 * cw], z[:, :pw + 3 * cw]], axis=0)
    rows = ext.shape[0]

    def back(x, k):
        return pltpu.roll(x, k, axis=0)

    a = ext[:, :pw]
    s2 = a + back(a, 1)
    s4 = s2 + back(s2, 2)
    s8 = s4 + back(s4, 4)
    s16 = s8 + back(s8, 8)
    lane = lax.broadcasted_iota(jnp.int32, (rows, pw), 1)
    grp = lane // (pw // POOL_GROUPS)
    win_sum = jnp.where(grp == 0, s2, jnp.where(grp == 1, s4, jnp.where(grp == 2, s8, s16)))
    win = jnp.where(grp == 0, 2, jnp.where(grp == 1, 4, jnp.where(grp == 2, 8, 16)))
    pos = i * ts - HALO + lax.broadcasted_iota(jnp.int32, (rows, pw), 0)
    count = jnp.minimum(pos + 1, win).astype(F32)
    pooled = (win_sum / jnp.maximum(count, 1.0) - a)[HALO:]
    y_a = jnp.dot(pooled.astype(BF16), poolw_ref[...], preferred_element_type=F32) * pscale_ref[...]
    o_ref[:, 0:pw] = y_a.astype(o_ref.dtype)

    b_gate = z[:, pw:pw + cw]
    zc = ext[:, pw + cw:pw + 2 * cw] * ext[:, pw + 2 * cw:pw + 3 * cw]
    conv = (convw_ref[0:1, :] * back(zc, 2) + convw_ref[1:2, :] * back(zc, 1)
            + convw_ref[2:3, :] * zc)[HALO:]
    o_ref[:, pw:pw + cw] = (b_gate * conv).astype(o_ref.dtype)

    sw = (z.shape[1] - pw - 3 * cw) // 2
    zc_uv = z[:, pw + 3 * cw:]
    uv = 0.5 * zc_uv * (1.0 + lax.erf(zc_uv * math.sqrt(0.5)))
    u = uv[:, :sw]
    v = uv[:, sw:]
    mu = jnp.mean(v, axis=-1, keepdims=True)
    var = jnp.mean(jnp.square(v - mu), axis=-1, keepdims=True)
    vn = (v - mu) * lax.rsqrt(var + NORM_EPS) * lng_ref[...]
    glane = lax.broadcasted_iota(jnp.int32, (SGU_SEG, sw), 1) // (sw // SGU_GROUPS)
    wcat = sguw_ref[...]
    bias = sgub_ref[...]
    for n in range(ts // SGU_SEG):
        seg = vn[n * SGU_SEG:(n + 1) * SGU_SEG]
        rhs = jnp.concatenate(
            [jnp.where(glane == g, seg, 0.0) for g in range(SGU_GROUPS)], axis=0).astype(BF16)
        s = jnp.dot(wcat, rhs, preferred_element_type=F32) + bias
        o_ref[n * SGU_SEG:(n + 1) * SGU_SEG, pw + cw:pw + cw + sw] = (
            u[n * SGU_SEG:(n + 1) * SGU_SEG] * s).astype(o_ref.dtype)


def _local_mix(z3, poolw_bd, pscale, convw, lng, sguw_cat, sgub_full, ts, pw, cw, sw):
    b, s, _ = z3.shape
    cols = pw + 3 * cw + 2 * sw
    hb = ts // HALO
    return pl.pallas_call(
        functools.partial(_local_mix_kernel, pw=pw, cw=cw),
        grid=(b, s // ts),
        in_specs=[
            pl.BlockSpec((None, ts, cols), lambda bi, i: (bi, i, 0)),
            pl.BlockSpec((None, HALO, cols), lambda bi, i: (bi, jnp.maximum(i * hb - 1, 0), 0)),
            _resident(poolw_bd.shape), _resident(pscale.shape), _resident(convw.shape),
            _resident(lng.shape), _resident(sguw_cat.shape), _resident(sgub_full.shape),
        ],
        out_specs=pl.BlockSpec((None, ts, pw + cw + sw), lambda bi, i: (bi, i, 0)),
        out_shape=jax.ShapeDtypeStruct((b, s, pw + cw + sw), BF16),
        compiler_params=_params(("parallel", "parallel")),
        name="local_mix",
    )(z3, z3, poolw_bd, pscale, convw, lng, sguw_cat, sgub_full)


def _diff_attn_kernel(q_ref, k_ref, v_ref, bias_ref, qg_ref, kg_ref, lam_ref, sg_ref, o_ref,
                      kn_ref, vt_ref, qs_ref, m_ref, l_ref, acc_ref, *, lam_init):
    tq = q_ref.shape[0]
    nt = kn_ref.shape[0]
    i = pl.program_id(2)
    half = lax.broadcasted_iota(jnp.int32, (1, 2 * DIFF_QK_DIM), 1) < DIFF_QK_DIM

    def qk_norm(x, g):
        sq = x * x
        ss0 = jnp.sum(jnp.where(half, sq, 0.0), axis=-1, keepdims=True)
        ss1 = jnp.sum(jnp.where(half, 0.0, sq), axis=-1, keepdims=True)
        r0 = lax.rsqrt(ss0 * (1.0 / DIFF_QK_DIM) + NORM_EPS)
        r1 = lax.rsqrt(ss1 * (1.0 / DIFF_QK_DIM) + NORM_EPS)
        return x * jnp.where(half, r0, r1) * g

    @pl.when(i == 0)
    def _():
        for j in range(nt):
            rows = slice(j * tq, (j + 1) * tq)
            kn_ref[j] = qk_norm(k_ref[rows, :].astype(F32), kg_ref[...]).astype(BF16)
            vt_ref[j] = v_ref[rows, :].astype(F32).T.astype(BF16)

    qn = qk_norm(q_ref[...].astype(F32), qg_ref[...]) * (DIFF_QK_DIM ** -0.5)
    qs_ref[0:tq, :] = jnp.where(half, qn, 0.0).astype(BF16)
    qs_ref[tq:2 * tq, :] = jnp.where(half, 0.0, qn).astype(BF16)
    m_ref[...] = jnp.full(m_ref.shape, NEG_BIG, F32)
    l_ref[...] = jnp.zeros(l_ref.shape, F32)
    acc_ref[...] = jnp.zeros(acc_ref.shape, F32)

    def tile(j, bias_idx):
        st = lax.dot_general(kn_ref[j], qs_ref[...], (((1,), (1,)), ((), ())),
                             preferred_element_type=F32)
        vt = vt_ref[j]
        for mi in range(2):
            sm = st[:, mi * tq:(mi + 1) * tq]
            if bias_idx is not None:
                sm = sm + bias_ref[mi, bias_idx]
            m_old = m_ref[mi]
            m_new = jnp.maximum(m_old, jnp.max(sm, axis=0, keepdims=True))
            alpha = jnp.exp(m_old - m_new)
            p = jnp.exp(sm - m_new)
            l_ref[mi] = alpha * l_ref[mi] + jnp.sum(p, axis=0, keepdims=True)
            acc_ref[mi] = alpha * acc_ref[mi] + jnp.dot(vt, p.astype(BF16),
                                                        preferred_element_type=F32)
            m_ref[mi] = m_new

    def far(j, c):
        tile(j, None)
        return c

    lax.fori_loop(0, jnp.maximum(i - 1, 0), far, 0)

    @pl.when(i > 0)
    def _():
        tile(i - 1, 0)

    tile(i, 1)

    lp = lam_ref[...]
    lam = (jnp.exp(jnp.sum(lp[0:1] * lp[1:2], axis=-1, keepdims=True))
           - jnp.exp(jnp.sum(lp[2:3] * lp[3:4], axis=-1, keepdims=True)) + lam_init)
    o = acc_ref[0] * (1.0 / l_ref[0]) - acc_ref[1] * (lam / l_ref[1])
    o = o * lax.rsqrt(jnp.mean(o * o, axis=0, keepdims=True) + NORM_EPS)
    o_ref[...] = (o.T * (sg_ref[...] * (1.0 - lam_init))).astype(o_ref.dtype)


def _diff_attn(z3, bias_near, qg2, kg2, lam_p, subln_g, lam_init, q_col, k_col, v_col):
    b, s, _ = z3.shape
    tq = ATT_T
    hw = 2 * DIFF_QK_DIM
    return pl.pallas_call(
        functools.partial(_diff_attn_kernel, lam_init=lam_init),
        grid=(b, DIFF_HEADS, s // tq),
        in_specs=[
            pl.BlockSpec((None, tq, hw), lambda bi, h, i: (bi, i, q_col + h)),
            pl.BlockSpec((None, s, hw), lambda bi, h, i: (bi, 0, k_col + h)),
            pl.BlockSpec((None, s, DIFF_V_DIM), lambda bi, h, i: (bi, 0, v_col + h)),
            pl.BlockSpec((None, 2, 2, tq, tq), lambda bi, h, i: (h, 0, 0, 0, 0)),
            _resident(qg2.shape), _resident(kg2.shape), _resident(lam_p.shape),
            _resident(subln_g.shape),
        ],
        out_specs=pl.BlockSpec((None, tq, DIFF_V_DIM), lambda bi, h, i: (bi, i, h)),
        out_shape=jax.ShapeDtypeStruct((b, s, DIFF_HEADS * DIFF_V_DIM), BF16),
        scratch_shapes=[
            pltpu.VMEM((s // tq, tq, hw), BF16),
            pltpu.VMEM((s // tq, DIFF_V_DIM, tq), BF16),
            pltpu.VMEM((2 * tq, hw), BF16),
            pltpu.VMEM((2, 1, tq), F32),
            pltpu.VMEM((2, 1, tq), F32),
            pltpu.VMEM((2, DIFF_V_DIM, tq), F32),
        ],
        compiler_params=_params(("parallel", "parallel", "arbitrary")),
        name="diff_attn",
    )(z3, z3, z3, bias_near, qg2, kg2, lam_p, subln_g)


def _merge_kernel(h_ref, yabc_ref, yd_ref, g_ref, wg_ref, wb_ref, wo_ref, o_ref, *, widths):
    h = h_ref[...]
    d = h.shape[1]
    xn = _rms(h, g_ref[...]).astype(BF16)
    merged = None
    off = 0
    yoff = 0
    for bi, w in enumerate(widths):
        gate = jax.nn.sigmoid(jnp.dot(xn, wg_ref[:, bi * d:(bi + 1) * d],
                                      preferred_element_type=F32))
        if bi < len(widths) - 1:
            y = yabc_ref[:, yoff:yoff + w]
            yoff += w
        else:
            y = yd_ref[...]
        proj = jnp.dot(y, wb_ref[off:off + w, :], preferred_element_type=F32)
        off += w
        merged = gate * proj if merged is None else merged + gate * proj
    o_ref[...] = h + jnp.dot(merged.astype(BF16), wo_ref[...], preferred_element_type=F32)


def _merge(h, y_abc, y_d, g, wg, wb, wo, tm, widths):
    t, d = h.shape
    return pl.pallas_call(
        functools.partial(_merge_kernel, widths=widths),
        grid=(t // tm,),
        in_specs=[
            pl.BlockSpec((tm, d), lambda i: (i, 0)),
            pl.BlockSpec((tm, y_abc.shape[1]), lambda i: (i, 0)),
            pl.BlockSpec((tm, y_d.shape[1]), lambda i: (i, 0)),
            _resident(g.shape), _resident(wg.shape), _resident(wb.shape), _resident(wo.shape),
        ],
        out_specs=pl.BlockSpec((tm, d), lambda i: (i, 0)),
        out_shape=jax.ShapeDtypeStruct((t, d), F32),
        compiler_params=_params(("parallel",)),
        name="merge",
    )(h, y_abc, y_d, g, wg, wb, wo)


def _swiglu_acc(xn, wgu_ref, wd_ref, d_ff):
    acc = None
    for c in range(d_ff // FF_CHUNK):
        lo = c * FF_CHUNK
        g = jnp.dot(xn, wgu_ref[:, lo:lo + FF_CHUNK], preferred_element_type=F32)
        u = jnp.dot(xn, wgu_ref[:, d_ff + lo:d_ff + lo + FF_CHUNK], preferred_element_type=F32)
        act = (g * jax.nn.sigmoid(g) * u).astype(BF16)
        part = jnp.dot(act, wd_ref[lo:lo + FF_CHUNK, :], preferred_element_type=F32)
        acc = part if acc is None else acc + part
    return acc


def _ffn_kernel(h_ref, g_ref, wgu_ref, wd_ref, o_ref):
    h = h_ref[...]
    xn = _rms(h, g_ref[...]).astype(BF16)
    o_ref[...] = h + _swiglu_acc(xn, wgu_ref, wd_ref, wd_ref.shape[0])


def _ffn(h, g, wgu, wd, tm):
    t, d = h.shape
    return pl.pallas_call(
        _ffn_kernel,
        grid=(t // tm,),
        in_specs=[pl.BlockSpec((tm, d), lambda i: (i, 0)), _resident(g.shape),
                  _resident(wgu.shape), _resident(wd.shape)],
        out_specs=pl.BlockSpec((tm, d), lambda i: (i, 0)),
        out_shape=jax.ShapeDtypeStruct((t, d), F32),
        compiler_params=_params(("parallel",)),
        name="ffn",
    )(h, g, wgu, wd)


def _router_kernel(h_ref, g_ref, rw_ref, o_ref, tot_ref, carry_ref):
    tm = h_ref.shape[0]

    @pl.when(pl.program_id(0) == 0)
    def _():
        carry_ref[...] = jnp.zeros(carry_ref.shape, F32)

    hn = _rms(h_ref[...], g_ref[...])
    logits = jnp.dot(hn, rw_ref[...], preferred_element_type=F32, precision=lax.Precision.HIGHEST)
    lane = lax.broadcasted_iota(jnp.int32, (tm, LANES), 1)
    logits = jnp.where(lane < N_EXPERTS, logits, NEG_BIG)
    v1 = jnp.max(logits, axis=-1, keepdims=True)
    i1 = jnp.min(jnp.where(logits == v1, lane, LANES), axis=-1, keepdims=True)
    rest = jnp.where(lane == i1, NEG_BIG, logits)
    v2 = jnp.max(rest, axis=-1, keepdims=True)
    i2 = jnp.min(jnp.where(rest == v2, lane, LANES), axis=-1, keepdims=True)
    e = jnp.exp(v2 - v1)
    w1 = 1.0 / (1.0 + e)
    w2 = e / (1.0 + e)
    cnt = jnp.where((lane == i1) | (lane == i2), 1.0, 0.0)
    r = lax.broadcasted_iota(jnp.int32, (tm, tm), 0)
    c = lax.broadcasted_iota(jnp.int32, (tm, tm), 1)
    tri = jnp.where(c < r, 1.0, 0.0).astype(BF16)
    excl = jnp.dot(tri, cnt.astype(BF16), preferred_element_type=F32) + carry_ref[...]
    rank1 = jnp.sum(jnp.where(lane == i1, excl, 0.0), axis=-1, keepdims=True)
    rank2 = jnp.sum(jnp.where(lane == i2, excl, 0.0), axis=-1, keepdims=True)
    carry_ref[...] = carry_ref[...] + jnp.sum(cnt, axis=0, keepdims=True)
    tot_ref[...] = carry_ref[...]
    packed = jnp.where(lane == 0, i1.astype(F32), jnp.where(lane == 1, i2.astype(F32),
             jnp.where(lane == 2, w1, jnp.where(lane == 3, w2,
             jnp.where(lane == 4, rank1, jnp.where(lane == 5, rank2, 0.0))))))
    o_ref[...] = packed


def _router(h, g, rw_pad, tm):
    t, d = h.shape
    return pl.pallas_call(
        _router_kernel,
        grid=(t // tm,),
        in_specs=[pl.BlockSpec((tm, d), lambda i: (i, 0)), _resident(g.shape),
                  _resident(rw_pad.shape)],
        out_specs=[pl.BlockSpec((tm, LANES), lambda i: (i, 0)),
                   pl.BlockSpec((1, LANES), lambda i: (0, 0))],
        out_shape=[jax.ShapeDtypeStruct((t, LANES), F32), jax.ShapeDtypeStruct((1, LANES), F32)],
        scratch_shapes=[pltpu.VMEM((1, LANES), F32)],
        compiler_params=_params(("arbitrary",)),
        name="router",
    )(h, g, rw_pad)


def _row_copy(src, dst, s, d, sem):
    return pltpu.make_async_copy(src.at[pl.ds(s, 1)], dst.at[pl.ds(d, 1)], sem)


def _dispatch_kernel(dest_ref, h_ref, xs_in_ref, xs_ref, sem):
    del xs_in_ref
    ch = h_ref.shape[0]

    def issue(t, c):
        _row_copy(h_ref, xs_ref, t, dest_ref[0, t], sem).start()
        _row_copy(h_ref, xs_ref, t, dest_ref[1, t], sem).start()
        return c

    lax.fori_loop(0, ch, issue, 0, unroll=8)

    def drain(t, c):
        _row_copy(h_ref, xs_ref, t, dest_ref[0, t], sem).wait()
        _row_copy(h_ref, xs_ref, t, dest_ref[1, t], sem).wait()
        return c

    lax.fori_loop(0, ch, drain, 0, unroll=8)


def _dispatch(dest, h, xs_zero):
    nc, _, ch = dest.shape
    d = h.shape[1]
    return pl.pallas_call(
        _dispatch_kernel,
        grid=(nc,),
        in_specs=[pl.BlockSpec((None, 2, ch), lambda c: (c, 0, 0), memory_space=pltpu.SMEM),
                  pl.BlockSpec((ch, d), lambda c: (c, 0)),
                  pl.BlockSpec(memory_space=pl.ANY)],
        out_specs=pl.BlockSpec(memory_space=pl.ANY),
        out_shape=jax.ShapeDtypeStruct(xs_zero.shape, xs_zero.dtype),
        scratch_shapes=[pltpu.SemaphoreType.DMA(())],
        input_output_aliases={2: 0},
        compiler_params=pltpu.CompilerParams(dimension_semantics=("arbitrary",),
                                             has_side_effects=True),
        name="moe_dispatch",
    )(dest, h, xs_zero)


def _expert_kernel(te_ref, nu_ref, x_ref, g_ref, wgu_ref, wd_ref, y_ref):
    del te_ref

    @pl.when(pl.program_id(0) < nu_ref[0])
    def _():
        xn = _rms(x_ref[...], g_ref[...]).astype(BF16)
        y_ref[...] = _swiglu_acc(xn, wgu_ref, wd_ref, wd_ref.shape[0])

    @pl.when(pl.program_id(0) >= nu_ref[0])
    def _():
        y_ref[...] = jnp.zeros(y_ref.shape, y_ref.dtype)


def _experts(tile_expert, n_used, xs, g, wgu, wd, tm):
    r, d = xs.shape
    d_ff = wd.shape[1]
    grid_spec = pltpu.PrefetchScalarGridSpec(
        num_scalar_prefetch=2,
        grid=(r // tm,),
        in_specs=[
            pl.BlockSpec((tm, d), lambda i, te, nu: (i, 0)),
            pl.BlockSpec(g.shape, lambda i, te, nu: (0, 0), pipeline_mode=pl.Buffered(1)),
            pl.BlockSpec((None, d, 2 * d_ff), lambda i, te, nu: (te[i], 0, 0),
                         pipeline_mode=pl.Buffered(1)),
            pl.BlockSpec((None, d_ff, d), lambda i, te, nu: (te[i], 0, 0),
                         pipeline_mode=pl.Buffered(1)),
        ],
        out_specs=pl.BlockSpec((tm, d), lambda i, te, nu: (i, 0)),
    )
    return pl.pallas_call(
        _expert_kernel,
        grid_spec=grid_spec,
        out_shape=jax.ShapeDtypeStruct((r, d), F32),
        compiler_params=_params(("arbitrary",)),
        name="moe_experts",
    )(tile_expert, n_used, xs, g, wgu, wd)


def _combine_kernel(dest_ref, h_ref, pk_ref, y_ref, o_ref, buf_ref, sem):
    tm = h_ref.shape[0]

    def issue(t, c):
        _row_copy(y_ref, buf_ref.at[0], dest_ref[0, t], t, sem).start()
        _row_copy(y_ref, buf_ref.at[1], dest_ref[1, t], t, sem).start()
        return c

    lax.fori_loop(0, tm, issue, 0, unroll=8)

    def drain(t, c):
        _row_copy(y_ref, buf_ref.at[0], dest_ref[0, t], t, sem).wait()
        _row_copy(y_ref, buf_ref.at[1], dest_ref[1, t], t, sem).wait()
        return c

    lax.fori_loop(0, tm, drain, 0, unroll=8)
    pk = pk_ref[...]
    o_ref[...] = h_ref[...] + pk[:, 2:3] * buf_ref[0] + pk[:, 3:4] * buf_ref[1]


def _combine(dest, h, packed, y, tm):
    t, d = h.shape
    return pl.pallas_call(
        _combine_kernel,
        grid=(t // tm,),
        in_specs=[pl.BlockSpec((None, 2, tm), lambda i: (i, 0, 0), memory_space=pltpu.SMEM),
                  pl.BlockSpec((tm, d), lambda i: (i, 0)),
                  pl.BlockSpec((tm, LANES), lambda i: (i, 0)),
                  pl.BlockSpec(memory_space=pl.ANY)],
        out_specs=pl.BlockSpec((tm, d), lambda i: (i, 0)),
        out_shape=jax.ShapeDtypeStruct((t, d), F32),
        scratch_shapes=[pltpu.VMEM((2, tm, d), F32), pltpu.SemaphoreType.DMA(())],
        compiler_params=_params(("arbitrary",)),
        name="moe_combine",
    )(dest, h, packed, y)


def _rel_bucket(rel):
    nb = REL_BUCKETS // 2
    max_exact = nb // 2
    n = jnp.abs(rel)
    nf = jnp.maximum(n, 1).astype(F32)
    large = max_exact + (jnp.log(nf / max_exact) / math.log(REL_MAX_DIST / max_exact)
                         * (nb - max_exact)).astype(jnp.int32)
    large = jnp.minimum(large, nb - 1)
    return jnp.where(rel > 0, nb, 0) + jnp.where(n < max_exact, n, large)


def _near_bias(rel_bias):
    t = ATT_T
    qp = jnp.arange(t)[:, None]
    kp = jnp.arange(t)[None, :]
    far = rel_bias[_rel_bucket(jnp.full((), -(2 * t), jnp.int32))]
    prev = rel_bias[_rel_bucket(kp - t - qp)] - far
    diag = rel_bias[_rel_bucket(kp - qp)] - far
    diag = jnp.where(((kp // CHUNK) <= (qp // CHUNK))[:, :, None], diag, NEG_BIG)
    both = jnp.stack([prev, diag], axis=0)
    return both.reshape(2, t, t, DIFF_HEADS, 2).transpose(3, 4, 0, 2, 1).astype(F32)


def kernel(x, rel_bias, norm1_g, w_in, pool_w, pool_scale, conv_w, sgu_ln_g, sgu_w, sgu_b, q_norm_g, k_norm_g, diff_lambda, subln_g, w_branch_pool, w_branch_conv, w_branch_sgu, w_branch_attn, w_out, norm2_g, ffn_w_gate_up, ffn_w_down, router_w, moe_w_gate_up, moe_w_down):
    b, s, d = x.shape
    t = b * s
    depth = w_in.shape[0]
    pw = pool_scale.shape[1]
    cw = conv_w.shape[2]
    sw = sgu_ln_g.shape[1]
    aw = w_branch_attn.shape[1]
    mix_cols = pw + 3 * cw + 2 * sw + 3 * aw
    nb = REL_BUCKETS // 2
    assert nb // 2 + int(math.log((ATT_T + 1) / (nb // 2)) / math.log(REL_MAX_DIST / (nb // 2))
                         * (nb - nb // 2)) >= nb - 1
    q_col = (pw + 3 * cw + 2 * sw) // LANES
    k_col = q_col + aw // LANES
    v_col = k_col + aw // LANES
    tm = min(512, t)
    ts = min(512, s)

    bias_near = _near_bias(rel_bias)
    tri = jnp.tril(jnp.ones((SGU_SEG, SGU_SEG), bool))
    gd = pw // POOL_GROUPS

    h = x.reshape(t, d)
    for layer in range(depth):
        lam_init = 0.8 - 0.6 * math.exp(-0.3 * layer)
        w_mix = w_in[layer, :, :mix_cols].astype(BF16)
        w_gate = w_in[layer, :, mix_cols:].astype(BF16)
        poolw_bd = jnp.zeros((pw, pw), F32)
        for g in range(POOL_GROUPS):
            poolw_bd = poolw_bd.at[g * gd:(g + 1) * gd, g * gd:(g + 1) * gd].set(pool_w[layer, g])
        sguw_cat = jnp.where(tri[None], sgu_w[layer], 0.0).transpose(1, 0, 2).reshape(
            SGU_SEG, SGU_GROUPS * SGU_SEG).astype(BF16)
        sgub_full = jnp.repeat(sgu_b[layer].T, sw // SGU_GROUPS, axis=1)
        wb = jnp.concatenate([w_branch_pool[layer], w_branch_conv[layer], w_branch_sgu[layer],
                              w_branch_attn[layer]], axis=0).astype(BF16)

        z = _in_proj(h, norm1_g[layer][None], w_mix, tm)
        z3 = z.reshape(b, s, mix_cols)
        y_abc = _local_mix(z3, poolw_bd.astype(BF16), pool_scale[layer][None], conv_w[layer],
                           sgu_ln_g[layer][None], sguw_cat, sgub_full, ts, pw, cw, sw)
        y_d = _diff_attn(z3, bias_near, jnp.tile(q_norm_g[layer], 2)[None],
                         jnp.tile(k_norm_g[layer], 2)[None], diff_lambda[layer],
                         subln_g[layer][None], lam_init, q_col, k_col, v_col)
        h = _merge(h, y_abc.reshape(t, -1), y_d.reshape(t, -1), norm1_g[layer][None], w_gate, wb,
                   w_out[layer].astype(BF16), tm, (pw, cw, sw, aw))

        g2 = norm2_g[layer][None]
        if layer % 2 == 0:
            h = _ffn(h, g2, ffn_w_gate_up[layer // 2].astype(BF16),
                     ffn_w_down[layer // 2].astype(BF16), tm)
        else:
            li = layer // 2
            rw_pad = jnp.zeros((d, LANES), F32).at[:, :N_EXPERTS].set(router_w[li])
            packed, totals = _router(h, g2, rw_pad, tm)
            n_e = totals[0, :N_EXPERTS].astype(jnp.int32)
            n_pad = ((n_e + tm - 1) // tm) * tm
            ends = jnp.cumsum(n_pad)
            starts = ends - n_pad
            e1 = packed[:, 0].astype(jnp.int32)
            e2 = packed[:, 1].astype(jnp.int32)
            eids = jnp.arange(N_EXPERTS)[None, :]
            dest1 = (jnp.sum(jnp.where(e1[:, None] == eids, starts[None, :], 0), axis=1)
                     + packed[:, 4].astype(jnp.int32))
            dest2 = (jnp.sum(jnp.where(e2[:, None] == eids, starts[None, :], 0), axis=1)
                     + packed[:, 5].astype(jnp.int32))
            rows = 2 * t + N_EXPERTS * tm
            n_tiles = rows // tm
            tile_expert = jnp.minimum(
                jnp.sum((jnp.arange(n_tiles)[:, None] * tm) >= ends[None, :], axis=1),
                N_EXPERTS - 1).astype(jnp.int32)
            n_used = (ends[-1] // tm).astype(jnp.int32)[None]
            ch = min(512, t)
            dest_d = jnp.stack([dest1.reshape(t // ch, ch), dest2.reshape(t // ch, ch)], axis=1)
            xs = _dispatch(dest_d, h, jnp.zeros((rows, d), F32))
            y = _experts(tile_expert, n_used, xs, g2, moe_w_gate_up[li].astype(BF16),
                         moe_w_down[li].astype(BF16), tm)
            tc = min(256, t)
            dest_c = jnp.stack([dest1.reshape(t // tc, tc), dest2.reshape(t // tc, tc)], axis=1)
            h = _combine(dest_c, h, packed, y, tc)
    return h.reshape(b, s, d)
```

```python
import functools
import math

import jax
import jax.numpy as jnp
import numpy as np
from jax import lax
from jax.experimental import pallas as pl
from jax.experimental.pallas import tpu as pltpu

F32 = jnp.float32
BF16 = jnp.bfloat16

NORM_EPS = 1e-6
CHUNK = 64
POOL_WINDOWS = (2, 4, 8, 16)
POOL_GROUPS = 4
CONV_K = 3
SGU_GROUPS = 4
SGU_SEG = 128
DIFF_HEADS = 4
DIFF_QK_DIM = 64
DIFF_V_DIM = 128
REL_BUCKETS = 32
REL_MAX_DIST = 128
N_EXPERTS = 8
LANES = 128
V7X_VMEM_BYTES = 64 * 1024 * 1024
VMEM_LIMIT = V7X_VMEM_BYTES - 8 * 1024 * 1024
NEG_BIG = -1e30
LOG2E = math.log2(math.e)

HALO = 16
ATT_T = 256
FF_CHUNK = 256


def _rms(x, g):
    return x * lax.rsqrt(jnp.mean(x * x, axis=-1, keepdims=True) + NORM_EPS) * g


def _resident(shape):
    nd = len(shape)
    return pl.BlockSpec(shape, lambda *_: (0,) * nd, pipeline_mode=pl.Buffered(1))


def _params(sem):
    return pltpu.CompilerParams(dimension_semantics=sem, vmem_limit_bytes=VMEM_LIMIT)


def _in_proj_kernel(x_ref, g_ref, w_ref, o_ref, *, n_chunk):
    xn = _rms(x_ref[...], g_ref[...]).astype(BF16)
    n = o_ref.shape[1]
    for j in range(n // n_chunk):
        sl = slice(j * n_chunk, (j + 1) * n_chunk)
        o_ref[:, sl] = jnp.dot(xn, w_ref[:, sl], preferred_element_type=F32).astype(o_ref.dtype)


def _in_proj(h, g, w, tm):
    t, d = h.shape
    n = w.shape[1]
    return pl.pallas_call(
        functools.partial(_in_proj_kernel, n_chunk=512),
        grid=(t // tm,),
        in_specs=[pl.BlockSpec((tm, d), lambda i: (i, 0)), _resident((1, d)), _resident((d, n))],
        out_specs=pl.BlockSpec((tm, n), lambda i: (i, 0)),
        out_shape=jax.ShapeDtypeStruct((t, n), BF16),
        compiler_params=_params(("parallel",)),
        name="in_proj",
    )(h, g, w)


def _local_mix_kernel(z_ref, halo_ref, poolw_ref, pscale_ref, convw_ref, lng_ref, sguw_ref,
                      sgub_ref, o_ref, *, pw, cw):
    ts = z_ref.shape[0]
    i = pl.program_id(1)
    z = z_ref[...].astype(F32)
    halo = halo_ref[...].astype(F32)
    halo = jnp.where(i > 0, halo, 0.0)
    ext = jnp.concatenate([halo[:, :pw + 3 * cw], z[:, :pw + 3 * cw]], axis=0)
    rows = ext.shape[0]

    def back(x, k):
        return pltpu.roll(x, k, axis=0)

    a = ext[:, :pw]
    s2 = a + back(a, 1)
    s4 = s2 + back(s2, 2)
    s8 = s4 + back(s4, 4)
    s16 = s8 + back(s8, 8)
    lane = lax.broadcasted_iota(jnp.int32, (rows, pw), 1)
    grp = lane // (pw // POOL_GROUPS)
    win_sum = jnp.where(grp == 0, s2, jnp.where(grp == 1, s4, jnp.where(grp == 2, s8, s16)))
    win = jnp.where(grp == 0, 2, jnp.where(grp == 1, 4, jnp.where(grp == 2, 8, 16)))
    pos = i * ts - HALO + lax.broadcasted_iota(jnp.int32, (rows, pw), 0)
    count = jnp.minimum(pos + 1, win).astype(F32)
    pooled = (win_sum / jnp.maximum(count, 1.0) - a)[HALO:]
    y_a = jnp.dot(pooled.astype(BF16), poolw_ref[...], preferred_element_type=F32) * pscale_ref[...]
    o_ref[:, 0:pw] = y_a.astype(o_ref.dtype)

    b_gate = z[:, pw:pw + cw]
    zc = ext[:, pw + cw:pw + 2 * cw] * ext[:, pw + 2 * cw:pw + 3 * cw]
    conv = (convw_ref[0:1, :] * back(zc, 2) + convw_ref[1:2, :] * back(zc, 1)
            + convw_ref[2:3, :] * zc)[HALO:]
    o_ref[:, pw:pw + cw] = (b_gate * conv).astype(o_ref.dtype)

    sw = (z.shape[1] - pw - 3 * cw) // 2
    zc_uv = z[:, pw + 3 * cw:]
    uv = 0.5 * zc_uv * (1.0 + lax.erf(zc_uv * math.sqrt(0.5)))
    u = uv[:, :sw]
    v = uv[:, sw:]
    mu = jnp.mean(v, axis=-1, keepdims=True)
    var = jnp.mean(jnp.square(v - mu), axis=-1, keepdims=True)
    vn = (v - mu) * lax.rsqrt(var + NORM_EPS) * lng_ref[...]
    glane = lax.broadcasted_iota(jnp.int32, (SGU_SEG, sw), 1) // (sw // SGU_GROUPS)
    wcat = sguw_ref[...]
    bias = sgub_ref[...]
    for n in range(ts // SGU_SEG):
        seg = vn[n * SGU_SEG:(n + 1) * SGU_SEG]
        rhs = jnp.concatenate(
            [jnp.where(glane == g, seg, 0.0) for g in range(SGU_GROUPS)], axis=0).astype(BF16)
        s = jnp.dot(wcat, rhs, preferred_element_type=F32) + bias
        o_ref[n * SGU_SEG:(n + 1) * SGU_SEG, pw + cw:pw + cw + sw] = (
            u[n * SGU_SEG:(n + 1) * SGU_SEG] * s).astype(o_ref.dtype)


def _local_mix(z3, poolw_bd, pscale, convw, lng, sguw_cat, sgub_full, ts, pw, cw, sw):
    b, s, _ = z3.shape
    cols = pw + 3 * cw + 2 * sw
    hb = ts // HALO
    return pl.pallas_call(
        functools.partial(_local_mix_kernel, pw=pw, cw=cw),
        grid=(b, s // ts),
        in_specs=[
            pl.BlockSpec((None, ts, cols), lambda bi, i: (bi, i, 0)),
            pl.BlockSpec((None, HALO, cols), lambda bi, i: (bi, jnp.maximum(i * hb - 1, 0), 0)),
            _resident(poolw_bd.shape), _resident(pscale.shape), _resident(convw.shape),
            _resident(lng.shape), _resident(sguw_cat.shape), _resident(sgub_full.shape),
        ],
        out_specs=pl.BlockSpec((None, ts, pw + cw + sw), lambda bi, i: (bi, i, 0)),
        out_shape=jax.ShapeDtypeStruct((b, s, pw + cw + sw), BF16),
        compiler_params=_params(("parallel", "parallel")),
        name="local_mix",
    )(z3, z3, poolw_bd, pscale, convw, lng, sguw_cat, sgub_full)


def _diff_attn_kernel(q_ref, k_ref, v_ref, bias_ref, qg_ref, kg_ref, lam_ref, sg_ref, o_ref,
                      kn_ref, vt_ref, qs_ref, st_ref, *, lam_init):
    tq = q_ref.shape[0]
    nt = kn_ref.shape[0] // tq
    i = pl.program_id(2)
    half = lax.broadcasted_iota(jnp.int32, (1, 2 * DIFF_QK_DIM), 1) < DIFF_QK_DIM

    def qk_norm(x, g):
        sq = x * x
        ss0 = jnp.sum(jnp.where(half, sq, 0.0), axis=-1, keepdims=True)
        ss1 = jnp.sum(jnp.where(half, 0.0, sq), axis=-1, keepdims=True)
        r0 = lax.rsqrt(ss0 * (1.0 / DIFF_QK_DIM) + NORM_EPS)
        r1 = lax.rsqrt(ss1 * (1.0 / DIFF_QK_DIM) + NORM_EPS)
        return x * jnp.where(half, r0, r1) * g

    @pl.when(i == 0)
    def _():
        for j in range(nt):
            rows = slice(j * tq, (j + 1) * tq)
            kn_ref[rows, :] = qk_norm(k_ref[rows, :].astype(F32), kg_ref[...]).astype(BF16)
            vt_ref[:, rows] = v_ref[rows, :].astype(F32).T.astype(BF16)

    qn = qk_norm(q_ref[...].astype(F32), qg_ref[...]) * (DIFF_QK_DIM ** -0.5 * LOG2E)
    qs_ref[0:tq, :] = jnp.where(half, qn, 0.0).astype(BF16)
    qs_ref[tq:2 * tq, :] = jnp.where(half, 0.0, qn).astype(BF16)

    lp = lam_ref[...]
    lam = (jnp.exp(jnp.sum(lp[0:1] * lp[1:2], axis=-1, keepdims=True))
           - jnp.exp(jnp.sum(lp[2:3] * lp[3:4], axis=-1, keepdims=True)) + lam_init)

    def attend(c):
        m = jnp.full((1, 2 * tq), NEG_BIG, F32)
        for j in range(c + 1):
            rows = slice(j * tq, (j + 1) * tq)
            st = lax.dot_general(kn_ref[rows, :], qs_ref[...], (((1,), (1,)), ((), ())),
                                 preferred_element_type=F32)
            if j >= c - 1:
                st = st + bias_ref[j - (c - 1)]
            st_ref[rows, :] = st
            m = jnp.maximum(m, jnp.max(st, axis=0, keepdims=True))
        l = jnp.zeros((1, 2 * tq), F32)
        acc = jnp.zeros((DIFF_V_DIM, 2 * tq), F32)
        for j in range(c + 1):
            rows = slice(j * tq, (j + 1) * tq)
            p = jnp.exp2(st_ref[rows, :] - m)
            l = l + jnp.sum(p, axis=0, keepdims=True)
            acc = acc + jnp.dot(vt_ref[:, rows], p.astype(BF16), preferred_element_type=F32)
        o = acc[:, :tq] * (1.0 / l[:, :tq]) - acc[:, tq:] * (lam / l[:, tq:])
        o = o * lax.rsqrt(jnp.mean(o * o, axis=0, keepdims=True) + NORM_EPS)
        o_ref[...] = (o.T * (sg_ref[...] * (1.0 - lam_init))).astype(o_ref.dtype)

    for c in range(nt):
        pl.when(i == c)(functools.partial(attend, c))


def _diff_attn(z3, bias_near, qg2, kg2, lam_p, subln_g, lam_init, q_col, k_col, v_col):
    b, s, _ = z3.shape
    tq = ATT_T
    hw = 2 * DIFF_QK_DIM
    return pl.pallas_call(
        functools.partial(_diff_attn_kernel, lam_init=lam_init),
        grid=(b, DIFF_HEADS, s // tq),
        in_specs=[
            pl.BlockSpec((None, tq, hw), lambda bi, h, i: (bi, i, q_col + h)),
            pl.BlockSpec((None, s, hw), lambda bi, h, i: (bi, 0, k_col + h)),
            pl.BlockSpec((None, s, DIFF_V_DIM), lambda bi, h, i: (bi, 0, v_col + h)),
            pl.BlockSpec((None, 2, tq, 2 * tq), lambda bi, h, i: (h, 0, 0, 0)),
            _resident(qg2.shape), _resident(kg2.shape), _resident(lam_p.shape),
            _resident(subln_g.shape),
        ],
        out_specs=pl.BlockSpec((None, tq, DIFF_V_DIM), lambda bi, h, i: (bi, i, h)),
        out_shape=jax.ShapeDtypeStruct((b, s, DIFF_HEADS * DIFF_V_DIM), BF16),
        scratch_shapes=[
            pltpu.VMEM((s, hw), BF16),
            pltpu.VMEM((DIFF_V_DIM, s), BF16),
            pltpu.VMEM((2 * tq, hw), BF16),
            pltpu.VMEM((s, 2 * tq), F32),
        ],
        compiler_params=_params(("parallel", "parallel", "arbitrary")),
        name="diff_attn",
    )(z3, z3, z3, bias_near, qg2, kg2, lam_p, subln_g)


def _merge_kernel(h_ref, yabc_ref, yd_ref, g_ref, wg_ref, wb_ref, wo_ref, o_ref, *, widths):
    h = h_ref[...]
    d = h.shape[1]
    xn = _rms(h, g_ref[...]).astype(BF16)
    merged = None
    off = 0
    yoff = 0
    for bi, w in enumerate(widths):
        gate = jax.nn.sigmoid(jnp.dot(xn, wg_ref[:, bi * d:(bi + 1) * d],
                                      preferred_element_type=F32))
        if bi < len(widths) - 1:
            y = yabc_ref[:, yoff:yoff + w]
            yoff += w
        else:
            y = yd_ref[...]
        proj = jnp.dot(y, wb_ref[off:off + w, :], preferred_element_type=F32)
        off += w
        merged = gate * proj if merged is None else merged + gate * proj
    o_ref[...] = h + jnp.dot(merged.astype(BF16), wo_ref[...], preferred_element_type=F32)


def _merge(h, y_abc, y_d, g, wg, wb, wo, tm, widths):
    t, d = h.shape
    return pl.pallas_call(
        functools.partial(_merge_kernel, widths=widths),
        grid=(t // tm,),
        in_specs=[
            pl.BlockSpec((tm, d), lambda i: (i, 0)),
            pl.BlockSpec((tm, y_abc.shape[1]), lambda i: (i, 0)),
            pl.BlockSpec((tm, y_d.shape[1]), lambda i: (i, 0)),
            _resident(g.shape), _resident(wg.shape), _resident(wb.shape), _resident(wo.shape),
        ],
        out_specs=pl.BlockSpec((tm, d), lambda i: (i, 0)),
        out_shape=jax.ShapeDtypeStruct((t, d), F32),
        compiler_params=_params(("parallel",)),
        name="merge",
    )(h, y_abc, y_d, g, wg, wb, wo)


def _swiglu_acc(xn, wgu_ref, wd_ref, d_ff):
    acc = None
    for c in range(d_ff // FF_CHUNK):
        lo = c * FF_CHUNK
        g = jnp.dot(xn, wgu_ref[:, lo:lo + FF_CHUNK], preferred_element_type=F32)
        u = jnp.dot(xn, wgu_ref[:, d_ff + lo:d_ff + lo + FF_CHUNK], preferred_element_type=F32)
        act = (g * jax.nn.sigmoid(g) * u).astype(BF16)
        part = jnp.dot(act, wd_ref[lo:lo + FF_CHUNK, :], preferred_element_type=F32)
        acc = part if acc is None else acc + part
    return acc


def _ffn_kernel(h_ref, g_ref, wgu_ref, wd_ref, o_ref):
    h = h_ref[...]
    xn = _rms(h, g_ref[...]).astype(BF16)
    o_ref[...] = h + _swiglu_acc(xn, wgu_ref, wd_ref, wd_ref.shape[0])


def _ffn(h, g, wgu, wd, tm):
    t, d = h.shape
    return pl.pallas_call(
        _ffn_kernel,
        grid=(t // tm,),
        in_specs=[pl.BlockSpec((tm, d), lambda i: (i, 0)), _resident(g.shape),
                  _resident(wgu.shape), _resident(wd.shape)],
        out_specs=pl.BlockSpec((tm, d), lambda i: (i, 0)),
        out_shape=jax.ShapeDtypeStruct((t, d), F32),
        compiler_params=_params(("parallel",)),
        name="ffn",
    )(h, g, wgu, wd)


def _router_kernel(h_ref, g_ref, rw_ref, o_ref, tot_ref, carry_ref):
    tm = h_ref.shape[0]

    @pl.when(pl.program_id(0) == 0)
    def _():
        carry_ref[...] = jnp.zeros(carry_ref.shape, F32)

    hn = _rms(h_ref[...], g_ref[...])
    logits = jnp.dot(hn, rw_ref[...], preferred_element_type=F32, precision=lax.Precision.HIGHEST)
    lane = lax.broadcasted_iota(jnp.int32, (tm, LANES), 1)
    logits = jnp.where(lane < N_EXPERTS, logits, NEG_BIG)
    v1 = jnp.max(logits, axis=-1, keepdims=True)
    i1 = jnp.min(jnp.where(logits == v1, lane, LANES), axis=-1, keepdims=True)
    rest = jnp.where(lane == i1, NEG_BIG, logits)
    v2 = jnp.max(rest, axis=-1, keepdims=True)
    i2 = jnp.min(jnp.where(rest == v2, lane, LANES), axis=-1, keepdims=True)
    e = jnp.exp(v2 - v1)
    w1 = 1.0 / (1.0 + e)
    w2 = e / (1.0 + e)
    cnt = jnp.where((lane == i1) | (lane == i2), 1.0, 0.0)
    r = lax.broadcasted_iota(jnp.int32, (tm, tm), 0)
    c = lax.broadcasted_iota(jnp.int32, (tm, tm), 1)
    tri = jnp.where(c < r, 1.0, 0.0).astype(BF16)
    excl = jnp.dot(tri, cnt.astype(BF16), preferred_element_type=F32) + carry_ref[...]
    rank1 = jnp.sum(jnp.where(lane == i1, excl, 0.0), axis=-1, keepdims=True)
    rank2 = jnp.sum(jnp.where(lane == i2, excl, 0.0), axis=-1, keepdims=True)
    carry_ref[...] = carry_ref[...] + jnp.sum(cnt, axis=0, keepdims=True)
    tot_ref[...] = carry_ref[...]
    packed = jnp.where(lane == 0, i1.astype(F32), jnp.where(lane == 1, i2.astype(F32),
             jnp.where(lane == 2, w1, jnp.where(lane == 3, w2,
             jnp.where(lane == 4, rank1, jnp.where(lane == 5, rank2, 0.0))))))
    o_ref[...] = packed


def _router(h, g, rw_pad, tm):
    t, d = h.shape
    return pl.pallas_call(
        _router_kernel,
        grid=(t // tm,),
        in_specs=[pl.BlockSpec((tm, d), lambda i: (i, 0)), _resident(g.shape),
                  _resident(rw_pad.shape)],
        out_specs=[pl.BlockSpec((tm, LANES), lambda i: (i, 0)),
                   pl.BlockSpec((1, LANES), lambda i: (0, 0))],
        out_shape=[jax.ShapeDtypeStruct((t, LANES), F32), jax.ShapeDtypeStruct((1, LANES), F32)],
        scratch_shapes=[pltpu.VMEM((1, LANES), F32)],
        compiler_params=_params(("arbitrary",)),
        name="router",
    )(h, g, rw_pad)


def _row_copy(src, dst, s, d, sem):
    return pltpu.make_async_copy(src.at[pl.ds(s, 1)], dst.at[pl.ds(d, 1)], sem)


def _dispatch_kernel(dest_ref, h_ref, xs_in_ref, xs_ref, sem):
    del xs_in_ref
    ch = h_ref.shape[0]

    def issue(t, c):
        _row_copy(h_ref, xs_ref, t, dest_ref[0, t], sem).start()
        _row_copy(h_ref, xs_ref, t, dest_ref[1, t], sem).start()
        return c

    lax.fori_loop(0, ch, issue, 0, unroll=8)

    def drain(t, c):
        _row_copy(h_ref, xs_ref, t, dest_ref[0, t], sem).wait()
        _row_copy(h_ref, xs_ref, t, dest_ref[1, t], sem).wait()
        return c

    lax.fori_loop(0, ch, drain, 0, unroll=8)


def _dispatch(dest, h, xs_zero):
    nc, _, ch = dest.shape
    d = h.shape[1]
    return pl.pallas_call(
        _dispatch_kernel,
        grid=(nc,),
        in_specs=[pl.BlockSpec((None, 2, ch), lambda c: (c, 0, 0), memory_space=pltpu.SMEM),
                  pl.BlockSpec((ch, d), lambda c: (c, 0)),
                  pl.BlockSpec(memory_space=pl.ANY)],
        out_specs=pl.BlockSpec(memory_space=pl.ANY),
        out_shape=jax.ShapeDtypeStruct(xs_zero.shape, xs_zero.dtype),
        scratch_shapes=[pltpu.SemaphoreType.DMA(())],
        input_output_aliases={2: 0},
        compiler_params=pltpu.CompilerParams(dimension_semantics=("arbitrary",),
                                             has_side_effects=True),
        name="moe_dispatch",
    )(dest, h, xs_zero)


def _expert_kernel(te_ref, nu_ref, x_ref, g_ref, wgu_ref, wd_ref, y_ref):
    del te_ref

    @pl.when(pl.program_id(0) < nu_ref[0])
    def _():
        xn = _rms(x_ref[...], g_ref[...]).astype(BF16)
        y_ref[...] = _swiglu_acc(xn, wgu_ref, wd_ref, wd_ref.shape[0])

    @pl.when(pl.program_id(0) >= nu_ref[0])
    def _():
        y_ref[...] = jnp.zeros(y_ref.shape, y_ref.dtype)


def _experts(tile_expert, n_used, xs, g, wgu, wd, tm):
    r, d = xs.shape
    d_ff = wd.shape[1]
    grid_spec = pltpu.PrefetchScalarGridSpec(
        num_scalar_prefetch=2,
        grid=(r // tm,),
        in_specs=[
            pl.BlockSpec((tm, d), lambda i, te, nu: (i, 0)),
            pl.BlockSpec(g.shape, lambda i, te, nu: (0, 0), pipeline_mode=pl.Buffered(1)),
            pl.BlockSpec((None, d, 2 * d_ff), lambda i, te, nu: (te[i], 0, 0),
                         pipeline_mode=pl.Buffered(1)),
            pl.BlockSpec((None, d_ff, d), lambda i, te, nu: (te[i], 0, 0),
                         pipeline_mode=pl.Buffered(1)),
        ],
        out_specs=pl.BlockSpec((tm, d), lambda i, te, nu: (i, 0)),
    )
    return pl.pallas_call(
        _expert_kernel,
        grid_spec=grid_spec,
        out_shape=jax.ShapeDtypeStruct((r, d), F32),
        compiler_params=_params(("arbitrary",)),
        name="moe_experts",
    )(tile_expert, n_used, xs, g, wgu, wd)


def _combine_kernel(dest_ref, h_ref, pk_ref, y_ref, o_ref, buf_ref, sem):
    tm = h_ref.shape[0]

    def issue(t, c):
        _row_copy(y_ref, buf_ref.at[0], dest_ref[0, t], t, sem).start()
        _row_copy(y_ref, buf_ref.at[1], dest_ref[1, t], t, sem).start()
        return c

    lax.fori_loop(0, tm, issue, 0, unroll=8)

    def drain(t, c):
        _row_copy(y_ref, buf_ref.at[0], dest_ref[0, t], t, sem).wait()
        _row_copy(y_ref, buf_ref.at[1], dest_ref[1, t], t, sem).wait()
        return c

    lax.fori_loop(0, tm, drain, 0, unroll=8)
    pk = pk_ref[...]
    o_ref[...] = h_ref[...] + pk[:, 2:3] * buf_ref[0] + pk[:, 3:4] * buf_ref[1]


def _combine(dest, h, packed, y, tm):
    t, d = h.shape
    return pl.pallas_call(
        _combine_kernel,
        grid=(t // tm,),
        in_specs=[pl.BlockSpec((None, 2, tm), lambda i: (i, 0, 0), memory_space=pltpu.SMEM),
                  pl.BlockSpec((tm, d), lambda i: (i, 0)),
                  pl.BlockSpec((tm, LANES), lambda i: (i, 0)),
                  pl.BlockSpec(memory_space=pl.ANY)],
        out_specs=pl.BlockSpec((tm, d), lambda i: (i, 0)),
        out_shape=jax.ShapeDtypeStruct((t, d), F32),
        scratch_shapes=[pltpu.VMEM((2, tm, d), F32), pltpu.SemaphoreType.DMA(())],
        compiler_params=_params(("arbitrary",)),
        name="moe_combine",
    )(dest, h, packed, y)


def _rel_bucket(rel):
    nb = REL_BUCKETS // 2
    max_exact = nb // 2
    n = jnp.abs(rel)
    nf = jnp.maximum(n, 1).astype(F32)
    large = max_exact + (jnp.log(nf / max_exact) / math.log(REL_MAX_DIST / max_exact)
                         * (nb - max_exact)).astype(jnp.int32)
    large = jnp.minimum(large, nb - 1)
    return jnp.where(rel > 0, nb, 0) + jnp.where(n < max_exact, n, large)


def _near_bias(rel_bias):
    t = ATT_T
    qp = jnp.arange(t)[:, None]
    kp = jnp.arange(t)[None, :]

    def lookup(rel):
        onehot = jax.nn.one_hot(_rel_bucket(rel), REL_BUCKETS, dtype=F32)
        return jnp.einsum('...b,bm->...m', onehot, rel_bias, precision=lax.Precision.HIGHEST)

    far = lookup(jnp.full((), -(2 * t), jnp.int32))
    prev = (lookup(kp - t - qp) - far) * LOG2E
    diag = (lookup(kp - qp) - far) * LOG2E
    diag = jnp.where(((kp // CHUNK) <= (qp // CHUNK))[:, :, None], diag, NEG_BIG)
    both = jnp.stack([prev, diag], axis=0).reshape(2, t, t, DIFF_HEADS, 2)
    return both.transpose(3, 0, 2, 4, 1).reshape(DIFF_HEADS, 2, t, 2 * t).astype(F32)


def kernel(x, rel_bias, norm1_g, w_in, pool_w, pool_scale, conv_w, sgu_ln_g, sgu_w, sgu_b, q_norm_g, k_norm_g, diff_lambda, subln_g, w_branch_pool, w_branch_conv, w_branch_sgu, w_branch_attn, w_out, norm2_g, ffn_w_gate_up, ffn_w_down, router_w, moe_w_gate_up, moe_w_down):
    b, s, d = x.shape
    t = b * s
    depth = w_in.shape[0]
    pw = pool_scale.shape[1]
    cw = conv_w.shape[2]
    sw = sgu_ln_g.shape[1]
    aw = w_branch_attn.shape[1]
    mix_cols = pw + 3 * cw + 2 * sw + 3 * aw
    nb = REL_BUCKETS // 2
    assert nb // 2 + int(math.log((ATT_T + 1) / (nb // 2)) / math.log(REL_MAX_DIST / (nb // 2))
                         * (nb - nb // 2)) >= nb - 1
    q_col = (pw + 3 * cw + 2 * sw) // LANES
    k_col = q_col + aw // LANES
    v_col = k_col + aw // LANES
    tm = min(512, t)
    ts = min(512, s)

    bias_near = _near_bias(rel_bias)
    tri = jnp.tril(jnp.ones((SGU_SEG, SGU_SEG), bool))
    gd = pw // POOL_GROUPS

    h = x.reshape(t, d)
    for layer in range(depth):
        lam_init = 0.8 - 0.6 * math.exp(-0.3 * layer)
        w_mix = w_in[layer, :, :mix_cols].astype(BF16)
        w_gate = w_in[layer, :, mix_cols:].astype(BF16)
        poolw_bd = jnp.zeros((pw, pw), F32)
        for g in range(POOL_GROUPS):
            poolw_bd = poolw_bd.at[g * gd:(g + 1) * gd, g * gd:(g + 1) * gd].set(pool_w[layer, g])
        sguw_cat = jnp.where(tri[None], sgu_w[layer], 0.0).transpose(1, 0, 2).reshape(
            SGU_SEG, SGU_GROUPS * SGU_SEG).astype(BF16)
        sgub_full = jnp.repeat(sgu_b[layer].T, sw // SGU_GROUPS, axis=1)
        wb = jnp.concatenate([w_branch_pool[layer], w_branch_conv[layer], w_branch_sgu[layer],
                              w_branch_attn[layer]], axis=0).astype(BF16)

        z = _in_proj(h, norm1_g[layer][None], w_mix, tm)
        z3 = z.reshape(b, s, mix_cols)
        y_abc = _local_mix(z3, poolw_bd.astype(BF16), pool_scale[layer][None], conv_w[layer],
                           sgu_ln_g[layer][None], sguw_cat, sgub_full, ts, pw, cw, sw)
        y_d = _diff_attn(z3, bias_near, jnp.tile(q_norm_g[layer], 2)[None],
                         jnp.tile(k_norm_g[layer], 2)[None], diff_lambda[layer],
                         subln_g[layer][None], lam_init, q_col, k_col, v_col)
        h = _merge(h, y_abc.reshape(t, -1), y_d.reshape(t, -1), norm1_g[layer][None], w_gate, wb,
                   w_out[layer].astype(BF16), tm, (pw, cw, sw, aw))

        g2 = norm2_g[layer][None]
        if layer % 2 == 0:
            h = _ffn(h, g2, ffn_w_gate_up[layer // 2].astype(BF16),
                     ffn_w_down[layer // 2].astype(BF16), tm)
        else:
            li = layer // 2
            rw_pad = jnp.zeros((d, LANES), F32).at[:, :N_EXPERTS].set(router_w[li])
            packed, totals = _router(h, g2, rw_pad, tm)
            n_e = totals[0, :N_EXPERTS].astype(jnp.int32)
            n_pad = ((n_e + tm - 1) // tm) * tm
            ends = jnp.cumsum(n_pad)
            starts = ends - n_pad
            e1 = packed[:, 0].astype(jnp.int32)
            e2 = packed[:, 1].astype(jnp.int32)
            eids = jnp.arange(N_EXPERTS)[None, :]
            dest1 = (jnp.sum(jnp.where(e1[:, None] == eids, starts[None, :], 0), axis=1)
                     + packed[:, 4].astype(jnp.int32))
            dest2 = (jnp.sum(jnp.where(e2[:, None] == eids, starts[None, :], 0), axis=1)
                     + packed[:, 5].astype(jnp.int32))
            rows = 2 * t + N_EXPERTS * tm
            n_tiles = rows // tm
            tile_expert = jnp.minimum(
                jnp.sum((jnp.arange(n_tiles)[:, None] * tm) >= ends[None, :], axis=1),
                N_EXPERTS - 1).astype(jnp.int32)
            n_used = (ends[-1] // tm).astype(jnp.int32)[None]
            ch = min(512, t)
            dest_d = jnp.stack([dest1.reshape(t // ch, ch), dest2.reshape(t // ch, ch)], axis=1)
            xs = _dispatch(dest_d, h, jnp.zeros((rows, d), F32))
            y = _experts(tile_expert, n_used, xs, g2, moe_w_gate_up[li].astype(BF16),
                         moe_w_down[li].astype(BF16), tm)
            tc = min(256, t)
            dest_c = jnp.stack([dest1.reshape(t // tc, tc), dest2.reshape(t // tc, tc)], axis=1)
            h = _combine(dest_c, h, packed, y, tc)
    return h.reshape(b, s, d)
```

```python
import functools
import math

import jax
import jax.numpy as jnp
import numpy as np
from jax import lax
from jax.experimental import pallas as pl
from jax.experimental.pallas import tpu as pltpu

F32 = jnp.float32
BF16 = jnp.bfloat16

NORM_EPS = 1e-6
CHUNK = 64
POOL_WINDOWS = (2, 4, 8, 16)
POOL_GROUPS = 4
CONV_K = 3
SGU_GROUPS = 4
SGU_SEG = 128
DIFF_HEADS = 4
DIFF_QK_DIM = 64
DIFF_V_DIM = 128
REL_BUCKETS = 32
REL_MAX_DIST = 128
N_EXPERTS = 8
LANES = 128
V7X_VMEM_BYTES = 64 * 1024 * 1024
VMEM_LIMIT = V7X_VMEM_BYTES - 8 * 1024 * 1024
NEG_BIG = -1e30
LOG2E = math.log2(math.e)

HALO = 16
ATT_T = 256
ATT_HEADS_PER_STEP = 2
FF_CHUNK = 256


def _rms(x, g):
    return x * lax.rsqrt(jnp.mean(x * x, axis=-1, keepdims=True) + NORM_EPS) * g


def _resident(shape):
    nd = len(shape)
    return pl.BlockSpec(shape, lambda *_: (0,) * nd, pipeline_mode=pl.Buffered(1))


def _params(sem):
    return pltpu.CompilerParams(dimension_semantics=sem, vmem_limit_bytes=VMEM_LIMIT)


def _in_proj_kernel(x_ref, g_ref, w_ref, o_ref, *, n_chunk):
    xn = _rms(x_ref[...], g_ref[...]).astype(BF16)
    n = o_ref.shape[1]
    for j in range(n // n_chunk):
        sl = slice(j * n_chunk, (j + 1) * n_chunk)
        o_ref[:, sl] = jnp.dot(xn, w_ref[:, sl], preferred_element_type=F32).astype(o_ref.dtype)


def _in_proj(h, g, w, tm):
    t, d = h.shape
    n = w.shape[1]
    return pl.pallas_call(
        functools.partial(_in_proj_kernel, n_chunk=512),
        grid=(t // tm,),
        in_specs=[pl.BlockSpec((tm, d), lambda i: (i, 0)), _resident((1, d)), _resident((d, n))],
        out_specs=pl.BlockSpec((tm, n), lambda i: (i, 0)),
        out_shape=jax.ShapeDtypeStruct((t, n), BF16),
        compiler_params=_params(("parallel",)),
        name="in_proj",
    )(h, g, w)


def _local_mix_kernel(z_ref, halo_ref, poolw_ref, pscale_ref, convw_ref, lng_ref, sguw_ref,
                      sgub_ref, o_ref, *, pw, cw):
    ts = z_ref.shape[0]
    i = pl.program_id(1)
    z = z_ref[...].astype(F32)
    halo = halo_ref[...].astype(F32)
    halo = jnp.where(i > 0, halo, 0.0)
    ext = jnp.concatenate([halo[:, :pw + 3 * cw], z[:, :pw + 3 * cw]], axis=0)
    rows = ext.shape[0]

    def back(x, k):
        return pltpu.roll(x, k, axis=0)

    a = ext[:, :pw]
    s2 = a + back(a, 1)
    s4 = s2 + back(s2, 2)
    s8 = s4 + back(s4, 4)
    s16 = s8 + back(s8, 8)
    lane = lax.broadcasted_iota(jnp.int32, (rows, pw), 1)
    grp = lane // (pw // POOL_GROUPS)
    win_sum = jnp.where(grp == 0, s2, jnp.where(grp == 1, s4, jnp.where(grp == 2, s8, s16)))
    win = jnp.where(grp == 0, 2, jnp.where(grp == 1, 4, jnp.where(grp == 2, 8, 16)))
    pos = i * ts - HALO + lax.broadcasted_iota(jnp.int32, (rows, pw), 0)
    count = jnp.minimum(pos + 1, win).astype(F32)
    pooled = (win_sum / jnp.maximum(count, 1.0) - a)[HALO:]
    y_a = jnp.dot(pooled.astype(BF16), poolw_ref[...], preferred_element_type=F32) * pscale_ref[...]
    o_ref[:, 0:pw] = y_a.astype(o_ref.dtype)

    b_gate = z[:, pw:pw + cw]
    zc = ext[:, pw + cw:pw + 2 * cw] * ext[:, pw + 2 * cw:pw + 3 * cw]
    conv = (convw_ref[0:1, :] * back(zc, 2) + convw_ref[1:2, :] * back(zc, 1)
            + convw_ref[2:3, :] * zc)[HALO:]
    o_ref[:, pw:pw + cw] = (b_gate * conv).astype(o_ref.dtype)

    sw = (z.shape[1] - pw - 3 * cw) // 2
    zc_uv = z[:, pw + 3 * cw:]
    uv = 0.5 * zc_uv * (1.0 + lax.erf(zc_uv * math.sqrt(0.5)))
    u = uv[:, :sw]
    v = uv[:, sw:]
    mu = jnp.mean(v, axis=-1, keepdims=True)
    var = jnp.mean(jnp.square(v - mu), axis=-1, keepdims=True)
    vn = (v - mu) * lax.rsqrt(var + NORM_EPS) * lng_ref[...]
    glane = lax.broadcasted_iota(jnp.int32, (SGU_SEG, sw), 1) // (sw // SGU_GROUPS)
    wcat = sguw_ref[...]
    bias = sgub_ref[...]
    for n in range(ts // SGU_SEG):
        seg = vn[n * SGU_SEG:(n + 1) * SGU_SEG]
        rhs = jnp.concatenate(
            [jnp.where(glane == g, seg, 0.0) for g in range(SGU_GROUPS)], axis=0).astype(BF16)
        s = jnp.dot(wcat, rhs, preferred_element_type=F32) + bias
        o_ref[n * SGU_SEG:(n + 1) * SGU_SEG, pw + cw:pw + cw + sw] = (
            u[n * SGU_SEG:(n + 1) * SGU_SEG] * s).astype(o_ref.dtype)


def _local_mix(z3, poolw_bd, pscale, convw, lng, sguw_cat, sgub_full, ts, pw, cw, sw):
    b, s, _ = z3.shape
    cols = pw + 3 * cw + 2 * sw
    hb = ts // HALO
    return pl.pallas_call(
        functools.partial(_local_mix_kernel, pw=pw, cw=cw),
        grid=(b, s // ts),
        in_specs=[
            pl.BlockSpec((None, ts, cols), lambda bi, i: (bi, i, 0)),
            pl.BlockSpec((None, HALO, cols), lambda bi, i: (bi, jnp.maximum(i * hb - 1, 0), 0)),
            _resident(poolw_bd.shape), _resident(pscale.shape), _resident(convw.shape),
            _resident(lng.shape), _resident(sguw_cat.shape), _resident(sgub_full.shape),
        ],
        out_specs=pl.BlockSpec((None, ts, pw + cw + sw), lambda bi, i: (bi, i, 0)),
        out_shape=jax.ShapeDtypeStruct((b, s, pw + cw + sw), BF16),
        compiler_params=_params(("parallel", "parallel")),
        name="local_mix",
    )(z3, z3, poolw_bd, pscale, convw, lng, sguw_cat, sgub_full)


def _diff_attn_kernel(q_ref, k_ref, v_ref, bias_ref, qg_ref, kg_ref, lam_ref, sg_ref, o_ref,
                      kn_ref, vt_ref, qs_ref, st_ref, m_ref, *, lam_init):
    tq = q_ref.shape[0]
    hp = kn_ref.shape[0]
    nt = kn_ref.shape[1] // tq
    hw = 2 * DIFF_QK_DIM
    i = pl.program_id(2)
    half = lax.broadcasted_iota(jnp.int32, (1, 2 * DIFF_QK_DIM), 1) < DIFF_QK_DIM

    def qk_norm(x, g):
        sq = x * x
        ss0 = jnp.sum(jnp.where(half, sq, 0.0), axis=-1, keepdims=True)
        ss1 = jnp.sum(jnp.where(half, 0.0, sq), axis=-1, keepdims=True)
        r0 = lax.rsqrt(ss0 * (1.0 / DIFF_QK_DIM) + NORM_EPS)
        r1 = lax.rsqrt(ss1 * (1.0 / DIFF_QK_DIM) + NORM_EPS)
        return x * jnp.where(half, r0, r1) * g

    def key_rows(j):
        if isinstance(j, int):
            return slice(j * tq, (j + 1) * tq)
        return pl.ds(pl.multiple_of(j * tq, tq), tq)

    @pl.when(i == 0)
    def _():
        for h in range(hp):
            cols = slice(h * hw, (h + 1) * hw)
            for j in range(nt):
                rows = key_rows(j)
                kn_ref[h, rows, :] = qk_norm(k_ref[rows, cols].astype(F32), kg_ref[...]).astype(BF16)
                vt_ref[h, j] = v_ref[rows, cols].astype(F32).T.astype(BF16)

    def step(c):
        cur, prv = c % 2, 1 - c % 2
        scoring = c < nt
        if scoring:
            for h in range(hp):
                qn = (qk_norm(q_ref[:, h * hw:(h + 1) * hw].astype(F32), qg_ref[...])
                      * (DIFF_QK_DIM ** -0.5 * LOG2E))
                qs_ref[h, 0:tq, :] = jnp.where(half, qn, 0.0).astype(BF16)
                qs_ref[h, tq:2 * tq, :] = jnp.where(half, 0.0, qn).astype(BF16)
        m_new = [jnp.full((1, 2 * tq), NEG_BIG, F32) for _ in range(hp)]
        m_old = [m_ref[prv, h] for h in range(hp)] if c >= 1 else None
        l = [jnp.zeros((1, 2 * tq), F32) for _ in range(hp)]
        acc = [jnp.zeros((DIFF_V_DIM, 2 * tq), F32) for _ in range(hp)]
        for j in range(c + 1):
            rows = key_rows(j)
            for h in range(hp):
                if scoring:
                    st = lax.dot_general(kn_ref[h, rows, :], qs_ref[h], (((1,), (1,)), ((), ())),
                                         preferred_element_type=F32)
                    if j >= c - 1:
                        st = st + bias_ref[h, j - (c - 1)]
                    st_ref[cur, h, rows, :] = st
                    m_new[h] = jnp.maximum(m_new[h], jnp.max(st, axis=0, keepdims=True))
                if j < c:
                    p = jnp.exp2(st_ref[prv, h, rows, :] - m_old[h])
                    l[h] = l[h] + jnp.sum(p, axis=0, keepdims=True)
                    acc[h] = acc[h] + jnp.dot(vt_ref[h, j], p.astype(BF16),
                                              preferred_element_type=F32)
        if scoring:
            for h in range(hp):
                m_ref[cur, h] = m_new[h]
        if c >= 1:
            lp = lam_ref[...]
            lam = (jnp.exp(jnp.sum(lp[0:1] * lp[1:2], axis=-1, keepdims=True))
                   - jnp.exp(jnp.sum(lp[2:3] * lp[3:4], axis=-1, keepdims=True)) + lam_init)
            for h in range(hp):
                o = (acc[h][:, :tq] * (1.0 / l[h][:, :tq])
                     - acc[h][:, tq:] * (lam / l[h][:, tq:]))
                o = o * lax.rsqrt(jnp.mean(o * o, axis=0, keepdims=True) + NORM_EPS)
                o_ref[:, h * DIFF_V_DIM:(h + 1) * DIFF_V_DIM] = (
                    o.T * (sg_ref[...] * (1.0 - lam_init))).astype(o_ref.dtype)

    for c in range(nt + 1):
        pl.when(i == c)(functools.partial(step, c))


def _diff_attn(z3, bias_near, qg2, kg2, lam_p, subln_g, lam_init, q_col, k_col, v_col):
    b, s, _ = z3.shape
    tq = ATT_T
    hp = ATT_HEADS_PER_STEP
    nt = s // tq
    hw = 2 * DIFF_QK_DIM
    return pl.pallas_call(
        functools.partial(_diff_attn_kernel, lam_init=lam_init),
        grid=(b, DIFF_HEADS // hp, nt + 1),
        in_specs=[
            pl.BlockSpec((None, tq, hp * hw),
                         lambda bi, g, i: (bi, jnp.minimum(i, nt - 1), q_col // hp + g)),
            pl.BlockSpec((None, s, hp * hw), lambda bi, g, i: (bi, 0, k_col // hp + g)),
            pl.BlockSpec((None, s, hp * DIFF_V_DIM), lambda bi, g, i: (bi, 0, v_col // hp + g)),
            pl.BlockSpec((hp, 2, tq, 2 * tq), lambda bi, g, i: (g, 0, 0, 0)),
            _resident(qg2.shape), _resident(kg2.shape), _resident(lam_p.shape),
            _resident(subln_g.shape),
        ],
        out_specs=pl.BlockSpec((None, tq, hp * DIFF_V_DIM),
                               lambda bi, g, i: (bi, jnp.maximum(i - 1, 0), g)),
        out_shape=jax.ShapeDtypeStruct((b, s, DIFF_HEADS * DIFF_V_DIM), BF16),
        scratch_shapes=[
            pltpu.VMEM((hp, s, hw), BF16),
            pltpu.VMEM((hp, nt, DIFF_V_DIM, tq), BF16),
            pltpu.VMEM((hp, 2 * tq, hw), BF16),
            pltpu.VMEM((2, hp, s, 2 * tq), F32),
            pltpu.VMEM((2, hp, 1, 2 * tq), F32),
        ],
        compiler_params=_params(("parallel", "parallel", "arbitrary")),
        name="diff_attn",
    )(z3, z3, z3, bias_near, qg2, kg2, lam_p, subln_g)


def _merge_kernel(h_ref, yabc_ref, yd_ref, g_ref, wg_ref, wb_ref, wo_ref, o_ref, *, widths):
    h = h_ref[...]
    d = h.shape[1]
    xn = _rms(h, g_ref[...]).astype(BF16)
    merged = None
    off = 0
    yoff = 0
    for bi, w in enumerate(widths):
        gate = jax.nn.sigmoid(jnp.dot(xn, wg_ref[:, bi * d:(bi + 1) * d],
                                      preferred_element_type=F32))
        if bi < len(widths) - 1:
            y = yabc_ref[:, yoff:yoff + w]
            yoff += w
        else:
            y = yd_ref[...]
        proj = jnp.dot(y, wb_ref[off:off + w, :], preferred_element_type=F32)
        off += w
        merged = gate * proj if merged is None else merged + gate * proj
    o_ref[...] = h + jnp.dot(merged.astype(BF16), wo_ref[...], preferred_element_type=F32)


def _merge(h, y_abc, y_d, g, wg, wb, wo, tm, widths):
    t, d = h.shape
    return pl.pallas_call(
        functools.partial(_merge_kernel, widths=widths),
        grid=(t // tm,),
        in_specs=[
            pl.BlockSpec((tm, d), lambda i: (i, 0)),
            pl.BlockSpec((tm, y_abc.shape[1]), lambda i: (i, 0)),
            pl.BlockSpec((tm, y_d.shape[1]), lambda i: (i, 0)),
            _resident(g.shape), _resident(wg.shape), _resident(wb.shape), _resident(wo.shape),
        ],
        out_specs=pl.BlockSpec((tm, d), lambda i: (i, 0)),
        out_shape=jax.ShapeDtypeStruct((t, d), F32),
        compiler_params=_params(("parallel",)),
        name="merge",
    )(h, y_abc, y_d, g, wg, wb, wo)


def _swiglu_acc(xn, wgu_ref, wd_ref, d_ff):
    acc = None
    for c in range(d_ff // FF_CHUNK):
        lo = c * FF_CHUNK
        g = jnp.dot(xn, wgu_ref[:, lo:lo + FF_CHUNK], preferred_element_type=F32)
        u = jnp.dot(xn, wgu_ref[:, d_ff + lo:d_ff + lo + FF_CHUNK], preferred_element_type=F32)
        act = (g * jax.nn.sigmoid(g) * u).astype(BF16)
        part = jnp.dot(act, wd_ref[lo:lo + FF_CHUNK, :], preferred_element_type=F32)
        acc = part if acc is None else acc + part
    return acc


def _ffn_kernel(h_ref, g_ref, wgu_ref, wd_ref, o_ref):
    h = h_ref[...]
    xn = _rms(h, g_ref[...]).astype(BF16)
    o_ref[...] = h + _swiglu_acc(xn, wgu_ref, wd_ref, wd_ref.shape[0])


def _ffn(h, g, wgu, wd, tm):
    t, d = h.shape
    return pl.pallas_call(
        _ffn_kernel,
        grid=(t // tm,),
        in_specs=[pl.BlockSpec((tm, d), lambda i: (i, 0)), _resident(g.shape),
                  _resident(wgu.shape), _resident(wd.shape)],
        out_specs=pl.BlockSpec((tm, d), lambda i: (i, 0)),
        out_shape=jax.ShapeDtypeStruct((t, d), F32),
        compiler_params=_params(("parallel",)),
        name="ffn",
    )(h, g, wgu, wd)


def _router_kernel(h_ref, g_ref, rw_ref, o_ref, tot_ref, carry_ref):
    tm = h_ref.shape[0]

    @pl.when(pl.program_id(0) == 0)
    def _():
        carry_ref[...] = jnp.zeros(carry_ref.shape, F32)

    hn = _rms(h_ref[...], g_ref[...])
    logits = jnp.dot(hn, rw_ref[...], preferred_element_type=F32, precision=lax.Precision.HIGHEST)
    lane = lax.broadcasted_iota(jnp.int32, (tm, LANES), 1)
    logits = jnp.where(lane < N_EXPERTS, logits, NEG_BIG)
    v1 = jnp.max(logits, axis=-1, keepdims=True)
    i1 = jnp.min(jnp.where(logits == v1, lane, LANES), axis=-1, keepdims=True)
    rest = jnp.where(lane == i1, NEG_BIG, logits)
    v2 = jnp.max(rest, axis=-1, keepdims=True)
    i2 = jnp.min(jnp.where(rest == v2, lane, LANES), axis=-1, keepdims=True)
    e = jnp.exp(v2 - v1)
    w1 = 1.0 / (1.0 + e)
    w2 = e / (1.0 + e)
    cnt = jnp.where((lane == i1) | (lane == i2), 1.0, 0.0)
    r = lax.broadcasted_iota(jnp.int32, (tm, tm), 0)
    c = lax.broadcasted_iota(jnp.int32, (tm, tm), 1)
    tri = jnp.where(c < r, 1.0, 0.0).astype(BF16)
    excl = jnp.dot(tri, cnt.astype(BF16), preferred_element_type=F32) + carry_ref[...]
    rank1 = jnp.sum(jnp.where(lane == i1, excl, 0.0), axis=-1, keepdims=True)
    rank2 = jnp.sum(jnp.where(lane == i2, excl, 0.0), axis=-1, keepdims=True)
    carry_ref[...] = carry_ref[...] + jnp.sum(cnt, axis=0, keepdims=True)
    tot_ref[...] = carry_ref[...]
    packed = jnp.where(lane == 0, i1.astype(F32), jnp.where(lane == 1, i2.astype(F32),
             jnp.where(lane == 2, w1, jnp.where(lane == 3, w2,
             jnp.where(lane == 4, rank1, jnp.where(lane == 5, rank2, 0.0))))))
    o_ref[...] = packed


def _router(h, g, rw_pad, tm):
    t, d = h.shape
    return pl.pallas_call(
        _router_kernel,
        grid=(t // tm,),
        in_specs=[pl.BlockSpec((tm, d), lambda i: (i, 0)), _resident(g.shape),
                  _resident(rw_pad.shape)],
        out_specs=[pl.BlockSpec((tm, LANES), lambda i: (i, 0)),
                   pl.BlockSpec((1, LANES), lambda i: (0, 0))],
        out_shape=[jax.ShapeDtypeStruct((t, LANES), F32), jax.ShapeDtypeStruct((1, LANES), F32)],
        scratch_shapes=[pltpu.VMEM((1, LANES), F32)],
        compiler_params=_params(("arbitrary",)),
        name="router",
    )(h, g, rw_pad)


def _row_copy(src, dst, s, d, sem):
    return pltpu.make_async_copy(src.at[pl.ds(s, 1)], dst.at[pl.ds(d, 1)], sem)


def _dispatch_kernel(dest_ref, h_ref, xs_in_ref, xs_ref, sem):
    del xs_in_ref
    ch = h_ref.shape[0]

    def issue(t, c):
        _row_copy(h_ref, xs_ref, t, dest_ref[0, t], sem).start()
        _row_copy(h_ref, xs_ref, t, dest_ref[0, ch + t], sem).start()
        return c

    lax.fori_loop(0, ch, issue, 0, unroll=8)

    def drain(t, c):
        _row_copy(h_ref, xs_ref, t, dest_ref[0, t], sem).wait()
        _row_copy(h_ref, xs_ref, t, dest_ref[0, ch + t], sem).wait()
        return c

    lax.fori_loop(0, ch, drain, 0, unroll=8)


def _dispatch(dest, h, xs_zero):
    nc, _, ch2 = dest.shape
    ch = ch2 // 2
    d = h.shape[1]
    return pl.pallas_call(
        _dispatch_kernel,
        grid=(nc,),
        in_specs=[pl.BlockSpec((None, 1, ch2), lambda c: (c, 0, 0), memory_space=pltpu.SMEM),
                  pl.BlockSpec((ch, d), lambda c: (c, 0)),
                  pl.BlockSpec(memory_space=pl.ANY)],
        out_specs=pl.BlockSpec(memory_space=pl.ANY),
        out_shape=jax.ShapeDtypeStruct(xs_zero.shape, xs_zero.dtype),
        scratch_shapes=[pltpu.SemaphoreType.DMA(())],
        input_output_aliases={2: 0},
        compiler_params=pltpu.CompilerParams(dimension_semantics=("arbitrary",),
                                             has_side_effects=True),
        name="moe_dispatch",
    )(dest, h, xs_zero)


def _expert_kernel(te_ref, nu_ref, x_ref, g_ref, wgu_ref, wd_ref, y_ref):
    del te_ref

    @pl.when(pl.program_id(0) < nu_ref[0])
    def _():
        xn = _rms(x_ref[...], g_ref[...]).astype(BF16)
        y_ref[...] = _swiglu_acc(xn, wgu_ref, wd_ref, wd_ref.shape[0])

    @pl.when(pl.program_id(0) >= nu_ref[0])
    def _():
        y_ref[...] = jnp.zeros(y_ref.shape, y_ref.dtype)


def _experts(tile_expert, n_used, xs, g, wgu, wd, tm):
    r, d = xs.shape
    d_ff = wd.shape[1]
    grid_spec = pltpu.PrefetchScalarGridSpec(
        num_scalar_prefetch=2,
        grid=(r // tm,),
        in_specs=[
            pl.BlockSpec((tm, d), lambda i, te, nu: (i, 0)),
            pl.BlockSpec(g.shape, lambda i, te, nu: (0, 0), pipeline_mode=pl.Buffered(1)),
            pl.BlockSpec((None, d, 2 * d_ff), lambda i, te, nu: (te[i], 0, 0),
                         pipeline_mode=pl.Buffered(1)),
            pl.BlockSpec((None, d_ff, d), lambda i, te, nu: (te[i], 0, 0),
                         pipeline_mode=pl.Buffered(1)),
        ],
        out_specs=pl.BlockSpec((tm, d), lambda i, te, nu: (i, 0)),
    )
    return pl.pallas_call(
        _expert_kernel,
        grid_spec=grid_spec,
        out_shape=jax.ShapeDtypeStruct((r, d), F32),
        compiler_params=_params(("arbitrary",)),
        name="moe_experts",
    )(tile_expert, n_used, xs, g, wgu, wd)


def _combine_kernel(dest_ref, h_ref, pk_ref, y_ref, o_ref, buf_ref, sem):
    tm = h_ref.shape[0]

    def issue(t, c):
        _row_copy(y_ref, buf_ref.at[0], dest_ref[0, t], t, sem).start()
        _row_copy(y_ref, buf_ref.at[1], dest_ref[0, tm + t], t, sem).start()
        return c

    lax.fori_loop(0, tm, issue, 0, unroll=8)

    def drain(t, c):
        _row_copy(y_ref, buf_ref.at[0], dest_ref[0, t], t, sem).wait()
        _row_copy(y_ref, buf_ref.at[1], dest_ref[0, tm + t], t, sem).wait()
        return c

    lax.fori_loop(0, tm, drain, 0, unroll=8)
    pk = pk_ref[...]
    o_ref[...] = h_ref[...] + pk[:, 2:3] * buf_ref[0] + pk[:, 3:4] * buf_ref[1]


def _combine(dest, h, packed, y, tm):
    t, d = h.shape
    return pl.pallas_call(
        _combine_kernel,
        grid=(t // tm,),
        in_specs=[pl.BlockSpec((None, 1, 2 * tm), lambda i: (i, 0, 0), memory_space=pltpu.SMEM),
                  pl.BlockSpec((tm, d), lambda i: (i, 0)),
                  pl.BlockSpec((tm, LANES), lambda i: (i, 0)),
                  pl.BlockSpec(memory_space=pl.ANY)],
        out_specs=pl.BlockSpec((tm, d), lambda i: (i, 0)),
        out_shape=jax.ShapeDtypeStruct((t, d), F32),
        scratch_shapes=[pltpu.VMEM((2, tm, d), F32), pltpu.SemaphoreType.DMA(())],
        compiler_params=_params(("arbitrary",)),
        name="moe_combine",
    )(dest, h, packed, y)


def _rel_bucket(rel):
    nb = REL_BUCKETS // 2
    max_exact = nb // 2
    n = jnp.abs(rel)
    nf = jnp.maximum(n, 1).astype(F32)
    large = max_exact + (jnp.log(nf / max_exact) / math.log(REL_MAX_DIST / max_exact)
                         * (nb - max_exact)).astype(jnp.int32)
    large = jnp.minimum(large, nb - 1)
    return jnp.where(rel > 0, nb, 0) + jnp.where(n < max_exact, n, large)


def _near_bias(rel_bias):
    t = ATT_T
    qp = jnp.arange(t)[:, None]
    kp = jnp.arange(t)[None, :]

    def lookup(rel):
        onehot = jax.nn.one_hot(_rel_bucket(rel), REL_BUCKETS, dtype=F32)
        return jnp.einsum('...b,bm->...m', onehot, rel_bias, precision=lax.Precision.HIGHEST)

    far = lookup(jnp.full((), -(2 * t), jnp.int32))
    prev = (lookup(kp - t - qp) - far) * LOG2E
    diag = (lookup(kp - qp) - far) * LOG2E
    diag = jnp.where(((kp // CHUNK) <= (qp // CHUNK))[:, :, None], diag, NEG_BIG)
    both = jnp.stack([prev, diag], axis=0).reshape(2, t, t, DIFF_HEADS, 2)
    return both.transpose(3, 0, 2, 4, 1).reshape(DIFF_HEADS, 2, t, 2 * t).astype(F32)


def kernel(x, rel_bias, norm1_g, w_in, pool_w, pool_scale, conv_w, sgu_ln_g, sgu_w, sgu_b, q_norm_g, k_norm_g, diff_lambda, subln_g, w_branch_pool, w_branch_conv, w_branch_sgu, w_branch_attn, w_out, norm2_g, ffn_w_gate_up, ffn_w_down, router_w, moe_w_gate_up, moe_w_down):
    b, s, d = x.shape
    t = b * s
    depth = w_in.shape[0]
    pw = pool_scale.shape[1]
    cw = conv_w.shape[2]
    sw = sgu_ln_g.shape[1]
    aw = w_branch_attn.shape[1]
    mix_cols = pw + 3 * cw + 2 * sw + 3 * aw
    nb = REL_BUCKETS // 2
    assert nb // 2 + int(math.log((ATT_T + 1) / (nb // 2)) / math.log(REL_MAX_DIST / (nb // 2))
                         * (nb - nb // 2)) >= nb - 1
    q_col = (pw + 3 * cw + 2 * sw) // LANES
    k_col = q_col + aw // LANES
    v_col = k_col + aw // LANES
    tm = min(512, t)
    ts = min(512, s)

    bias_near = _near_bias(rel_bias)
    tri = jnp.tril(jnp.ones((SGU_SEG, SGU_SEG), bool))
    gd = pw // POOL_GROUPS

    h = x.reshape(t, d)
    for layer in range(depth):
        lam_init = 0.8 - 0.6 * math.exp(-0.3 * layer)
        w_mix = w_in[layer, :, :mix_cols].astype(BF16)
        w_gate = w_in[layer, :, mix_cols:].astype(BF16)
        poolw_bd = jnp.zeros((pw, pw), F32)
        for g in range(POOL_GROUPS):
            poolw_bd = poolw_bd.at[g * gd:(g + 1) * gd, g * gd:(g + 1) * gd].set(pool_w[layer, g])
        sguw_cat = jnp.where(tri[None], sgu_w[layer], 0.0).transpose(1, 0, 2).reshape(
            SGU_SEG, SGU_GROUPS * SGU_SEG).astype(BF16)
        sgub_full = jnp.repeat(sgu_b[layer].T, sw // SGU_GROUPS, axis=1)
        wb = jnp.concatenate([w_branch_pool[layer], w_branch_conv[layer], w_branch_sgu[layer],
                              w_branch_attn[layer]], axis=0).astype(BF16)

        z = _in_proj(h, norm1_g[layer][None], w_mix, tm)
        z3 = z.reshape(b, s, mix_cols)
        y_abc = _local_mix(z3, poolw_bd.astype(BF16), pool_scale[layer][None], conv_w[layer],
                           sgu_ln_g[layer][None], sguw_cat, sgub_full, ts, pw, cw, sw)
        y_d = _diff_attn(z3, bias_near, jnp.tile(q_norm_g[layer], 2)[None],
                         jnp.tile(k_norm_g[layer], 2)[None], diff_lambda[layer],
                         subln_g[layer][None], lam_init, q_col, k_col, v_col)
        h = _merge(h, y_abc.reshape(t, -1), y_d.reshape(t, -1), norm1_g[layer][None], w_gate, wb,
                   w_out[layer].astype(BF16), tm, (pw, cw, sw, aw))

        g2 = norm2_g[layer][None]
        if layer % 2 == 0:
            h = _ffn(h, g2, ffn_w_gate_up[layer // 2].astype(BF16),
                     ffn_w_down[layer // 2].astype(BF16), tm)
        else:
            li = layer // 2
            rw_pad = jnp.zeros((d, LANES), F32).at[:, :N_EXPERTS].set(router_w[li])
            packed, totals = _router(h, g2, rw_pad, tm)
            n_e = totals[0, :N_EXPERTS].astype(jnp.int32)
            n_pad = ((n_e + tm - 1) // tm) * tm
            ends = jnp.cumsum(n_pad)
            starts = ends - n_pad
            e1 = packed[:, 0].astype(jnp.int32)
            e2 = packed[:, 1].astype(jnp.int32)
            eids = jnp.arange(N_EXPERTS)[None, :]
            dest1 = (jnp.sum(jnp.where(e1[:, None] == eids, starts[None, :], 0), axis=1)
                     + packed[:, 4].astype(jnp.int32))
            dest2 = (jnp.sum(jnp.where(e2[:, None] == eids, starts[None, :], 0), axis=1)
                     + packed[:, 5].astype(jnp.int32))
            rows = 2 * t + N_EXPERTS * tm
            n_tiles = rows // tm
            tile_expert = jnp.minimum(
                jnp.sum((jnp.arange(n_tiles)[:, None] * tm) >= ends[None, :], axis=1),
                N_EXPERTS - 1).astype(jnp.int32)
            n_used = (ends[-1] // tm).astype(jnp.int32)[None]
            ch = min(512, t)
            dest_d = jnp.concatenate([dest1.reshape(t // ch, 1, ch), dest2.reshape(t // ch, 1, ch)],
                                     axis=2)
            xs = _dispatch(dest_d, h, jnp.zeros((rows, d), F32))
            y = _experts(tile_expert, n_used, xs, g2, moe_w_gate_up[li].astype(BF16),
                         moe_w_down[li].astype(BF16), tm)
            tc = min(256, t)
            dest_c = jnp.concatenate([dest1.reshape(t // tc, 1, tc), dest2.reshape(t // tc, 1, tc)],
                                     axis=2)
            h = _combine(dest_c, h, packed, y, tc)
    return h.reshape(b, s, d)
```

```python
import functools
import math

import jax
import jax.numpy as jnp
import numpy as np
from jax import lax
from jax.experimental import pallas as pl
from jax.experimental.pallas import tpu as pltpu

F32 = jnp.float32
BF16 = jnp.bfloat16

NORM_EPS = 1e-6
CHUNK = 64
POOL_WINDOWS = (2, 4, 8, 16)
POOL_GROUPS = 4
CONV_K = 3
SGU_GROUPS = 4
SGU_SEG = 128
DIFF_HEADS = 4
DIFF_QK_DIM = 64
DIFF_V_DIM = 128
REL_BUCKETS = 32
REL_MAX_DIST = 128
N_EXPERTS = 8
LANES = 128
V7X_VMEM_BYTES = 64 * 1024 * 1024
VMEM_LIMIT = V7X_VMEM_BYTES - 8 * 1024 * 1024
NEG_BIG = -1e30
LOG2E = math.log2(math.e)

HALO = 16
ATT_T = 256
ATT_HEADS_PER_STEP = 2
FF_CHUNK = 256


def _rms(x, g):
    return x * lax.rsqrt(jnp.mean(x * x, axis=-1, keepdims=True) + NORM_EPS) * g


def _resident(shape):
    nd = len(shape)
    return pl.BlockSpec(shape, lambda *_: (0,) * nd, pipeline_mode=pl.Buffered(1))


def _params(sem):
    return pltpu.CompilerParams(dimension_semantics=sem, vmem_limit_bytes=VMEM_LIMIT)


def _in_proj_kernel(x_ref, g_ref, w_ref, o_ref, *, n_chunk):
    xn = _rms(x_ref[...], g_ref[...]).astype(BF16)
    n = o_ref.shape[1]
    for j in range(n // n_chunk):
        sl = slice(j * n_chunk, (j + 1) * n_chunk)
        o_ref[:, sl] = jnp.dot(xn, w_ref[:, sl], preferred_element_type=F32).astype(o_ref.dtype)


def _in_proj(h, g, w, tm):
    t, d = h.shape
    n = w.shape[1]
    return pl.pallas_call(
        functools.partial(_in_proj_kernel, n_chunk=512),
        grid=(t // tm,),
        in_specs=[pl.BlockSpec((tm, d), lambda i: (i, 0)), _resident((1, d)), _resident((d, n))],
        out_specs=pl.BlockSpec((tm, n), lambda i: (i, 0)),
        out_shape=jax.ShapeDtypeStruct((t, n), BF16),
        compiler_params=_params(("parallel",)),
        name="in_proj",
    )(h, g, w)


def _local_mix_kernel(z_ref, halo_ref, poolw_ref, pscale_ref, convw_ref, lng_ref, sguw_ref,
                      sgub_ref, o_ref, *, pw, cw):
    ts = z_ref.shape[0]
    i = pl.program_id(1)
    z = z_ref[...].astype(F32)
    halo = halo_ref[...].astype(F32)
    halo = jnp.where(i > 0, halo, 0.0)
    ext = jnp.concatenate([halo[:, :pw + 3 * cw], z[:, :pw + 3 * cw]], axis=0)
    rows = ext.shape[0]

    def back(x, k):
        return pltpu.roll(x, k, axis=0)

    a = ext[:, :pw]
    s2 = a + back(a, 1)
    s4 = s2 + back(s2, 2)
    s8 = s4 + back(s4, 4)
    s16 = s8 + back(s8, 8)
    lane = lax.broadcasted_iota(jnp.int32, (rows, pw), 1)
    grp = lane // (pw // POOL_GROUPS)
    win_sum = jnp.where(grp == 0, s2, jnp.where(grp == 1, s4, jnp.where(grp == 2, s8, s16)))
    win = jnp.where(grp == 0, 2, jnp.where(grp == 1, 4, jnp.where(grp == 2, 8, 16)))
    pos = i * ts - HALO + lax.broadcasted_iota(jnp.int32, (rows, pw), 0)
    count = jnp.minimum(pos + 1, win).astype(F32)
    pooled = (win_sum / jnp.maximum(count, 1.0) - a)[HALO:]
    y_a = jnp.dot(pooled.astype(BF16), poolw_ref[...], preferred_element_type=F32) * pscale_ref[...]
    o_ref[:, 0:pw] = y_a.astype(o_ref.dtype)

    b_gate = z[:, pw:pw + cw]
    zc = ext[:, pw + cw:pw + 2 * cw] * ext[:, pw + 2 * cw:pw + 3 * cw]
    conv = (convw_ref[0:1, :] * back(zc, 2) + convw_ref[1:2, :] * back(zc, 1)
            + convw_ref[2:3, :] * zc)[HALO:]
    o_ref[:, pw:pw + cw] = (b_gate * conv).astype(o_ref.dtype)

    sw = (z.shape[1] - pw - 3 * cw) // 2
    zc_uv = z[:, pw + 3 * cw:]
    uv = 0.5 * zc_uv * (1.0 + lax.erf(zc_uv * math.sqrt(0.5)))
    u = uv[:, :sw]
    v = uv[:, sw:]
    mu = jnp.mean(v, axis=-1, keepdims=True)
    var = jnp.mean(jnp.square(v - mu), axis=-1, keepdims=True)
    vn = (v - mu) * lax.rsqrt(var + NORM_EPS) * lng_ref[...]
    glane = lax.broadcasted_iota(jnp.int32, (SGU_SEG, sw), 1) // (sw // SGU_GROUPS)
    wcat = sguw_ref[...]
    bias = sgub_ref[...]
    for n in range(ts // SGU_SEG):
        seg = vn[n * SGU_SEG:(n + 1) * SGU_SEG]
        rhs = jnp.concatenate(
            [jnp.where(glane == g, seg, 0.0) for g in range(SGU_GROUPS)], axis=0).astype(BF16)
        s = jnp.dot(wcat, rhs, preferred_element_type=F32) + bias
        o_ref[n * SGU_SEG:(n + 1) * SGU_SEG, pw + cw:pw + cw + sw] = (
            u[n * SGU_SEG:(n + 1) * SGU_SEG] * s).astype(o_ref.dtype)


def _local_mix(z3, poolw_bd, pscale, convw, lng, sguw_cat, sgub_full, ts, pw, cw, sw):
    b, s, _ = z3.shape
    cols = pw + 3 * cw + 2 * sw
    hb = ts // HALO
    return pl.pallas_call(
        functools.partial(_local_mix_kernel, pw=pw, cw=cw),
        grid=(b, s // ts),
        in_specs=[
            pl.BlockSpec((None, ts, cols), lambda bi, i: (bi, i, 0)),
            pl.BlockSpec((None, HALO, cols), lambda bi, i: (bi, jnp.maximum(i * hb - 1, 0), 0)),
            _resident(poolw_bd.shape), _resident(pscale.shape), _resident(convw.shape),
            _resident(lng.shape), _resident(sguw_cat.shape), _resident(sgub_full.shape),
        ],
        out_specs=pl.BlockSpec((None, ts, pw + cw + sw), lambda bi, i: (bi, i, 0)),
        out_shape=jax.ShapeDtypeStruct((b, s, pw + cw + sw), BF16),
        compiler_params=_params(("parallel", "parallel")),
        name="local_mix",
    )(z3, z3, poolw_bd, pscale, convw, lng, sguw_cat, sgub_full)


def _diff_attn_kernel(q_ref, k_ref, v_ref, bias_ref, qg_ref, kg_ref, lam_ref, sg_ref, o_ref,
                      kn_ref, vt_ref, qs_ref, st_ref, m_ref, *, lam_init):
    tq = q_ref.shape[0]
    hp = kn_ref.shape[0]
    nt = kn_ref.shape[1] // tq
    hw = 2 * DIFF_QK_DIM
    i = pl.program_id(2)
    half = lax.broadcasted_iota(jnp.int32, (1, 2 * DIFF_QK_DIM), 1) < DIFF_QK_DIM

    def qk_norm(x, g):
        sq = x * x
        ss0 = jnp.sum(jnp.where(half, sq, 0.0), axis=-1, keepdims=True)
        ss1 = jnp.sum(jnp.where(half, 0.0, sq), axis=-1, keepdims=True)
        r0 = lax.rsqrt(ss0 * (1.0 / DIFF_QK_DIM) + NORM_EPS)
        r1 = lax.rsqrt(ss1 * (1.0 / DIFF_QK_DIM) + NORM_EPS)
        return x * jnp.where(half, r0, r1) * g

    def key_rows(j):
        if isinstance(j, int):
            return slice(j * tq, (j + 1) * tq)
        return pl.ds(pl.multiple_of(j * tq, tq), tq)

    @pl.when(i == 0)
    def _():
        for h in range(hp):
            cols = slice(h * hw, (h + 1) * hw)
            for j in range(nt):
                rows = key_rows(j)
                kn_ref[h, rows, :] = qk_norm(k_ref[rows, cols].astype(F32), kg_ref[...]).astype(BF16)
                vt_ref[h, j] = v_ref[rows, cols].astype(F32).T.astype(BF16)

    def step(c):
        cur, prv = c % 2, 1 - c % 2
        scoring = c < nt
        if scoring:
            for h in range(hp):
                qn = (qk_norm(q_ref[:, h * hw:(h + 1) * hw].astype(F32), qg_ref[...])
                      * (DIFF_QK_DIM ** -0.5 * LOG2E))
                qs_ref[h, 0:tq, :] = jnp.where(half, qn, 0.0).astype(BF16)
                qs_ref[h, tq:2 * tq, :] = jnp.where(half, 0.0, qn).astype(BF16)
        m_new = [jnp.full((1, 2 * tq), NEG_BIG, F32) for _ in range(hp)]
        m_old = [m_ref[prv, h] for h in range(hp)] if c >= 1 else None
        l = [jnp.zeros((1, 2 * tq), F32) for _ in range(hp)]
        acc = [jnp.zeros((DIFF_V_DIM, 2 * tq), F32) for _ in range(hp)]
        for j in range(c + 1):
            rows = key_rows(j)
            for h in range(hp):
                if scoring:
                    st = lax.dot_general(kn_ref[h, rows, :], qs_ref[h], (((1,), (1,)), ((), ())),
                                         preferred_element_type=F32)
                    if j >= c - 1:
                        st = st + bias_ref[h, j - (c - 1)]
                    st_ref[cur, h, rows, :] = st
                    m_new[h] = jnp.maximum(m_new[h], jnp.max(st, axis=0, keepdims=True))
                if j < c:
                    p = jnp.exp2(st_ref[prv, h, rows, :] - m_old[h])
                    l[h] = l[h] + jnp.sum(p, axis=0, keepdims=True)
                    acc[h] = acc[h] + jnp.dot(vt_ref[h, j], p.astype(BF16),
                                              preferred_element_type=F32)
        if scoring:
            for h in range(hp):
                m_ref[cur, h] = m_new[h]
        if c >= 1:
            lp = lam_ref[...]
            lam = (jnp.exp(jnp.sum(lp[0:1] * lp[1:2], axis=-1, keepdims=True))
                   - jnp.exp(jnp.sum(lp[2:3] * lp[3:4], axis=-1, keepdims=True)) + lam_init)
            for h in range(hp):
                o = (acc[h][:, :tq] * (1.0 / l[h][:, :tq])
                     - acc[h][:, tq:] * (lam / l[h][:, tq:]))
                o = o * lax.rsqrt(jnp.mean(o * o, axis=0, keepdims=True) + NORM_EPS)
                o_ref[:, h * DIFF_V_DIM:(h + 1) * DIFF_V_DIM] = (
                    o.T * (sg_ref[...] * (1.0 - lam_init))).astype(o_ref.dtype)

    for c in range(nt + 1):
        pl.when(i == c)(functools.partial(step, c))


def _diff_attn(z3, bias_near, qg2, kg2, lam_p, subln_g, lam_init, q_col, k_col, v_col):
    b, s, _ = z3.shape
    tq = ATT_T
    hp = ATT_HEADS_PER_STEP
    nt = s // tq
    hw = 2 * DIFF_QK_DIM
    return pl.pallas_call(
        functools.partial(_diff_attn_kernel, lam_init=lam_init),
        grid=(b, DIFF_HEADS // hp, nt + 1),
        in_specs=[
            pl.BlockSpec((None, tq, hp * hw),
                         lambda bi, g, i: (bi, jnp.minimum(i, nt - 1), q_col // hp + g)),
            pl.BlockSpec((None, s, hp * hw), lambda bi, g, i: (bi, 0, k_col // hp + g)),
            pl.BlockSpec((None, s, hp * DIFF_V_DIM), lambda bi, g, i: (bi, 0, v_col // hp + g)),
            pl.BlockSpec((hp, 2, tq, 2 * tq), lambda bi, g, i: (g, 0, 0, 0)),
            _resident(qg2.shape), _resident(kg2.shape), _resident(lam_p.shape),
            _resident(subln_g.shape),
        ],
        out_specs=pl.BlockSpec((None, tq, hp * DIFF_V_DIM),
                               lambda bi, g, i: (bi, jnp.maximum(i - 1, 0), g)),
        out_shape=jax.ShapeDtypeStruct((b, s, DIFF_HEADS * DIFF_V_DIM), BF16),
        scratch_shapes=[
            pltpu.VMEM((hp, s, hw), BF16),
            pltpu.VMEM((hp, nt, DIFF_V_DIM, tq), BF16),
            pltpu.VMEM((hp, 2 * tq, hw), BF16),
            pltpu.VMEM((2, hp, s, 2 * tq), F32),
            pltpu.VMEM((2, hp, 1, 2 * tq), F32),
        ],
        compiler_params=_params(("parallel", "parallel", "arbitrary")),
        name="diff_attn",
    )(z3, z3, z3, bias_near, qg2, kg2, lam_p, subln_g)


def _merge_kernel(h_ref, yabc_ref, yd_ref, g_ref, wg_ref, wb_ref, wo_ref, o_ref, *, widths):
    h = h_ref[...]
    d = h.shape[1]
    xn = _rms(h, g_ref[...]).astype(BF16)
    merged = None
    off = 0
    yoff = 0
    for bi, w in enumerate(widths):
        gate = jax.nn.sigmoid(jnp.dot(xn, wg_ref[:, bi * d:(bi + 1) * d],
                                      preferred_element_type=F32))
        if bi < len(widths) - 1:
            y = yabc_ref[:, yoff:yoff + w]
            yoff += w
        else:
            y = yd_ref[...]
        proj = jnp.dot(y, wb_ref[off:off + w, :], preferred_element_type=F32)
        off += w
        merged = gate * proj if merged is None else merged + gate * proj
    o_ref[...] = h + jnp.dot(merged.astype(BF16), wo_ref[...], preferred_element_type=F32)


def _merge(h, y_abc, y_d, g, wg, wb, wo, tm, widths):
    t, d = h.shape
    return pl.pallas_call(
        functools.partial(_merge_kernel, widths=widths),
        grid=(t // tm,),
        in_specs=[
            pl.BlockSpec((tm, d), lambda i: (i, 0)),
            pl.BlockSpec((tm, y_abc.shape[1]), lambda i: (i, 0)),
            pl.BlockSpec((tm, y_d.shape[1]), lambda i: (i, 0)),
            _resident(g.shape), _resident(wg.shape), _resident(wb.shape), _resident(wo.shape),
        ],
        out_specs=pl.BlockSpec((tm, d), lambda i: (i, 0)),
        out_shape=jax.ShapeDtypeStruct((t, d), F32),
        compiler_params=_params(("parallel",)),
        name="merge",
    )(h, y_abc, y_d, g, wg, wb, wo)


def _swiglu_acc(xn, wgu_ref, wd_ref, d_ff, between=None):
    acc = None
    n_chunks = d_ff // FF_CHUNK
    for c in range(n_chunks):
        lo = c * FF_CHUNK
        g = jnp.dot(xn, wgu_ref[:, lo:lo + FF_CHUNK], preferred_element_type=F32)
        u = jnp.dot(xn, wgu_ref[:, d_ff + lo:d_ff + lo + FF_CHUNK], preferred_element_type=F32)
        act = (g * jax.nn.sigmoid(g) * u).astype(BF16)
        part = jnp.dot(act, wd_ref[lo:lo + FF_CHUNK, :], preferred_element_type=F32)
        acc = part if acc is None else acc + part
        if between is not None:
            between(c, n_chunks)
    return acc


def _ffn_kernel(h_ref, g_ref, wgu_ref, wd_ref, o_ref):
    h = h_ref[...]
    xn = _rms(h, g_ref[...]).astype(BF16)
    o_ref[...] = h + _swiglu_acc(xn, wgu_ref, wd_ref, wd_ref.shape[0])


def _ffn(h, g, wgu, wd, tm):
    t, d = h.shape
    return pl.pallas_call(
        _ffn_kernel,
        grid=(t // tm,),
        in_specs=[pl.BlockSpec((tm, d), lambda i: (i, 0)), _resident(g.shape),
                  _resident(wgu.shape), _resident(wd.shape)],
        out_specs=pl.BlockSpec((tm, d), lambda i: (i, 0)),
        out_shape=jax.ShapeDtypeStruct((t, d), F32),
        compiler_params=_params(("parallel",)),
        name="ffn",
    )(h, g, wgu, wd)


def _router_kernel(h_ref, g_ref, rw_ref, o_ref, tot_ref, carry_ref):
    tm = h_ref.shape[0]

    @pl.when(pl.program_id(0) == 0)
    def _():
        carry_ref[...] = jnp.zeros(carry_ref.shape, F32)

    hn = _rms(h_ref[...], g_ref[...])
    logits = jnp.dot(hn, rw_ref[...], preferred_element_type=F32, precision=lax.Precision.HIGHEST)
    lane = lax.broadcasted_iota(jnp.int32, (tm, LANES), 1)
    logits = jnp.where(lane < N_EXPERTS, logits, NEG_BIG)
    v1 = jnp.max(logits, axis=-1, keepdims=True)
    i1 = jnp.min(jnp.where(logits == v1, lane, LANES), axis=-1, keepdims=True)
    rest = jnp.where(lane == i1, NEG_BIG, logits)
    v2 = jnp.max(rest, axis=-1, keepdims=True)
    i2 = jnp.min(jnp.where(rest == v2, lane, LANES), axis=-1, keepdims=True)
    e = jnp.exp(v2 - v1)
    w1 = 1.0 / (1.0 + e)
    w2 = e / (1.0 + e)
    cnt = jnp.where((lane == i1) | (lane == i2), 1.0, 0.0)
    r = lax.broadcasted_iota(jnp.int32, (tm, tm), 0)
    c = lax.broadcasted_iota(jnp.int32, (tm, tm), 1)
    tri = jnp.where(c < r, 1.0, 0.0).astype(BF16)
    excl = jnp.dot(tri, cnt.astype(BF16), preferred_element_type=F32) + carry_ref[...]
    rank1 = jnp.sum(jnp.where(lane == i1, excl, 0.0), axis=-1, keepdims=True)
    rank2 = jnp.sum(jnp.where(lane == i2, excl, 0.0), axis=-1, keepdims=True)
    carry_ref[...] = carry_ref[...] + jnp.sum(cnt, axis=0, keepdims=True)
    tot_ref[...] = carry_ref[...]
    packed = jnp.where(lane == 0, i1.astype(F32), jnp.where(lane == 1, i2.astype(F32),
             jnp.where(lane == 2, w1, jnp.where(lane == 3, w2,
             jnp.where(lane == 4, rank1, jnp.where(lane == 5, rank2, 0.0))))))
    o_ref[...] = packed


def _router(h, g, rw_pad, tm):
    t, d = h.shape
    return pl.pallas_call(
        _router_kernel,
        grid=(t // tm,),
        in_specs=[pl.BlockSpec((tm, d), lambda i: (i, 0)), _resident(g.shape),
                  _resident(rw_pad.shape)],
        out_specs=[pl.BlockSpec((tm, LANES), lambda i: (i, 0)),
                   pl.BlockSpec((1, LANES), lambda i: (0, 0))],
        out_shape=[jax.ShapeDtypeStruct((t, LANES), F32), jax.ShapeDtypeStruct((1, LANES), F32)],
        scratch_shapes=[pltpu.VMEM((1, LANES), F32)],
        compiler_params=_params(("arbitrary",)),
        name="router",
    )(h, g, rw_pad)


def _invert_kernel(dest_ref, src_ref, *, n_tok):
    c = pl.program_id(0)
    ch = dest_ref.shape[1] // 2

    @pl.when(c == 0)
    def _():
        def clear(r, x):
            src_ref[r] = -1
            return x

        lax.fori_loop(0, src_ref.shape[0], clear, 0, unroll=8)

    base = c * ch

    def place(t, x):
        src_ref[dest_ref[0, t]] = base + t
        src_ref[dest_ref[0, ch + t]] = n_tok + base + t
        return x

    lax.fori_loop(0, ch, place, 0, unroll=8)


def _invert(dest, rows, n_tok):
    nc, _, ch2 = dest.shape
    return pl.pallas_call(
        functools.partial(_invert_kernel, n_tok=n_tok),
        grid=(nc,),
        in_specs=[pl.BlockSpec((None, 1, ch2), lambda c: (c, 0, 0), memory_space=pltpu.SMEM)],
        out_specs=pl.BlockSpec(memory_space=pltpu.SMEM),
        out_shape=jax.ShapeDtypeStruct((rows,), jnp.int32),
        compiler_params=pltpu.CompilerParams(dimension_semantics=("arbitrary",)),
        name="moe_invert",
    )(dest)


def _expert_kernel(te_ref, src_cur, src_nxt, orow_prv, orow_cur, h_hbm, g_ref, wgu_ref, wd_ref,
                   yt_hbm, xbuf, ybuf, gsem, ssem):
    del te_ref
    i = pl.program_id(0)
    last = pl.num_programs(0) - 1
    tm = xbuf.shape[1]
    s = lax.rem(i, 2)
    o = 1 - s

    def gather(idx_ref, r, slot):
        return pltpu.make_async_copy(h_hbm.at[pl.ds(idx_ref[0, r], 1)],
                                     xbuf.at[slot, pl.ds(r, 1)], gsem.at[slot])

    def scatter(idx_ref, r, slot):
        return pltpu.make_async_copy(ybuf.at[slot, pl.ds(r, 1)],
                                     yt_hbm.at[pl.ds(idx_ref[0, r], 1)], ssem.at[slot])

    def for_rows(fn):
        def body(r, x):
            fn(r)
            return x

        lax.fori_loop(0, tm, body, 0, unroll=8)

    @pl.when(i == 0)
    def _():
        ybuf[1] = jnp.zeros(ybuf.shape[1:], ybuf.dtype)
        for_rows(lambda r: gather(src_cur, r, 0).start())

    for_rows(lambda r: gather(src_cur, r, s).wait())
    xn = _rms(xbuf[s], g_ref[...]).astype(BF16)

    def between(c, n_chunks):
        per = -(-tm // n_chunks)
        for r in range(c * per, min((c + 1) * per, tm)):
            gather(src_nxt, r, o).start()
            scatter(orow_prv, r, o).start()

    ybuf[s] = _swiglu_acc(xn, wgu_ref, wd_ref, wd_ref.shape[0], between)
    for_rows(lambda r: scatter(orow_prv, r, o).wait())

    @pl.when(i == last)
    def _():
        for_rows(lambda r: gather(src_nxt, r, o).wait())
        for_rows(lambda r: scatter(orow_cur, r, s).start())
        for_rows(lambda r: scatter(orow_cur, r, s).wait())


def _experts(tile_expert, src_tok, out_row, h, g, wgu, wd, yt_rows):
    n, _, tm = src_tok.shape
    d = h.shape[1]
    d_ff = wd.shape[1]
    smem = functools.partial(pl.BlockSpec, (None, 1, tm), memory_space=pltpu.SMEM)
    grid_spec = pltpu.PrefetchScalarGridSpec(
        num_scalar_prefetch=1,
        grid=(n,),
        in_specs=[
            smem(lambda i, te: (i, 0, 0)),
            smem(lambda i, te: (jnp.minimum(i + 1, n - 1), 0, 0)),
            smem(lambda i, te: (i, 0, 0)),
            smem(lambda i, te: (i + 1, 0, 0)),
            pl.BlockSpec(memory_space=pl.ANY),
            pl.BlockSpec(g.shape, lambda i, te: (0, 0), pipeline_mode=pl.Buffered(1)),
            pl.BlockSpec((None, d, 2 * d_ff), lambda i, te: (te[i], 0, 0),
                         pipeline_mode=pl.Buffered(1)),
            pl.BlockSpec((None, d_ff, d), lambda i, te: (te[i], 0, 0),
                         pipeline_mode=pl.Buffered(1)),
        ],
        out_specs=pl.BlockSpec(memory_space=pl.ANY),
        scratch_shapes=[pltpu.VMEM((2, tm, d), F32), pltpu.VMEM((2, tm, d), F32),
                        pltpu.SemaphoreType.DMA((2,)), pltpu.SemaphoreType.DMA((2,))],
    )
    return pl.pallas_call(
        _expert_kernel,
        grid_spec=grid_spec,
        out_shape=jax.ShapeDtypeStruct((yt_rows, d), F32),
        compiler_params=_params(("arbitrary",)),
        name="moe_experts",
    )(tile_expert, src_tok, src_tok, out_row, out_row, h, g, wgu, wd)


def _combine_kernel(h_ref, pk_ref, y1_ref, y2_ref, o_ref):
    pk = pk_ref[...]
    o_ref[...] = h_ref[...] + pk[:, 2:3] * y1_ref[...] + pk[:, 3:4] * y2_ref[...]


def _combine(h, packed, yt, tm):
    t, d = h.shape
    nb = t // tm
    return pl.pallas_call(
        _combine_kernel,
        grid=(nb,),
        in_specs=[pl.BlockSpec((tm, d), lambda i: (i, 0)),
                  pl.BlockSpec((tm, LANES), lambda i: (i, 0)),
                  pl.BlockSpec((tm, d), lambda i: (i, 0)),
                  pl.BlockSpec((tm, d), lambda i: (nb + i, 0))],
        out_specs=pl.BlockSpec((tm, d), lambda i: (i, 0)),
        out_shape=jax.ShapeDtypeStruct((t, d), F32),
        compiler_params=_params(("parallel",)),
        name="moe_combine",
    )(h, packed, yt, yt)


def _rel_bucket(rel):
    nb = REL_BUCKETS // 2
    max_exact = nb // 2
    n = jnp.abs(rel)
    nf = jnp.maximum(n, 1).astype(F32)
    large = max_exact + (jnp.log(nf / max_exact) / math.log(REL_MAX_DIST / max_exact)
                         * (nb - max_exact)).astype(jnp.int32)
    large = jnp.minimum(large, nb - 1)
    return jnp.where(rel > 0, nb, 0) + jnp.where(n < max_exact, n, large)


def _near_bias(rel_bias):
    t = ATT_T
    qp = jnp.arange(t)[:, None]
    kp = jnp.arange(t)[None, :]

    def lookup(rel):
        onehot = jax.nn.one_hot(_rel_bucket(rel), REL_BUCKETS, dtype=F32)
        return jnp.einsum('...b,bm->...m', onehot, rel_bias, precision=lax.Precision.HIGHEST)

    far = lookup(jnp.full((), -(2 * t), jnp.int32))
    prev = (lookup(kp - t - qp) - far) * LOG2E
    diag = (lookup(kp - qp) - far) * LOG2E
    diag = jnp.where(((kp // CHUNK) <= (qp // CHUNK))[:, :, None], diag, NEG_BIG)
    both = jnp.stack([prev, diag], axis=0).reshape(2, t, t, DIFF_HEADS, 2)
    return both.transpose(3, 0, 2, 4, 1).reshape(DIFF_HEADS, 2, t, 2 * t).astype(F32)


def kernel(x, rel_bias, norm1_g, w_in, pool_w, pool_scale, conv_w, sgu_ln_g, sgu_w, sgu_b, q_norm_g, k_norm_g, diff_lambda, subln_g, w_branch_pool, w_branch_conv, w_branch_sgu, w_branch_attn, w_out, norm2_g, ffn_w_gate_up, ffn_w_down, router_w, moe_w_gate_up, moe_w_down):
    b, s, d = x.shape
    t = b * s
    depth = w_in.shape[0]
    pw = pool_scale.shape[1]
    cw = conv_w.shape[2]
    sw = sgu_ln_g.shape[1]
    aw = w_branch_attn.shape[1]
    mix_cols = pw + 3 * cw + 2 * sw + 3 * aw
    nb = REL_BUCKETS // 2
    assert nb // 2 + int(math.log((ATT_T + 1) / (nb // 2)) / math.log(REL_MAX_DIST / (nb // 2))
                         * (nb - nb // 2)) >= nb - 1
    q_col = (pw + 3 * cw + 2 * sw) // LANES
    k_col = q_col + aw // LANES
    v_col = k_col + aw // LANES
    tm = min(512, t)
    ts = min(512, s)

    bias_near = _near_bias(rel_bias)
    tri = jnp.tril(jnp.ones((SGU_SEG, SGU_SEG), bool))
    gd = pw // POOL_GROUPS

    h = x.reshape(t, d)
    for layer in range(depth):
        lam_init = 0.8 - 0.6 * math.exp(-0.3 * layer)
        w_mix = w_in[layer, :, :mix_cols].astype(BF16)
        w_gate = w_in[layer, :, mix_cols:].astype(BF16)
        poolw_bd = jnp.zeros((pw, pw), F32)
        for g in range(POOL_GROUPS):
            poolw_bd = poolw_bd.at[g * gd:(g + 1) * gd, g * gd:(g + 1) * gd].set(pool_w[layer, g])
        sguw_cat = jnp.where(tri[None], sgu_w[layer], 0.0).transpose(1, 0, 2).reshape(
            SGU_SEG, SGU_GROUPS * SGU_SEG).astype(BF16)
        sgub_full = jnp.repeat(sgu_b[layer].T, sw // SGU_GROUPS, axis=1)
        wb = jnp.concatenate([w_branch_pool[layer], w_branch_conv[layer], w_branch_sgu[layer],
                              w_branch_attn[layer]], axis=0).astype(BF16)

        z = _in_proj(h, norm1_g[layer][None], w_mix, tm)
        z3 = z.reshape(b, s, mix_cols)
        y_abc = _local_mix(z3, poolw_bd.astype(BF16), pool_scale[layer][None], conv_w[layer],
                           sgu_ln_g[layer][None], sguw_cat, sgub_full, ts, pw, cw, sw)
        y_d = _diff_attn(z3, bias_near, jnp.tile(q_norm_g[layer], 2)[None],
                         jnp.tile(k_norm_g[layer], 2)[None], diff_lambda[layer],
                         subln_g[layer][None], lam_init, q_col, k_col, v_col)
        h = _merge(h, y_abc.reshape(t, -1), y_d.reshape(t, -1), norm1_g[layer][None], w_gate, wb,
                   w_out[layer].astype(BF16), tm, (pw, cw, sw, aw))

        g2 = norm2_g[layer][None]
        if layer % 2 == 0:
            h = _ffn(h, g2, ffn_w_gate_up[layer // 2].astype(BF16),
                     ffn_w_down[layer // 2].astype(BF16), tm)
        else:
            li = layer // 2
            rw_pad = jnp.zeros((d, LANES), F32).at[:, :N_EXPERTS].set(router_w[li])
            packed, totals = _router(h, g2, rw_pad, tm)
            n_e = totals[0, :N_EXPERTS].astype(jnp.int32)
            n_pad = ((n_e + tm - 1) // tm) * tm
            ends = jnp.cumsum(n_pad)
            starts = ends - n_pad
            e1 = packed[:, 0].astype(jnp.int32)
            e2 = packed[:, 1].astype(jnp.int32)
            eids = jnp.arange(N_EXPERTS)[None, :]
            dest1 = (jnp.sum(jnp.where(e1[:, None] == eids, starts[None, :], 0), axis=1)
                     + packed[:, 4].astype(jnp.int32))
            dest2 = (jnp.sum(jnp.where(e2[:, None] == eids, starts[None, :], 0), axis=1)
                     + packed[:, 5].astype(jnp.int32))
            rows = 2 * t + N_EXPERTS * tm
            n_tiles = rows // tm
            tile_expert = jnp.minimum(
                jnp.sum((jnp.arange(n_tiles)[:, None] * tm) >= ends[None, :], axis=1),
                N_EXPERTS - 1).astype(jnp.int32)
            ch = min(2048, t)
            dest_d = jnp.concatenate([dest1.reshape(t // ch, 1, ch), dest2.reshape(t // ch, 1, ch)],
                                     axis=2)
            src = _invert(dest_d, rows, t)
            is_pad = src < 0
            src_tok = jnp.where(is_pad, 0, jnp.where(src >= t, src - t, src))
            pad_rank = jnp.cumsum(is_pad.astype(jnp.int32)) - 1
            out_row = jnp.where(is_pad, 2 * t + tm + pad_rank, src)
            spare = 2 * t + jnp.arange(tm, dtype=jnp.int32)
            yt = _experts(tile_expert, src_tok.reshape(n_tiles, 1, tm),
                          jnp.concatenate([spare, out_row]).reshape(n_tiles + 1, 1, tm), h, g2,
                          moe_w_gate_up[li].astype(BF16), moe_w_down[li].astype(BF16),
                          rows + tm)
            h = _combine(h, packed, yt, tm)
    return h.reshape(b, s, d)
```

```python
import functools
import math

import jax
import jax.numpy as jnp
import numpy as np
from jax import lax
from jax.experimental import pallas as pl
from jax.experimental.pallas import tpu as pltpu

F32 = jnp.float32
BF16 = jnp.bfloat16

NORM_EPS = 1e-6
CHUNK = 64
POOL_WINDOWS = (2, 4, 8, 16)
POOL_GROUPS = 4
CONV_K = 3
SGU_GROUPS = 4
SGU_SEG = 128
DIFF_HEADS = 4
DIFF_QK_DIM = 64
DIFF_V_DIM = 128
REL_BUCKETS = 32
REL_MAX_DIST = 128
N_EXPERTS = 8
LANES = 128
V7X_VMEM_BYTES = 64 * 1024 * 1024
VMEM_LIMIT = V7X_VMEM_BYTES - 8 * 1024 * 1024
NEG_BIG = -1e30
LOG2E = math.log2(math.e)

HALO = 16
ATT_T = 256
ATT_HEADS_PER_STEP = 2
FF_CHUNK = 256


def _rms(x, g):
    return x * lax.rsqrt(jnp.mean(x * x, axis=-1, keepdims=True) + NORM_EPS) * g


def _resident(shape):
    nd = len(shape)
    return pl.BlockSpec(shape, lambda *_: (0,) * nd, pipeline_mode=pl.Buffered(1))


def _params(sem):
    return pltpu.CompilerParams(dimension_semantics=sem, vmem_limit_bytes=VMEM_LIMIT)


def _in_proj_kernel(x_ref, g_ref, w_ref, o_ref, *, n_chunk):
    xn = _rms(x_ref[...], g_ref[...]).astype(BF16)
    n = o_ref.shape[1]
    for j in range(n // n_chunk):
        sl = slice(j * n_chunk, (j + 1) * n_chunk)
        o_ref[:, sl] = jnp.dot(xn, w_ref[:, sl], preferred_element_type=F32).astype(o_ref.dtype)


def _in_proj(h, g, w, tm):
    t, d = h.shape
    n = w.shape[1]
    return pl.pallas_call(
        functools.partial(_in_proj_kernel, n_chunk=512),
        grid=(t // tm,),
        in_specs=[pl.BlockSpec((tm, d), lambda i: (i, 0)), _resident((1, d)), _resident((d, n))],
        out_specs=pl.BlockSpec((tm, n), lambda i: (i, 0)),
        out_shape=jax.ShapeDtypeStruct((t, n), BF16),
        compiler_params=_params(("parallel",)),
        name="in_proj",
    )(h, g, w)


def _local_mix_kernel(z_ref, halo_ref, poolw_ref, pscale_ref, convw_ref, lng_ref, sguw_ref,
                      sgub_ref, o_ref, *, pw, cw):
    ts = z_ref.shape[0]
    i = pl.program_id(1)
    z = z_ref[...].astype(F32)
    halo = halo_ref[...].astype(F32)
    halo = jnp.where(i > 0, halo, 0.0)
    ext = jnp.concatenate([halo[:, :pw + 3 * cw], z[:, :pw + 3 * cw]], axis=0)
    rows = ext.shape[0]

    def back(x, k):
        return pltpu.roll(x, k, axis=0)

    a = ext[:, :pw]
    s2 = a + back(a, 1)
    s4 = s2 + back(s2, 2)
    s8 = s4 + back(s4, 4)
    s16 = s8 + back(s8, 8)
    lane = lax.broadcasted_iota(jnp.int32, (rows, pw), 1)
    grp = lane // (pw // POOL_GROUPS)
    win_sum = jnp.where(grp == 0, s2, jnp.where(grp == 1, s4, jnp.where(grp == 2, s8, s16)))
    win = jnp.where(grp == 0, 2, jnp.where(grp == 1, 4, jnp.where(grp == 2, 8, 16)))
    pos = i * ts - HALO + lax.broadcasted_iota(jnp.int32, (rows, pw), 0)
    count = jnp.minimum(pos + 1, win).astype(F32)
    pooled = (win_sum / jnp.maximum(count, 1.0) - a)[HALO:]
    y_a = jnp.dot(pooled.astype(BF16), poolw_ref[...], preferred_element_type=F32) * pscale_ref[...]
    o_ref[:, 0:pw] = y_a.astype(o_ref.dtype)

    b_gate = z[:, pw:pw + cw]
    zc = ext[:, pw + cw:pw + 2 * cw] * ext[:, pw + 2 * cw:pw + 3 * cw]
    conv = (convw_ref[0:1, :] * back(zc, 2) + convw_ref[1:2, :] * back(zc, 1)
            + convw_ref[2:3, :] * zc)[HALO:]
    o_ref[:, pw:pw + cw] = (b_gate * conv).astype(o_ref.dtype)

    sw = (z.shape[1] - pw - 3 * cw) // 2
    zc_uv = z[:, pw + 3 * cw:]
    uv = 0.5 * zc_uv * (1.0 + lax.erf(zc_uv * math.sqrt(0.5)))
    u = uv[:, :sw]
    v = uv[:, sw:]
    mu = jnp.mean(v, axis=-1, keepdims=True)
    var = jnp.mean(jnp.square(v - mu), axis=-1, keepdims=True)
    vn = (v - mu) * lax.rsqrt(var + NORM_EPS) * lng_ref[...]
    glane = lax.broadcasted_iota(jnp.int32, (SGU_SEG, sw), 1) // (sw // SGU_GROUPS)
    wcat = sguw_ref[...]
    bias = sgub_ref[...]
    for n in range(ts // SGU_SEG):
        seg = vn[n * SGU_SEG:(n + 1) * SGU_SEG]
        rhs = jnp.concatenate(
            [jnp.where(glane == g, seg, 0.0) for g in range(SGU_GROUPS)], axis=0).astype(BF16)
        s = jnp.dot(wcat, rhs, preferred_element_type=F32) + bias
        o_ref[n * SGU_SEG:(n + 1) * SGU_SEG, pw + cw:pw + cw + sw] = (
            u[n * SGU_SEG:(n + 1) * SGU_SEG] * s).astype(o_ref.dtype)


def _local_mix(z3, poolw_bd, pscale, convw, lng, sguw_cat, sgub_full, ts, pw, cw, sw):
    b, s, _ = z3.shape
    cols = pw + 3 * cw + 2 * sw
    hb = ts // HALO
    return pl.pallas_call(
        functools.partial(_local_mix_kernel, pw=pw, cw=cw),
        grid=(b, s // ts),
        in_specs=[
            pl.BlockSpec((None, ts, cols), lambda bi, i: (bi, i, 0)),
            pl.BlockSpec((None, HALO, cols), lambda bi, i: (bi, jnp.maximum(i * hb - 1, 0), 0)),
            _resident(poolw_bd.shape), _resident(pscale.shape), _resident(convw.shape),
            _resident(lng.shape), _resident(sguw_cat.shape), _resident(sgub_full.shape),
        ],
        out_specs=pl.BlockSpec((None, ts, pw + cw + sw), lambda bi, i: (bi, i, 0)),
        out_shape=jax.ShapeDtypeStruct((b, s, pw + cw + sw), BF16),
        compiler_params=_params(("parallel", "parallel")),
        name="local_mix",
    )(z3, z3, poolw_bd, pscale, convw, lng, sguw_cat, sgub_full)


def _diff_attn_kernel(q_ref, k_ref, v_ref, bias_ref, qg_ref, kg_ref, lam_ref, sg_ref, o_ref,
                      kn_ref, vt_ref, qs_ref, st_ref, m_ref, *, lam_init):
    tq = q_ref.shape[0]
    hp = kn_ref.shape[0]
    nt = kn_ref.shape[1] // tq
    hw = 2 * DIFF_QK_DIM
    i = pl.program_id(2)
    half = lax.broadcasted_iota(jnp.int32, (1, 2 * DIFF_QK_DIM), 1) < DIFF_QK_DIM

    def qk_norm(x, g):
        sq = x * x
        ss0 = jnp.sum(jnp.where(half, sq, 0.0), axis=-1, keepdims=True)
        ss1 = jnp.sum(jnp.where(half, 0.0, sq), axis=-1, keepdims=True)
        r0 = lax.rsqrt(ss0 * (1.0 / DIFF_QK_DIM) + NORM_EPS)
        r1 = lax.rsqrt(ss1 * (1.0 / DIFF_QK_DIM) + NORM_EPS)
        return x * jnp.where(half, r0, r1) * g

    def key_rows(j):
        if isinstance(j, int):
            return slice(j * tq, (j + 1) * tq)
        return pl.ds(pl.multiple_of(j * tq, tq), tq)

    @pl.when(i == 0)
    def _():
        for h in range(hp):
            cols = slice(h * hw, (h + 1) * hw)
            for j in range(nt):
                rows = key_rows(j)
                kn_ref[h, rows, :] = qk_norm(k_ref[rows, cols].astype(F32), kg_ref[...]).astype(BF16)
                vt_ref[h, j] = v_ref[rows, cols].astype(F32).T.astype(BF16)

    def step(c):
        cur, prv = c % 2, 1 - c % 2
        scoring = c < nt
        if scoring:
            for h in range(hp):
                qn = (qk_norm(q_ref[:, h * hw:(h + 1) * hw].astype(F32), qg_ref[...])
                      * (DIFF_QK_DIM ** -0.5 * LOG2E))
                qs_ref[h, 0:tq, :] = jnp.where(half, qn, 0.0).astype(BF16)
                qs_ref[h, tq:2 * tq, :] = jnp.where(half, 0.0, qn).astype(BF16)
        m_new = [jnp.full((1, 2 * tq), NEG_BIG, F32) for _ in range(hp)]
        m_old = [m_ref[prv, h] for h in range(hp)] if c >= 1 else None
        l = [jnp.zeros((1, 2 * tq), F32) for _ in range(hp)]
        acc = [jnp.zeros((DIFF_V_DIM, 2 * tq), F32) for _ in range(hp)]
        for j in range(c + 1):
            rows = key_rows(j)
            for h in range(hp):
                if scoring:
                    st = lax.dot_general(kn_ref[h, rows, :], qs_ref[h], (((1,), (1,)), ((), ())),
                                         preferred_element_type=F32)
                    if j >= c - 1:
                        st = st + bias_ref[h, j - (c - 1)]
                    st_ref[cur, h, rows, :] = st
                    m_new[h] = jnp.maximum(m_new[h], jnp.max(st, axis=0, keepdims=True))
                if j < c:
                    p = jnp.exp2(st_ref[prv, h, rows, :] - m_old[h])
                    l[h] = l[h] + jnp.sum(p, axis=0, keepdims=True)
                    acc[h] = acc[h] + jnp.dot(vt_ref[h, j], p.astype(BF16),
                                              preferred_element_type=F32)
        if scoring:
            for h in range(hp):
                m_ref[cur, h] = m_new[h]
        if c >= 1:
            lp = lam_ref[...]
            lam = (jnp.exp(jnp.sum(lp[0:1] * lp[1:2], axis=-1, keepdims=True))
                   - jnp.exp(jnp.sum(lp[2:3] * lp[3:4], axis=-1, keepdims=True)) + lam_init)
            for h in range(hp):
                o = (acc[h][:, :tq] * (1.0 / l[h][:, :tq])
                     - acc[h][:, tq:] * (lam / l[h][:, tq:]))
                o = o * lax.rsqrt(jnp.mean(o * o, axis=0, keepdims=True) + NORM_EPS)
                o_ref[:, h * DIFF_V_DIM:(h + 1) * DIFF_V_DIM] = (
                    o.T * (sg_ref[...] * (1.0 - lam_init))).astype(o_ref.dtype)

    for c in range(nt + 1):
        pl.when(i == c)(functools.partial(step, c))


def _diff_attn(z3, bias_near, qg2, kg2, lam_p, subln_g, lam_init, q_col, k_col, v_col):
    b, s, _ = z3.shape
    tq = ATT_T
    hp = ATT_HEADS_PER_STEP
    nt = s // tq
    hw = 2 * DIFF_QK_DIM
    return pl.pallas_call(
        functools.partial(_diff_attn_kernel, lam_init=lam_init),
        grid=(b, DIFF_HEADS // hp, nt + 1),
        in_specs=[
            pl.BlockSpec((None, tq, hp * hw),
                         lambda bi, g, i: (bi, jnp.minimum(i, nt - 1), q_col // hp + g)),
            pl.BlockSpec((None, s, hp * hw), lambda bi, g, i: (bi, 0, k_col // hp + g)),
            pl.BlockSpec((None, s, hp * DIFF_V_DIM), lambda bi, g, i: (bi, 0, v_col // hp + g)),
            pl.BlockSpec((hp, 2, tq, 2 * tq), lambda bi, g, i: (g, 0, 0, 0)),
            _resident(qg2.shape), _resident(kg2.shape), _resident(lam_p.shape),
            _resident(subln_g.shape),
        ],
        out_specs=pl.BlockSpec((None, tq, hp * DIFF_V_DIM),
                               lambda bi, g, i: (bi, jnp.maximum(i - 1, 0), g)),
        out_shape=jax.ShapeDtypeStruct((b, s, DIFF_HEADS * DIFF_V_DIM), BF16),
        scratch_shapes=[
            pltpu.VMEM((hp, s, hw), BF16),
            pltpu.VMEM((hp, nt, DIFF_V_DIM, tq), BF16),
            pltpu.VMEM((hp, 2 * tq, hw), BF16),
            pltpu.VMEM((2, hp, s, 2 * tq), F32),
            pltpu.VMEM((2, hp, 1, 2 * tq), F32),
        ],
        compiler_params=_params(("parallel", "parallel", "arbitrary")),
        name="diff_attn",
    )(z3, z3, z3, bias_near, qg2, kg2, lam_p, subln_g)


def _merge_kernel(h_ref, yabc_ref, yd_ref, g_ref, wg_ref, wb_ref, wo_ref, o_ref, *, widths):
    h = h_ref[...]
    d = h.shape[1]
    xn = _rms(h, g_ref[...]).astype(BF16)
    merged = None
    off = 0
    yoff = 0
    for bi, w in enumerate(widths):
        gate = jax.nn.sigmoid(jnp.dot(xn, wg_ref[:, bi * d:(bi + 1) * d],
                                      preferred_element_type=F32))
        if bi < len(widths) - 1:
            y = yabc_ref[:, yoff:yoff + w]
            yoff += w
        else:
            y = yd_ref[...]
        proj = jnp.dot(y, wb_ref[off:off + w, :], preferred_element_type=F32)
        off += w
        merged = gate * proj if merged is None else merged + gate * proj
    o_ref[...] = h + jnp.dot(merged.astype(BF16), wo_ref[...], preferred_element_type=F32)


def _merge(h, y_abc, y_d, g, wg, wb, wo, tm, widths):
    t, d = h.shape
    return pl.pallas_call(
        functools.partial(_merge_kernel, widths=widths),
        grid=(t // tm,),
        in_specs=[
            pl.BlockSpec((tm, d), lambda i: (i, 0)),
            pl.BlockSpec((tm, y_abc.shape[1]), lambda i: (i, 0)),
            pl.BlockSpec((tm, y_d.shape[1]), lambda i: (i, 0)),
            _resident(g.shape), _resident(wg.shape), _resident(wb.shape), _resident(wo.shape),
        ],
        out_specs=pl.BlockSpec((tm, d), lambda i: (i, 0)),
        out_shape=jax.ShapeDtypeStruct((t, d), F32),
        compiler_params=_params(("parallel",)),
        name="merge",
    )(h, y_abc, y_d, g, wg, wb, wo)


def _swiglu_acc(xn, wgu_ref, wd_ref, d_ff, between=None):
    acc = None
    n_chunks = d_ff // FF_CHUNK
    for c in range(n_chunks):
        lo = c * FF_CHUNK
        g = jnp.dot(xn, wgu_ref[:, lo:lo + FF_CHUNK], preferred_element_type=F32)
        u = jnp.dot(xn, wgu_ref[:, d_ff + lo:d_ff + lo + FF_CHUNK], preferred_element_type=F32)
        act = (g * jax.nn.sigmoid(g) * u).astype(BF16)
        part = jnp.dot(act, wd_ref[lo:lo + FF_CHUNK, :], preferred_element_type=F32)
        acc = part if acc is None else acc + part
        if between is not None:
            between(c, n_chunks)
    return acc


def _ffn_kernel(h_ref, g_ref, wgu_ref, wd_ref, o_ref):
    h = h_ref[...]
    xn = _rms(h, g_ref[...]).astype(BF16)
    o_ref[...] = h + _swiglu_acc(xn, wgu_ref, wd_ref, wd_ref.shape[0])


def _ffn(h, g, wgu, wd, tm):
    t, d = h.shape
    return pl.pallas_call(
        _ffn_kernel,
        grid=(t // tm,),
        in_specs=[pl.BlockSpec((tm, d), lambda i: (i, 0)), _resident(g.shape),
                  _resident(wgu.shape), _resident(wd.shape)],
        out_specs=pl.BlockSpec((tm, d), lambda i: (i, 0)),
        out_shape=jax.ShapeDtypeStruct((t, d), F32),
        compiler_params=_params(("parallel",)),
        name="ffn",
    )(h, g, wgu, wd)


def _router_kernel(h_ref, g_ref, rw_ref, o_ref, tot_ref, carry_ref):
    tm = h_ref.shape[0]

    @pl.when(pl.program_id(0) == 0)
    def _():
        carry_ref[...] = jnp.zeros(carry_ref.shape, F32)

    hn = _rms(h_ref[...], g_ref[...])
    logits = jnp.dot(hn, rw_ref[...], preferred_element_type=F32, precision=lax.Precision.HIGHEST)
    lane = lax.broadcasted_iota(jnp.int32, (tm, LANES), 1)
    logits = jnp.where(lane < N_EXPERTS, logits, NEG_BIG)
    v1 = jnp.max(logits, axis=-1, keepdims=True)
    i1 = jnp.min(jnp.where(logits == v1, lane, LANES), axis=-1, keepdims=True)
    rest = jnp.where(lane == i1, NEG_BIG, logits)
    v2 = jnp.max(rest, axis=-1, keepdims=True)
    i2 = jnp.min(jnp.where(rest == v2, lane, LANES), axis=-1, keepdims=True)
    e = jnp.exp(v2 - v1)
    w1 = 1.0 / (1.0 + e)
    w2 = e / (1.0 + e)
    cnt = jnp.where((lane == i1) | (lane == i2), 1.0, 0.0)
    r = lax.broadcasted_iota(jnp.int32, (tm, tm), 0)
    c = lax.broadcasted_iota(jnp.int32, (tm, tm), 1)
    tri = jnp.where(c < r, 1.0, 0.0).astype(BF16)
    excl = jnp.dot(tri, cnt.astype(BF16), preferred_element_type=F32) + carry_ref[...]
    rank1 = jnp.sum(jnp.where(lane == i1, excl, 0.0), axis=-1, keepdims=True)
    rank2 = jnp.sum(jnp.where(lane == i2, excl, 0.0), axis=-1, keepdims=True)
    carry_ref[...] = carry_ref[...] + jnp.sum(cnt, axis=0, keepdims=True)
    tot_ref[...] = carry_ref[...]
    packed = jnp.where(lane == 0, i1.astype(F32), jnp.where(lane == 1, i2.astype(F32),
             jnp.where(lane == 2, w1, jnp.where(lane == 3, w2,
             jnp.where(lane == 4, rank1, jnp.where(lane == 5, rank2, 0.0))))))
    o_ref[...] = packed


def _router(h, g, rw_pad, tm):
    t, d = h.shape
    return pl.pallas_call(
        _router_kernel,
        grid=(t // tm,),
        in_specs=[pl.BlockSpec((tm, d), lambda i: (i, 0)), _resident(g.shape),
                  _resident(rw_pad.shape)],
        out_specs=[pl.BlockSpec((tm, LANES), lambda i: (i, 0)),
                   pl.BlockSpec((1, LANES), lambda i: (0, 0))],
        out_shape=[jax.ShapeDtypeStruct((t, LANES), F32), jax.ShapeDtypeStruct((1, LANES), F32)],
        scratch_shapes=[pltpu.VMEM((1, LANES), F32)],
        compiler_params=_params(("arbitrary",)),
        name="router",
    )(h, g, rw_pad)


def _invert_kernel(dest_ref, src_ref, *, n_tok):
    c = pl.program_id(0)
    ch = dest_ref.shape[1] // 2

    @pl.when(c == 0)
    def _():
        def clear(r, x):
            src_ref[r] = -1
            return x

        lax.fori_loop(0, src_ref.shape[0], clear, 0, unroll=16)

    base = c * ch

    def place(t, x):
        src_ref[dest_ref[0, t]] = base + t
        src_ref[dest_ref[0, ch + t]] = n_tok + base + t
        return x

    lax.fori_loop(0, ch, place, 0, unroll=8)


def _invert(dest, rows, n_tok):
    nc, _, ch2 = dest.shape
    return pl.pallas_call(
        functools.partial(_invert_kernel, n_tok=n_tok),
        grid=(nc,),
        in_specs=[pl.BlockSpec((None, 1, ch2), lambda c: (c, 0, 0), memory_space=pltpu.SMEM)],
        out_specs=pl.BlockSpec(memory_space=pltpu.SMEM),
        out_shape=jax.ShapeDtypeStruct((rows,), jnp.int32),
        compiler_params=pltpu.CompilerParams(dimension_semantics=("arbitrary",)),
        name="moe_invert",
    )(dest)


def _expert_kernel(te_ref, src_cur, src_nxt, orow_prv, orow_cur, zero_ref, h_hbm, g_ref, wgu_ref,
                   wd_ref, yt_hbm, xbuf, ybuf, gsem, ssem):
    del te_ref
    i = pl.program_id(0)
    last = pl.num_programs(0) - 1
    tm = xbuf.shape[1]
    s = lax.rem(i, 2)
    o = 1 - s

    def gather(tok, r, slot):
        return pltpu.make_async_copy(h_hbm.at[pl.ds(tok, 1)],
                                     xbuf.at[slot, pl.ds(r, 1)], gsem.at[slot])

    def scatter(row, r, slot):
        return pltpu.make_async_copy(ybuf.at[slot, pl.ds(r, 1)],
                                     yt_hbm.at[pl.ds(row, 1)], ssem.at[slot])

    def for_rows(fn):
        def body(r, x):
            fn(r)
            return x

        lax.fori_loop(0, tm, body, 0, unroll=8)

    @pl.when(i == 0)
    def _():
        ybuf[1] = jnp.zeros(ybuf.shape[1:], ybuf.dtype)
        for_rows(lambda r: gather(src_cur[0, r], r, 0).start())

    for_rows(lambda r: gather(src_cur[0, r], r, s).wait())
    xn = _rms(xbuf[s], g_ref[...]).astype(BF16)

    pace = [src_nxt[0, 0]]

    def between(c, n_chunks):
        per = -(-tm // n_chunks)
        for r in range(c * per, min((c + 1) * per, tm)):
            tok = src_nxt[0, r] + zero_ref[zero_ref[jnp.minimum(pace[0], 0)]]
            gather(tok, r, o).start()
            row = orow_prv[0, r] + zero_ref[zero_ref[jnp.minimum(tok, 0)]]
            scatter(row, r, o).start(priority=1)
            pace[0] = row

    ybuf[s] = _swiglu_acc(xn, wgu_ref, wd_ref, wd_ref.shape[0], between)
    for_rows(lambda r: scatter(orow_prv[0, r], r, o).wait())

    @pl.when(i == last)
    def _():
        for_rows(lambda r: gather(src_nxt[0, r], r, o).wait())
        for_rows(lambda r: scatter(orow_cur[0, r], r, s).start())
        for_rows(lambda r: scatter(orow_cur[0, r], r, s).wait())


def _experts(tile_expert, src_tok, out_row, h, g, wgu, wd, yt_rows):
    n, _, tm = src_tok.shape
    d = h.shape[1]
    d_ff = wd.shape[1]
    smem = functools.partial(pl.BlockSpec, (None, 1, tm), memory_space=pltpu.SMEM)
    grid_spec = pltpu.PrefetchScalarGridSpec(
        num_scalar_prefetch=1,
        grid=(n,),
        in_specs=[
            smem(lambda i, te: (i, 0, 0)),
            smem(lambda i, te: (jnp.minimum(i + 1, n - 1), 0, 0)),
            smem(lambda i, te: (i, 0, 0)),
            smem(lambda i, te: (i + 1, 0, 0)),
            pl.BlockSpec(memory_space=pltpu.SMEM),
            pl.BlockSpec(memory_space=pl.ANY),
            pl.BlockSpec(g.shape, lambda i, te: (0, 0), pipeline_mode=pl.Buffered(1)),
            pl.BlockSpec((None, d, 2 * d_ff), lambda i, te: (te[i], 0, 0),
                         pipeline_mode=pl.Buffered(1)),
            pl.BlockSpec((None, d_ff, d), lambda i, te: (te[i], 0, 0),
                         pipeline_mode=pl.Buffered(1)),
        ],
        out_specs=pl.BlockSpec(memory_space=pl.ANY),
        scratch_shapes=[pltpu.VMEM((2, tm, d), F32), pltpu.VMEM((2, tm, d), F32),
                        pltpu.SemaphoreType.DMA((2,)), pltpu.SemaphoreType.DMA((2,))],
    )
    return pl.pallas_call(
        _expert_kernel,
        grid_spec=grid_spec,
        out_shape=jax.ShapeDtypeStruct((yt_rows, d), F32),
        compiler_params=_params(("arbitrary",)),
        name="moe_experts",
    )(tile_expert, src_tok, src_tok, out_row, out_row, jnp.zeros((8,), jnp.int32), h, g, wgu, wd)


def _combine_kernel(h_ref, pk_ref, y1_ref, y2_ref, o_ref):
    pk = pk_ref[...]
    o_ref[...] = h_ref[...] + pk[:, 2:3] * y1_ref[...] + pk[:, 3:4] * y2_ref[...]


def _combine(h, packed, yt, tm):
    t, d = h.shape
    nb = t // tm
    return pl.pallas_call(
        _combine_kernel,
        grid=(nb,),
        in_specs=[pl.BlockSpec((tm, d), lambda i: (i, 0)),
                  pl.BlockSpec((tm, LANES), lambda i: (i, 0)),
                  pl.BlockSpec((tm, d), lambda i: (i, 0)),
                  pl.BlockSpec((tm, d), lambda i: (nb + i, 0))],
        out_specs=pl.BlockSpec((tm, d), lambda i: (i, 0)),
        out_shape=jax.ShapeDtypeStruct((t, d), F32),
        compiler_params=_params(("parallel",)),
        name="moe_combine",
    )(h, packed, yt, yt)


def _rel_bucket(rel):
    nb = REL_BUCKETS // 2
    max_exact = nb // 2
    n = jnp.abs(rel)
    nf = jnp.maximum(n, 1).astype(F32)
    large = max_exact + (jnp.log(nf / max_exact) / math.log(REL_MAX_DIST / max_exact)
                         * (nb - max_exact)).astype(jnp.int32)
    large = jnp.minimum(large, nb - 1)
    return jnp.where(rel > 0, nb, 0) + jnp.where(n < max_exact, n, large)


def _near_bias(rel_bias):
    t = ATT_T
    qp = jnp.arange(t)[:, None]
    kp = jnp.arange(t)[None, :]

    def lookup(rel):
        onehot = jax.nn.one_hot(_rel_bucket(rel), REL_BUCKETS, dtype=F32)
        return jnp.einsum('...b,bm->...m', onehot, rel_bias, precision=lax.Precision.HIGHEST)

    far = lookup(jnp.full((), -(2 * t), jnp.int32))
    prev = (lookup(kp - t - qp) - far) * LOG2E
    diag = (lookup(kp - qp) - far) * LOG2E
    diag = jnp.where(((kp // CHUNK) <= (qp // CHUNK))[:, :, None], diag, NEG_BIG)
    both = jnp.stack([prev, diag], axis=0).reshape(2, t, t, DIFF_HEADS, 2)
    return both.transpose(3, 0, 2, 4, 1).reshape(DIFF_HEADS, 2, t, 2 * t).astype(F32)


def kernel(x, rel_bias, norm1_g, w_in, pool_w, pool_scale, conv_w, sgu_ln_g, sgu_w, sgu_b, q_norm_g, k_norm_g, diff_lambda, subln_g, w_branch_pool, w_branch_conv, w_branch_sgu, w_branch_attn, w_out, norm2_g, ffn_w_gate_up, ffn_w_down, router_w, moe_w_gate_up, moe_w_down):
    b, s, d = x.shape
    t = b * s
    depth = w_in.shape[0]
    pw = pool_scale.shape[1]
    cw = conv_w.shape[2]
    sw = sgu_ln_g.shape[1]
    aw = w_branch_attn.shape[1]
    mix_cols = pw + 3 * cw + 2 * sw + 3 * aw
    nb = REL_BUCKETS // 2
    assert nb // 2 + int(math.log((ATT_T + 1) / (nb // 2)) / math.log(REL_MAX_DIST / (nb // 2))
                         * (nb - nb // 2)) >= nb - 1
    q_col = (pw + 3 * cw + 2 * sw) // LANES
    k_col = q_col + aw // LANES
    v_col = k_col + aw // LANES
    tm = min(512, t)
    ts = min(512, s)

    bias_near = _near_bias(rel_bias)
    tri = jnp.tril(jnp.ones((SGU_SEG, SGU_SEG), bool))
    gd = pw // POOL_GROUPS

    h = x.reshape(t, d)
    for layer in range(depth):
        lam_init = 0.8 - 0.6 * math.exp(-0.3 * layer)
        w_mix = w_in[layer, :, :mix_cols].astype(BF16)
        w_gate = w_in[layer, :, mix_cols:].astype(BF16)
        poolw_bd = jnp.zeros((pw, pw), F32)
        for g in range(POOL_GROUPS):
            poolw_bd = poolw_bd.at[g * gd:(g + 1) * gd, g * gd:(g + 1) * gd].set(pool_w[layer, g])
        sguw_cat = jnp.where(tri[None], sgu_w[layer], 0.0).transpose(1, 0, 2).reshape(
            SGU_SEG, SGU_GROUPS * SGU_SEG).astype(BF16)
        sgub_full = jnp.repeat(sgu_b[layer].T, sw // SGU_GROUPS, axis=1)
        wb = jnp.concatenate([w_branch_pool[layer], w_branch_conv[layer], w_branch_sgu[layer],
                              w_branch_attn[layer]], axis=0).astype(BF16)

        z = _in_proj(h, norm1_g[layer][None], w_mix, tm)
        z3 = z.reshape(b, s, mix_cols)
        y_abc = _local_mix(z3, poolw_bd.astype(BF16), pool_scale[layer][None], conv_w[layer],
                           sgu_ln_g[layer][None], sguw_cat, sgub_full, ts, pw, cw, sw)
        y_d = _diff_attn(z3, bias_near, jnp.tile(q_norm_g[layer], 2)[None],
                         jnp.tile(k_norm_g[layer], 2)[None], diff_lambda[layer],
                         subln_g[layer][None], lam_init, q_col, k_col, v_col)
        h = _merge(h, y_abc.reshape(t, -1), y_d.reshape(t, -1), norm1_g[layer][None], w_gate, wb,
                   w_out[layer].astype(BF16), tm, (pw, cw, sw, aw))

        g2 = norm2_g[layer][None]
        if layer % 2 == 0:
            h = _ffn(h, g2, ffn_w_gate_up[layer // 2].astype(BF16),
                     ffn_w_down[layer // 2].astype(BF16), tm)
        else:
            li = layer // 2
            rw_pad = jnp.zeros((d, LANES), F32).at[:, :N_EXPERTS].set(router_w[li])
            packed, totals = _router(h, g2, rw_pad, tm)
            n_e = totals[0, :N_EXPERTS].astype(jnp.int32)
            n_pad = ((n_e + tm - 1) // tm) * tm
            ends = jnp.cumsum(n_pad)
            starts = ends - n_pad
            e1 = packed[:, 0].astype(jnp.int32)
            e2 = packed[:, 1].astype(jnp.int32)
            eids = jnp.arange(N_EXPERTS)[None, :]
            dest1 = (jnp.sum(jnp.where(e1[:, None] == eids, starts[None, :], 0), axis=1)
                     + packed[:, 4].astype(jnp.int32))
            dest2 = (jnp.sum(jnp.where(e2[:, None] == eids, starts[None, :], 0), axis=1)
                     + packed[:, 5].astype(jnp.int32))
            rows = 2 * t + N_EXPERTS * tm
            n_tiles = rows // tm
            tile_expert = jnp.minimum(
                jnp.sum((jnp.arange(n_tiles)[:, None] * tm) >= ends[None, :], axis=1),
                N_EXPERTS - 1).astype(jnp.int32)
            ch = min(2048, t)
            dest_d = jnp.concatenate([dest1.reshape(t // ch, 1, ch), dest2.reshape(t // ch, 1, ch)],
                                     axis=2)
            src = _invert(dest_d, rows, t)
            is_pad = src < 0
            src_tok = jnp.where(is_pad, 0, jnp.where(src >= t, src - t, src))
            pad_rank = jnp.cumsum(is_pad.astype(jnp.int32)) - 1
            out_row = jnp.where(is_pad, 2 * t + tm + pad_rank, src)
            spare = 2 * t + jnp.arange(tm, dtype=jnp.int32)
            yt = _experts(tile_expert, src_tok.reshape(n_tiles, 1, tm),
                          jnp.concatenate([spare, out_row]).reshape(n_tiles + 1, 1, tm), h, g2,
                          moe_w_gate_up[li].astype(BF16), moe_w_down[li].astype(BF16),
                          rows + tm)
            h = _combine(h, packed, yt, tm)
    return h.reshape(b, s, d)
```

```python
import functools
import math

import jax
import jax.numpy as jnp
import numpy as np
from jax import lax
from jax.experimental import pallas as pl
from jax.experimental.pallas import tpu as pltpu

F32 = jnp.float32
BF16 = jnp.bfloat16

NORM_EPS = 1e-6
CHUNK = 64
POOL_WINDOWS = (2, 4, 8, 16)
POOL_GROUPS = 4
CONV_K = 3
SGU_GROUPS = 4
SGU_SEG = 128
DIFF_HEADS = 4
DIFF_QK_DIM = 64
DIFF_V_DIM = 128
REL_BUCKETS = 32
REL_MAX_DIST = 128
N_EXPERTS = 8
LANES = 128
V7X_VMEM_BYTES = 64 * 1024 * 1024
VMEM_LIMIT = V7X_VMEM_BYTES - 8 * 1024 * 1024
NEG_BIG = -1e30
LOG2E = math.log2(math.e)

HALO = 16
ATT_T = 256
ATT_HEADS_PER_STEP = 2
FF_CHUNK = 256
PACE_LOADS = 4


def _rms(x, g):
    return x * lax.rsqrt(jnp.mean(x * x, axis=-1, keepdims=True) + NORM_EPS) * g


def _resident(shape):
    nd = len(shape)
    return pl.BlockSpec(shape, lambda *_: (0,) * nd, pipeline_mode=pl.Buffered(1))


def _params(sem):
    return pltpu.CompilerParams(dimension_semantics=sem, vmem_limit_bytes=VMEM_LIMIT)


def _in_proj_kernel(x_ref, g_ref, w_ref, o_ref, *, n_chunk):
    xn = _rms(x_ref[...], g_ref[...]).astype(BF16)
    n = o_ref.shape[1]
    for j in range(n // n_chunk):
        sl = slice(j * n_chunk, (j + 1) * n_chunk)
        o_ref[:, sl] = jnp.dot(xn, w_ref[:, sl], preferred_element_type=F32).astype(o_ref.dtype)


def _in_proj(h, g, w, tm):
    t, d = h.shape
    n = w.shape[1]
    return pl.pallas_call(
        functools.partial(_in_proj_kernel, n_chunk=512),
        grid=(t // tm,),
        in_specs=[pl.BlockSpec((tm, d), lambda i: (i, 0)), _resident((1, d)), _resident((d, n))],
        out_specs=pl.BlockSpec((tm, n), lambda i: (i, 0)),
        out_shape=jax.ShapeDtypeStruct((t, n), BF16),
        compiler_params=_params(("parallel",)),
        name="in_proj",
    )(h, g, w)


def _local_mix_kernel(z_ref, halo_ref, poolw_ref, pscale_ref, convw_ref, lng_ref, sguw_ref,
                      sgub_ref, o_ref, *, pw, cw):
    ts = z_ref.shape[0]
    i = pl.program_id(1)
    z = z_ref[...].astype(F32)
    halo = halo_ref[...].astype(F32)
    halo = jnp.where(i > 0, halo, 0.0)
    ext = jnp.concatenate([halo[:, :pw + 3 * cw], z[:, :pw + 3 * cw]], axis=0)
    rows = ext.shape[0]

    def back(x, k):
        return pltpu.roll(x, k, axis=0)

    a = ext[:, :pw]
    s2 = a + back(a, 1)
    s4 = s2 + back(s2, 2)
    s8 = s4 + back(s4, 4)
    s16 = s8 + back(s8, 8)
    lane = lax.broadcasted_iota(jnp.int32, (rows, pw), 1)
    grp = lane // (pw // POOL_GROUPS)
    win_sum = jnp.where(grp == 0, s2, jnp.where(grp == 1, s4, jnp.where(grp == 2, s8, s16)))
    win = jnp.where(grp == 0, 2, jnp.where(grp == 1, 4, jnp.where(grp == 2, 8, 16)))
    pos = i * ts - HALO + lax.broadcasted_iota(jnp.int32, (rows, pw), 0)
    count = jnp.minimum(pos + 1, win).astype(F32)
    pooled = (win_sum / jnp.maximum(count, 1.0) - a)[HALO:]
    y_a = jnp.dot(pooled.astype(BF16), poolw_ref[...], preferred_element_type=F32) * pscale_ref[...]
    o_ref[:, 0:pw] = y_a.astype(o_ref.dtype)

    b_gate = z[:, pw:pw + cw]
    zc = ext[:, pw + cw:pw + 2 * cw] * ext[:, pw + 2 * cw:pw + 3 * cw]
    conv = (convw_ref[0:1, :] * back(zc, 2) + convw_ref[1:2, :] * back(zc, 1)
            + convw_ref[2:3, :] * zc)[HALO:]
    o_ref[:, pw:pw + cw] = (b_gate * conv).astype(o_ref.dtype)

    sw = (z.shape[1] - pw - 3 * cw) // 2
    zc_uv = z[:, pw + 3 * cw:]
    uv = 0.5 * zc_uv * (1.0 + lax.erf(zc_uv * math.sqrt(0.5)))
    u = uv[:, :sw]
    v = uv[:, sw:]
    mu = jnp.mean(v, axis=-1, keepdims=True)
    var = jnp.mean(jnp.square(v - mu), axis=-1, keepdims=True)
    vn = (v - mu) * lax.rsqrt(var + NORM_EPS) * lng_ref[...]
    glane = lax.broadcasted_iota(jnp.int32, (SGU_SEG, sw), 1) // (sw // SGU_GROUPS)
    wcat = sguw_ref[...]
    bias = sgub_ref[...]
    for n in range(ts // SGU_SEG):
        seg = vn[n * SGU_SEG:(n + 1) * SGU_SEG]
        rhs = jnp.concatenate(
            [jnp.where(glane == g, seg, 0.0) for g in range(SGU_GROUPS)], axis=0).astype(BF16)
        s = jnp.dot(wcat, rhs, preferred_element_type=F32) + bias
        o_ref[n * SGU_SEG:(n + 1) * SGU_SEG, pw + cw:pw + cw + sw] = (
            u[n * SGU_SEG:(n + 1) * SGU_SEG] * s).astype(o_ref.dtype)


def _local_mix(z3, poolw_bd, pscale, convw, lng, sguw_cat, sgub_full, ts, pw, cw, sw):
    b, s, _ = z3.shape
    cols = pw + 3 * cw + 2 * sw
    hb = ts // HALO
    return pl.pallas_call(
        functools.partial(_local_mix_kernel, pw=pw, cw=cw),
        grid=(b, s // ts),
        in_specs=[
            pl.BlockSpec((None, ts, cols), lambda bi, i: (bi, i, 0)),
            pl.BlockSpec((None, HALO, cols), lambda bi, i: (bi, jnp.maximum(i * hb - 1, 0), 0)),
            _resident(poolw_bd.shape), _resident(pscale.shape), _resident(convw.shape),
            _resident(lng.shape), _resident(sguw_cat.shape), _resident(sgub_full.shape),
        ],
        out_specs=pl.BlockSpec((None, ts, pw + cw + sw), lambda bi, i: (bi, i, 0)),
        out_shape=jax.ShapeDtypeStruct((b, s, pw + cw + sw), BF16),
        compiler_params=_params(("parallel", "parallel")),
        name="local_mix",
    )(z3, z3, poolw_bd, pscale, convw, lng, sguw_cat, sgub_full)


def _diff_attn_kernel(q_ref, k_ref, v_ref, bias_ref, qg_ref, kg_ref, lam_ref, sg_ref, o_ref,
                      kn_ref, vt_ref, qs_ref, st_ref, m_ref, *, lam_init):
    tq = q_ref.shape[0]
    hp = kn_ref.shape[0]
    nt = kn_ref.shape[1] // tq
    hw = 2 * DIFF_QK_DIM
    i = pl.program_id(2)
    half = lax.broadcasted_iota(jnp.int32, (1, 2 * DIFF_QK_DIM), 1) < DIFF_QK_DIM

    def qk_norm(x, g):
        sq = x * x
        ss0 = jnp.sum(jnp.where(half, sq, 0.0), axis=-1, keepdims=True)
        ss1 = jnp.sum(jnp.where(half, 0.0, sq), axis=-1, keepdims=True)
        r0 = lax.rsqrt(ss0 * (1.0 / DIFF_QK_DIM) + NORM_EPS)
        r1 = lax.rsqrt(ss1 * (1.0 / DIFF_QK_DIM) + NORM_EPS)
        return x * jnp.where(half, r0, r1) * g

    def key_rows(j):
        if isinstance(j, int):
            return slice(j * tq, (j + 1) * tq)
        return pl.ds(pl.multiple_of(j * tq, tq), tq)

    @pl.when(i == 0)
    def _():
        for h in range(hp):
            cols = slice(h * hw, (h + 1) * hw)
            for j in range(nt):
                rows = key_rows(j)
                kn_ref[h, rows, :] = qk_norm(k_ref[rows, cols].astype(F32), kg_ref[...]).astype(BF16)
                vt_ref[h, j] = v_ref[rows, cols].astype(F32).T.astype(BF16)

    def step(c):
        cur, prv = c % 2, 1 - c % 2
        scoring = c < nt
        if scoring:
            for h in range(hp):
                qn = (qk_norm(q_ref[:, h * hw:(h + 1) * hw].astype(F32), qg_ref[...])
                      * (DIFF_QK_DIM ** -0.5 * LOG2E))
                qs_ref[h, 0:tq, :] = jnp.where(half, qn, 0.0).astype(BF16)
                qs_ref[h, tq:2 * tq, :] = jnp.where(half, 0.0, qn).astype(BF16)
        m_new = [jnp.full((1, 2 * tq), NEG_BIG, F32) for _ in range(hp)]
        m_old = [m_ref[prv, h] for h in range(hp)] if c >= 1 else None
        l = [jnp.zeros((1, 2 * tq), F32) for _ in range(hp)]
        acc = [jnp.zeros((DIFF_V_DIM, 2 * tq), F32) for _ in range(hp)]
        for j in range(c + 1):
            rows = key_rows(j)
            for h in range(hp):
                if scoring:
                    st = lax.dot_general(kn_ref[h, rows, :], qs_ref[h], (((1,), (1,)), ((), ())),
                                         preferred_element_type=F32)
                    if j >= c - 1:
                        st = st + bias_ref[h, j - (c - 1)]
                    st_ref[cur, h, rows, :] = st
                    m_new[h] = jnp.maximum(m_new[h], jnp.max(st, axis=0, keepdims=True))
                if j < c:
                    p = jnp.exp2(st_ref[prv, h, rows, :] - m_old[h])
                    l[h] = l[h] + jnp.sum(p, axis=0, keepdims=True)
                    acc[h] = acc[h] + jnp.dot(vt_ref[h, j], p.astype(BF16),
                                              preferred_element_type=F32)
        if scoring:
            for h in range(hp):
                m_ref[cur, h] = m_new[h]
        if c >= 1:
            lp = lam_ref[...]
            lam = (jnp.exp(jnp.sum(lp[0:1] * lp[1:2], axis=-1, keepdims=True))
                   - jnp.exp(jnp.sum(lp[2:3] * lp[3:4], axis=-1, keepdims=True)) + lam_init)
            for h in range(hp):
                o = (acc[h][:, :tq] * (1.0 / l[h][:, :tq])
                     - acc[h][:, tq:] * (lam / l[h][:, tq:]))
                o = o * lax.rsqrt(jnp.mean(o * o, axis=0, keepdims=True) + NORM_EPS)
                o_ref[:, h * DIFF_V_DIM:(h + 1) * DIFF_V_DIM] = (
                    o.T * (sg_ref[...] * (1.0 - lam_init))).astype(o_ref.dtype)

    for c in range(nt + 1):
        pl.when(i == c)(functools.partial(step, c))


def _diff_attn(z3, bias_near, qg2, kg2, lam_p, subln_g, lam_init, q_col, k_col, v_col):
    b, s, _ = z3.shape
    tq = ATT_T
    hp = ATT_HEADS_PER_STEP
    nt = s // tq
    hw = 2 * DIFF_QK_DIM
    return pl.pallas_call(
        functools.partial(_diff_attn_kernel, lam_init=lam_init),
        grid=(b, DIFF_HEADS // hp, nt + 1),
        in_specs=[
            pl.BlockSpec((None, tq, hp * hw),
                         lambda bi, g, i: (bi, jnp.minimum(i, nt - 1), q_col // hp + g)),
            pl.BlockSpec((None, s, hp * hw), lambda bi, g, i: (bi, 0, k_col // hp + g)),
            pl.BlockSpec((None, s, hp * DIFF_V_DIM), lambda bi, g, i: (bi, 0, v_col // hp + g)),
            pl.BlockSpec((hp, 2, tq, 2 * tq), lambda bi, g, i: (g, 0, 0, 0)),
            _resident(qg2.shape), _resident(kg2.shape), _resident(lam_p.shape),
            _resident(subln_g.shape),
        ],
        out_specs=pl.BlockSpec((None, tq, hp * DIFF_V_DIM),
                               lambda bi, g, i: (bi, jnp.maximum(i - 1, 0), g)),
        out_shape=jax.ShapeDtypeStruct((b, s, DIFF_HEADS * DIFF_V_DIM), BF16),
        scratch_shapes=[
            pltpu.VMEM((hp, s, hw), BF16),
            pltpu.VMEM((hp, nt, DIFF_V_DIM, tq), BF16),
            pltpu.VMEM((hp, 2 * tq, hw), BF16),
            pltpu.VMEM((2, hp, s, 2 * tq), F32),
            pltpu.VMEM((2, hp, 1, 2 * tq), F32),
        ],
        compiler_params=_params(("parallel", "parallel", "arbitrary")),
        name="diff_attn",
    )(z3, z3, z3, bias_near, qg2, kg2, lam_p, subln_g)


def _merge_kernel(h_ref, yabc_ref, yd_ref, g_ref, wg_ref, wb_ref, wo_ref, o_ref, *, widths):
    h = h_ref[...]
    d = h.shape[1]
    xn = _rms(h, g_ref[...]).astype(BF16)
    merged = None
    off = 0
    yoff = 0
    for bi, w in enumerate(widths):
        gate = jax.nn.sigmoid(jnp.dot(xn, wg_ref[:, bi * d:(bi + 1) * d],
                                      preferred_element_type=F32))
        if bi < len(widths) - 1:
            y = yabc_ref[:, yoff:yoff + w]
            yoff += w
        else:
            y = yd_ref[...]
        proj = jnp.dot(y, wb_ref[off:off + w, :], preferred_element_type=F32)
        off += w
        merged = gate * proj if merged is None else merged + gate * proj
    o_ref[...] = h + jnp.dot(merged.astype(BF16), wo_ref[...], preferred_element_type=F32)


def _merge(h, y_abc, y_d, g, wg, wb, wo, tm, widths):
    t, d = h.shape
    return pl.pallas_call(
        functools.partial(_merge_kernel, widths=widths),
        grid=(t // tm,),
        in_specs=[
            pl.BlockSpec((tm, d), lambda i: (i, 0)),
            pl.BlockSpec((tm, y_abc.shape[1]), lambda i: (i, 0)),
            pl.BlockSpec((tm, y_d.shape[1]), lambda i: (i, 0)),
            _resident(g.shape), _resident(wg.shape), _resident(wb.shape), _resident(wo.shape),
        ],
        out_specs=pl.BlockSpec((tm, d), lambda i: (i, 0)),
        out_shape=jax.ShapeDtypeStruct((t, d), F32),
        compiler_params=_params(("parallel",)),
        name="merge",
    )(h, y_abc, y_d, g, wg, wb, wo)


def _swiglu_acc(xn, wgu_ref, wd_ref, d_ff, between=None):
    acc = None
    n_chunks = d_ff // FF_CHUNK
    for c in range(n_chunks):
        lo = c * FF_CHUNK
        g = jnp.dot(xn, wgu_ref[:, lo:lo + FF_CHUNK], preferred_element_type=F32)
        u = jnp.dot(xn, wgu_ref[:, d_ff + lo:d_ff + lo + FF_CHUNK], preferred_element_type=F32)
        act = (g * jax.nn.sigmoid(g) * u).astype(BF16)
        part = jnp.dot(act, wd_ref[lo:lo + FF_CHUNK, :], preferred_element_type=F32)
        acc = part if acc is None else acc + part
        if between is not None:
            between(c, n_chunks)
    return acc


def _ffn_kernel(h_ref, g_ref, wgu_ref, wd_ref, o_ref):
    h = h_ref[...]
    xn = _rms(h, g_ref[...]).astype(BF16)
    o_ref[...] = h + _swiglu_acc(xn, wgu_ref, wd_ref, wd_ref.shape[0])


def _ffn(h, g, wgu, wd, tm):
    t, d = h.shape
    return pl.pallas_call(
        _ffn_kernel,
        grid=(t // tm,),
        in_specs=[pl.BlockSpec((tm, d), lambda i: (i, 0)), _resident(g.shape),
                  _resident(wgu.shape), _resident(wd.shape)],
        out_specs=pl.BlockSpec((tm, d), lambda i: (i, 0)),
        out_shape=jax.ShapeDtypeStruct((t, d), F32),
        compiler_params=_params(("parallel",)),
        name="ffn",
    )(h, g, wgu, wd)


def _router_kernel(h_ref, g_ref, rw_ref, o_ref, tot_ref, carry_ref):
    tm = h_ref.shape[0]

    @pl.when(pl.program_id(0) == 0)
    def _():
        carry_ref[...] = jnp.zeros(carry_ref.shape, F32)

    hn = _rms(h_ref[...], g_ref[...])
    logits = jnp.dot(hn, rw_ref[...], preferred_element_type=F32, precision=lax.Precision.HIGHEST)
    lane = lax.broadcasted_iota(jnp.int32, (tm, LANES), 1)
    logits = jnp.where(lane < N_EXPERTS, logits, NEG_BIG)
    v1 = jnp.max(logits, axis=-1, keepdims=True)
    i1 = jnp.min(jnp.where(logits == v1, lane, LANES), axis=-1, keepdims=True)
    rest = jnp.where(lane == i1, NEG_BIG, logits)
    v2 = jnp.max(rest, axis=-1, keepdims=True)
    i2 = jnp.min(jnp.where(rest == v2, lane, LANES), axis=-1, keepdims=True)
    e = jnp.exp(v2 - v1)
    w1 = 1.0 / (1.0 + e)
    w2 = e / (1.0 + e)
    cnt = jnp.where((lane == i1) | (lane == i2), 1.0, 0.0)
    r = lax.broadcasted_iota(jnp.int32, (tm, tm), 0)
    c = lax.broadcasted_iota(jnp.int32, (tm, tm), 1)
    tri = jnp.where(c < r, 1.0, 0.0).astype(BF16)
    excl = jnp.dot(tri, cnt.astype(BF16), preferred_element_type=F32) + carry_ref[...]
    rank1 = jnp.sum(jnp.where(lane == i1, excl, 0.0), axis=-1, keepdims=True)
    rank2 = jnp.sum(jnp.where(lane == i2, excl, 0.0), axis=-1, keepdims=True)
    carry_ref[...] = carry_ref[...] + jnp.sum(cnt, axis=0, keepdims=True)
    tot_ref[...] = carry_ref[...]
    packed = jnp.where(lane == 0, i1.astype(F32), jnp.where(lane == 1, i2.astype(F32),
             jnp.where(lane == 2, w1, jnp.where(lane == 3, w2,
             jnp.where(lane == 4, rank1, jnp.where(lane == 5, rank2, 0.0))))))
    o_ref[...] = packed


def _router(h, g, rw_pad, tm):
    t, d = h.shape
    return pl.pallas_call(
        _router_kernel,
        grid=(t // tm,),
        in_specs=[pl.BlockSpec((tm, d), lambda i: (i, 0)), _resident(g.shape),
                  _resident(rw_pad.shape)],
        out_specs=[pl.BlockSpec((tm, LANES), lambda i: (i, 0)),
                   pl.BlockSpec((1, LANES), lambda i: (0, 0))],
        out_shape=[jax.ShapeDtypeStruct((t, LANES), F32), jax.ShapeDtypeStruct((1, LANES), F32)],
        scratch_shapes=[pltpu.VMEM((1, LANES), F32)],
        compiler_params=_params(("arbitrary",)),
        name="router",
    )(h, g, rw_pad)


def _invert_kernel(dest_ref, src_ref, *, n_tok):
    c = pl.program_id(0)
    ch = dest_ref.shape[1] // 2

    @pl.when(c == 0)
    def _():
        def clear(r, x):
            src_ref[r] = -1
            return x

        lax.fori_loop(0, src_ref.shape[0], clear, 0, unroll=16)

    base = c * ch

    def place(t, x):
        src_ref[dest_ref[0, t]] = base + t
        src_ref[dest_ref[0, ch + t]] = n_tok + base + t
        return x

    lax.fori_loop(0, ch, place, 0, unroll=8)


def _invert(dest, rows, n_tok):
    nc, _, ch2 = dest.shape
    return pl.pallas_call(
        functools.partial(_invert_kernel, n_tok=n_tok),
        grid=(nc,),
        in_specs=[pl.BlockSpec((None, 1, ch2), lambda c: (c, 0, 0), memory_space=pltpu.SMEM)],
        out_specs=pl.BlockSpec(memory_space=pltpu.SMEM),
        out_shape=jax.ShapeDtypeStruct((rows,), jnp.int32),
        compiler_params=pltpu.CompilerParams(dimension_semantics=("arbitrary",)),
        name="moe_invert",
    )(dest)


def _expert_kernel(te_ref, src_cur, src_nxt, orow_prv, orow_cur, zero_ref, h_hbm, g_ref, wgu_ref,
                   wd_ref, yt_hbm, xbuf, ybuf, gsem, ssem):
    del te_ref
    i = pl.program_id(0)
    last = pl.num_programs(0) - 1
    tm = xbuf.shape[1]
    s = lax.rem(i, 2)
    o = 1 - s

    def gather(tok, r, slot):
        return pltpu.make_async_copy(h_hbm.at[pl.ds(tok, 1)],
                                     xbuf.at[slot, pl.ds(r, 1)], gsem.at[slot])

    def scatter(row, r, slot):
        return pltpu.make_async_copy(ybuf.at[slot, pl.ds(r, 1)],
                                     yt_hbm.at[pl.ds(row, 1)], ssem.at[slot])

    def for_rows(fn):
        def body(r, x):
            fn(r)
            return x

        lax.fori_loop(0, tm, body, 0, unroll=8)

    @pl.when(i == 0)
    def _():
        ybuf[1] = jnp.zeros(ybuf.shape[1:], ybuf.dtype)
        for_rows(lambda r: gather(src_cur[0, r], r, 0).start())

    for_rows(lambda r: gather(src_cur[0, r], r, s).wait())
    xn = _rms(xbuf[s], g_ref[...]).astype(BF16)

    del zero_ref

    def between(c, n_chunks):
        per = -(-tm // (n_chunks // 2))
        for r in range(c * per, min((c + 1) * per, tm)):
            gather(src_nxt[0, r], r, o).start()
            scatter(orow_prv[0, r], r, o).start()

    ybuf[s] = _swiglu_acc(xn, wgu_ref, wd_ref, wd_ref.shape[0], between)
    for_rows(lambda r: scatter(orow_prv[0, r], r, o).wait())

    @pl.when(i == last)
    def _():
        for_rows(lambda r: gather(src_nxt[0, r], r, o).wait())
        for_rows(lambda r: scatter(orow_cur[0, r], r, s).start())
        for_rows(lambda r: scatter(orow_cur[0, r], r, s).wait())


def _experts(tile_expert, src_tok, out_row, h, g, wgu, wd, yt_rows):
    n, _, tm = src_tok.shape
    d = h.shape[1]
    d_ff = wd.shape[1]
    smem = functools.partial(pl.BlockSpec, (None, 1, tm), memory_space=pltpu.SMEM)
    grid_spec = pltpu.PrefetchScalarGridSpec(
        num_scalar_prefetch=1,
        grid=(n,),
        in_specs=[
            smem(lambda i, te: (i, 0, 0)),
            smem(lambda i, te: (jnp.minimum(i + 1, n - 1), 0, 0)),
            smem(lambda i, te: (i, 0, 0)),
            smem(lambda i, te: (i + 1, 0, 0)),
            pl.BlockSpec(memory_space=pltpu.SMEM),
            pl.BlockSpec(memory_space=pl.ANY),
            pl.BlockSpec(g.shape, lambda i, te: (0, 0), pipeline_mode=pl.Buffered(1)),
            pl.BlockSpec((None, d, 2 * d_ff), lambda i, te: (te[i], 0, 0),
                         pipeline_mode=pl.Buffered(1)),
            pl.BlockSpec((None, d_ff, d), lambda i, te: (te[i], 0, 0),
                         pipeline_mode=pl.Buffered(1)),
        ],
        out_specs=pl.BlockSpec(memory_space=pl.ANY),
        scratch_shapes=[pltpu.VMEM((2, tm, d), F32), pltpu.VMEM((2, tm, d), F32),
                        pltpu.SemaphoreType.DMA((2,)), pltpu.SemaphoreType.DMA((2,))],
    )
    return pl.pallas_call(
        _expert_kernel,
        grid_spec=grid_spec,
        out_shape=jax.ShapeDtypeStruct((yt_rows, d), F32),
        compiler_params=_params(("arbitrary",)),
        name="moe_experts",
    )(tile_expert, src_tok, src_tok, out_row, out_row, jnp.zeros((8,), jnp.int32), h, g, wgu, wd)


def _combine_kernel(h_ref, pk_ref, y1_ref, y2_ref, o_ref):
    pk = pk_ref[...]
    o_ref[...] = h_ref[...] + pk[:, 2:3] * y1_ref[...] + pk[:, 3:4] * y2_ref[...]


def _combine(h, packed, yt, tm):
    t, d = h.shape
    nb = t // tm
    return pl.pallas_call(
        _combine_kernel,
        grid=(nb,),
        in_specs=[pl.BlockSpec((tm, d), lambda i: (i, 0)),
                  pl.BlockSpec((tm, LANES), lambda i: (i, 0)),
                  pl.BlockSpec((tm, d), lambda i: (i, 0)),
                  pl.BlockSpec((tm, d), lambda i: (nb + i, 0))],
        out_specs=pl.BlockSpec((tm, d), lambda i: (i, 0)),
        out_shape=jax.ShapeDtypeStruct((t, d), F32),
        compiler_params=_params(("parallel",)),
        name="moe_combine",
    )(h, packed, yt, yt)


def _rel_bucket(rel):
    nb = REL_BUCKETS // 2
    max_exact = nb // 2
    n = jnp.abs(rel)
    nf = jnp.maximum(n, 1).astype(F32)
    large = max_exact + (jnp.log(nf / max_exact) / math.log(REL_MAX_DIST / max_exact)
                         * (nb - max_exact)).astype(jnp.int32)
    large = jnp.minimum(large, nb - 1)
    return jnp.where(rel > 0, nb, 0) + jnp.where(n < max_exact, n, large)


def _near_bias(rel_bias):
    t = ATT_T
    qp = jnp.arange(t)[:, None]
    kp = jnp.arange(t)[None, :]

    def lookup(rel):
        onehot = jax.nn.one_hot(_rel_bucket(rel), REL_BUCKETS, dtype=F32)
        return jnp.einsum('...b,bm->...m', onehot, rel_bias, precision=lax.Precision.HIGHEST)

    far = lookup(jnp.full((), -(2 * t), jnp.int32))
    prev = (lookup(kp - t - qp) - far) * LOG2E
    diag = (lookup(kp - qp) - far) * LOG2E
    diag = jnp.where(((kp // CHUNK) <= (qp // CHUNK))[:, :, None], diag, NEG_BIG)
    both = jnp.stack([prev, diag], axis=0).reshape(2, t, t, DIFF_HEADS, 2)
    return both.transpose(3, 0, 2, 4, 1).reshape(DIFF_HEADS, 2, t, 2 * t).astype(F32)


def kernel(x, rel_bias, norm1_g, w_in, pool_w, pool_scale, conv_w, sgu_ln_g, sgu_w, sgu_b, q_norm_g, k_norm_g, diff_lambda, subln_g, w_branch_pool, w_branch_conv, w_branch_sgu, w_branch_attn, w_out, norm2_g, ffn_w_gate_up, ffn_w_down, router_w, moe_w_gate_up, moe_w_down):
    b, s, d = x.shape
    t = b * s
    depth = w_in.shape[0]
    pw = pool_scale.shape[1]
    cw = conv_w.shape[2]
    sw = sgu_ln_g.shape[1]
    aw = w_branch_attn.shape[1]
    mix_cols = pw + 3 * cw + 2 * sw + 3 * aw
    nb = REL_BUCKETS // 2
    assert nb // 2 + int(math.log((ATT_T + 1) / (nb // 2)) / math.log(REL_MAX_DIST / (nb // 2))
                         * (nb - nb // 2)) >= nb - 1
    q_col = (pw + 3 * cw + 2 * sw) // LANES
    k_col = q_col + aw // LANES
    v_col = k_col + aw // LANES
    tm = min(512, t)
    ts = min(512, s)

    bias_near = _near_bias(rel_bias)
    tri = jnp.tril(jnp.ones((SGU_SEG, SGU_SEG), bool))
    gd = pw // POOL_GROUPS

    h = x.reshape(t, d)
    for layer in range(depth):
        lam_init = 0.8 - 0.6 * math.exp(-0.3 * layer)
        w_mix = w_in[layer, :, :mix_cols].astype(BF16)
        w_gate = w_in[layer, :, mix_cols:].astype(BF16)
        poolw_bd = jnp.zeros((pw, pw), F32)
        for g in range(POOL_GROUPS):
            poolw_bd = poolw_bd.at[g * gd:(g + 1) * gd, g * gd:(g + 1) * gd].set(pool_w[layer, g])
        sguw_cat = jnp.where(tri[None], sgu_w[layer], 0.0).transpose(1, 0, 2).reshape(
            SGU_SEG, SGU_GROUPS * SGU_SEG).astype(BF16)
        sgub_full = jnp.repeat(sgu_b[layer].T, sw // SGU_GROUPS, axis=1)
        wb = jnp.concatenate([w_branch_pool[layer], w_branch_conv[layer], w_branch_sgu[layer],
                              w_branch_attn[layer]], axis=0).astype(BF16)

        z = _in_proj(h, norm1_g[layer][None], w_mix, tm)
        z3 = z.reshape(b, s, mix_cols)
        y_abc = _local_mix(z3, poolw_bd.astype(BF16), pool_scale[layer][None], conv_w[layer],
                           sgu_ln_g[layer][None], sguw_cat, sgub_full, ts, pw, cw, sw)
        y_d = _diff_attn(z3, bias_near, jnp.tile(q_norm_g[layer], 2)[None],
                         jnp.tile(k_norm_g[layer], 2)[None], diff_lambda[layer],
                         subln_g[layer][None], lam_init, q_col, k_col, v_col)
        h = _merge(h, y_abc.reshape(t, -1), y_d.reshape(t, -1), norm1_g[layer][None], w_gate, wb,
                   w_out[layer].astype(BF16), tm, (pw, cw, sw, aw))

        g2 = norm2_g[layer][None]
        if layer % 2 == 0:
            h = _ffn(h, g2, ffn_w_gate_up[layer // 2].astype(BF16),
                     ffn_w_down[layer // 2].astype(BF16), tm)
        else:
            li = layer // 2
            rw_pad = jnp.zeros((d, LANES), F32).at[:, :N_EXPERTS].set(router_w[li])
            packed, totals = _router(h, g2, rw_pad, tm)
            n_e = totals[0, :N_EXPERTS].astype(jnp.int32)
            n_pad = ((n_e + tm - 1) // tm) * tm
            ends = jnp.cumsum(n_pad)
            starts = ends - n_pad
            e1 = packed[:, 0].astype(jnp.int32)
            e2 = packed[:, 1].astype(jnp.int32)
            eids = jnp.arange(N_EXPERTS)[None, :]
            dest1 = (jnp.sum(jnp.where(e1[:, None] == eids, starts[None, :], 0), axis=1)
                     + packed[:, 4].astype(jnp.int32))
            dest2 = (jnp.sum(jnp.where(e2[:, None] == eids, starts[None, :], 0), axis=1)
                     + packed[:, 5].astype(jnp.int32))
            rows = 2 * t + N_EXPERTS * tm
            n_tiles = rows // tm
            tile_expert = jnp.minimum(
                jnp.sum((jnp.arange(n_tiles)[:, None] * tm) >= ends[None, :], axis=1),
                N_EXPERTS - 1).astype(jnp.int32)
            ch = min(2048, t)
            dest_d = jnp.concatenate([dest1.reshape(t // ch, 1, ch), dest2.reshape(t // ch, 1, ch)],
                                     axis=2)
            src = _invert(dest_d, rows, t)
            is_pad = src < 0
            src_tok = jnp.where(is_pad, 0, jnp.where(src >= t, src - t, src))
            pad_rank = jnp.cumsum(is_pad.astype(jnp.int32)) - 1
            out_row = jnp.where(is_pad, 2 * t + tm + pad_rank, src)
            spare = 2 * t + jnp.arange(tm, dtype=jnp.int32)
            yt = _experts(tile_expert, src_tok.reshape(n_tiles, 1, tm),
                          jnp.concatenate([spare, out_row]).reshape(n_tiles + 1, 1, tm), h, g2,
                          moe_w_gate_up[li].astype(BF16), moe_w_down[li].astype(BF16),
                          rows + tm)
            h = _combine(h, packed, yt, tm)
    return h.reshape(b, s, d)
```

```python
import functools
import math

import jax
import jax.numpy as jnp
import numpy as np
from jax import lax
from jax.experimental import pallas as pl
from jax.experimental.pallas import tpu as pltpu

F32 = jnp.float32
BF16 = jnp.bfloat16

NORM_EPS = 1e-6
CHUNK = 64
POOL_WINDOWS = (2, 4, 8, 16)
POOL_GROUPS = 4
CONV_K = 3
SGU_GROUPS = 4
SGU_SEG = 128
DIFF_HEADS = 4
DIFF_QK_DIM = 64
DIFF_V_DIM = 128
REL_BUCKETS = 32
REL_MAX_DIST = 128
N_EXPERTS = 8
LANES = 128
V7X_VMEM_BYTES = 64 * 1024 * 1024
VMEM_LIMIT = V7X_VMEM_BYTES - 8 * 1024 * 1024
NEG_BIG = -1e30
LOG2E = math.log2(math.e)

HALO = 16
ATT_T = 256
ATT_HEADS_PER_STEP = 2
FF_CHUNK = 256


def _rms(x, g):
    return x * lax.rsqrt(jnp.mean(x * x, axis=-1, keepdims=True) + NORM_EPS) * g


def _resident(shape):
    nd = len(shape)
    return pl.BlockSpec(shape, lambda *_: (0,) * nd, pipeline_mode=pl.Buffered(1))


def _params(sem):
    return pltpu.CompilerParams(dimension_semantics=sem, vmem_limit_bytes=VMEM_LIMIT)


def _in_proj_kernel(x_ref, g_ref, w_ref, o_ref, *, n_chunk):
    xn = _rms(x_ref[...], g_ref[...]).astype(BF16)
    n = o_ref.shape[1]
    for j in range(n // n_chunk):
        sl = slice(j * n_chunk, (j + 1) * n_chunk)
        o_ref[:, sl] = jnp.dot(xn, w_ref[:, sl], preferred_element_type=F32).astype(o_ref.dtype)


def _in_proj(h, g, w, tm):
    t, d = h.shape
    n = w.shape[1]
    return pl.pallas_call(
        functools.partial(_in_proj_kernel, n_chunk=512),
        grid=(t // tm,),
        in_specs=[pl.BlockSpec((tm, d), lambda i: (i, 0)), _resident((1, d)), _resident((d, n))],
        out_specs=pl.BlockSpec((tm, n), lambda i: (i, 0)),
        out_shape=jax.ShapeDtypeStruct((t, n), BF16),
        compiler_params=_params(("parallel",)),
        name="in_proj",
    )(h, g, w)


def _local_mix_kernel(z_ref, halo_ref, poolw_ref, pscale_ref, convw_ref, lng_ref, sguw_ref,
                      sgub_ref, o_ref, *, pw, cw):
    ts = z_ref.shape[0]
    i = pl.program_id(1)
    z = z_ref[...].astype(F32)
    halo = halo_ref[...].astype(F32)
    halo = jnp.where(i > 0, halo, 0.0)
    ext = jnp.concatenate([halo[:, :pw + 3 * cw], z[:, :pw + 3 * cw]], axis=0)
    rows = ext.shape[0]

    def back(x, k):
        return pltpu.roll(x, k, axis=0)

    a = ext[:, :pw]
    s2 = a + back(a, 1)
    s4 = s2 + back(s2, 2)
    s8 = s4 + back(s4, 4)
    s16 = s8 + back(s8, 8)
    lane = lax.broadcasted_iota(jnp.int32, (rows, pw), 1)
    grp = lane // (pw // POOL_GROUPS)
    win_sum = jnp.where(grp == 0, s2, jnp.where(grp == 1, s4, jnp.where(grp == 2, s8, s16)))
    win = jnp.where(grp == 0, 2, jnp.where(grp == 1, 4, jnp.where(grp == 2, 8, 16)))
    pos = i * ts - HALO + lax.broadcasted_iota(jnp.int32, (rows, pw), 0)
    count = jnp.minimum(pos + 1, win).astype(F32)
    pooled = (win_sum / jnp.maximum(count, 1.0) - a)[HALO:]
    y_a = jnp.dot(pooled.astype(BF16), poolw_ref[...], preferred_element_type=F32) * pscale_ref[...]
    o_ref[:, 0:pw] = y_a.astype(o_ref.dtype)

    b_gate = z[:, pw:pw + cw]
    zc = ext[:, pw + cw:pw + 2 * cw] * ext[:, pw + 2 * cw:pw + 3 * cw]
    conv = (convw_ref[0:1, :] * back(zc, 2) + convw_ref[1:2, :] * back(zc, 1)
            + convw_ref[2:3, :] * zc)[HALO:]
    o_ref[:, pw:pw + cw] = (b_gate * conv).astype(o_ref.dtype)

    sw = (z.shape[1] - pw - 3 * cw) // 2
    zc_uv = z[:, pw + 3 * cw:]
    uv = 0.5 * zc_uv * (1.0 + lax.erf(zc_uv * math.sqrt(0.5)))
    u = uv[:, :sw]
    v = uv[:, sw:]
    mu = jnp.mean(v, axis=-1, keepdims=True)
    var = jnp.mean(jnp.square(v - mu), axis=-1, keepdims=True)
    vn = (v - mu) * lax.rsqrt(var + NORM_EPS) * lng_ref[...]
    glane = lax.broadcasted_iota(jnp.int32, (SGU_SEG, sw), 1) // (sw // SGU_GROUPS)
    wcat = sguw_ref[...]
    bias = sgub_ref[...]
    for n in range(ts // SGU_SEG):
        seg = vn[n * SGU_SEG:(n + 1) * SGU_SEG]
        rhs = jnp.concatenate(
            [jnp.where(glane == g, seg, 0.0) for g in range(SGU_GROUPS)], axis=0).astype(BF16)
        s = jnp.dot(wcat, rhs, preferred_element_type=F32) + bias
        o_ref[n * SGU_SEG:(n + 1) * SGU_SEG, pw + cw:pw + cw + sw] = (
            u[n * SGU_SEG:(n + 1) * SGU_SEG] * s).astype(o_ref.dtype)


def _local_mix(z3, poolw_bd, pscale, convw, lng, sguw_cat, sgub_full, ts, pw, cw, sw):
    b, s, _ = z3.shape
    cols = pw + 3 * cw + 2 * sw
    hb = ts // HALO
    return pl.pallas_call(
        functools.partial(_local_mix_kernel, pw=pw, cw=cw),
        grid=(b, s // ts),
        in_specs=[
            pl.BlockSpec((None, ts, cols), lambda bi, i: (bi, i, 0)),
            pl.BlockSpec((None, HALO, cols), lambda bi, i: (bi, jnp.maximum(i * hb - 1, 0), 0)),
            _resident(poolw_bd.shape), _resident(pscale.shape), _resident(convw.shape),
            _resident(lng.shape), _resident(sguw_cat.shape), _resident(sgub_full.shape),
        ],
        out_specs=pl.BlockSpec((None, ts, pw + cw + sw), lambda bi, i: (bi, i, 0)),
        out_shape=jax.ShapeDtypeStruct((b, s, pw + cw + sw), BF16),
        compiler_params=_params(("parallel", "parallel")),
        name="local_mix",
    )(z3, z3, poolw_bd, pscale, convw, lng, sguw_cat, sgub_full)


def _diff_attn_kernel(q_ref, k_ref, v_ref, bias_ref, qg_ref, kg_ref, lam_ref, sg_ref, o_ref,
                      kn_ref, vt_ref, qs_ref, st_ref, m_ref, *, lam_init):
    tq = q_ref.shape[0]
    hp = kn_ref.shape[0]
    nt = kn_ref.shape[1] // tq
    hw = 2 * DIFF_QK_DIM
    i = pl.program_id(2)
    half = lax.broadcasted_iota(jnp.int32, (1, 2 * DIFF_QK_DIM), 1) < DIFF_QK_DIM

    def qk_norm(x, g):
        sq = x * x
        ss0 = jnp.sum(jnp.where(half, sq, 0.0), axis=-1, keepdims=True)
        ss1 = jnp.sum(jnp.where(half, 0.0, sq), axis=-1, keepdims=True)
        r0 = lax.rsqrt(ss0 * (1.0 / DIFF_QK_DIM) + NORM_EPS)
        r1 = lax.rsqrt(ss1 * (1.0 / DIFF_QK_DIM) + NORM_EPS)
        return x * jnp.where(half, r0, r1) * g

    def key_rows(j):
        if isinstance(j, int):
            return slice(j * tq, (j + 1) * tq)
        return pl.ds(pl.multiple_of(j * tq, tq), tq)

    @pl.when(i == 0)
    def _():
        for h in range(hp):
            cols = slice(h * hw, (h + 1) * hw)
            for j in range(nt):
                rows = key_rows(j)
                kn_ref[h, rows, :] = qk_norm(k_ref[rows, cols].astype(F32), kg_ref[...]).astype(BF16)
                vt_ref[h, j] = v_ref[rows, cols].astype(F32).T.astype(BF16)

    def step(sidx):
        c, f = sidx - 1, sidx - 2
        scoring = 0 <= c < nt
        finishing = f >= 0
        if sidx < nt:
            for h in range(hp):
                qn = (qk_norm(q_ref[:, h * hw:(h + 1) * hw].astype(F32), qg_ref[...])
                      * (DIFF_QK_DIM ** -0.5 * LOG2E))
                qs_ref[sidx % 2, h, 0:tq, :] = jnp.where(half, qn, 0.0).astype(BF16)
                qs_ref[sidx % 2, h, tq:2 * tq, :] = jnp.where(half, 0.0, qn).astype(BF16)
        m_new = [jnp.full((1, 2 * tq), NEG_BIG, F32) for _ in range(hp)]
        m_old = [m_ref[f % 2, h] for h in range(hp)] if finishing else None
        l = [jnp.zeros((1, 2 * tq), F32) for _ in range(hp)]
        acc = [jnp.zeros((DIFF_V_DIM, 2 * tq), F32) for _ in range(hp)]
        for j in range(max(c + 1 if scoring else 0, f + 1)):
            rows = key_rows(j)
            for h in range(hp):
                if scoring and j <= c:
                    st = lax.dot_general(kn_ref[h, rows, :], qs_ref[c % 2, h],
                                         (((1,), (1,)), ((), ())), preferred_element_type=F32)
                    if j >= c - 1:
                        st = st + bias_ref[h, j - (c - 1)]
                    st_ref[c % 2, h, rows, :] = st
                    m_new[h] = jnp.maximum(m_new[h], jnp.max(st, axis=0, keepdims=True))
                if finishing and j <= f:
                    p = jnp.exp2(st_ref[f % 2, h, rows, :] - m_old[h])
                    l[h] = l[h] + jnp.sum(p, axis=0, keepdims=True)
                    acc[h] = acc[h] + jnp.dot(vt_ref[h, j], p.astype(BF16),
                                              preferred_element_type=F32)
        if scoring:
            for h in range(hp):
                m_ref[c % 2, h] = m_new[h]
        if finishing:
            lp = lam_ref[...]
            lam = (jnp.exp(jnp.sum(lp[0:1] * lp[1:2], axis=-1, keepdims=True))
                   - jnp.exp(jnp.sum(lp[2:3] * lp[3:4], axis=-1, keepdims=True)) + lam_init)
            for h in range(hp):
                o = (acc[h][:, :tq] * (1.0 / l[h][:, :tq])
                     - acc[h][:, tq:] * (lam / l[h][:, tq:]))
                o = o * lax.rsqrt(jnp.mean(o * o, axis=0, keepdims=True) + NORM_EPS)
                o_ref[:, h * DIFF_V_DIM:(h + 1) * DIFF_V_DIM] = (
                    o.T * (sg_ref[...] * (1.0 - lam_init))).astype(o_ref.dtype)

    for sidx in range(nt + 2):
        pl.when(i == sidx)(functools.partial(step, sidx))


def _diff_attn(z3, bias_near, qg2, kg2, lam_p, subln_g, lam_init, q_col, k_col, v_col):
    b, s, _ = z3.shape
    tq = ATT_T
    hp = ATT_HEADS_PER_STEP
    nt = s // tq
    hw = 2 * DIFF_QK_DIM
    return pl.pallas_call(
        functools.partial(_diff_attn_kernel, lam_init=lam_init),
        grid=(b, DIFF_HEADS // hp, nt + 2),
        in_specs=[
            pl.BlockSpec((None, tq, hp * hw),
                         lambda bi, g, i: (bi, jnp.minimum(i, nt - 1), q_col // hp + g)),
            pl.BlockSpec((None, s, hp * hw), lambda bi, g, i: (bi, 0, k_col // hp + g)),
            pl.BlockSpec((None, s, hp * DIFF_V_DIM), lambda bi, g, i: (bi, 0, v_col // hp + g)),
            pl.BlockSpec((hp, 2, tq, 2 * tq), lambda bi, g, i: (g, 0, 0, 0)),
            _resident(qg2.shape), _resident(kg2.shape), _resident(lam_p.shape),
            _resident(subln_g.shape),
        ],
        out_specs=pl.BlockSpec((None, tq, hp * DIFF_V_DIM),
                               lambda bi, g, i: (bi, jnp.maximum(i - 2, 0), g)),
        out_shape=jax.ShapeDtypeStruct((b, s, DIFF_HEADS * DIFF_V_DIM), BF16),
        scratch_shapes=[
            pltpu.VMEM((hp, s, hw), BF16),
            pltpu.VMEM((hp, nt, DIFF_V_DIM, tq), BF16),
            pltpu.VMEM((2, hp, 2 * tq, hw), BF16),
            pltpu.VMEM((2, hp, s, 2 * tq), F32),
            pltpu.VMEM((2, hp, 1, 2 * tq), F32),
        ],
        compiler_params=_params(("parallel", "parallel", "arbitrary")),
        name="diff_attn",
    )(z3, z3, z3, bias_near, qg2, kg2, lam_p, subln_g)


def _merge_kernel(h_ref, yabc_ref, yd_ref, g_ref, wg_ref, wb_ref, wo_ref, o_ref, *, widths):
    h = h_ref[...]
    d = h.shape[1]
    xn = _rms(h, g_ref[...]).astype(BF16)
    merged = None
    off = 0
    yoff = 0
    for bi, w in enumerate(widths):
        gate = jax.nn.sigmoid(jnp.dot(xn, wg_ref[:, bi * d:(bi + 1) * d],
                                      preferred_element_type=F32))
        if bi < len(widths) - 1:
            y = yabc_ref[:, yoff:yoff + w]
            yoff += w
        else:
            y = yd_ref[...]
        proj = jnp.dot(y, wb_ref[off:off + w, :], preferred_element_type=F32)
        off += w
        merged = gate * proj if merged is None else merged + gate * proj
    o_ref[...] = h + jnp.dot(merged.astype(BF16), wo_ref[...], preferred_element_type=F32)


def _merge(h, y_abc, y_d, g, wg, wb, wo, tm, widths):
    t, d = h.shape
    return pl.pallas_call(
        functools.partial(_merge_kernel, widths=widths),
        grid=(t // tm,),
        in_specs=[
            pl.BlockSpec((tm, d), lambda i: (i, 0)),
            pl.BlockSpec((tm, y_abc.shape[1]), lambda i: (i, 0)),
            pl.BlockSpec((tm, y_d.shape[1]), lambda i: (i, 0)),
            _resident(g.shape), _resident(wg.shape), _resident(wb.shape), _resident(wo.shape),
        ],
        out_specs=pl.BlockSpec((tm, d), lambda i: (i, 0)),
        out_shape=jax.ShapeDtypeStruct((t, d), F32),
        compiler_params=_params(("parallel",)),
        name="merge",
    )(h, y_abc, y_d, g, wg, wb, wo)


def _swiglu_acc(xn, wgu_ref, wd_ref, d_ff, between=None):
    acc = None
    n_chunks = d_ff // FF_CHUNK
    for c in range(n_chunks):
        lo = c * FF_CHUNK
        g = jnp.dot(xn, wgu_ref[:, lo:lo + FF_CHUNK], preferred_element_type=F32)
        u = jnp.dot(xn, wgu_ref[:, d_ff + lo:d_ff + lo + FF_CHUNK], preferred_element_type=F32)
        act = (g * jax.nn.sigmoid(g) * u).astype(BF16)
        part = jnp.dot(act, wd_ref[lo:lo + FF_CHUNK, :], preferred_element_type=F32)
        acc = part if acc is None else acc + part
        if between is not None:
            between(c, n_chunks)
    return acc


def _ffn_kernel(h_ref, g_ref, wgu_ref, wd_ref, o_ref):
    h = h_ref[...]
    xn = _rms(h, g_ref[...]).astype(BF16)
    o_ref[...] = h + _swiglu_acc(xn, wgu_ref, wd_ref, wd_ref.shape[0])


def _ffn(h, g, wgu, wd, tm):
    t, d = h.shape
    return pl.pallas_call(
        _ffn_kernel,
        grid=(t // tm,),
        in_specs=[pl.BlockSpec((tm, d), lambda i: (i, 0)), _resident(g.shape),
                  _resident(wgu.shape), _resident(wd.shape)],
        out_specs=pl.BlockSpec((tm, d), lambda i: (i, 0)),
        out_shape=jax.ShapeDtypeStruct((t, d), F32),
        compiler_params=_params(("parallel",)),
        name="ffn",
    )(h, g, wgu, wd)


def _router_kernel(h_ref, g_ref, rw_ref, o_ref, tot_ref, carry_ref):
    tm = h_ref.shape[0]

    @pl.when(pl.program_id(0) == 0)
    def _():
        carry_ref[...] = jnp.zeros(carry_ref.shape, F32)

    hn = _rms(h_ref[...], g_ref[...])
    hn_hi = hn.astype(BF16)
    hn_lo = (hn - hn_hi.astype(F32)).astype(BF16)
    logits = (jnp.dot(hn_hi, rw_ref[0], preferred_element_type=F32)
              + jnp.dot(hn_lo, rw_ref[0], preferred_element_type=F32)
              + jnp.dot(hn_hi, rw_ref[1], preferred_element_type=F32))
    lane = lax.broadcasted_iota(jnp.int32, (tm, LANES), 1)
    logits = jnp.where(lane < N_EXPERTS, logits, NEG_BIG)
    v1 = jnp.max(logits, axis=-1, keepdims=True)
    i1 = jnp.min(jnp.where(logits == v1, lane, LANES), axis=-1, keepdims=True)
    rest = jnp.where(lane == i1, NEG_BIG, logits)
    v2 = jnp.max(rest, axis=-1, keepdims=True)
    i2 = jnp.min(jnp.where(rest == v2, lane, LANES), axis=-1, keepdims=True)
    e = jnp.exp(v2 - v1)
    w1 = 1.0 / (1.0 + e)
    w2 = e / (1.0 + e)
    cnt = jnp.where((lane == i1) | (lane == i2), 1.0, 0.0)
    r = lax.broadcasted_iota(jnp.int32, (tm, tm), 0)
    c = lax.broadcasted_iota(jnp.int32, (tm, tm), 1)
    tri = jnp.where(c < r, 1.0, 0.0).astype(BF16)
    excl = jnp.dot(tri, cnt.astype(BF16), preferred_element_type=F32) + carry_ref[...]
    rank1 = jnp.sum(jnp.where(lane == i1, excl, 0.0), axis=-1, keepdims=True)
    rank2 = jnp.sum(jnp.where(lane == i2, excl, 0.0), axis=-1, keepdims=True)
    carry_ref[...] = carry_ref[...] + jnp.sum(cnt, axis=0, keepdims=True)
    tot_ref[...] = carry_ref[...]
    packed = jnp.where(lane == 0, i1.astype(F32), jnp.where(lane == 1, i2.astype(F32),
             jnp.where(lane == 2, w1, jnp.where(lane == 3, w2,
             jnp.where(lane == 4, rank1, jnp.where(lane == 5, rank2, 0.0))))))
    o_ref[...] = packed


def _router(h, g, rw_pad, tm):
    t, d = h.shape
    return pl.pallas_call(
        _router_kernel,
        grid=(t // tm,),
        in_specs=[pl.BlockSpec((tm, d), lambda i: (i, 0)), _resident(g.shape),
                  _resident(rw_pad.shape)],
        out_specs=[pl.BlockSpec((tm, LANES), lambda i: (i, 0)),
                   pl.BlockSpec((1, LANES), lambda i: (0, 0))],
        out_shape=[jax.ShapeDtypeStruct((t, LANES), F32), jax.ShapeDtypeStruct((1, LANES), F32)],
        scratch_shapes=[pltpu.VMEM((1, LANES), F32)],
        compiler_params=_params(("arbitrary",)),
        name="router",
    )(h, g, rw_pad)


def _invert_kernel(dest_ref, src_ref):
    c = pl.program_id(0)
    ch = dest_ref.shape[1]

    @pl.when(c == 0)
    def _():
        def clear(r, x):
            src_ref[r] = -1
            return x

        lax.fori_loop(0, src_ref.shape[0], clear, 0, unroll=16)

    base = c * ch

    def place(a, x):
        src_ref[dest_ref[0, a]] = base + a
        return x

    lax.fori_loop(0, ch, place, 0, unroll=16)


def _invert(dest, rows):
    nc, _, ch = dest.shape
    return pl.pallas_call(
        _invert_kernel,
        grid=(nc,),
        in_specs=[pl.BlockSpec((None, 1, ch), lambda c: (c, 0, 0), memory_space=pltpu.SMEM)],
        out_specs=pl.BlockSpec(memory_space=pltpu.SMEM),
        out_shape=jax.ShapeDtypeStruct((rows,), jnp.int32),
        compiler_params=pltpu.CompilerParams(dimension_semantics=("arbitrary",)),
        name="moe_invert",
    )(dest)


def _expert_kernel(te_ref, src_cur, src_nxt, orow_prv, orow_cur, h_hbm, g_ref, wgu_ref, wd_ref,
                   yt_hbm, xbuf, ybuf, gsem, ssem):
    del te_ref
    i = pl.program_id(0)
    last = pl.num_programs(0) - 1
    tm = xbuf.shape[1]
    s = lax.rem(i, 2)
    o = 1 - s

    def gather(tok, r, slot):
        return pltpu.make_async_copy(h_hbm.at[pl.ds(tok, 1)],
                                     xbuf.at[slot, pl.ds(r, 1)], gsem.at[slot])

    def scatter(row, r, slot):
        return pltpu.make_async_copy(ybuf.at[slot, pl.ds(r, 1)],
                                     yt_hbm.at[pl.ds(row, 1)], ssem.at[slot])

    def for_rows(fn):
        def body(r, x):
            fn(r)
            return x

        lax.fori_loop(0, tm, body, 0, unroll=8)

    @pl.when(i == 0)
    def _():
        ybuf[1] = jnp.zeros(ybuf.shape[1:], ybuf.dtype)
        for_rows(lambda r: gather(src_cur[0, r], r, 0).start())

    for_rows(lambda r: gather(src_cur[0, r], r, s).wait())
    xn = _rms(xbuf[s], g_ref[...]).astype(BF16)

    def between(c, n_chunks):
        per = -(-tm // (n_chunks // 2))
        for r in range(c * per, min((c + 1) * per, tm)):
            gather(src_nxt[0, r], r, o).start()
            scatter(orow_prv[0, r], r, o).start()

    ybuf[s] = _swiglu_acc(xn, wgu_ref, wd_ref, wd_ref.shape[0], between)
    for_rows(lambda r: scatter(orow_prv[0, r], r, o).wait())

    @pl.when(i == last)
    def _():
        for_rows(lambda r: gather(src_nxt[0, r], r, o).wait())
        for_rows(lambda r: scatter(orow_cur[0, r], r, s).start())
        for_rows(lambda r: scatter(orow_cur[0, r], r, s).wait())


def _experts(tile_expert, src_tok, out_row, h, g, wgu, wd, yt_rows):
    n, _, tm = src_tok.shape
    d = h.shape[1]
    d_ff = wd.shape[1]
    smem = functools.partial(pl.BlockSpec, (None, 1, tm), memory_space=pltpu.SMEM)
    grid_spec = pltpu.PrefetchScalarGridSpec(
        num_scalar_prefetch=1,
        grid=(n,),
        in_specs=[
            smem(lambda i, te: (i, 0, 0)),
            smem(lambda i, te: (jnp.minimum(i + 1, n - 1), 0, 0)),
            smem(lambda i, te: (i, 0, 0)),
            smem(lambda i, te: (i + 1, 0, 0)),
            pl.BlockSpec(memory_space=pl.ANY),
            pl.BlockSpec(g.shape, lambda i, te: (0, 0), pipeline_mode=pl.Buffered(1)),
            pl.BlockSpec((None, d, 2 * d_ff), lambda i, te: (te[i], 0, 0),
                         pipeline_mode=pl.Buffered(1)),
            pl.BlockSpec((None, d_ff, d), lambda i, te: (te[i], 0, 0),
                         pipeline_mode=pl.Buffered(1)),
        ],
        out_specs=pl.BlockSpec(memory_space=pl.ANY),
        scratch_shapes=[pltpu.VMEM((2, tm, d), F32), pltpu.VMEM((2, tm, d), F32),
                        pltpu.SemaphoreType.DMA((2,)), pltpu.SemaphoreType.DMA((2,))],
    )
    return pl.pallas_call(
        _expert_kernel,
        grid_spec=grid_spec,
        out_shape=jax.ShapeDtypeStruct((yt_rows, d), F32),
        compiler_params=_params(("arbitrary",)),
        name="moe_experts",
    )(tile_expert, src_tok, src_tok, out_row, out_row, h, g, wgu, wd)


def _combine_kernel(h_ref, pk_ref, y1_ref, y2_ref, o_ref):
    pk = pk_ref[...]
    o_ref[...] = h_ref[...] + pk[:, 2:3] * y1_ref[...] + pk[:, 3:4] * y2_ref[...]


def _combine(h, packed, yt, tm):
    t, d = h.shape
    nb = t // tm
    return pl.pallas_call(
        _combine_kernel,
        grid=(nb,),
        in_specs=[pl.BlockSpec((tm, d), lambda i: (i, 0)),
                  pl.BlockSpec((tm, LANES), lambda i: (i, 0)),
                  pl.BlockSpec((tm, d), lambda i: (i, 0)),
                  pl.BlockSpec((tm, d), lambda i: (nb + i, 0))],
        out_specs=pl.BlockSpec((tm, d), lambda i: (i, 0)),
        out_shape=jax.ShapeDtypeStruct((t, d), F32),
        compiler_params=_params(("parallel",)),
        name="moe_combine",
    )(h, packed, yt, yt)


def _rel_bucket(rel):
    nb = REL_BUCKETS // 2
    max_exact = nb // 2
    n = jnp.abs(rel)
    nf = jnp.maximum(n, 1).astype(F32)
    large = max_exact + (jnp.log(nf / max_exact) / math.log(REL_MAX_DIST / max_exact)
                         * (nb - max_exact)).astype(jnp.int32)
    large = jnp.minimum(large, nb - 1)
    return jnp.where(rel > 0, nb, 0) + jnp.where(n < max_exact, n, large)


def _near_bias(rel_bias):
    t = ATT_T
    qp = jnp.arange(t)[:, None]
    kp = jnp.arange(t)[None, :]

    def lookup(rel):
        onehot = jax.nn.one_hot(_rel_bucket(rel), REL_BUCKETS, dtype=F32)
        return jnp.einsum('...b,bm->...m', onehot, rel_bias, precision=lax.Precision.HIGHEST)

    far = lookup(jnp.full((), -(2 * t), jnp.int32))
    prev = (lookup(kp - t - qp) - far) * LOG2E
    diag = (lookup(kp - qp) - far) * LOG2E
    diag = jnp.where(((kp // CHUNK) <= (qp // CHUNK))[:, :, None], diag, NEG_BIG)
    both = jnp.stack([prev, diag], axis=0).reshape(2, t, t, DIFF_HEADS, 2)
    return both.transpose(3, 0, 2, 4, 1).reshape(DIFF_HEADS, 2, t, 2 * t).astype(F32)


def kernel(x, rel_bias, norm1_g, w_in, pool_w, pool_scale, conv_w, sgu_ln_g, sgu_w, sgu_b, q_norm_g, k_norm_g, diff_lambda, subln_g, w_branch_pool, w_branch_conv, w_branch_sgu, w_branch_attn, w_out, norm2_g, ffn_w_gate_up, ffn_w_down, router_w, moe_w_gate_up, moe_w_down):
    b, s, d = x.shape
    t = b * s
    depth = w_in.shape[0]
    pw = pool_scale.shape[1]
    cw = conv_w.shape[2]
    sw = sgu_ln_g.shape[1]
    aw = w_branch_attn.shape[1]
    mix_cols = pw + 3 * cw + 2 * sw + 3 * aw
    nb = REL_BUCKETS // 2
    assert nb // 2 + int(math.log((ATT_T + 1) / (nb // 2)) / math.log(REL_MAX_DIST / (nb // 2))
                         * (nb - nb // 2)) >= nb - 1
    q_col = (pw + 3 * cw + 2 * sw) // LANES
    k_col = q_col + aw // LANES
    v_col = k_col + aw // LANES
    tm = min(512, t)
    ts = min(512, s)

    bias_near = _near_bias(rel_bias)
    tri = jnp.tril(jnp.ones((SGU_SEG, SGU_SEG), bool))
    gd = pw // POOL_GROUPS

    h = x.reshape(t, d)
    for layer in range(depth):
        lam_init = 0.8 - 0.6 * math.exp(-0.3 * layer)
        w_mix = w_in[layer, :, :mix_cols].astype(BF16)
        w_gate = w_in[layer, :, mix_cols:].astype(BF16)
        poolw_bd = jnp.zeros((pw, pw), F32)
        for g in range(POOL_GROUPS):
            poolw_bd = poolw_bd.at[g * gd:(g + 1) * gd, g * gd:(g + 1) * gd].set(pool_w[layer, g])
        sguw_cat = jnp.where(tri[None], sgu_w[layer], 0.0).transpose(1, 0, 2).reshape(
            SGU_SEG, SGU_GROUPS * SGU_SEG).astype(BF16)
        sgub_full = jnp.repeat(sgu_b[layer].T, sw // SGU_GROUPS, axis=1)
        wb = jnp.concatenate([w_branch_pool[layer], w_branch_conv[layer], w_branch_sgu[layer],
                              w_branch_attn[layer]], axis=0).astype(BF16)

        z = _in_proj(h, norm1_g[layer][None], w_mix, tm)
        z3 = z.reshape(b, s, mix_cols)
        y_abc = _local_mix(z3, poolw_bd.astype(BF16), pool_scale[layer][None], conv_w[layer],
                           sgu_ln_g[layer][None], sguw_cat, sgub_full, ts, pw, cw, sw)
        y_d = _diff_attn(z3, bias_near, jnp.tile(q_norm_g[layer], 2)[None],
                         jnp.tile(k_norm_g[layer], 2)[None], diff_lambda[layer],
                         subln_g[layer][None], lam_init, q_col, k_col, v_col)
        h = _merge(h, y_abc.reshape(t, -1), y_d.reshape(t, -1), norm1_g[layer][None], w_gate, wb,
                   w_out[layer].astype(BF16), tm, (pw, cw, sw, aw))

        g2 = norm2_g[layer][None]
        if layer % 2 == 0:
            h = _ffn(h, g2, ffn_w_gate_up[layer // 2].astype(BF16),
                     ffn_w_down[layer // 2].astype(BF16), tm)
        else:
            li = layer // 2
            rw_pad = jnp.zeros((d, LANES), F32).at[:, :N_EXPERTS].set(router_w[li])
            rw_hi = rw_pad.astype(BF16)
            rw_split = jnp.stack([rw_hi, (rw_pad - rw_hi.astype(F32)).astype(BF16)])
            packed, totals = _router(h, g2, rw_split, tm)
            n_e = totals[0, :N_EXPERTS].astype(jnp.int32)
            n_pad = ((n_e + tm - 1) // tm) * tm
            ends = jnp.cumsum(n_pad)
            starts = ends - n_pad
            e1 = packed[:, 0].astype(jnp.int32)
            e2 = packed[:, 1].astype(jnp.int32)
            eids = jnp.arange(N_EXPERTS)[None, :]
            dest1 = (jnp.sum(jnp.where(e1[:, None] == eids, starts[None, :], 0), axis=1)
                     + packed[:, 4].astype(jnp.int32))
            dest2 = (jnp.sum(jnp.where(e2[:, None] == eids, starts[None, :], 0), axis=1)
                     + packed[:, 5].astype(jnp.int32))
            rows = 2 * t + N_EXPERTS * tm
            n_tiles = rows // tm
            tile_expert = jnp.minimum(
                jnp.sum((jnp.arange(n_tiles)[:, None] * tm) >= ends[None, :], axis=1),
                N_EXPERTS - 1).astype(jnp.int32)
            ch = min(4096, t)
            dest_a = jnp.concatenate([dest1, dest2]).reshape(2 * t // ch, 1, ch)
            src = _invert(dest_a, rows)
            is_pad = src < 0
            src_tok = jnp.where(is_pad, 0, jnp.where(src >= t, src - t, src))
            pad_rank = jnp.cumsum(is_pad.astype(jnp.int32)) - 1
            out_row = jnp.where(is_pad, 2 * t + tm + pad_rank, src)
            spare = 2 * t + jnp.arange(tm, dtype=jnp.int32)
            yt = _experts(tile_expert, src_tok.reshape(n_tiles, 1, tm),
                          jnp.concatenate([spare, out_row]).reshape(n_tiles + 1, 1, tm), h, g2,
                          moe_w_gate_up[li].astype(BF16), moe_w_down[li].astype(BF16),
                          rows + tm)
            h = _combine(h, packed, yt, tm)
    return h.reshape(b, s, d)
```

```python
import functools
import math

import jax
import jax.numpy as jnp
import numpy as np
from jax import lax
from jax.experimental import pallas as pl
from jax.experimental.pallas import tpu as pltpu

F32 = jnp.float32
BF16 = jnp.bfloat16

NORM_EPS = 1e-6
CHUNK = 64
POOL_WINDOWS = (2, 4, 8, 16)
POOL_GROUPS = 4
CONV_K = 3
SGU_GROUPS = 4
SGU_SEG = 128
DIFF_HEADS = 4
DIFF_QK_DIM = 64
DIFF_V_DIM = 128
REL_BUCKETS = 32
REL_MAX_DIST = 128
N_EXPERTS = 8
LANES = 128
V7X_VMEM_BYTES = 64 * 1024 * 1024
VMEM_LIMIT = V7X_VMEM_BYTES - 8 * 1024 * 1024
NEG_BIG = -1e30
LOG2E = math.log2(math.e)

HALO = 16
ATT_T = 256
ATT_HEADS_PER_STEP = 2
FF_CHUNK = 256
WEIGHT_STAGE = 512


def _rms(x, g):
    return x * lax.rsqrt(jnp.mean(x * x, axis=-1, keepdims=True) + NORM_EPS) * g


def _resident(shape):
    nd = len(shape)
    return pl.BlockSpec(shape, lambda *_: (0,) * nd, pipeline_mode=pl.Buffered(1))


def _params(sem):
    return pltpu.CompilerParams(dimension_semantics=sem, vmem_limit_bytes=VMEM_LIMIT)


def _hi_lo(x):
    hi = x.astype(BF16)
    return hi, (x - hi.astype(F32)).astype(BF16)


def _in_proj_kernel(x_ref, g_ref, w_ref, wvt_ref, o_ref, vt_ref, *, n_chunk):
    xn = _rms(x_ref[...], g_ref[...]).astype(BF16)
    n = o_ref.shape[1]
    for j in range(n // n_chunk):
        sl = slice(j * n_chunk, (j + 1) * n_chunk)
        o_ref[:, sl] = jnp.dot(xn, w_ref[:, sl], preferred_element_type=F32).astype(o_ref.dtype)
    vt_ref[...] = lax.dot_general(wvt_ref[...], xn, (((1,), (1,)), ((), ())),
                                  preferred_element_type=F32).astype(vt_ref.dtype)


def _in_proj(h, g, w, w_vt, tm, seq):
    t, d = h.shape
    n = w.shape[1]
    nv = w_vt.shape[0]
    per_seq = seq // tm
    return pl.pallas_call(
        functools.partial(_in_proj_kernel, n_chunk=512),
        grid=(t // tm,),
        in_specs=[pl.BlockSpec((tm, d), lambda i: (i, 0)), _resident((1, d)), _resident((d, n)),
                  _resident(w_vt.shape)],
        out_specs=[pl.BlockSpec((tm, n), lambda i: (i, 0)),
                   pl.BlockSpec((None, nv, tm), lambda i: (i // per_seq, 0, i % per_seq))],
        out_shape=[jax.ShapeDtypeStruct((t, n), BF16),
                   jax.ShapeDtypeStruct((t // seq, nv, seq), BF16)],
        compiler_params=_params(("parallel",)),
        name="in_proj",
    )(h, g, w, w_vt)


def _local_mix_kernel(z_ref, halo_ref, poolw_ref, pscale_ref, convw_ref, lng_ref, sguw_ref,
                      sgub_ref, o_ref, *, pw, cw):
    ts = z_ref.shape[0]
    i = pl.program_id(1)
    z = z_ref[...].astype(F32)
    halo = halo_ref[...].astype(F32)
    halo = jnp.where(i > 0, halo, 0.0)
    ext = jnp.concatenate([halo[:, :pw + 3 * cw], z[:, :pw + 3 * cw]], axis=0)
    rows = ext.shape[0]

    def back(x, k):
        return pltpu.roll(x, k, axis=0)

    a = ext[:, :pw]
    s2 = a + back(a, 1)
    s4 = s2 + back(s2, 2)
    s8 = s4 + back(s4, 4)
    s16 = s8 + back(s8, 8)
    lane = lax.broadcasted_iota(jnp.int32, (rows, pw), 1)
    grp = lane // (pw // POOL_GROUPS)
    win_sum = jnp.where(grp == 0, s2, jnp.where(grp == 1, s4, jnp.where(grp == 2, s8, s16)))
    win = jnp.where(grp == 0, 2, jnp.where(grp == 1, 4, jnp.where(grp == 2, 8, 16)))
    pos = i * ts - HALO + lax.broadcasted_iota(jnp.int32, (rows, pw), 0)
    count = jnp.minimum(pos + 1, win).astype(F32)
    pooled = (win_sum / jnp.maximum(count, 1.0) - a)[HALO:]
    y_a = jnp.dot(pooled.astype(BF16), poolw_ref[...], preferred_element_type=F32) * pscale_ref[...]
    o_ref[:, 0:pw] = y_a.astype(o_ref.dtype)

    b_gate = z[:, pw:pw + cw]
    zc = ext[:, pw + cw:pw + 2 * cw] * ext[:, pw + 2 * cw:pw + 3 * cw]
    conv = (convw_ref[0:1, :] * back(zc, 2) + convw_ref[1:2, :] * back(zc, 1)
            + convw_ref[2:3, :] * zc)[HALO:]
    o_ref[:, pw:pw + cw] = (b_gate * conv).astype(o_ref.dtype)

    sw = (z.shape[1] - pw - 3 * cw) // 2
    zc_uv = z[:, pw + 3 * cw:]
    uv = 0.5 * zc_uv * (1.0 + lax.erf(zc_uv * math.sqrt(0.5)))
    u = uv[:, :sw]
    v = uv[:, sw:]
    mu = jnp.mean(v, axis=-1, keepdims=True)
    var = jnp.mean(jnp.square(v - mu), axis=-1, keepdims=True)
    vn = (v - mu) * lax.rsqrt(var + NORM_EPS) * lng_ref[...]
    glane = lax.broadcasted_iota(jnp.int32, (SGU_SEG, sw), 1) // (sw // SGU_GROUPS)
    wcat = sguw_ref[...]
    bias = sgub_ref[...]
    for n in range(ts // SGU_SEG):
        seg = vn[n * SGU_SEG:(n + 1) * SGU_SEG]
        rhs = jnp.concatenate(
            [jnp.where(glane == g, seg, 0.0) for g in range(SGU_GROUPS)], axis=0).astype(BF16)
        s = jnp.dot(wcat, rhs, preferred_element_type=F32) + bias
        o_ref[n * SGU_SEG:(n + 1) * SGU_SEG, pw + cw:pw + cw + sw] = (
            u[n * SGU_SEG:(n + 1) * SGU_SEG] * s).astype(o_ref.dtype)


def _local_mix(z3, poolw_bd, pscale, convw, lng, sguw_cat, sgub_full, ts, pw, cw, sw):
    b, s, _ = z3.shape
    cols = pw + 3 * cw + 2 * sw
    hb = ts // HALO
    return pl.pallas_call(
        functools.partial(_local_mix_kernel, pw=pw, cw=cw),
        grid=(b, s // ts),
        in_specs=[
            pl.BlockSpec((None, ts, cols), lambda bi, i: (bi, i, 0)),
            pl.BlockSpec((None, HALO, cols), lambda bi, i: (bi, jnp.maximum(i * hb - 1, 0), 0)),
            _resident(poolw_bd.shape), _resident(pscale.shape), _resident(convw.shape),
            _resident(lng.shape), _resident(sguw_cat.shape), _resident(sgub_full.shape),
        ],
        out_specs=pl.BlockSpec((None, ts, pw + cw + sw), lambda bi, i: (bi, i, 0)),
        out_shape=jax.ShapeDtypeStruct((b, s, pw + cw + sw), BF16),
        compiler_params=_params(("parallel", "parallel")),
        name="local_mix",
    )(z3, z3, poolw_bd, pscale, convw, lng, sguw_cat, sgub_full)


def _diff_attn_kernel(q_ref, k_ref, vt_ref, bias_ref, qg_ref, kg_ref, lam_ref, sg_ref, o_ref,
                      kn_ref, qs_ref, st_ref, m_ref, *, lam_init):
    tq = q_ref.shape[0]
    hp = qs_ref.shape[0]
    nt = k_ref.shape[0] // tq
    hw = 2 * DIFF_QK_DIM
    i = pl.program_id(2)
    half = lax.broadcasted_iota(jnp.int32, (1, hw), 1) < DIFF_QK_DIM
    same_map = (lax.broadcasted_iota(jnp.int32, (hw, hw), 0) // DIFF_QK_DIM
                == lax.broadcasted_iota(jnp.int32, (hw, hw), 1) // DIFF_QK_DIM)
    ones_map = jnp.where(same_map, 1.0, 0.0).astype(BF16)

    def qk_norm_mxu(x, g):
        sq_hi, sq_lo = _hi_lo(x * x)
        ss = (jnp.dot(sq_hi, ones_map, preferred_element_type=F32)
              + jnp.dot(sq_lo, ones_map, preferred_element_type=F32))
        return x * lax.rsqrt(ss * (1.0 / DIFF_QK_DIM) + NORM_EPS) * g

    def qk_norm(x, g):
        sq = x * x
        ss0 = jnp.sum(jnp.where(half, sq, 0.0), axis=-1, keepdims=True)
        ss1 = jnp.sum(jnp.where(half, 0.0, sq), axis=-1, keepdims=True)
        r0 = lax.rsqrt(ss0 * (1.0 / DIFF_QK_DIM) + NORM_EPS)
        r1 = lax.rsqrt(ss1 * (1.0 / DIFF_QK_DIM) + NORM_EPS)
        return x * jnp.where(half, r0, r1) * g

    @pl.when(i == 0)
    def _():
        for h in range(hp):
            cols = slice(h * hw, (h + 1) * hw)
            for j in range(nt):
                rows = slice(j * tq, (j + 1) * tq)
                kn_ref[h, rows, :] = qk_norm_mxu(k_ref[rows, cols].astype(F32),
                                                 kg_ref[...]).astype(BF16)

    def step(c):
        cur, prv = c % 2, 1 - c % 2
        scoring = c < nt
        if scoring:
            for h in range(hp):
                qn = (qk_norm(q_ref[:, h * hw:(h + 1) * hw].astype(F32), qg_ref[...])
                      * (DIFF_QK_DIM ** -0.5 * LOG2E))
                qs_ref[h, 0:tq, :] = jnp.where(half, qn, 0.0).astype(BF16)
                qs_ref[h, tq:2 * tq, :] = jnp.where(half, 0.0, qn).astype(BF16)
        m_new = [jnp.full((1, 2 * tq), NEG_BIG, F32) for _ in range(hp)]
        m_old = [m_ref[prv, h] for h in range(hp)] if c >= 1 else None
        l = [jnp.zeros((1, 2 * tq), F32) for _ in range(hp)]
        acc = [jnp.zeros((DIFF_V_DIM, 2 * tq), F32) for _ in range(hp)]
        for j in range(c + 1):
            rows = slice(j * tq, (j + 1) * tq)
            for h in range(hp):
                if scoring:
                    st = lax.dot_general(kn_ref[h, rows, :], qs_ref[h],
                                         (((1,), (1,)), ((), ())), preferred_element_type=F32)
                    if j >= c - 1:
                        st = st + bias_ref[h, j - (c - 1)]
                    st_ref[cur, h, rows, :] = st
                    m_new[h] = jnp.maximum(m_new[h], jnp.max(st, axis=0, keepdims=True))
                if j < c:
                    p = jnp.exp2(st_ref[prv, h, rows, :] - m_old[h])
                    l[h] = l[h] + jnp.sum(p, axis=0, keepdims=True)
                    acc[h] = acc[h] + jnp.dot(vt_ref[h * DIFF_V_DIM:(h + 1) * DIFF_V_DIM, rows],
                                              p.astype(BF16), preferred_element_type=F32)
        if scoring:
            for h in range(hp):
                m_ref[cur, h] = m_new[h]
        if c >= 1:
            lp = lam_ref[...]
            lam = (jnp.exp(jnp.sum(lp[0:1] * lp[1:2], axis=-1, keepdims=True))
                   - jnp.exp(jnp.sum(lp[2:3] * lp[3:4], axis=-1, keepdims=True)) + lam_init)
            for h in range(hp):
                o = (acc[h][:, :tq] * (1.0 / l[h][:, :tq])
                     - acc[h][:, tq:] * (lam / l[h][:, tq:]))
                o = o * lax.rsqrt(jnp.mean(o * o, axis=0, keepdims=True) + NORM_EPS)
                o_ref[:, h * DIFF_V_DIM:(h + 1) * DIFF_V_DIM] = (
                    o.T * (sg_ref[...] * (1.0 - lam_init))).astype(o_ref.dtype)

    for c in range(nt + 1):
        pl.when(i == c)(functools.partial(step, c))


def _diff_attn(z3, vt, bias_near, qg2, kg2, lam_p, subln_g, lam_init, q_col, k_col):
    b, s, _ = z3.shape
    tq = ATT_T
    hp = ATT_HEADS_PER_STEP
    nt = s // tq
    hw = 2 * DIFF_QK_DIM
    return pl.pallas_call(
        functools.partial(_diff_attn_kernel, lam_init=lam_init),
        grid=(b, DIFF_HEADS // hp, nt + 1),
        in_specs=[
            pl.BlockSpec((None, tq, hp * hw),
                         lambda bi, g, i: (bi, jnp.minimum(i, nt - 1), q_col // hp + g)),
            pl.BlockSpec((None, s, hp * hw), lambda bi, g, i: (bi, 0, k_col // hp + g)),
            pl.BlockSpec((None, hp * DIFF_V_DIM, s), lambda bi, g, i: (bi, g, 0)),
            pl.BlockSpec((hp, 2, tq, 2 * tq), lambda bi, g, i: (g, 0, 0, 0)),
            _resident(qg2.shape), _resident(kg2.shape), _resident(lam_p.shape),
            _resident(subln_g.shape),
        ],
        out_specs=pl.BlockSpec((None, tq, hp * DIFF_V_DIM),
                               lambda bi, g, i: (bi, jnp.maximum(i - 1, 0), g)),
        out_shape=jax.ShapeDtypeStruct((b, s, DIFF_HEADS * DIFF_V_DIM), BF16),
        scratch_shapes=[
            pltpu.VMEM((hp, s, hw), BF16),
            pltpu.VMEM((hp, 2 * tq, hw), BF16),
            pltpu.VMEM((2, hp, s, 2 * tq), F32),
            pltpu.VMEM((2, hp, 1, 2 * tq), F32),
        ],
        compiler_params=_params(("parallel", "parallel", "arbitrary")),
        name="diff_attn",
    )(z3, z3, vt, bias_near, qg2, kg2, lam_p, subln_g)


def _merge_kernel(h_ref, yabc_ref, yd_ref, g_ref, wg_ref, wb_ref, wo_ref, o_ref, *, widths):
    h = h_ref[...]
    d = h.shape[1]
    xn = _rms(h, g_ref[...]).astype(BF16)
    merged = None
    off = 0
    yoff = 0
    for bi, w in enumerate(widths):
        gate = jax.nn.sigmoid(jnp.dot(xn, wg_ref[:, bi * d:(bi + 1) * d],
                                      preferred_element_type=F32))
        if bi < len(widths) - 1:
            y = yabc_ref[:, yoff:yoff + w]
            yoff += w
        else:
            y = yd_ref[...]
        proj = jnp.dot(y, wb_ref[off:off + w, :], preferred_element_type=F32)
        off += w
        merged = gate * proj if merged is None else merged + gate * proj
    o_ref[...] = h + jnp.dot(merged.astype(BF16), wo_ref[...], preferred_element_type=F32)


def _merge(h, y_abc, y_d, g, wg, wb, wo, tm, widths):
    t, d = h.shape
    return pl.pallas_call(
        functools.partial(_merge_kernel, widths=widths),
        grid=(t // tm,),
        in_specs=[
            pl.BlockSpec((tm, d), lambda i: (i, 0)),
            pl.BlockSpec((tm, y_abc.shape[1]), lambda i: (i, 0)),
            pl.BlockSpec((tm, y_d.shape[1]), lambda i: (i, 0)),
            _resident(g.shape), _resident(wg.shape), _resident(wb.shape), _resident(wo.shape),
        ],
        out_specs=pl.BlockSpec((tm, d), lambda i: (i, 0)),
        out_shape=jax.ShapeDtypeStruct((t, d), F32),
        compiler_params=_params(("parallel",)),
        name="merge",
    )(h, y_abc, y_d, g, wg, wb, wo)


def _swiglu_acc(xn, wgu_ref, wd_ref, d_ff, between=None):
    acc = None
    n_chunks = d_ff // FF_CHUNK
    for c in range(n_chunks):
        lo = c * FF_CHUNK
        g = jnp.dot(xn, wgu_ref[:, lo:lo + FF_CHUNK], preferred_element_type=F32)
        u = jnp.dot(xn, wgu_ref[:, d_ff + lo:d_ff + lo + FF_CHUNK], preferred_element_type=F32)
        act = (g * jax.nn.sigmoid(g) * u).astype(BF16)
        part = jnp.dot(act, wd_ref[lo:lo + FF_CHUNK, :], preferred_element_type=F32)
        acc = part if acc is None else acc + part
        if between is not None:
            between(c, n_chunks)
    return acc


def _ffn_kernel(h_ref, g_ref, wgu_ref, wd_ref, o_ref):
    h = h_ref[...]
    xn = _rms(h, g_ref[...]).astype(BF16)
    o_ref[...] = h + _swiglu_acc(xn, wgu_ref, wd_ref, wd_ref.shape[0])


def _ffn(h, g, wgu, wd, tm):
    t, d = h.shape
    return pl.pallas_call(
        _ffn_kernel,
        grid=(t // tm,),
        in_specs=[pl.BlockSpec((tm, d), lambda i: (i, 0)), _resident(g.shape),
                  _resident(wgu.shape), _resident(wd.shape)],
        out_specs=pl.BlockSpec((tm, d), lambda i: (i, 0)),
        out_shape=jax.ShapeDtypeStruct((t, d), F32),
        compiler_params=_params(("parallel",)),
        name="ffn",
    )(h, g, wgu, wd)


def _router_kernel(h_ref, g_ref, rw_ref, o_ref, tot_ref, carry_ref):
    tm = h_ref.shape[0]

    @pl.when(pl.program_id(0) == 0)
    def _():
        carry_ref[...] = jnp.zeros(carry_ref.shape, F32)

    hn = _rms(h_ref[...], g_ref[...])
    hn_hi = hn.astype(BF16)
    hn_lo = (hn - hn_hi.astype(F32)).astype(BF16)
    logits = (jnp.dot(hn_hi, rw_ref[0], preferred_element_type=F32)
              + jnp.dot(hn_lo, rw_ref[0], preferred_element_type=F32)
              + jnp.dot(hn_hi, rw_ref[1], preferred_element_type=F32))
    lane = lax.broadcasted_iota(jnp.int32, (tm, LANES), 1)
    logits = jnp.where(lane < N_EXPERTS, logits, NEG_BIG)
    v1 = jnp.max(logits, axis=-1, keepdims=True)
    i1 = jnp.min(jnp.where(logits == v1, lane, LANES), axis=-1, keepdims=True)
    rest = jnp.where(lane == i1, NEG_BIG, logits)
    v2 = jnp.max(rest, axis=-1, keepdims=True)
    i2 = jnp.min(jnp.where(rest == v2, lane, LANES), axis=-1, keepdims=True)
    e = jnp.exp(v2 - v1)
    w1 = 1.0 / (1.0 + e)
    w2 = e / (1.0 + e)
    cnt = jnp.where((lane == i1) | (lane == i2), 1.0, 0.0)
    r = lax.broadcasted_iota(jnp.int32, (tm, tm), 0)
    c = lax.broadcasted_iota(jnp.int32, (tm, tm), 1)
    tri = jnp.where(c < r, 1.0, 0.0).astype(BF16)
    excl = jnp.dot(tri, cnt.astype(BF16), preferred_element_type=F32) + carry_ref[...]
    rank1 = jnp.sum(jnp.where(lane == i1, excl, 0.0), axis=-1, keepdims=True)
    rank2 = jnp.sum(jnp.where(lane == i2, excl, 0.0), axis=-1, keepdims=True)
    carry_ref[...] = carry_ref[...] + jnp.sum(cnt, axis=0, keepdims=True)
    tot_ref[...] = carry_ref[...]
    packed = jnp.where(lane == 0, i1.astype(F32), jnp.where(lane == 1, i2.astype(F32),
             jnp.where(lane == 2, w1, jnp.where(lane == 3, w2,
             jnp.where(lane == 4, rank1, jnp.where(lane == 5, rank2, 0.0))))))
    o_ref[...] = packed


def _router(h, g, rw_pad, tm):
    t, d = h.shape
    return pl.pallas_call(
        _router_kernel,
        grid=(t // tm,),
        in_specs=[pl.BlockSpec((tm, d), lambda i: (i, 0)), _resident(g.shape),
                  _resident(rw_pad.shape)],
        out_specs=[pl.BlockSpec((tm, LANES), lambda i: (i, 0)),
                   pl.BlockSpec((1, LANES), lambda i: (0, 0))],
        out_shape=[jax.ShapeDtypeStruct((t, LANES), F32), jax.ShapeDtypeStruct((1, LANES), F32)],
        scratch_shapes=[pltpu.VMEM((1, LANES), F32)],
        compiler_params=_params(("arbitrary",)),
        name="router",
    )(h, g, rw_pad)


def _invert_kernel(dest_ref, src_ref):
    c = pl.program_id(0)
    ch = dest_ref.shape[1]

    @pl.when(c == 0)
    def _():
        def clear(r, x):
            src_ref[r] = -1
            return x

        lax.fori_loop(0, src_ref.shape[0], clear, 0, unroll=16)

    base = c * ch

    def place(a, x):
        src_ref[dest_ref[0, a]] = base + a
        return x

    lax.fori_loop(0, ch, place, 0, unroll=16)


def _invert(dest, rows):
    nc, _, ch = dest.shape
    return pl.pallas_call(
        _invert_kernel,
        grid=(nc,),
        in_specs=[pl.BlockSpec((None, 1, ch), lambda c: (c, 0, 0), memory_space=pltpu.SMEM)],
        out_specs=pl.BlockSpec(memory_space=pltpu.SMEM),
        out_shape=jax.ShapeDtypeStruct((rows,), jnp.int32),
        compiler_params=pltpu.CompilerParams(dimension_semantics=("arbitrary",)),
        name="moe_invert",
    )(dest)


def _expert_kernel(te_ref, src_cur, src_nxt, orow_prv, orow_cur, h_hbm, g_ref, wgu_hbm, wd_hbm,
                   yt_hbm, xbuf, ybuf, wgu_ref, wd_ref, stage_gu, stage_d, gsem, ssem, wsem):
    i = pl.program_id(0)
    last = pl.num_programs(0) - 1
    tm = xbuf.shape[1]
    s = lax.rem(i, 2)
    o = 1 - s
    e = te_ref[i]

    @pl.when((i == 0) | (e != te_ref[jnp.maximum(i - 1, 0)]))
    def _():
        def stream(copy, n, store):
            copy(0, 0).start()
            for c in range(n):
                if c + 1 < n:
                    copy(c + 1, (c + 1) % 2).start()
                copy(c, c % 2).wait()
                store(c, c % 2)

        wc = stage_gu.shape[2]
        rc = stage_d.shape[1]

        def copy_gu(c, slot):
            return pltpu.make_async_copy(wgu_hbm.at[e, :, pl.ds(c * wc, wc)], stage_gu.at[slot],
                                         wsem.at[slot])

        def store_gu(c, slot):
            wgu_ref[:, c * wc:(c + 1) * wc] = stage_gu[slot].astype(BF16)

        def copy_d(c, slot):
            return pltpu.make_async_copy(wd_hbm.at[e, pl.ds(c * rc, rc), :], stage_d.at[slot],
                                         wsem.at[slot])

        def store_d(c, slot):
            wd_ref[c * rc:(c + 1) * rc, :] = stage_d[slot].astype(BF16)

        stream(copy_gu, wgu_ref.shape[1] // wc, store_gu)
        stream(copy_d, wd_ref.shape[0] // rc, store_d)

    def gather(tok, r, slot):
        return pltpu.make_async_copy(h_hbm.at[pl.ds(tok, 1)],
                                     xbuf.at[slot, pl.ds(r, 1)], gsem.at[slot])

    def scatter(row, r, slot):
        return pltpu.make_async_copy(ybuf.at[slot, pl.ds(r, 1)],
                                     yt_hbm.at[pl.ds(row, 1)], ssem.at[slot])

    def for_rows(fn):
        def body(r, x):
            fn(r)
            return x

        lax.fori_loop(0, tm, body, 0, unroll=8)

    @pl.when(i == 0)
    def _():
        ybuf[1] = jnp.zeros(ybuf.shape[1:], ybuf.dtype)
        for_rows(lambda r: gather(src_cur[0, r], r, 0).start())

    for_rows(lambda r: gather(src_cur[0, r], r, s).wait())
    xn = _rms(xbuf[s], g_ref[...]).astype(BF16)

    def between(c, n_chunks):
        per = -(-tm // (n_chunks // 2))
        for r in range(c * per, min((c + 1) * per, tm)):
            gather(src_nxt[0, r], r, o).start()
            scatter(orow_prv[0, r], r, o).start()

    ybuf[s] = _swiglu_acc(xn, wgu_ref, wd_ref, wd_ref.shape[0], between)
    for_rows(lambda r: scatter(orow_prv[0, r], r, o).wait())

    @pl.when(i == last)
    def _():
        for_rows(lambda r: gather(src_nxt[0, r], r, o).wait())
        for_rows(lambda r: scatter(orow_cur[0, r], r, s).start())
        for_rows(lambda r: scatter(orow_cur[0, r], r, s).wait())


def _experts(tile_expert, src_tok, out_row, h, g, wgu, wd, yt_rows):
    n, _, tm = src_tok.shape
    d = h.shape[1]
    d_ff = wd.shape[1]
    smem = functools.partial(pl.BlockSpec, (None, 1, tm), memory_space=pltpu.SMEM)
    grid_spec = pltpu.PrefetchScalarGridSpec(
        num_scalar_prefetch=1,
        grid=(n,),
        in_specs=[
            smem(lambda i, te: (i, 0, 0)),
            smem(lambda i, te: (jnp.minimum(i + 1, n - 1), 0, 0)),
            smem(lambda i, te: (i, 0, 0)),
            smem(lambda i, te: (i + 1, 0, 0)),
            pl.BlockSpec(memory_space=pl.ANY),
            pl.BlockSpec(g.shape, lambda i, te: (0, 0), pipeline_mode=pl.Buffered(1)),
            pl.BlockSpec(memory_space=pl.ANY),
            pl.BlockSpec(memory_space=pl.ANY),
        ],
        out_specs=pl.BlockSpec(memory_space=pl.ANY),
        scratch_shapes=[pltpu.VMEM((2, tm, d), F32), pltpu.VMEM((2, tm, d), F32),
                        pltpu.VMEM((d, 2 * d_ff), BF16), pltpu.VMEM((d_ff, d), BF16),
                        pltpu.VMEM((2, d, WEIGHT_STAGE), F32), pltpu.VMEM((2, WEIGHT_STAGE, d), F32),
                        pltpu.SemaphoreType.DMA((2,)), pltpu.SemaphoreType.DMA((2,)),
                        pltpu.SemaphoreType.DMA((2,))],
    )
    return pl.pallas_call(
        _expert_kernel,
        grid_spec=grid_spec,
        out_shape=jax.ShapeDtypeStruct((yt_rows, d), F32),
        compiler_params=_params(("arbitrary",)),
        name="moe_experts",
    )(tile_expert, src_tok, src_tok, out_row, out_row, h, g, wgu, wd)


def _combine_kernel(h_ref, pk_ref, y1_ref, y2_ref, o_ref):
    pk = pk_ref[...]
    o_ref[...] = h_ref[...] + pk[:, 2:3] * y1_ref[...] + pk[:, 3:4] * y2_ref[...]


def _combine(h, packed, yt, tm):
    t, d = h.shape
    nb = t // tm
    return pl.pallas_call(
        _combine_kernel,
        grid=(nb,),
        in_specs=[pl.BlockSpec((tm, d), lambda i: (i, 0)),
                  pl.BlockSpec((tm, LANES), lambda i: (i, 0)),
                  pl.BlockSpec((tm, d), lambda i: (i, 0)),
                  pl.BlockSpec((tm, d), lambda i: (nb + i, 0))],
        out_specs=pl.BlockSpec((tm, d), lambda i: (i, 0)),
        out_shape=jax.ShapeDtypeStruct((t, d), F32),
        compiler_params=_params(("parallel",)),
        name="moe_combine",
    )(h, packed, yt, yt)


def _rel_bucket(rel):
    nb = REL_BUCKETS // 2
    max_exact = nb // 2
    n = jnp.abs(rel)
    nf = jnp.maximum(n, 1).astype(F32)
    large = max_exact + (jnp.log(nf / max_exact) / math.log(REL_MAX_DIST / max_exact)
                         * (nb - max_exact)).astype(jnp.int32)
    large = jnp.minimum(large, nb - 1)
    return jnp.where(rel > 0, nb, 0) + jnp.where(n < max_exact, n, large)


def _near_bias(rel_bias):
    t = ATT_T
    qp = jnp.arange(t)[:, None]
    kp = jnp.arange(t)[None, :]

    def lookup(rel):
        onehot = jax.nn.one_hot(_rel_bucket(rel), REL_BUCKETS, dtype=F32)
        return jnp.einsum('...b,bm->...m', onehot, rel_bias, precision=lax.Precision.HIGHEST)

    far = lookup(jnp.full((), -(2 * t), jnp.int32))
    prev = (lookup(kp - t - qp) - far) * LOG2E
    diag = (lookup(kp - qp) - far) * LOG2E
    diag = jnp.where(((kp // CHUNK) <= (qp // CHUNK))[:, :, None], diag, NEG_BIG)
    both = jnp.stack([prev, diag], axis=0).reshape(2, t, t, DIFF_HEADS, 2)
    return both.transpose(3, 0, 2, 4, 1).reshape(DIFF_HEADS, 2, t, 2 * t).astype(F32)


def kernel(x, rel_bias, norm1_g, w_in, pool_w, pool_scale, conv_w, sgu_ln_g, sgu_w, sgu_b, q_norm_g, k_norm_g, diff_lambda, subln_g, w_branch_pool, w_branch_conv, w_branch_sgu, w_branch_attn, w_out, norm2_g, ffn_w_gate_up, ffn_w_down, router_w, moe_w_gate_up, moe_w_down):
    b, s, d = x.shape
    t = b * s
    depth = w_in.shape[0]
    pw = pool_scale.shape[1]
    cw = conv_w.shape[2]
    sw = sgu_ln_g.shape[1]
    aw = w_branch_attn.shape[1]
    mix_cols = pw + 3 * cw + 2 * sw + 3 * aw
    nb = REL_BUCKETS // 2
    assert nb // 2 + int(math.log((ATT_T + 1) / (nb // 2)) / math.log(REL_MAX_DIST / (nb // 2))
                         * (nb - nb // 2)) >= nb - 1
    qk_off = pw + 3 * cw + 2 * sw
    z_cols = qk_off + 2 * aw
    q_col = qk_off // LANES
    k_col = q_col + aw // LANES
    tm = min(512, t)
    ts = min(512, s)

    bias_near = _near_bias(rel_bias)
    tri = jnp.tril(jnp.ones((SGU_SEG, SGU_SEG), bool))
    gd = pw // POOL_GROUPS

    h = x.reshape(t, d)
    for layer in range(depth):
        lam_init = 0.8 - 0.6 * math.exp(-0.3 * layer)
        w_mix = w_in[layer, :, :z_cols].astype(BF16)
        w_vt = w_in[layer, :, z_cols:mix_cols].T.astype(BF16)
        w_gate = w_in[layer, :, mix_cols:].astype(BF16)
        poolw_bd = jnp.zeros((pw, pw), F32)
        for g in range(POOL_GROUPS):
            poolw_bd = poolw_bd.at[g * gd:(g + 1) * gd, g * gd:(g + 1) * gd].set(pool_w[layer, g])
        sguw_cat = jnp.where(tri[None], sgu_w[layer], 0.0).transpose(1, 0, 2).reshape(
            SGU_SEG, SGU_GROUPS * SGU_SEG).astype(BF16)
        sgub_full = jnp.repeat(sgu_b[layer].T, sw // SGU_GROUPS, axis=1)
        wb = jnp.concatenate([w_branch_pool[layer], w_branch_conv[layer], w_branch_sgu[layer],
                              w_branch_attn[layer]], axis=0).astype(BF16)

        z, vt = _in_proj(h, norm1_g[layer][None], w_mix, w_vt, tm, s)
        z3 = z.reshape(b, s, z_cols)
        y_abc = _local_mix(z3, poolw_bd.astype(BF16), pool_scale[layer][None], conv_w[layer],
                           sgu_ln_g[layer][None], sguw_cat, sgub_full, ts, pw, cw, sw)
        y_d = _diff_attn(z3, vt, bias_near, jnp.tile(q_norm_g[layer], 2)[None],
                         jnp.tile(k_norm_g[layer], 2)[None], diff_lambda[layer],
                         subln_g[layer][None], lam_init, q_col, k_col)
        h = _merge(h, y_abc.reshape(t, -1), y_d.reshape(t, -1), norm1_g[layer][None], w_gate, wb,
                   w_out[layer].astype(BF16), tm, (pw, cw, sw, aw))

        g2 = norm2_g[layer][None]
        if layer % 2 == 0:
            h = _ffn(h, g2, ffn_w_gate_up[layer // 2].astype(BF16),
                     ffn_w_down[layer // 2].astype(BF16), tm)
        else:
            li = layer // 2
            rw_pad = jnp.zeros((d, LANES), F32).at[:, :N_EXPERTS].set(router_w[li])
            rw_hi = rw_pad.astype(BF16)
            rw_split = jnp.stack([rw_hi, (rw_pad - rw_hi.astype(F32)).astype(BF16)])
            packed, totals = _router(h, g2, rw_split, tm)
            n_e = totals[0, :N_EXPERTS].astype(jnp.int32)
            n_pad = ((n_e + tm - 1) // tm) * tm
            ends = jnp.cumsum(n_pad)
            starts = ends - n_pad
            e1 = packed[:, 0].astype(jnp.int32)
            e2 = packed[:, 1].astype(jnp.int32)
            eids = jnp.arange(N_EXPERTS)[None, :]
            dest1 = (jnp.sum(jnp.where(e1[:, None] == eids, starts[None, :], 0), axis=1)
                     + packed[:, 4].astype(jnp.int32))
            dest2 = (jnp.sum(jnp.where(e2[:, None] == eids, starts[None, :], 0), axis=1)
                     + packed[:, 5].astype(jnp.int32))
            rows = 2 * t + N_EXPERTS * tm
            n_tiles = rows // tm
            tile_expert = jnp.minimum(
                jnp.sum((jnp.arange(n_tiles)[:, None] * tm) >= ends[None, :], axis=1),
                N_EXPERTS - 1).astype(jnp.int32)
            ch = min(4096, t)
            dest_a = jnp.concatenate([dest1, dest2]).reshape(2 * t // ch, 1, ch)
            src = _invert(dest_a, rows)
            is_pad = src < 0
            src_tok = jnp.where(is_pad, 0, jnp.where(src >= t, src - t, src))
            pad_rank = jnp.cumsum(is_pad.astype(jnp.int32)) - 1
            out_row = jnp.where(is_pad, 2 * t + tm + pad_rank, src)
            spare = 2 * t + jnp.arange(tm, dtype=jnp.int32)
            yt = _experts(tile_expert, src_tok.reshape(n_tiles, 1, tm),
                          jnp.concatenate([spare, out_row]).reshape(n_tiles + 1, 1, tm), h, g2,
                          moe_w_gate_up[li], moe_w_down[li], rows + tm)
            h = _combine(h, packed, yt, tm)
    return h.reshape(b, s, d)
```

```python
import functools
import math

import jax
import jax.numpy as jnp
import numpy as np
from jax import lax
from jax.experimental import pallas as pl
from jax.experimental.pallas import tpu as pltpu

F32 = jnp.float32
BF16 = jnp.bfloat16

NORM_EPS = 1e-6
CHUNK = 64
POOL_WINDOWS = (2, 4, 8, 16)
POOL_GROUPS = 4
CONV_K = 3
SGU_GROUPS = 4
SGU_SEG = 128
DIFF_HEADS = 4
DIFF_QK_DIM = 64
DIFF_V_DIM = 128
REL_BUCKETS = 32
REL_MAX_DIST = 128
N_EXPERTS = 8
LANES = 128
V7X_VMEM_BYTES = 64 * 1024 * 1024
VMEM_LIMIT = V7X_VMEM_BYTES - 8 * 1024 * 1024
NEG_BIG = -1e30
LOG2E = math.log2(math.e)

HALO = 16
ATT_T = 256
ATT_HEADS_PER_STEP = 2
FF_CHUNK = 256
WEIGHT_STAGE = 512
DENSE_TM = 1024


def _rms(x, g):
    return x * lax.rsqrt(jnp.mean(x * x, axis=-1, keepdims=True) + NORM_EPS) * g


def _resident(shape):
    nd = len(shape)
    return pl.BlockSpec(shape, lambda *_: (0,) * nd, pipeline_mode=pl.Buffered(1))


def _params(sem):
    return pltpu.CompilerParams(dimension_semantics=sem, vmem_limit_bytes=VMEM_LIMIT)


def _hi_lo(x):
    hi = x.astype(BF16)
    return hi, (x - hi.astype(F32)).astype(BF16)


def _pack_halves(x):
    n = x.shape[1] // 2
    bits = lax.bitcast_convert_type(x, jnp.uint32) + jnp.uint32(0x8000)
    return (bits[:, :n] & jnp.uint32(0xFFFF0000)) | (bits[:, n:] >> 16)


def _unpack_halves(p):
    hi = lax.bitcast_convert_type(p & jnp.uint32(0xFFFF0000), F32)
    lo = lax.bitcast_convert_type(p << 16, F32)
    return jnp.concatenate([hi, lo], axis=1)


def _in_proj_kernel(x_ref, g_ref, w_ref, wvt_ref, o_ref, vt_ref, *, n_chunk):
    xn = _rms(x_ref[...], g_ref[...]).astype(BF16)
    n = o_ref.shape[1]
    for j in range(n // n_chunk):
        sl = slice(j * n_chunk, (j + 1) * n_chunk)
        o_ref[:, sl] = jnp.dot(xn, w_ref[:, sl], preferred_element_type=F32).astype(o_ref.dtype)
    vt_ref[...] = lax.dot_general(wvt_ref[...], xn, (((1,), (1,)), ((), ())),
                                  preferred_element_type=F32).astype(vt_ref.dtype)


def _in_proj(h, g, w, w_vt, tm, seq):
    t, d = h.shape
    n = w.shape[1]
    nv = w_vt.shape[0]
    per_seq = seq // tm
    return pl.pallas_call(
        functools.partial(_in_proj_kernel, n_chunk=512),
        grid=(t // tm,),
        in_specs=[pl.BlockSpec((tm, d), lambda i: (i, 0)), _resident((1, d)), _resident((d, n)),
                  _resident(w_vt.shape)],
        out_specs=[pl.BlockSpec((tm, n), lambda i: (i, 0)),
                   pl.BlockSpec((None, nv, tm), lambda i: (i // per_seq, 0, i % per_seq))],
        out_shape=[jax.ShapeDtypeStruct((t, n), BF16),
                   jax.ShapeDtypeStruct((t // seq, nv, seq), BF16)],
        compiler_params=_params(("parallel",)),
        name="in_proj",
    )(h, g, w, w_vt)


def _local_mix_kernel(z_ref, halo_ref, poolw_ref, pscale_ref, convw_ref, lng_ref, sguw_ref,
                      sgub_ref, o_ref, *, pw, cw):
    ts = z_ref.shape[0]
    i = pl.program_id(1)
    z = z_ref[...].astype(F32)
    halo = halo_ref[...].astype(F32)
    halo = jnp.where(i > 0, halo, 0.0)
    ext = jnp.concatenate([halo[:, :pw + 3 * cw], z[:, :pw + 3 * cw]], axis=0)
    rows = ext.shape[0]

    def back(x, k):
        return pltpu.roll(x, k, axis=0)

    a = ext[:, :pw]
    s2 = a + back(a, 1)
    s4 = s2 + back(s2, 2)
    s8 = s4 + back(s4, 4)
    s16 = s8 + back(s8, 8)
    lane = lax.broadcasted_iota(jnp.int32, (rows, pw), 1)
    grp = lane // (pw // POOL_GROUPS)
    win_sum = jnp.where(grp == 0, s2, jnp.where(grp == 1, s4, jnp.where(grp == 2, s8, s16)))
    win = jnp.where(grp == 0, 2, jnp.where(grp == 1, 4, jnp.where(grp == 2, 8, 16)))
    pos = i * ts - HALO + lax.broadcasted_iota(jnp.int32, (rows, pw), 0)
    count = jnp.minimum(pos + 1, win).astype(F32)
    pooled = (win_sum / jnp.maximum(count, 1.0) - a)[HALO:]
    y_a = jnp.dot(pooled.astype(BF16), poolw_ref[...], preferred_element_type=F32) * pscale_ref[...]
    o_ref[:, 0:pw] = y_a.astype(o_ref.dtype)

    b_gate = z[:, pw:pw + cw]
    zc = ext[:, pw + cw:pw + 2 * cw] * ext[:, pw + 2 * cw:pw + 3 * cw]
    conv = (convw_ref[0:1, :] * back(zc, 2) + convw_ref[1:2, :] * back(zc, 1)
            + convw_ref[2:3, :] * zc)[HALO:]
    o_ref[:, pw:pw + cw] = (b_gate * conv).astype(o_ref.dtype)

    sw = (z.shape[1] - pw - 3 * cw) // 2
    zc_uv = z[:, pw + 3 * cw:]
    uv = 0.5 * zc_uv * (1.0 + lax.erf(zc_uv * math.sqrt(0.5)))
    u = uv[:, :sw]
    v = uv[:, sw:]
    mu = jnp.mean(v, axis=-1, keepdims=True)
    var = jnp.mean(jnp.square(v - mu), axis=-1, keepdims=True)
    vn = (v - mu) * lax.rsqrt(var + NORM_EPS) * lng_ref[...]
    glane = lax.broadcasted_iota(jnp.int32, (SGU_SEG, sw), 1) // (sw // SGU_GROUPS)
    wcat = sguw_ref[...]
    bias = sgub_ref[...]
    for n in range(ts // SGU_SEG):
        seg = vn[n * SGU_SEG:(n + 1) * SGU_SEG]
        rhs = jnp.concatenate(
            [jnp.where(glane == g, seg, 0.0) for g in range(SGU_GROUPS)], axis=0).astype(BF16)
        s = jnp.dot(wcat, rhs, preferred_element_type=F32) + bias
        o_ref[n * SGU_SEG:(n + 1) * SGU_SEG, pw + cw:pw + cw + sw] = (
            u[n * SGU_SEG:(n + 1) * SGU_SEG] * s).astype(o_ref.dtype)


def _local_mix(z3, poolw_bd, pscale, convw, lng, sguw_cat, sgub_full, ts, pw, cw, sw):
    b, s, _ = z3.shape
    cols = pw + 3 * cw + 2 * sw
    hb = ts // HALO
    return pl.pallas_call(
        functools.partial(_local_mix_kernel, pw=pw, cw=cw),
        grid=(b, s // ts),
        in_specs=[
            pl.BlockSpec((None, ts, cols), lambda bi, i: (bi, i, 0)),
            pl.BlockSpec((None, HALO, cols), lambda bi, i: (bi, jnp.maximum(i * hb - 1, 0), 0)),
            _resident(poolw_bd.shape), _resident(pscale.shape), _resident(convw.shape),
            _resident(lng.shape), _resident(sguw_cat.shape), _resident(sgub_full.shape),
        ],
        out_specs=pl.BlockSpec((None, ts, pw + cw + sw), lambda bi, i: (bi, i, 0)),
        out_shape=jax.ShapeDtypeStruct((b, s, pw + cw + sw), BF16),
        compiler_params=_params(("parallel", "parallel")),
        name="local_mix",
    )(z3, z3, poolw_bd, pscale, convw, lng, sguw_cat, sgub_full)


def _diff_attn_kernel(q_ref, k_ref, vt_ref, bias_ref, qg_ref, kg_ref, lam_ref, sg_ref, o_ref,
                      kn_ref, qs_ref, st_ref, m_ref, *, lam_init):
    tq = q_ref.shape[0]
    hp = qs_ref.shape[0]
    nt = k_ref.shape[0] // tq
    hw = 2 * DIFF_QK_DIM
    i = pl.program_id(2)
    half = lax.broadcasted_iota(jnp.int32, (1, hw), 1) < DIFF_QK_DIM
    same_map = (lax.broadcasted_iota(jnp.int32, (hw, hw), 0) // DIFF_QK_DIM
                == lax.broadcasted_iota(jnp.int32, (hw, hw), 1) // DIFF_QK_DIM)
    ones_map = jnp.where(same_map, 1.0, 0.0).astype(BF16)

    def qk_norm_mxu(x, g):
        sq_hi, sq_lo = _hi_lo(x * x)
        ss = (jnp.dot(sq_hi, ones_map, preferred_element_type=F32)
              + jnp.dot(sq_lo, ones_map, preferred_element_type=F32))
        return x * lax.rsqrt(ss * (1.0 / DIFF_QK_DIM) + NORM_EPS) * g

    def qk_norm(x, g):
        sq = x * x
        ss0 = jnp.sum(jnp.where(half, sq, 0.0), axis=-1, keepdims=True)
        ss1 = jnp.sum(jnp.where(half, 0.0, sq), axis=-1, keepdims=True)
        r0 = lax.rsqrt(ss0 * (1.0 / DIFF_QK_DIM) + NORM_EPS)
        r1 = lax.rsqrt(ss1 * (1.0 / DIFF_QK_DIM) + NORM_EPS)
        return x * jnp.where(half, r0, r1) * g

    @pl.when(i == 0)
    def _():
        for h in range(hp):
            cols = slice(h * hw, (h + 1) * hw)
            for j in range(nt):
                rows = slice(j * tq, (j + 1) * tq)
                kn_ref[h, rows, :] = qk_norm_mxu(k_ref[rows, cols].astype(F32),
                                                 kg_ref[...]).astype(BF16)

    def step(c):
        cur, prv = c % 2, 1 - c % 2
        scoring = c < nt
        if scoring:
            for h in range(hp):
                qn = (qk_norm(q_ref[:, h * hw:(h + 1) * hw].astype(F32), qg_ref[...])
                      * (DIFF_QK_DIM ** -0.5 * LOG2E))
                qs_ref[h, 0:tq, :] = jnp.where(half, qn, 0.0).astype(BF16)
                qs_ref[h, tq:2 * tq, :] = jnp.where(half, 0.0, qn).astype(BF16)
        m_new = [jnp.full((1, 2 * tq), NEG_BIG, F32) for _ in range(hp)]
        m_old = [m_ref[prv, h] for h in range(hp)] if c >= 1 else None
        l = [jnp.zeros((1, 2 * tq), F32) for _ in range(hp)]
        acc = [jnp.zeros((DIFF_V_DIM, 2 * tq), F32) for _ in range(hp)]
        for j in range(c + 1):
            rows = slice(j * tq, (j + 1) * tq)
            for h in range(hp):
                if scoring:
                    st = lax.dot_general(kn_ref[h, rows, :], qs_ref[h],
                                         (((1,), (1,)), ((), ())), preferred_element_type=F32)
                    if j >= c - 1:
                        st = st + bias_ref[h, j - (c - 1)]
                    st_ref[cur, h, rows, :] = st
                    m_new[h] = jnp.maximum(m_new[h], jnp.max(st, axis=0, keepdims=True))
                if j < c:
                    p = jnp.exp2(st_ref[prv, h, rows, :] - m_old[h])
                    l[h] = l[h] + jnp.sum(p, axis=0, keepdims=True)
                    acc[h] = acc[h] + jnp.dot(vt_ref[h * DIFF_V_DIM:(h + 1) * DIFF_V_DIM, rows],
                                              p.astype(BF16), preferred_element_type=F32)
        if scoring:
            for h in range(hp):
                m_ref[cur, h] = m_new[h]
        if c >= 1:
            lp = lam_ref[...]
            lam = (jnp.exp(jnp.sum(lp[0:1] * lp[1:2], axis=-1, keepdims=True))
                   - jnp.exp(jnp.sum(lp[2:3] * lp[3:4], axis=-1, keepdims=True)) + lam_init)
            for h in range(hp):
                o = (acc[h][:, :tq] * (1.0 / l[h][:, :tq])
                     - acc[h][:, tq:] * (lam / l[h][:, tq:]))
                o = o * lax.rsqrt(jnp.mean(o * o, axis=0, keepdims=True) + NORM_EPS)
                o_ref[:, h * DIFF_V_DIM:(h + 1) * DIFF_V_DIM] = (
                    o.T * (sg_ref[...] * (1.0 - lam_init))).astype(o_ref.dtype)

    for c in range(nt + 1):
        pl.when(i == c)(functools.partial(step, c))


def _diff_attn(z3, vt, bias_near, qg2, kg2, lam_p, subln_g, lam_init, q_col, k_col):
    b, s, _ = z3.shape
    tq = ATT_T
    hp = ATT_HEADS_PER_STEP
    nt = s // tq
    hw = 2 * DIFF_QK_DIM
    return pl.pallas_call(
        functools.partial(_diff_attn_kernel, lam_init=lam_init),
        grid=(b, DIFF_HEADS // hp, nt + 1),
        in_specs=[
            pl.BlockSpec((None, tq, hp * hw),
                         lambda bi, g, i: (bi, jnp.minimum(i, nt - 1), q_col // hp + g)),
            pl.BlockSpec((None, s, hp * hw), lambda bi, g, i: (bi, 0, k_col // hp + g)),
            pl.BlockSpec((None, hp * DIFF_V_DIM, s), lambda bi, g, i: (bi, g, 0)),
            pl.BlockSpec((hp, 2, tq, 2 * tq), lambda bi, g, i: (g, 0, 0, 0)),
            _resident(qg2.shape), _resident(kg2.shape), _resident(lam_p.shape),
            _resident(subln_g.shape),
        ],
        out_specs=pl.BlockSpec((None, tq, hp * DIFF_V_DIM),
                               lambda bi, g, i: (bi, jnp.maximum(i - 1, 0), g)),
        out_shape=jax.ShapeDtypeStruct((b, s, DIFF_HEADS * DIFF_V_DIM), BF16),
        scratch_shapes=[
            pltpu.VMEM((hp, s, hw), BF16),
            pltpu.VMEM((hp, 2 * tq, hw), BF16),
            pltpu.VMEM((2, hp, s, 2 * tq), F32),
            pltpu.VMEM((2, hp, 1, 2 * tq), F32),
        ],
        compiler_params=_params(("parallel", "parallel", "arbitrary")),
        name="diff_attn",
    )(z3, z3, vt, bias_near, qg2, kg2, lam_p, subln_g)


def _merge_kernel(h_ref, yabc_ref, yd_ref, g_ref, wg_ref, wb_ref, wo_ref, o_ref, *, widths):
    h = h_ref[...]
    d = h.shape[1]
    xn = _rms(h, g_ref[...]).astype(BF16)
    merged = None
    off = 0
    yoff = 0
    for bi, w in enumerate(widths):
        gate = jax.nn.sigmoid(jnp.dot(xn, wg_ref[:, bi * d:(bi + 1) * d],
                                      preferred_element_type=F32))
        if bi < len(widths) - 1:
            y = yabc_ref[:, yoff:yoff + w]
            yoff += w
        else:
            y = yd_ref[...]
        proj = jnp.dot(y, wb_ref[off:off + w, :], preferred_element_type=F32)
        off += w
        merged = gate * proj if merged is None else merged + gate * proj
    o_ref[...] = h + jnp.dot(merged.astype(BF16), wo_ref[...], preferred_element_type=F32)


def _merge(h, y_abc, y_d, g, wg, wb, wo, tm, widths):
    t, d = h.shape
    return pl.pallas_call(
        functools.partial(_merge_kernel, widths=widths),
        grid=(t // tm,),
        in_specs=[
            pl.BlockSpec((tm, d), lambda i: (i, 0)),
            pl.BlockSpec((tm, y_abc.shape[1]), lambda i: (i, 0)),
            pl.BlockSpec((tm, y_d.shape[1]), lambda i: (i, 0)),
            _resident(g.shape), _resident(wg.shape), _resident(wb.shape), _resident(wo.shape),
        ],
        out_specs=pl.BlockSpec((tm, d), lambda i: (i, 0)),
        out_shape=jax.ShapeDtypeStruct((t, d), F32),
        compiler_params=_params(("parallel",)),
        name="merge",
    )(h, y_abc, y_d, g, wg, wb, wo)


def _swiglu_acc(xn, wgu_ref, wd_ref, d_ff, between=None):
    acc = None
    n_chunks = d_ff // FF_CHUNK
    for c in range(n_chunks):
        lo = c * FF_CHUNK
        g = jnp.dot(xn, wgu_ref[:, lo:lo + FF_CHUNK], preferred_element_type=F32)
        u = jnp.dot(xn, wgu_ref[:, d_ff + lo:d_ff + lo + FF_CHUNK], preferred_element_type=F32)
        act = (g * jax.nn.sigmoid(g) * u).astype(BF16)
        part = jnp.dot(act, wd_ref[lo:lo + FF_CHUNK, :], preferred_element_type=F32)
        acc = part if acc is None else acc + part
        if between is not None:
            between(c, n_chunks)
    return acc


def _ffn_kernel(h_ref, g_ref, wgu_ref, wd_ref, o_ref):
    h = h_ref[...]
    xn = _rms(h, g_ref[...]).astype(BF16)
    o_ref[...] = h + _swiglu_acc(xn, wgu_ref, wd_ref, wd_ref.shape[0])


def _ffn(h, g, wgu, wd, tm):
    t, d = h.shape
    return pl.pallas_call(
        _ffn_kernel,
        grid=(t // tm,),
        in_specs=[pl.BlockSpec((tm, d), lambda i: (i, 0)), _resident(g.shape),
                  _resident(wgu.shape), _resident(wd.shape)],
        out_specs=pl.BlockSpec((tm, d), lambda i: (i, 0)),
        out_shape=jax.ShapeDtypeStruct((t, d), F32),
        compiler_params=_params(("parallel",)),
        name="ffn",
    )(h, g, wgu, wd)


def _router_kernel(h_ref, g_ref, rw_ref, o_ref, tot_ref, carry_ref):
    tm = h_ref.shape[0]

    @pl.when(pl.program_id(0) == 0)
    def _():
        carry_ref[...] = jnp.zeros(carry_ref.shape, F32)

    hn = _rms(h_ref[...], g_ref[...])
    hn_hi = hn.astype(BF16)
    hn_lo = (hn - hn_hi.astype(F32)).astype(BF16)
    logits = (jnp.dot(hn_hi, rw_ref[0], preferred_element_type=F32)
              + jnp.dot(hn_lo, rw_ref[0], preferred_element_type=F32)
              + jnp.dot(hn_hi, rw_ref[1], preferred_element_type=F32))
    lane = lax.broadcasted_iota(jnp.int32, (tm, LANES), 1)
    logits = jnp.where(lane < N_EXPERTS, logits, NEG_BIG)
    v1 = jnp.max(logits, axis=-1, keepdims=True)
    i1 = jnp.min(jnp.where(logits == v1, lane, LANES), axis=-1, keepdims=True)
    rest = jnp.where(lane == i1, NEG_BIG, logits)
    v2 = jnp.max(rest, axis=-1, keepdims=True)
    i2 = jnp.min(jnp.where(rest == v2, lane, LANES), axis=-1, keepdims=True)
    e = jnp.exp(v2 - v1)
    w1 = 1.0 / (1.0 + e)
    w2 = e / (1.0 + e)
    cnt = jnp.where((lane == i1) | (lane == i2), 1.0, 0.0)
    r = lax.broadcasted_iota(jnp.int32, (tm, tm), 0)
    c = lax.broadcasted_iota(jnp.int32, (tm, tm), 1)
    tri = jnp.where(c < r, 1.0, 0.0).astype(BF16)
    excl = jnp.dot(tri, cnt.astype(BF16), preferred_element_type=F32) + carry_ref[...]
    rank1 = jnp.sum(jnp.where(lane == i1, excl, 0.0), axis=-1, keepdims=True)
    rank2 = jnp.sum(jnp.where(lane == i2, excl, 0.0), axis=-1, keepdims=True)
    carry_ref[...] = carry_ref[...] + jnp.sum(cnt, axis=0, keepdims=True)
    tot_ref[...] = carry_ref[...]
    packed = jnp.where(lane == 0, i1.astype(F32), jnp.where(lane == 1, i2.astype(F32),
             jnp.where(lane == 2, w1, jnp.where(lane == 3, w2,
             jnp.where(lane == 4, rank1, jnp.where(lane == 5, rank2, 0.0))))))
    o_ref[...] = packed


def _router(h, g, rw_pad, tm):
    t, d = h.shape
    return pl.pallas_call(
        _router_kernel,
        grid=(t // tm,),
        in_specs=[pl.BlockSpec((tm, d), lambda i: (i, 0)), _resident(g.shape),
                  _resident(rw_pad.shape)],
        out_specs=[pl.BlockSpec((tm, LANES), lambda i: (i, 0)),
                   pl.BlockSpec((1, LANES), lambda i: (0, 0))],
        out_shape=[jax.ShapeDtypeStruct((t, LANES), F32), jax.ShapeDtypeStruct((1, LANES), F32)],
        scratch_shapes=[pltpu.VMEM((1, LANES), F32)],
        compiler_params=_params(("arbitrary",)),
        name="router",
    )(h, g, rw_pad)


def _invert_kernel(dest_ref, init_hbm, src_ref):
    c = pl.program_id(0)
    ch = dest_ref.shape[1]

    @pl.when(c == 0)
    def _():
        pltpu.sync_copy(init_hbm, src_ref)

    base = c * ch

    def place(a, x):
        src_ref[dest_ref[0, a]] = base + a
        return x

    lax.fori_loop(0, ch, place, 0, unroll=16)


def _invert(dest, rows):
    nc, _, ch = dest.shape
    return pl.pallas_call(
        _invert_kernel,
        grid=(nc,),
        in_specs=[pl.BlockSpec((None, 1, ch), lambda c: (c, 0, 0), memory_space=pltpu.SMEM),
                  pl.BlockSpec(memory_space=pl.ANY)],
        out_specs=pl.BlockSpec(memory_space=pltpu.SMEM),
        out_shape=jax.ShapeDtypeStruct((rows,), jnp.int32),
        compiler_params=pltpu.CompilerParams(dimension_semantics=("arbitrary",)),
        name="moe_invert",
    )(dest, jnp.full((rows,), -1, jnp.int32))


def _expert_kernel(te_ref, src_cur, src_nxt, orow_prv, orow_cur, h_hbm, g_ref, wgu_hbm, wd_hbm,
                   yt_hbm, xbuf, ybuf, wgu_ref, wd_ref, stage_gu, stage_d, gsem, ssem, wsem):
    i = pl.program_id(0)
    last = pl.num_programs(0) - 1
    tm = xbuf.shape[1]
    s = lax.rem(i, 2)
    o = 1 - s
    e = te_ref[i]

    @pl.when((i == 0) | (e != te_ref[jnp.maximum(i - 1, 0)]))
    def _():
        def stream(copy, n, store):
            copy(0, 0).start()
            for c in range(n):
                if c + 1 < n:
                    copy(c + 1, (c + 1) % 2).start()
                copy(c, c % 2).wait()
                store(c, c % 2)

        wc = stage_gu.shape[2]
        rc = stage_d.shape[1]

        def copy_gu(c, slot):
            return pltpu.make_async_copy(wgu_hbm.at[e, :, pl.ds(c * wc, wc)], stage_gu.at[slot],
                                         wsem.at[slot])

        def store_gu(c, slot):
            wgu_ref[:, c * wc:(c + 1) * wc] = stage_gu[slot].astype(BF16)

        def copy_d(c, slot):
            return pltpu.make_async_copy(wd_hbm.at[e, pl.ds(c * rc, rc), :], stage_d.at[slot],
                                         wsem.at[slot])

        def store_d(c, slot):
            wd_ref[c * rc:(c + 1) * rc, :] = stage_d[slot].astype(BF16)

        stream(copy_gu, wgu_ref.shape[1] // wc, store_gu)
        stream(copy_d, wd_ref.shape[0] // rc, store_d)

    def gather(tok, r, slot):
        return pltpu.make_async_copy(h_hbm.at[pl.ds(tok, 1)],
                                     xbuf.at[slot, pl.ds(r, 1)], gsem.at[slot])

    def scatter(row, r, slot):
        return pltpu.make_async_copy(ybuf.at[slot, pl.ds(r, 1)],
                                     yt_hbm.at[pl.ds(row, 1)], ssem.at[slot])

    def for_rows(fn):
        def body(r, x):
            fn(r)
            return x

        lax.fori_loop(0, tm, body, 0, unroll=8)

    @pl.when(i == 0)
    def _():
        ybuf[1] = jnp.zeros(ybuf.shape[1:], ybuf.dtype)
        for_rows(lambda r: gather(src_cur[0, r], r, 0).start())

    for_rows(lambda r: gather(src_cur[0, r], r, s).wait())
    xn = _rms(xbuf[s], g_ref[...]).astype(BF16)

    def between(c, n_chunks):
        per = -(-tm // (n_chunks // 2))
        for r in range(c * per, min((c + 1) * per, tm)):
            gather(src_nxt[0, r], r, o).start()
            scatter(orow_prv[0, r], r, o).start()

    ybuf[s] = _pack_halves(_swiglu_acc(xn, wgu_ref, wd_ref, wd_ref.shape[0], between))
    for_rows(lambda r: scatter(orow_prv[0, r], r, o).wait())

    @pl.when(i == last)
    def _():
        for_rows(lambda r: gather(src_nxt[0, r], r, o).wait())
        for_rows(lambda r: scatter(orow_cur[0, r], r, s).start())
        for_rows(lambda r: scatter(orow_cur[0, r], r, s).wait())


def _experts(tile_expert, src_tok, out_row, h, g, wgu, wd, yt_rows):
    n, _, tm = src_tok.shape
    d = h.shape[1]
    d_ff = wd.shape[1]
    smem = functools.partial(pl.BlockSpec, (None, 1, tm), memory_space=pltpu.SMEM)
    grid_spec = pltpu.PrefetchScalarGridSpec(
        num_scalar_prefetch=1,
        grid=(n,),
        in_specs=[
            smem(lambda i, te: (i, 0, 0)),
            smem(lambda i, te: (jnp.minimum(i + 1, n - 1), 0, 0)),
            smem(lambda i, te: (i, 0, 0)),
            smem(lambda i, te: (i + 1, 0, 0)),
            pl.BlockSpec(memory_space=pl.ANY),
            pl.BlockSpec(g.shape, lambda i, te: (0, 0), pipeline_mode=pl.Buffered(1)),
            pl.BlockSpec(memory_space=pl.ANY),
            pl.BlockSpec(memory_space=pl.ANY),
        ],
        out_specs=pl.BlockSpec(memory_space=pl.ANY),
        scratch_shapes=[pltpu.VMEM((2, tm, d), F32), pltpu.VMEM((2, tm, d // 2), jnp.uint32),
                        pltpu.VMEM((d, 2 * d_ff), BF16), pltpu.VMEM((d_ff, d), BF16),
                        pltpu.VMEM((2, d, WEIGHT_STAGE), F32), pltpu.VMEM((2, WEIGHT_STAGE, d), F32),
                        pltpu.SemaphoreType.DMA((2,)), pltpu.SemaphoreType.DMA((2,)),
                        pltpu.SemaphoreType.DMA((2,))],
    )
    return pl.pallas_call(
        _expert_kernel,
        grid_spec=grid_spec,
        out_shape=jax.ShapeDtypeStruct((yt_rows, d // 2), jnp.uint32),
        compiler_params=_params(("arbitrary",)),
        name="moe_experts",
    )(tile_expert, src_tok, src_tok, out_row, out_row, h, g, wgu, wd)


def _combine_kernel(h_ref, pk_ref, y1_ref, y2_ref, o_ref):
    pk = pk_ref[...]
    o_ref[...] = (h_ref[...] + pk[:, 2:3] * _unpack_halves(y1_ref[...])
                  + pk[:, 3:4] * _unpack_halves(y2_ref[...]))


def _combine(h, packed, yt, tm):
    t, d = h.shape
    nb = t // tm
    return pl.pallas_call(
        _combine_kernel,
        grid=(nb,),
        in_specs=[pl.BlockSpec((tm, d), lambda i: (i, 0)),
                  pl.BlockSpec((tm, LANES), lambda i: (i, 0)),
                  pl.BlockSpec((tm, d // 2), lambda i: (i, 0)),
                  pl.BlockSpec((tm, d // 2), lambda i: (nb + i, 0))],
        out_specs=pl.BlockSpec((tm, d), lambda i: (i, 0)),
        out_shape=jax.ShapeDtypeStruct((t, d), F32),
        compiler_params=_params(("parallel",)),
        name="moe_combine",
    )(h, packed, yt, yt)


def _rel_bucket(rel):
    nb = REL_BUCKETS // 2
    max_exact = nb // 2
    n = jnp.abs(rel)
    nf = jnp.maximum(n, 1).astype(F32)
    large = max_exact + (jnp.log(nf / max_exact) / math.log(REL_MAX_DIST / max_exact)
                         * (nb - max_exact)).astype(jnp.int32)
    large = jnp.minimum(large, nb - 1)
    return jnp.where(rel > 0, nb, 0) + jnp.where(n < max_exact, n, large)


def _near_bias(rel_bias):
    t = ATT_T
    qp = jnp.arange(t)[:, None]
    kp = jnp.arange(t)[None, :]

    def lookup(rel):
        onehot = jax.nn.one_hot(_rel_bucket(rel), REL_BUCKETS, dtype=F32)
        return jnp.einsum('...b,bm->...m', onehot, rel_bias, precision=lax.Precision.HIGHEST)

    far = lookup(jnp.full((), -(2 * t), jnp.int32))
    prev = (lookup(kp - t - qp) - far) * LOG2E
    diag = (lookup(kp - qp) - far) * LOG2E
    diag = jnp.where(((kp // CHUNK) <= (qp // CHUNK))[:, :, None], diag, NEG_BIG)
    both = jnp.stack([prev, diag], axis=0).reshape(2, t, t, DIFF_HEADS, 2)
    return both.transpose(3, 0, 2, 4, 1).reshape(DIFF_HEADS, 2, t, 2 * t).astype(F32)


def kernel(x, rel_bias, norm1_g, w_in, pool_w, pool_scale, conv_w, sgu_ln_g, sgu_w, sgu_b, q_norm_g, k_norm_g, diff_lambda, subln_g, w_branch_pool, w_branch_conv, w_branch_sgu, w_branch_attn, w_out, norm2_g, ffn_w_gate_up, ffn_w_down, router_w, moe_w_gate_up, moe_w_down):
    b, s, d = x.shape
    t = b * s
    depth = w_in.shape[0]
    pw = pool_scale.shape[1]
    cw = conv_w.shape[2]
    sw = sgu_ln_g.shape[1]
    aw = w_branch_attn.shape[1]
    mix_cols = pw + 3 * cw + 2 * sw + 3 * aw
    nb = REL_BUCKETS // 2
    assert nb // 2 + int(math.log((ATT_T + 1) / (nb // 2)) / math.log(REL_MAX_DIST / (nb // 2))
                         * (nb - nb // 2)) >= nb - 1
    qk_off = pw + 3 * cw + 2 * sw
    z_cols = qk_off + 2 * aw
    q_col = qk_off // LANES
    k_col = q_col + aw // LANES
    tm = min(512, t)
    td = min(DENSE_TM, s)
    ts = min(512, s)

    bias_near = _near_bias(rel_bias)
    tri = jnp.tril(jnp.ones((SGU_SEG, SGU_SEG), bool))
    gd = pw // POOL_GROUPS

    h = x.reshape(t, d)
    for layer in range(depth):
        lam_init = 0.8 - 0.6 * math.exp(-0.3 * layer)
        w_mix = w_in[layer, :, :z_cols].astype(BF16)
        w_vt = w_in[layer, :, z_cols:mix_cols].T.astype(BF16)
        w_gate = w_in[layer, :, mix_cols:].astype(BF16)
        poolw_bd = jnp.zeros((pw, pw), F32)
        for g in range(POOL_GROUPS):
            poolw_bd = poolw_bd.at[g * gd:(g + 1) * gd, g * gd:(g + 1) * gd].set(pool_w[layer, g])
        sguw_cat = jnp.where(tri[None], sgu_w[layer], 0.0).transpose(1, 0, 2).reshape(
            SGU_SEG, SGU_GROUPS * SGU_SEG).astype(BF16)
        sgub_full = jnp.repeat(sgu_b[layer].T, sw // SGU_GROUPS, axis=1)
        wb = jnp.concatenate([w_branch_pool[layer], w_branch_conv[layer], w_branch_sgu[layer],
                              w_branch_attn[layer]], axis=0).astype(BF16)

        z, vt = _in_proj(h, norm1_g[layer][None], w_mix, w_vt, td, s)
        z3 = z.reshape(b, s, z_cols)
        y_abc = _local_mix(z3, poolw_bd.astype(BF16), pool_scale[layer][None], conv_w[layer],
                           sgu_ln_g[layer][None], sguw_cat, sgub_full, ts, pw, cw, sw)
        y_d = _diff_attn(z3, vt, bias_near, jnp.tile(q_norm_g[layer], 2)[None],
                         jnp.tile(k_norm_g[layer], 2)[None], diff_lambda[layer],
                         subln_g[layer][None], lam_init, q_col, k_col)
        h = _merge(h, y_abc.reshape(t, -1), y_d.reshape(t, -1), norm1_g[layer][None], w_gate, wb,
                   w_out[layer].astype(BF16), td, (pw, cw, sw, aw))

        g2 = norm2_g[layer][None]
        if layer % 2 == 0:
            h = _ffn(h, g2, ffn_w_gate_up[layer // 2].astype(BF16),
                     ffn_w_down[layer // 2].astype(BF16), td)
        else:
            li = layer // 2
            rw_pad = jnp.zeros((d, LANES), F32).at[:, :N_EXPERTS].set(router_w[li])
            rw_hi = rw_pad.astype(BF16)
            rw_split = jnp.stack([rw_hi, (rw_pad - rw_hi.astype(F32)).astype(BF16)])
            packed, totals = _router(h, g2, rw_split, tm)
            n_e = totals[0, :N_EXPERTS].astype(jnp.int32)
            n_pad = ((n_e + tm - 1) // tm) * tm
            ends = jnp.cumsum(n_pad)
            starts = ends - n_pad
            e1 = packed[:, 0].astype(jnp.int32)
            e2 = packed[:, 1].astype(jnp.int32)
            eids = jnp.arange(N_EXPERTS)[None, :]
            dest1 = (jnp.sum(jnp.where(e1[:, None] == eids, starts[None, :], 0), axis=1)
                     + packed[:, 4].astype(jnp.int32))
            dest2 = (jnp.sum(jnp.where(e2[:, None] == eids, starts[None, :], 0), axis=1)
                     + packed[:, 5].astype(jnp.int32))
            rows = 2 * t + N_EXPERTS * tm
            n_tiles = rows // tm
            tile_expert = jnp.minimum(
                jnp.sum((jnp.arange(n_tiles)[:, None] * tm) >= ends[None, :], axis=1),
                N_EXPERTS - 1).astype(jnp.int32)
            ch = min(4096, t)
            dest_a = jnp.concatenate([dest1, dest2]).reshape(2 * t // ch, 1, ch)
            src = _invert(dest_a, rows)
            is_pad = src < 0
            src_tok = jnp.where(is_pad, 0, jnp.where(src >= t, src - t, src))
            pad_rank = jnp.cumsum(is_pad.astype(jnp.int32)) - 1
            out_row = jnp.where(is_pad, 2 * t + tm + pad_rank, src)
            spare = 2 * t + jnp.arange(tm, dtype=jnp.int32)
            yt = _experts(tile_expert, src_tok.reshape(n_tiles, 1, tm),
                          jnp.concatenate([spare, out_row]).reshape(n_tiles + 1, 1, tm), h, g2,
                          moe_w_gate_up[li], moe_w_down[li], rows + tm)
            h = _combine(h, packed, yt, tm)
    return h.reshape(b, s, d)
```

```python
import functools
import math

import jax
import jax.numpy as jnp
import numpy as np
from jax import lax
from jax.experimental import pallas as pl
from jax.experimental.pallas import tpu as pltpu

F32 = jnp.float32
BF16 = jnp.bfloat16

NORM_EPS = 1e-6
CHUNK = 64
POOL_WINDOWS = (2, 4, 8, 16)
POOL_GROUPS = 4
CONV_K = 3
SGU_GROUPS = 4
SGU_SEG = 128
DIFF_HEADS = 4
DIFF_QK_DIM = 64
DIFF_V_DIM = 128
REL_BUCKETS = 32
REL_MAX_DIST = 128
N_EXPERTS = 8
LANES = 128
V7X_VMEM_BYTES = 64 * 1024 * 1024
VMEM_LIMIT = V7X_VMEM_BYTES - 8 * 1024 * 1024
NEG_BIG = -1e30
LOG2E = math.log2(math.e)

HALO = 16
ATT_T = 256
ATT_HEADS_PER_STEP = 2
FF_CHUNK = 256
WEIGHT_STAGE = 512
DENSE_TM = 1024
STAGE_ROWS = 128


def _rms(x, g):
    return x * lax.rsqrt(jnp.mean(x * x, axis=-1, keepdims=True) + NORM_EPS) * g


def _resident(shape):
    nd = len(shape)
    return pl.BlockSpec(shape, lambda *_: (0,) * nd, pipeline_mode=pl.Buffered(1))


def _params(sem):
    return pltpu.CompilerParams(dimension_semantics=sem, vmem_limit_bytes=VMEM_LIMIT)


def _hi_lo(x):
    hi = x.astype(BF16)
    return hi, (x - hi.astype(F32)).astype(BF16)


def _pack_halves(x):
    n = x.shape[1] // 2
    bits = lax.bitcast_convert_type(x, jnp.uint32) + jnp.uint32(0x8000)
    return (bits[:, :n] & jnp.uint32(0xFFFF0000)) | (bits[:, n:] >> 16)


def _unpack_halves(p):
    hi = lax.bitcast_convert_type(p & jnp.uint32(0xFFFF0000), F32)
    lo = lax.bitcast_convert_type(p << 16, F32)
    return jnp.concatenate([hi, lo], axis=1)


def _stream_chunks(copy, n, consume):
    copy(0, 0).start()
    for c in range(n):
        if c + 1 < n:
            copy(c + 1, (c + 1) % 2).start()
        copy(c, c % 2).wait()
        consume(c, c % 2)


def _load_bf16(w_hbm, lead, col0, dst_ref, stage_ref, sem, transpose=False):
    rows, cols = (dst_ref.shape[1], dst_ref.shape[0]) if transpose else dst_ref.shape
    rc = stage_ref.shape[1]

    def copy(c, slot):
        src = w_hbm.at[(*lead, pl.ds(c * rc, rc), pl.ds(col0, cols))]
        return pltpu.make_async_copy(src, stage_ref.at[slot, :, pl.ds(0, cols)], sem.at[slot])

    def consume(c, slot):
        chunk = stage_ref[slot, :, 0:cols]
        if transpose:
            dst_ref[:, c * rc:(c + 1) * rc] = chunk.T.astype(BF16)
        else:
            dst_ref[c * rc:(c + 1) * rc, :] = chunk.astype(BF16)

    _stream_chunks(copy, rows // rc, consume)


def _in_proj_kernel(x_ref, g_ref, win_hbm, o_ref, vt_ref, w_ref, wvt_ref, stage_ref, sem, *,
                    n_chunk, layer):
    @pl.when(pl.program_id(0) == 0)
    def _():
        _load_bf16(win_hbm, (layer,), 0, w_ref, stage_ref, sem)
        _load_bf16(win_hbm, (layer,), w_ref.shape[1], wvt_ref, stage_ref, sem, transpose=True)

    xn = _rms(x_ref[...], g_ref[...]).astype(BF16)
    n = o_ref.shape[1]
    for j in range(n // n_chunk):
        sl = slice(j * n_chunk, (j + 1) * n_chunk)
        o_ref[:, sl] = jnp.dot(xn, w_ref[:, sl], preferred_element_type=F32).astype(o_ref.dtype)
    vt_ref[...] = lax.dot_general(wvt_ref[...], xn, (((1,), (1,)), ((), ())),
                                  preferred_element_type=F32).astype(vt_ref.dtype)


def _in_proj(h, g, w_in, layer, n, nv, tm, seq):
    t, d = h.shape
    per_seq = seq // tm
    return pl.pallas_call(
        functools.partial(_in_proj_kernel, n_chunk=512, layer=layer),
        grid=(t // tm,),
        in_specs=[pl.BlockSpec((tm, d), lambda i: (i, 0)), _resident((1, d)),
                  pl.BlockSpec(memory_space=pl.ANY)],
        out_specs=[pl.BlockSpec((tm, n), lambda i: (i, 0)),
                   pl.BlockSpec((None, nv, tm), lambda i: (i // per_seq, 0, i % per_seq))],
        out_shape=[jax.ShapeDtypeStruct((t, n), BF16),
                   jax.ShapeDtypeStruct((t // seq, nv, seq), BF16)],
        scratch_shapes=[pltpu.VMEM((d, n), BF16), pltpu.VMEM((nv, d), BF16),
                        pltpu.VMEM((2, STAGE_ROWS, n), F32), pltpu.SemaphoreType.DMA((2,))],
        compiler_params=_params(("arbitrary",)),
        name="in_proj",
    )(h, g, w_in)


def _local_mix_kernel(z_ref, halo_ref, poolw_ref, pscale_ref, convw_ref, lng_ref, sguw_ref,
                      sgub_ref, o_ref, *, pw, cw):
    ts = z_ref.shape[0]
    i = pl.program_id(1)
    z = z_ref[...].astype(F32)
    halo = halo_ref[...].astype(F32)
    halo = jnp.where(i > 0, halo, 0.0)
    ext = jnp.concatenate([halo[:, :pw + 3 * cw], z[:, :pw + 3 * cw]], axis=0)
    rows = ext.shape[0]

    def back(x, k):
        return pltpu.roll(x, k, axis=0)

    a = ext[:, :pw]
    s2 = a + back(a, 1)
    s4 = s2 + back(s2, 2)
    s8 = s4 + back(s4, 4)
    s16 = s8 + back(s8, 8)
    lane = lax.broadcasted_iota(jnp.int32, (rows, pw), 1)
    grp = lane // (pw // POOL_GROUPS)
    win_sum = jnp.where(grp == 0, s2, jnp.where(grp == 1, s4, jnp.where(grp == 2, s8, s16)))
    win = jnp.where(grp == 0, 2, jnp.where(grp == 1, 4, jnp.where(grp == 2, 8, 16)))
    pos = i * ts - HALO + lax.broadcasted_iota(jnp.int32, (rows, pw), 0)
    count = jnp.minimum(pos + 1, win).astype(F32)
    pooled = (win_sum / jnp.maximum(count, 1.0) - a)[HALO:]
    y_a = jnp.dot(pooled.astype(BF16), poolw_ref[...], preferred_element_type=F32) * pscale_ref[...]
    o_ref[:, 0:pw] = y_a.astype(o_ref.dtype)

    b_gate = z[:, pw:pw + cw]
    zc = ext[:, pw + cw:pw + 2 * cw] * ext[:, pw + 2 * cw:pw + 3 * cw]
    conv = (convw_ref[0:1, :] * back(zc, 2) + convw_ref[1:2, :] * back(zc, 1)
            + convw_ref[2:3, :] * zc)[HALO:]
    o_ref[:, pw:pw + cw] = (b_gate * conv).astype(o_ref.dtype)

    sw = (z.shape[1] - pw - 3 * cw) // 2
    zc_uv = z[:, pw + 3 * cw:]
    uv = 0.5 * zc_uv * (1.0 + lax.erf(zc_uv * math.sqrt(0.5)))
    u = uv[:, :sw]
    v = uv[:, sw:]
    mu = jnp.mean(v, axis=-1, keepdims=True)
    var = jnp.mean(jnp.square(v - mu), axis=-1, keepdims=True)
    vn = (v - mu) * lax.rsqrt(var + NORM_EPS) * lng_ref[...]
    glane = lax.broadcasted_iota(jnp.int32, (SGU_SEG, sw), 1) // (sw // SGU_GROUPS)
    wcat = sguw_ref[...]
    bias = sgub_ref[...]
    for n in range(ts // SGU_SEG):
        seg = vn[n * SGU_SEG:(n + 1) * SGU_SEG]
        rhs = jnp.concatenate(
            [jnp.where(glane == g, seg, 0.0) for g in range(SGU_GROUPS)], axis=0).astype(BF16)
        s = jnp.dot(wcat, rhs, preferred_element_type=F32) + bias
        o_ref[n * SGU_SEG:(n + 1) * SGU_SEG, pw + cw:pw + cw + sw] = (
            u[n * SGU_SEG:(n + 1) * SGU_SEG] * s).astype(o_ref.dtype)


def _local_mix(z3, poolw_bd, pscale, convw, lng, sguw_cat, sgub_full, ts, pw, cw, sw):
    b, s, _ = z3.shape
    cols = pw + 3 * cw + 2 * sw
    hb = ts // HALO
    return pl.pallas_call(
        functools.partial(_local_mix_kernel, pw=pw, cw=cw),
        grid=(b, s // ts),
        in_specs=[
            pl.BlockSpec((None, ts, cols), lambda bi, i: (bi, i, 0)),
            pl.BlockSpec((None, HALO, cols), lambda bi, i: (bi, jnp.maximum(i * hb - 1, 0), 0)),
            _resident(poolw_bd.shape), _resident(pscale.shape), _resident(convw.shape),
            _resident(lng.shape), _resident(sguw_cat.shape), _resident(sgub_full.shape),
        ],
        out_specs=pl.BlockSpec((None, ts, pw + cw + sw), lambda bi, i: (bi, i, 0)),
        out_shape=jax.ShapeDtypeStruct((b, s, pw + cw + sw), BF16),
        compiler_params=_params(("parallel", "parallel")),
        name="local_mix",
    )(z3, z3, poolw_bd, pscale, convw, lng, sguw_cat, sgub_full)


def _diff_attn_kernel(q_ref, k_ref, vt_ref, bias_ref, qg_ref, kg_ref, lam_ref, sg_ref, o_ref,
                      kn_ref, qs_ref, st_ref, m_ref, *, lam_init):
    tq = q_ref.shape[0]
    hp = qs_ref.shape[0]
    nt = k_ref.shape[0] // tq
    hw = 2 * DIFF_QK_DIM
    i = pl.program_id(2)
    half = lax.broadcasted_iota(jnp.int32, (1, hw), 1) < DIFF_QK_DIM
    same_map = (lax.broadcasted_iota(jnp.int32, (hw, hw), 0) // DIFF_QK_DIM
                == lax.broadcasted_iota(jnp.int32, (hw, hw), 1) // DIFF_QK_DIM)
    ones_map = jnp.where(same_map, 1.0, 0.0).astype(BF16)

    def qk_norm_mxu(x, g):
        sq_hi, sq_lo = _hi_lo(x * x)
        ss = (jnp.dot(sq_hi, ones_map, preferred_element_type=F32)
              + jnp.dot(sq_lo, ones_map, preferred_element_type=F32))
        return x * lax.rsqrt(ss * (1.0 / DIFF_QK_DIM) + NORM_EPS) * g

    def qk_norm(x, g):
        sq = x * x
        ss0 = jnp.sum(jnp.where(half, sq, 0.0), axis=-1, keepdims=True)
        ss1 = jnp.sum(jnp.where(half, 0.0, sq), axis=-1, keepdims=True)
        r0 = lax.rsqrt(ss0 * (1.0 / DIFF_QK_DIM) + NORM_EPS)
        r1 = lax.rsqrt(ss1 * (1.0 / DIFF_QK_DIM) + NORM_EPS)
        return x * jnp.where(half, r0, r1) * g

    @pl.when(i == 0)
    def _():
        for h in range(hp):
            cols = slice(h * hw, (h + 1) * hw)
            for j in range(nt):
                rows = slice(j * tq, (j + 1) * tq)
                kn_ref[h, rows, :] = qk_norm_mxu(k_ref[rows, cols].astype(F32),
                                                 kg_ref[...]).astype(BF16)

    def step(c):
        cur, prv = c % 2, 1 - c % 2
        scoring = c < nt
        if scoring:
            for h in range(hp):
                qn = (qk_norm(q_ref[:, h * hw:(h + 1) * hw].astype(F32), qg_ref[...])
                      * (DIFF_QK_DIM ** -0.5 * LOG2E))
                qs_ref[h, 0:tq, :] = jnp.where(half, qn, 0.0).astype(BF16)
                qs_ref[h, tq:2 * tq, :] = jnp.where(half, 0.0, qn).astype(BF16)
        m_new = [jnp.full((1, 2 * tq), NEG_BIG, F32) for _ in range(hp)]
        m_old = [m_ref[prv, h] for h in range(hp)] if c >= 1 else None
        l = [jnp.zeros((1, 2 * tq), F32) for _ in range(hp)]
        acc = [jnp.zeros((DIFF_V_DIM, 2 * tq), F32) for _ in range(hp)]
        for j in range(c + 1):
            rows = slice(j * tq, (j + 1) * tq)
            for h in range(hp):
                if scoring:
                    st = lax.dot_general(kn_ref[h, rows, :], qs_ref[h],
                                         (((1,), (1,)), ((), ())), preferred_element_type=F32)
                    if j >= c - 1:
                        st = st + bias_ref[h, j - (c - 1)]
                    st_ref[cur, h, rows, :] = st
                    m_new[h] = jnp.maximum(m_new[h], jnp.max(st, axis=0, keepdims=True))
                if j < c:
                    p = jnp.exp2(st_ref[prv, h, rows, :] - m_old[h])
                    l[h] = l[h] + jnp.sum(p, axis=0, keepdims=True)
                    acc[h] = acc[h] + jnp.dot(vt_ref[h * DIFF_V_DIM:(h + 1) * DIFF_V_DIM, rows],
                                              p.astype(BF16), preferred_element_type=F32)
        if scoring:
            for h in range(hp):
                m_ref[cur, h] = m_new[h]
        if c >= 1:
            lp = lam_ref[...]
            lam = (jnp.exp(jnp.sum(lp[0:1] * lp[1:2], axis=-1, keepdims=True))
                   - jnp.exp(jnp.sum(lp[2:3] * lp[3:4], axis=-1, keepdims=True)) + lam_init)
            for h in range(hp):
                o = (acc[h][:, :tq] * (1.0 / l[h][:, :tq])
                     - acc[h][:, tq:] * (lam / l[h][:, tq:]))
                o = o * lax.rsqrt(jnp.mean(o * o, axis=0, keepdims=True) + NORM_EPS)
                o_ref[:, h * DIFF_V_DIM:(h + 1) * DIFF_V_DIM] = (
                    o.T * (sg_ref[...] * (1.0 - lam_init))).astype(o_ref.dtype)

    for c in range(nt + 1):
        pl.when(i == c)(functools.partial(step, c))


def _diff_attn(z3, vt, bias_near, qg2, kg2, lam_p, subln_g, lam_init, q_col, k_col):
    b, s, _ = z3.shape
    tq = ATT_T
    hp = ATT_HEADS_PER_STEP
    nt = s // tq
    hw = 2 * DIFF_QK_DIM
    return pl.pallas_call(
        functools.partial(_diff_attn_kernel, lam_init=lam_init),
        grid=(b, DIFF_HEADS // hp, nt + 1),
        in_specs=[
            pl.BlockSpec((None, tq, hp * hw),
                         lambda bi, g, i: (bi, jnp.minimum(i, nt - 1), q_col // hp + g)),
            pl.BlockSpec((None, s, hp * hw), lambda bi, g, i: (bi, 0, k_col // hp + g)),
            pl.BlockSpec((None, hp * DIFF_V_DIM, s), lambda bi, g, i: (bi, g, 0)),
            pl.BlockSpec((hp, 2, tq, 2 * tq), lambda bi, g, i: (g, 0, 0, 0)),
            _resident(qg2.shape), _resident(kg2.shape), _resident(lam_p.shape),
            _resident(subln_g.shape),
        ],
        out_specs=pl.BlockSpec((None, tq, hp * DIFF_V_DIM),
                               lambda bi, g, i: (bi, jnp.maximum(i - 1, 0), g)),
        out_shape=jax.ShapeDtypeStruct((b, s, DIFF_HEADS * DIFF_V_DIM), BF16),
        scratch_shapes=[
            pltpu.VMEM((hp, s, hw), BF16),
            pltpu.VMEM((hp, 2 * tq, hw), BF16),
            pltpu.VMEM((2, hp, s, 2 * tq), F32),
            pltpu.VMEM((2, hp, 1, 2 * tq), F32),
        ],
        compiler_params=_params(("parallel", "parallel", "arbitrary")),
        name="diff_attn",
    )(z3, z3, vt, bias_near, qg2, kg2, lam_p, subln_g)


def _merge_kernel(h_ref, yabc_ref, yd_ref, g_ref, win_hbm, wbp_hbm, wbc_hbm, wbs_hbm, wba_hbm,
                  wout_hbm, o_ref, wg_ref, wb_ref, wo_ref, stage_ref, sem, *, widths, layer, gate_col):
    @pl.when(pl.program_id(0) == 0)
    def _():
        _load_bf16(win_hbm, (layer,), gate_col, wg_ref, stage_ref, sem)
        off = 0
        for w_hbm, w in zip((wbp_hbm, wbc_hbm, wbs_hbm, wba_hbm), widths):
            _load_bf16(w_hbm, (layer,), 0, wb_ref.at[off:off + w, :], stage_ref, sem)
            off += w
        _load_bf16(wout_hbm, (layer,), 0, wo_ref, stage_ref, sem)

    h = h_ref[...]
    d = h.shape[1]
    xn = _rms(h, g_ref[...]).astype(BF16)
    merged = None
    off = 0
    yoff = 0
    for bi, w in enumerate(widths):
        gate = jax.nn.sigmoid(jnp.dot(xn, wg_ref[:, bi * d:(bi + 1) * d],
                                      preferred_element_type=F32))
        if bi < len(widths) - 1:
            y = yabc_ref[:, yoff:yoff + w]
            yoff += w
        else:
            y = yd_ref[...]
        proj = jnp.dot(y, wb_ref[off:off + w, :], preferred_element_type=F32)
        off += w
        merged = gate * proj if merged is None else merged + gate * proj
    o_ref[...] = h + jnp.dot(merged.astype(BF16), wo_ref[...], preferred_element_type=F32)


def _merge(h, y_abc, y_d, g, w_in, w_branches, w_out, layer, gate_col, tm, widths):
    t, d = h.shape
    n_gate = len(widths) * d
    hbm = pl.BlockSpec(memory_space=pl.ANY)
    return pl.pallas_call(
        functools.partial(_merge_kernel, widths=widths, layer=layer, gate_col=gate_col),
        grid=(t // tm,),
        in_specs=[
            pl.BlockSpec((tm, d), lambda i: (i, 0)),
            pl.BlockSpec((tm, y_abc.shape[1]), lambda i: (i, 0)),
            pl.BlockSpec((tm, y_d.shape[1]), lambda i: (i, 0)),
            _resident(g.shape), hbm, hbm, hbm, hbm, hbm, hbm,
        ],
        out_specs=pl.BlockSpec((tm, d), lambda i: (i, 0)),
        out_shape=jax.ShapeDtypeStruct((t, d), F32),
        scratch_shapes=[pltpu.VMEM((d, n_gate), BF16), pltpu.VMEM((sum(widths), d), BF16),
                        pltpu.VMEM((d, d), BF16), pltpu.VMEM((2, STAGE_ROWS, n_gate), F32),
                        pltpu.SemaphoreType.DMA((2,))],
        compiler_params=_params(("arbitrary",)),
        name="merge",
    )(h, y_abc, y_d, g, w_in, *w_branches, w_out)


def _swiglu_acc(xn, wgu_ref, wd_ref, d_ff, between=None):
    acc = None
    n_chunks = d_ff // FF_CHUNK
    for c in range(n_chunks):
        lo = c * FF_CHUNK
        g = jnp.dot(xn, wgu_ref[:, lo:lo + FF_CHUNK], preferred_element_type=F32)
        u = jnp.dot(xn, wgu_ref[:, d_ff + lo:d_ff + lo + FF_CHUNK], preferred_element_type=F32)
        act = (g * jax.nn.sigmoid(g) * u).astype(BF16)
        part = jnp.dot(act, wd_ref[lo:lo + FF_CHUNK, :], preferred_element_type=F32)
        acc = part if acc is None else acc + part
        if between is not None:
            between(c, n_chunks)
    return acc


def _ffn_kernel(h_ref, g_ref, wgu_hbm, wd_hbm, o_ref, wgu_ref, wd_ref, stage_ref, sem, *, index):
    @pl.when(pl.program_id(0) == 0)
    def _():
        _load_bf16(wgu_hbm, (index,), 0, wgu_ref, stage_ref, sem)
        _load_bf16(wd_hbm, (index,), 0, wd_ref, stage_ref, sem)

    h = h_ref[...]
    xn = _rms(h, g_ref[...]).astype(BF16)
    o_ref[...] = h + _swiglu_acc(xn, wgu_ref, wd_ref, wd_ref.shape[0])


def _ffn(h, g, wgu, wd, index, tm):
    t, d = h.shape
    d_ff = wd.shape[1]
    hbm = pl.BlockSpec(memory_space=pl.ANY)
    return pl.pallas_call(
        functools.partial(_ffn_kernel, index=index),
        grid=(t // tm,),
        in_specs=[pl.BlockSpec((tm, d), lambda i: (i, 0)), _resident(g.shape), hbm, hbm],
        out_specs=pl.BlockSpec((tm, d), lambda i: (i, 0)),
        out_shape=jax.ShapeDtypeStruct((t, d), F32),
        scratch_shapes=[pltpu.VMEM((d, 2 * d_ff), BF16), pltpu.VMEM((d_ff, d), BF16),
                        pltpu.VMEM((2, STAGE_ROWS, 2 * d_ff), F32), pltpu.SemaphoreType.DMA((2,))],
        compiler_params=_params(("arbitrary",)),
        name="ffn",
    )(h, g, wgu, wd)


def _router_kernel(h_ref, g_ref, rw_ref, o_ref, tot_ref, carry_ref):
    tm = h_ref.shape[0]

    @pl.when(pl.program_id(0) == 0)
    def _():
        carry_ref[...] = jnp.zeros(carry_ref.shape, F32)

    hn = _rms(h_ref[...], g_ref[...])
    hn_hi = hn.astype(BF16)
    hn_lo = (hn - hn_hi.astype(F32)).astype(BF16)
    logits = (jnp.dot(hn_hi, rw_ref[0], preferred_element_type=F32)
              + jnp.dot(hn_lo, rw_ref[0], preferred_element_type=F32)
              + jnp.dot(hn_hi, rw_ref[1], preferred_element_type=F32))
    lane = lax.broadcasted_iota(jnp.int32, (tm, LANES), 1)
    logits = jnp.where(lane < N_EXPERTS, logits, NEG_BIG)
    v1 = jnp.max(logits, axis=-1, keepdims=True)
    i1 = jnp.min(jnp.where(logits == v1, lane, LANES), axis=-1, keepdims=True)
    rest = jnp.where(lane == i1, NEG_BIG, logits)
    v2 = jnp.max(rest, axis=-1, keepdims=True)
    i2 = jnp.min(jnp.where(rest == v2, lane, LANES), axis=-1, keepdims=True)
    e = jnp.exp(v2 - v1)
    w1 = 1.0 / (1.0 + e)
    w2 = e / (1.0 + e)
    cnt = jnp.where((lane == i1) | (lane == i2), 1.0, 0.0)
    r = lax.broadcasted_iota(jnp.int32, (tm, tm), 0)
    c = lax.broadcasted_iota(jnp.int32, (tm, tm), 1)
    tri = jnp.where(c < r, 1.0, 0.0).astype(BF16)
    excl = jnp.dot(tri, cnt.astype(BF16), preferred_element_type=F32) + carry_ref[...]
    rank1 = jnp.sum(jnp.where(lane == i1, excl, 0.0), axis=-1, keepdims=True)
    rank2 = jnp.sum(jnp.where(lane == i2, excl, 0.0), axis=-1, keepdims=True)
    carry_ref[...] = carry_ref[...] + jnp.sum(cnt, axis=0, keepdims=True)
    tot_ref[...] = carry_ref[...]
    packed = jnp.where(lane == 0, i1.astype(F32), jnp.where(lane == 1, i2.astype(F32),
             jnp.where(lane == 2, w1, jnp.where(lane == 3, w2,
             jnp.where(lane == 4, rank1, jnp.where(lane == 5, rank2, 0.0))))))
    o_ref[...] = packed


def _router(h, g, rw_pad, tm):
    t, d = h.shape
    return pl.pallas_call(
        _router_kernel,
        grid=(t // tm,),
        in_specs=[pl.BlockSpec((tm, d), lambda i: (i, 0)), _resident(g.shape),
                  _resident(rw_pad.shape)],
        out_specs=[pl.BlockSpec((tm, LANES), lambda i: (i, 0)),
                   pl.BlockSpec((1, LANES), lambda i: (0, 0))],
        out_shape=[jax.ShapeDtypeStruct((t, LANES), F32), jax.ShapeDtypeStruct((1, LANES), F32)],
        scratch_shapes=[pltpu.VMEM((1, LANES), F32)],
        compiler_params=_params(("arbitrary",)),
        name="router",
    )(h, g, rw_pad)


def _invert_kernel(dest_ref, init_hbm, src_ref):
    c = pl.program_id(0)
    ch = dest_ref.shape[1]

    @pl.when(c == 0)
    def _():
        pltpu.sync_copy(init_hbm, src_ref)

    base = c * ch

    def place(a, x):
        src_ref[dest_ref[0, a]] = base + a
        return x

    lax.fori_loop(0, ch, place, 0, unroll=16)


def _invert(dest, rows):
    nc, _, ch = dest.shape
    return pl.pallas_call(
        _invert_kernel,
        grid=(nc,),
        in_specs=[pl.BlockSpec((None, 1, ch), lambda c: (c, 0, 0), memory_space=pltpu.SMEM),
                  pl.BlockSpec(memory_space=pl.ANY)],
        out_specs=pl.BlockSpec(memory_space=pltpu.SMEM),
        out_shape=jax.ShapeDtypeStruct((rows,), jnp.int32),
        compiler_params=pltpu.CompilerParams(dimension_semantics=("arbitrary",)),
        name="moe_invert",
    )(dest, jnp.full((rows,), -1, jnp.int32))


def _expert_kernel(te_ref, src_cur, src_nxt, orow_prv, orow_cur, h_hbm, g_ref, wgu_hbm, wd_hbm,
                   yt_hbm, xbuf, ybuf, wgu_ref, wd_ref, stage_gu, stage_d, gsem, ssem, wsem):
    i = pl.program_id(0)
    last = pl.num_programs(0) - 1
    tm = xbuf.shape[1]
    s = lax.rem(i, 2)
    o = 1 - s
    e = te_ref[i]

    @pl.when((i == 0) | (e != te_ref[jnp.maximum(i - 1, 0)]))
    def _():
        def stream(copy, n, store):
            copy(0, 0).start()
            for c in range(n):
                if c + 1 < n:
                    copy(c + 1, (c + 1) % 2).start()
                copy(c, c % 2).wait()
                store(c, c % 2)

        wc = stage_gu.shape[2]
        rc = stage_d.shape[1]

        def copy_gu(c, slot):
            return pltpu.make_async_copy(wgu_hbm.at[e, :, pl.ds(c * wc, wc)], stage_gu.at[slot],
                                         wsem.at[slot])

        def store_gu(c, slot):
            wgu_ref[:, c * wc:(c + 1) * wc] = stage_gu[slot].astype(BF16)

        def copy_d(c, slot):
            return pltpu.make_async_copy(wd_hbm.at[e, pl.ds(c * rc, rc), :], stage_d.at[slot],
                                         wsem.at[slot])

        def store_d(c, slot):
            wd_ref[c * rc:(c + 1) * rc, :] = stage_d[slot].astype(BF16)

        stream(copy_gu, wgu_ref.shape[1] // wc, store_gu)
        stream(copy_d, wd_ref.shape[0] // rc, store_d)

    def gather(tok, r, slot):
        return pltpu.make_async_copy(h_hbm.at[pl.ds(tok, 1)],
                                     xbuf.at[slot, pl.ds(r, 1)], gsem.at[slot])

    def scatter(row, r, slot):
        return pltpu.make_async_copy(ybuf.at[slot, pl.ds(r, 1)],
                                     yt_hbm.at[pl.ds(row, 1)], ssem.at[slot])

    def for_rows(fn):
        def body(r, x):
            fn(r)
            return x

        lax.fori_loop(0, tm, body, 0, unroll=8)

    @pl.when(i == 0)
    def _():
        ybuf[1] = jnp.zeros(ybuf.shape[1:], ybuf.dtype)
        for_rows(lambda r: gather(src_cur[0, r], r, 0).start())

    for_rows(lambda r: gather(src_cur[0, r], r, s).wait())
    xn = _rms(xbuf[s], g_ref[...]).astype(BF16)

    def between(c, n_chunks):
        per = -(-tm // (n_chunks // 2))
        for r in range(c * per, min((c + 1) * per, tm)):
            gather(src_nxt[0, r], r, o).start()
            scatter(orow_prv[0, r], r, o).start()

    ybuf[s] = _pack_halves(_swiglu_acc(xn, wgu_ref, wd_ref, wd_ref.shape[0], between))
    for_rows(lambda r: scatter(orow_prv[0, r], r, o).wait())

    @pl.when(i == last)
    def _():
        for_rows(lambda r: gather(src_nxt[0, r], r, o).wait())
        for_rows(lambda r: scatter(orow_cur[0, r], r, s).start())
        for_rows(lambda r: scatter(orow_cur[0, r], r, s).wait())


def _experts(tile_expert, src_tok, out_row, h, g, wgu, wd, yt_rows):
    n, _, tm = src_tok.shape
    d = h.shape[1]
    d_ff = wd.shape[1]
    smem = functools.partial(pl.BlockSpec, (None, 1, tm), memory_space=pltpu.SMEM)
    grid_spec = pltpu.PrefetchScalarGridSpec(
        num_scalar_prefetch=1,
        grid=(n,),
        in_specs=[
            smem(lambda i, te: (i, 0, 0)),
            smem(lambda i, te: (jnp.minimum(i + 1, n - 1), 0, 0)),
            smem(lambda i, te: (i, 0, 0)),
            smem(lambda i, te: (i + 1, 0, 0)),
            pl.BlockSpec(memory_space=pl.ANY),
            pl.BlockSpec(g.shape, lambda i, te: (0, 0), pipeline_mode=pl.Buffered(1)),
            pl.BlockSpec(memory_space=pl.ANY),
            pl.BlockSpec(memory_space=pl.ANY),
        ],
        out_specs=pl.BlockSpec(memory_space=pl.ANY),
        scratch_shapes=[pltpu.VMEM((2, tm, d), F32), pltpu.VMEM((2, tm, d // 2), jnp.uint32),
                        pltpu.VMEM((d, 2 * d_ff), BF16), pltpu.VMEM((d_ff, d), BF16),
                        pltpu.VMEM((2, d, WEIGHT_STAGE), F32), pltpu.VMEM((2, WEIGHT_STAGE, d), F32),
                        pltpu.SemaphoreType.DMA((2,)), pltpu.SemaphoreType.DMA((2,)),
                        pltpu.SemaphoreType.DMA((2,))],
    )
    return pl.pallas_call(
        _expert_kernel,
        grid_spec=grid_spec,
        out_shape=jax.ShapeDtypeStruct((yt_rows, d // 2), jnp.uint32),
        compiler_params=_params(("arbitrary",)),
        name="moe_experts",
    )(tile_expert, src_tok, src_tok, out_row, out_row, h, g, wgu, wd)


def _combine_kernel(h_ref, pk_ref, y1_ref, y2_ref, o_ref):
    pk = pk_ref[...]
    o_ref[...] = (h_ref[...] + pk[:, 2:3] * _unpack_halves(y1_ref[...])
                  + pk[:, 3:4] * _unpack_halves(y2_ref[...]))


def _combine(h, packed, yt, tm):
    t, d = h.shape
    nb = t // tm
    return pl.pallas_call(
        _combine_kernel,
        grid=(nb,),
        in_specs=[pl.BlockSpec((tm, d), lambda i: (i, 0)),
                  pl.BlockSpec((tm, LANES), lambda i: (i, 0)),
                  pl.BlockSpec((tm, d // 2), lambda i: (i, 0)),
                  pl.BlockSpec((tm, d // 2), lambda i: (nb + i, 0))],
        out_specs=pl.BlockSpec((tm, d), lambda i: (i, 0)),
        out_shape=jax.ShapeDtypeStruct((t, d), F32),
        compiler_params=_params(("parallel",)),
        name="moe_combine",
    )(h, packed, yt, yt)


def _rel_bucket(rel):
    nb = REL_BUCKETS // 2
    max_exact = nb // 2
    n = jnp.abs(rel)
    nf = jnp.maximum(n, 1).astype(F32)
    large = max_exact + (jnp.log(nf / max_exact) / math.log(REL_MAX_DIST / max_exact)
                         * (nb - max_exact)).astype(jnp.int32)
    large = jnp.minimum(large, nb - 1)
    return jnp.where(rel > 0, nb, 0) + jnp.where(n < max_exact, n, large)


def _near_bias(rel_bias):
    t = ATT_T
    qp = jnp.arange(t)[:, None]
    kp = jnp.arange(t)[None, :]

    def lookup(rel):
        onehot = jax.nn.one_hot(_rel_bucket(rel), REL_BUCKETS, dtype=F32)
        return jnp.einsum('...b,bm->...m', onehot, rel_bias, precision=lax.Precision.HIGHEST)

    far = lookup(jnp.full((), -(2 * t), jnp.int32))
    prev = (lookup(kp - t - qp) - far) * LOG2E
    diag = (lookup(kp - qp) - far) * LOG2E
    diag = jnp.where(((kp // CHUNK) <= (qp // CHUNK))[:, :, None], diag, NEG_BIG)
    both = jnp.stack([prev, diag], axis=0).reshape(2, t, t, DIFF_HEADS, 2)
    return both.transpose(3, 0, 2, 4, 1).reshape(DIFF_HEADS, 2, t, 2 * t).astype(F32)


def kernel(x, rel_bias, norm1_g, w_in, pool_w, pool_scale, conv_w, sgu_ln_g, sgu_w, sgu_b, q_norm_g, k_norm_g, diff_lambda, subln_g, w_branch_pool, w_branch_conv, w_branch_sgu, w_branch_attn, w_out, norm2_g, ffn_w_gate_up, ffn_w_down, router_w, moe_w_gate_up, moe_w_down):
    b, s, d = x.shape
    t = b * s
    depth = w_in.shape[0]
    pw = pool_scale.shape[1]
    cw = conv_w.shape[2]
    sw = sgu_ln_g.shape[1]
    aw = w_branch_attn.shape[1]
    mix_cols = pw + 3 * cw + 2 * sw + 3 * aw
    nb = REL_BUCKETS // 2
    assert nb // 2 + int(math.log((ATT_T + 1) / (nb // 2)) / math.log(REL_MAX_DIST / (nb // 2))
                         * (nb - nb // 2)) >= nb - 1
    qk_off = pw + 3 * cw + 2 * sw
    z_cols = qk_off + 2 * aw
    q_col = qk_off // LANES
    k_col = q_col + aw // LANES
    tm = min(512, t)
    td = min(DENSE_TM, s)
    ts = min(512, s)

    bias_near = _near_bias(rel_bias)
    tri = jnp.tril(jnp.ones((SGU_SEG, SGU_SEG), bool))
    gd = pw // POOL_GROUPS

    h = x.reshape(t, d)
    for layer in range(depth):
        lam_init = 0.8 - 0.6 * math.exp(-0.3 * layer)
        poolw_bd = jnp.zeros((pw, pw), F32)
        for g in range(POOL_GROUPS):
            poolw_bd = poolw_bd.at[g * gd:(g + 1) * gd, g * gd:(g + 1) * gd].set(pool_w[layer, g])
        sguw_cat = jnp.where(tri[None], sgu_w[layer], 0.0).transpose(1, 0, 2).reshape(
            SGU_SEG, SGU_GROUPS * SGU_SEG).astype(BF16)
        sgub_full = jnp.repeat(sgu_b[layer].T, sw // SGU_GROUPS, axis=1)

        z, vt = _in_proj(h, norm1_g[layer][None], w_in, layer, z_cols, mix_cols - z_cols, td, s)
        z3 = z.reshape(b, s, z_cols)
        y_abc = _local_mix(z3, poolw_bd.astype(BF16), pool_scale[layer][None], conv_w[layer],
                           sgu_ln_g[layer][None], sguw_cat, sgub_full, ts, pw, cw, sw)
        y_d = _diff_attn(z3, vt, bias_near, jnp.tile(q_norm_g[layer], 2)[None],
                         jnp.tile(k_norm_g[layer], 2)[None], diff_lambda[layer],
                         subln_g[layer][None], lam_init, q_col, k_col)
        h = _merge(h, y_abc.reshape(t, -1), y_d.reshape(t, -1), norm1_g[layer][None], w_in,
                   (w_branch_pool, w_branch_conv, w_branch_sgu, w_branch_attn), w_out, layer,
                   mix_cols, td, (pw, cw, sw, aw))

        g2 = norm2_g[layer][None]
        if layer % 2 == 0:
            h = _ffn(h, g2, ffn_w_gate_up, ffn_w_down, layer // 2, td)
        else:
            li = layer // 2
            rw_pad = jnp.zeros((d, LANES), F32).at[:, :N_EXPERTS].set(router_w[li])
            rw_hi = rw_pad.astype(BF16)
            rw_split = jnp.stack([rw_hi, (rw_pad - rw_hi.astype(F32)).astype(BF16)])
            packed, totals = _router(h, g2, rw_split, tm)
            n_e = totals[0, :N_EXPERTS].astype(jnp.int32)
            n_pad = ((n_e + tm - 1) // tm) * tm
            ends = jnp.cumsum(n_pad)
            starts = ends - n_pad
            e1 = packed[:, 0].astype(jnp.int32)
            e2 = packed[:, 1].astype(jnp.int32)
            eids = jnp.arange(N_EXPERTS)[None, :]
            dest1 = (jnp.sum(jnp.where(e1[:, None] == eids, starts[None, :], 0), axis=1)
                     + packed[:, 4].astype(jnp.int32))
            dest2 = (jnp.sum(jnp.where(e2[:, None] == eids, starts[None, :], 0), axis=1)
                     + packed[:, 5].astype(jnp.int32))
            rows = 2 * t + N_EXPERTS * tm
            n_tiles = rows // tm
            tile_expert = jnp.minimum(
                jnp.sum((jnp.arange(n_tiles)[:, None] * tm) >= ends[None, :], axis=1),
                N_EXPERTS - 1).astype(jnp.int32)
            ch = min(4096, t)
            dest_a = jnp.concatenate([dest1, dest2]).reshape(2 * t // ch, 1, ch)
            src = _invert(dest_a, rows)
            is_pad = src < 0
            src_tok = jnp.where(is_pad, 0, jnp.where(src >= t, src - t, src))
            pad_rank = jnp.cumsum(is_pad.astype(jnp.int32)) - 1
            out_row = jnp.where(is_pad, 2 * t + tm + pad_rank, src)
            spare = 2 * t + jnp.arange(tm, dtype=jnp.int32)
            yt = _experts(tile_expert, src_tok.reshape(n_tiles, 1, tm),
                          jnp.concatenate([spare, out_row]).reshape(n_tiles + 1, 1, tm), h, g2,
                          moe_w_gate_up[li], moe_w_down[li], rows + tm)
            h = _combine(h, packed, yt, tm)
    return h.reshape(b, s, d)
```

```python
import functools
import math

import jax
import jax.numpy as jnp
from jax import lax
from jax.experimental import pallas as pl
from jax.experimental.pallas import tpu as pltpu

F32 = jnp.float32
BF16 = jnp.bfloat16

NORM_EPS = 1e-6
CHUNK = 64
POOL_GROUPS = 4
SGU_GROUPS = 4
SGU_SEG = 128
DIFF_HEADS = 4
DIFF_QK_DIM = 64
DIFF_V_DIM = 128
REL_BUCKETS = 32
REL_MAX_DIST = 128
N_EXPERTS = 8
LANES = 128
V7X_VMEM_BYTES = 64 * 1024 * 1024
VMEM_LIMIT = V7X_VMEM_BYTES - 8 * 1024 * 1024
NEG_BIG = -1e30
LOG2E = math.log2(math.e)

HALO = 16
ATT_T = 256
ATT_HEADS_PER_STEP = 2
FF_CHUNK = 256
WEIGHT_STAGE = 512
DENSE_TM = 1024


def _rms(x, g):
    return x * lax.rsqrt(jnp.mean(x * x, axis=-1, keepdims=True) + NORM_EPS) * g


def _resident(shape):
    nd = len(shape)
    return pl.BlockSpec(shape, lambda *_: (0,) * nd, pipeline_mode=pl.Buffered(1))


def _params(sem):
    return pltpu.CompilerParams(dimension_semantics=sem, vmem_limit_bytes=VMEM_LIMIT)


def _hi_lo(x):
    hi = x.astype(BF16)
    return hi, (x - hi.astype(F32)).astype(BF16)


def _pack_halves(x):
    n = x.shape[1] // 2
    bits = lax.bitcast_convert_type(x, jnp.uint32) + jnp.uint32(0x8000)
    return (bits[:, :n] & jnp.uint32(0xFFFF0000)) | (bits[:, n:] >> 16)


def _unpack_halves(p):
    hi = lax.bitcast_convert_type(p & jnp.uint32(0xFFFF0000), F32)
    lo = lax.bitcast_convert_type(p << 16, F32)
    return jnp.concatenate([hi, lo], axis=1)


def _in_proj_kernel(x_ref, g_ref, w_ref, wvt_ref, o_ref, vt_ref, *, n_chunk):
    xn = _rms(x_ref[...], g_ref[...]).astype(BF16)
    n = o_ref.shape[1]
    for j in range(n // n_chunk):
        sl = slice(j * n_chunk, (j + 1) * n_chunk)
        o_ref[:, sl] = jnp.dot(xn, w_ref[:, sl], preferred_element_type=F32).astype(o_ref.dtype)
    vt_ref[...] = lax.dot_general(wvt_ref[...], xn, (((1,), (1,)), ((), ())),
                                  preferred_element_type=F32).astype(vt_ref.dtype)


def _in_proj(h, g, w, w_vt, tm, seq):
    t, d = h.shape
    n = w.shape[1]
    nv = w_vt.shape[0]
    per_seq = seq // tm
    return pl.pallas_call(
        functools.partial(_in_proj_kernel, n_chunk=512),
        grid=(t // tm,),
        in_specs=[pl.BlockSpec((tm, d), lambda i: (i, 0)), _resident((1, d)), _resident((d, n)),
                  _resident(w_vt.shape)],
        out_specs=[pl.BlockSpec((tm, n), lambda i: (i, 0)),
                   pl.BlockSpec((None, nv, tm), lambda i: (i // per_seq, 0, i % per_seq))],
        out_shape=[jax.ShapeDtypeStruct((t, n), BF16),
                   jax.ShapeDtypeStruct((t // seq, nv, seq), BF16)],
        compiler_params=_params(("parallel",)),
        name="in_proj",
    )(h, g, w, w_vt)


def _local_mix_kernel(z_ref, halo_ref, poolw_ref, pscale_ref, convw_ref, lng_ref, sguw_ref,
                      sgub_ref, o_ref, *, pw, cw):
    ts = z_ref.shape[0]
    i = pl.program_id(1)
    z = z_ref[...].astype(F32)
    halo = halo_ref[...].astype(F32)
    halo = jnp.where(i > 0, halo, 0.0)
    ext = jnp.concatenate([halo[:, :pw + 3 * cw], z[:, :pw + 3 * cw]], axis=0)
    rows = ext.shape[0]

    def back(x, k):
        return pltpu.roll(x, k, axis=0)

    a = ext[:, :pw]
    s2 = a + back(a, 1)
    s4 = s2 + back(s2, 2)
    s8 = s4 + back(s4, 4)
    s16 = s8 + back(s8, 8)
    lane = lax.broadcasted_iota(jnp.int32, (rows, pw), 1)
    grp = lane // (pw // POOL_GROUPS)
    win_sum = jnp.where(grp == 0, s2, jnp.where(grp == 1, s4, jnp.where(grp == 2, s8, s16)))
    win = jnp.where(grp == 0, 2, jnp.where(grp == 1, 4, jnp.where(grp == 2, 8, 16)))
    pos = i * ts - HALO + lax.broadcasted_iota(jnp.int32, (rows, pw), 0)
    count = jnp.minimum(pos + 1, win).astype(F32)
    pooled = (win_sum / jnp.maximum(count, 1.0) - a)[HALO:]
    y_a = jnp.dot(pooled.astype(BF16), poolw_ref[...], preferred_element_type=F32) * pscale_ref[...]
    o_ref[:, 0:pw] = y_a.astype(o_ref.dtype)

    b_gate = z[:, pw:pw + cw]
    zc = ext[:, pw + cw:pw + 2 * cw] * ext[:, pw + 2 * cw:pw + 3 * cw]
    conv = (convw_ref[0:1, :] * back(zc, 2) + convw_ref[1:2, :] * back(zc, 1)
            + convw_ref[2:3, :] * zc)[HALO:]
    o_ref[:, pw:pw + cw] = (b_gate * conv).astype(o_ref.dtype)

    sw = (z.shape[1] - pw - 3 * cw) // 2
    zc_uv = z[:, pw + 3 * cw:]
    uv = 0.5 * zc_uv * (1.0 + lax.erf(zc_uv * math.sqrt(0.5)))
    u = uv[:, :sw]
    v = uv[:, sw:]
    mu = jnp.mean(v, axis=-1, keepdims=True)
    var = jnp.mean(jnp.square(v - mu), axis=-1, keepdims=True)
    vn = (v - mu) * lax.rsqrt(var + NORM_EPS) * lng_ref[...]
    glane = lax.broadcasted_iota(jnp.int32, (SGU_SEG, sw), 1) // (sw // SGU_GROUPS)
    wcat = sguw_ref[...]
    bias = sgub_ref[...]
    for n in range(ts // SGU_SEG):
        seg = vn[n * SGU_SEG:(n + 1) * SGU_SEG]
        rhs = jnp.concatenate(
            [jnp.where(glane == g, seg, 0.0) for g in range(SGU_GROUPS)], axis=0).astype(BF16)
        s = jnp.dot(wcat, rhs, preferred_element_type=F32) + bias
        o_ref[n * SGU_SEG:(n + 1) * SGU_SEG, pw + cw:pw + cw + sw] = (
            u[n * SGU_SEG:(n + 1) * SGU_SEG] * s).astype(o_ref.dtype)


def _local_mix(z3, poolw_bd, pscale, convw, lng, sguw_cat, sgub_full, ts, pw, cw, sw):
    b, s, _ = z3.shape
    cols = pw + 3 * cw + 2 * sw
    hb = ts // HALO
    return pl.pallas_call(
        functools.partial(_local_mix_kernel, pw=pw, cw=cw),
        grid=(b, s // ts),
        in_specs=[
            pl.BlockSpec((None, ts, cols), lambda bi, i: (bi, i, 0)),
            pl.BlockSpec((None, HALO, cols), lambda bi, i: (bi, jnp.maximum(i * hb - 1, 0), 0)),
            _resident(poolw_bd.shape), _resident(pscale.shape), _resident(convw.shape),
            _resident(lng.shape), _resident(sguw_cat.shape), _resident(sgub_full.shape),
        ],
        out_specs=pl.BlockSpec((None, ts, pw + cw + sw), lambda bi, i: (bi, i, 0)),
        out_shape=jax.ShapeDtypeStruct((b, s, pw + cw + sw), BF16),
        compiler_params=_params(("parallel", "parallel")),
        name="local_mix",
    )(z3, z3, poolw_bd, pscale, convw, lng, sguw_cat, sgub_full)


def _diff_attn_kernel(q_ref, k_ref, vt_ref, bias_ref, qg_ref, kg_ref, lam_ref, sg_ref, o_ref,
                      kn_ref, qs_ref, st_ref, m_ref, *, lam_init):
    tq = q_ref.shape[0]
    hp = qs_ref.shape[0]
    nt = k_ref.shape[0] // tq
    hw = 2 * DIFF_QK_DIM
    i = pl.program_id(2)
    half = lax.broadcasted_iota(jnp.int32, (1, hw), 1) < DIFF_QK_DIM
    same_map = (lax.broadcasted_iota(jnp.int32, (hw, hw), 0) // DIFF_QK_DIM
                == lax.broadcasted_iota(jnp.int32, (hw, hw), 1) // DIFF_QK_DIM)
    ones_map = jnp.where(same_map, 1.0, 0.0).astype(BF16)

    def qk_norm_mxu(x, g):
        sq_hi, sq_lo = _hi_lo(x * x)
        ss = (jnp.dot(sq_hi, ones_map, preferred_element_type=F32)
              + jnp.dot(sq_lo, ones_map, preferred_element_type=F32))
        return x * lax.rsqrt(ss * (1.0 / DIFF_QK_DIM) + NORM_EPS) * g

    def qk_norm(x, g):
        sq = x * x
        ss0 = jnp.sum(jnp.where(half, sq, 0.0), axis=-1, keepdims=True)
        ss1 = jnp.sum(jnp.where(half, 0.0, sq), axis=-1, keepdims=True)
        r0 = lax.rsqrt(ss0 * (1.0 / DIFF_QK_DIM) + NORM_EPS)
        r1 = lax.rsqrt(ss1 * (1.0 / DIFF_QK_DIM) + NORM_EPS)
        return x * jnp.where(half, r0, r1) * g

    @pl.when(i == 0)
    def _():
        for h in range(hp):
            cols = slice(h * hw, (h + 1) * hw)
            for j in range(nt):
                rows = slice(j * tq, (j + 1) * tq)
                kn_ref[h, rows, :] = qk_norm_mxu(k_ref[rows, cols].astype(F32),
                                                 kg_ref[...]).astype(BF16)

    def step(c):
        cur, prv = c % 2, 1 - c % 2
        scoring = c < nt
        if scoring:
            for h in range(hp):
                qn = (qk_norm(q_ref[:, h * hw:(h + 1) * hw].astype(F32), qg_ref[...])
                      * (DIFF_QK_DIM ** -0.5 * LOG2E))
                qs_ref[h, 0:tq, :] = jnp.where(half, qn, 0.0).astype(BF16)
                qs_ref[h, tq:2 * tq, :] = jnp.where(half, 0.0, qn).astype(BF16)
        m_new = [jnp.full((1, 2 * tq), NEG_BIG, F32) for _ in range(hp)]
        m_old = [m_ref[prv, h] for h in range(hp)] if c >= 1 else None
        l = [jnp.zeros((1, 2 * tq), F32) for _ in range(hp)]
        acc = [jnp.zeros((DIFF_V_DIM, 2 * tq), F32) for _ in range(hp)]
        for j in range(c + 1):
            rows = slice(j * tq, (j + 1) * tq)
            for h in range(hp):
                if scoring:
                    st = lax.dot_general(kn_ref[h, rows, :], qs_ref[h],
                                         (((1,), (1,)), ((), ())), preferred_element_type=F32)
                    if j >= c - 1:
                        st = st + bias_ref[h, j - (c - 1)]
                    st_ref[cur, h, rows, :] = st
                    m_new[h] = jnp.maximum(m_new[h], jnp.max(st, axis=0, keepdims=True))
                if j < c:
                    p = jnp.exp2(st_ref[prv, h, rows, :] - m_old[h])
                    l[h] = l[h] + jnp.sum(p, axis=0, keepdims=True)
                    acc[h] = acc[h] + jnp.dot(vt_ref[h * DIFF_V_DIM:(h + 1) * DIFF_V_DIM, rows],
                                              p.astype(BF16), preferred_element_type=F32)
        if scoring:
            for h in range(hp):
                m_ref[cur, h] = m_new[h]
        if c >= 1:
            lp = lam_ref[...]
            lam = (jnp.exp(jnp.sum(lp[0:1] * lp[1:2], axis=-1, keepdims=True))
                   - jnp.exp(jnp.sum(lp[2:3] * lp[3:4], axis=-1, keepdims=True)) + lam_init)
            for h in range(hp):
                o = (acc[h][:, :tq] * (1.0 / l[h][:, :tq])
                     - acc[h][:, tq:] * (lam / l[h][:, tq:]))
                o = o * lax.rsqrt(jnp.mean(o * o, axis=0, keepdims=True) + NORM_EPS)
                o_ref[:, h * DIFF_V_DIM:(h + 1) * DIFF_V_DIM] = (
                    o.T * (sg_ref[...] * (1.0 - lam_init))).astype(o_ref.dtype)

    for c in range(nt + 1):
        pl.when(i == c)(functools.partial(step, c))


def _diff_attn(z3, vt, bias_near, qg2, kg2, lam_p, subln_g, lam_init, q_col, k_col):
    b, s, _ = z3.shape
    tq = ATT_T
    hp = ATT_HEADS_PER_STEP
    nt = s // tq
    hw = 2 * DIFF_QK_DIM
    return pl.pallas_call(
        functools.partial(_diff_attn_kernel, lam_init=lam_init),
        grid=(b, DIFF_HEADS // hp, nt + 1),
        in_specs=[
            pl.BlockSpec((None, tq, hp * hw),
                         lambda bi, g, i: (bi, jnp.minimum(i, nt - 1), q_col // hp + g)),
            pl.BlockSpec((None, s, hp * hw), lambda bi, g, i: (bi, 0, k_col // hp + g)),
            pl.BlockSpec((None, hp * DIFF_V_DIM, s), lambda bi, g, i: (bi, g, 0)),
            pl.BlockSpec((hp, 2, tq, 2 * tq), lambda bi, g, i: (g, 0, 0, 0)),
            _resident(qg2.shape), _resident(kg2.shape), _resident(lam_p.shape),
            _resident(subln_g.shape),
        ],
        out_specs=pl.BlockSpec((None, tq, hp * DIFF_V_DIM),
                               lambda bi, g, i: (bi, jnp.maximum(i - 1, 0), g)),
        out_shape=jax.ShapeDtypeStruct((b, s, DIFF_HEADS * DIFF_V_DIM), BF16),
        scratch_shapes=[
            pltpu.VMEM((hp, s, hw), BF16),
            pltpu.VMEM((hp, 2 * tq, hw), BF16),
            pltpu.VMEM((2, hp, s, 2 * tq), F32),
            pltpu.VMEM((2, hp, 1, 2 * tq), F32),
        ],
        compiler_params=_params(("parallel", "parallel", "arbitrary")),
        name="diff_attn",
    )(z3, z3, vt, bias_near, qg2, kg2, lam_p, subln_g)


def _merge_kernel(h_ref, yabc_ref, yd_ref, g_ref, wg_ref, wb_ref, wo_ref, o_ref, *, widths):
    h = h_ref[...]
    d = h.shape[1]
    xn = _rms(h, g_ref[...]).astype(BF16)
    merged = None
    off = 0
    yoff = 0
    for bi, w in enumerate(widths):
        gate = jax.nn.sigmoid(jnp.dot(xn, wg_ref[:, bi * d:(bi + 1) * d],
                                      preferred_element_type=F32))
        if bi < len(widths) - 1:
            y = yabc_ref[:, yoff:yoff + w]
            yoff += w
        else:
            y = yd_ref[...]
        proj = jnp.dot(y, wb_ref[off:off + w, :], preferred_element_type=F32)
        off += w
        merged = gate * proj if merged is None else merged + gate * proj
    o_ref[...] = h + jnp.dot(merged.astype(BF16), wo_ref[...], preferred_element_type=F32)


def _merge(h, y_abc, y_d, g, wg, wb, wo, tm, widths):
    t, d = h.shape
    return pl.pallas_call(
        functools.partial(_merge_kernel, widths=widths),
        grid=(t // tm,),
        in_specs=[
            pl.BlockSpec((tm, d), lambda i: (i, 0)),
            pl.BlockSpec((tm, y_abc.shape[1]), lambda i: (i, 0)),
            pl.BlockSpec((tm, y_d.shape[1]), lambda i: (i, 0)),
            _resident(g.shape), _resident(wg.shape), _resident(wb.shape), _resident(wo.shape),
        ],
        out_specs=pl.BlockSpec((tm, d), lambda i: (i, 0)),
        out_shape=jax.ShapeDtypeStruct((t, d), F32),
        compiler_params=_params(("parallel",)),
        name="merge",
    )(h, y_abc, y_d, g, wg, wb, wo)


def _swiglu_acc(xn, wgu_ref, wd_ref, d_ff, between=None):
    acc = None
    n_chunks = d_ff // FF_CHUNK
    for c in range(n_chunks):
        lo = c * FF_CHUNK
        g = jnp.dot(xn, wgu_ref[:, lo:lo + FF_CHUNK], preferred_element_type=F32)
        u = jnp.dot(xn, wgu_ref[:, d_ff + lo:d_ff + lo + FF_CHUNK], preferred_element_type=F32)
        act = (g * jax.nn.sigmoid(g) * u).astype(BF16)
        part = jnp.dot(act, wd_ref[lo:lo + FF_CHUNK, :], preferred_element_type=F32)
        acc = part if acc is None else acc + part
        if between is not None:
            between(c, n_chunks)
    return acc


def _ffn_kernel(h_ref, g_ref, wgu_ref, wd_ref, o_ref):
    h = h_ref[...]
    xn = _rms(h, g_ref[...]).astype(BF16)
    o_ref[...] = h + _swiglu_acc(xn, wgu_ref, wd_ref, wd_ref.shape[0])


def _ffn(h, g, wgu, wd, tm):
    t, d = h.shape
    return pl.pallas_call(
        _ffn_kernel,
        grid=(t // tm,),
        in_specs=[pl.BlockSpec((tm, d), lambda i: (i, 0)), _resident(g.shape),
                  _resident(wgu.shape), _resident(wd.shape)],
        out_specs=pl.BlockSpec((tm, d), lambda i: (i, 0)),
        out_shape=jax.ShapeDtypeStruct((t, d), F32),
        compiler_params=_params(("parallel",)),
        name="ffn",
    )(h, g, wgu, wd)


def _router_kernel(h_ref, g_ref, rw_ref, o_ref, tot_ref, carry_ref):
    tm = h_ref.shape[0]

    @pl.when(pl.program_id(0) == 0)
    def _():
        carry_ref[...] = jnp.zeros(carry_ref.shape, F32)

    hn = _rms(h_ref[...], g_ref[...])
    hn_hi, hn_lo = _hi_lo(hn)
    logits = (jnp.dot(hn_hi, rw_ref[0], preferred_element_type=F32)
              + jnp.dot(hn_lo, rw_ref[0], preferred_element_type=F32)
              + jnp.dot(hn_hi, rw_ref[1], preferred_element_type=F32))
    lane = lax.broadcasted_iota(jnp.int32, (tm, LANES), 1)
    logits = jnp.where(lane < N_EXPERTS, logits, NEG_BIG)
    v1 = jnp.max(logits, axis=-1, keepdims=True)
    i1 = jnp.min(jnp.where(logits == v1, lane, LANES), axis=-1, keepdims=True)
    rest = jnp.where(lane == i1, NEG_BIG, logits)
    v2 = jnp.max(rest, axis=-1, keepdims=True)
    i2 = jnp.min(jnp.where(rest == v2, lane, LANES), axis=-1, keepdims=True)
    e = jnp.exp(v2 - v1)
    w1 = 1.0 / (1.0 + e)
    w2 = e / (1.0 + e)
    cnt = jnp.where((lane == i1) | (lane == i2), 1.0, 0.0)
    r = lax.broadcasted_iota(jnp.int32, (tm, tm), 0)
    c = lax.broadcasted_iota(jnp.int32, (tm, tm), 1)
    tri = jnp.where(c < r, 1.0, 0.0).astype(BF16)
    excl = jnp.dot(tri, cnt.astype(BF16), preferred_element_type=F32) + carry_ref[...]
    rank1 = jnp.sum(jnp.where(lane == i1, excl, 0.0), axis=-1, keepdims=True)
    rank2 = jnp.sum(jnp.where(lane == i2, excl, 0.0), axis=-1, keepdims=True)
    carry_ref[...] = carry_ref[...] + jnp.sum(cnt, axis=0, keepdims=True)
    tot_ref[...] = carry_ref[...]
    packed = jnp.where(lane == 0, i1.astype(F32), jnp.where(lane == 1, i2.astype(F32),
             jnp.where(lane == 2, w1, jnp.where(lane == 3, w2,
             jnp.where(lane == 4, rank1, jnp.where(lane == 5, rank2, 0.0))))))
    o_ref[...] = packed


def _router(h, g, rw_pad, tm):
    t, d = h.shape
    return pl.pallas_call(
        _router_kernel,
        grid=(t // tm,),
        in_specs=[pl.BlockSpec((tm, d), lambda i: (i, 0)), _resident(g.shape),
                  _resident(rw_pad.shape)],
        out_specs=[pl.BlockSpec((tm, LANES), lambda i: (i, 0)),
                   pl.BlockSpec((1, LANES), lambda i: (0, 0))],
        out_shape=[jax.ShapeDtypeStruct((t, LANES), F32), jax.ShapeDtypeStruct((1, LANES), F32)],
        scratch_shapes=[pltpu.VMEM((1, LANES), F32)],
        compiler_params=_params(("arbitrary",)),
        name="router",
    )(h, g, rw_pad)


def _invert_kernel(dest_ref, init_hbm, src_ref):
    c = pl.program_id(0)
    ch = dest_ref.shape[1]

    @pl.when(c == 0)
    def _():
        pltpu.sync_copy(init_hbm, src_ref)

    base = c * ch

    def place(a, x):
        src_ref[dest_ref[0, a]] = base + a
        return x

    lax.fori_loop(0, ch, place, 0, unroll=16)


def _invert(dest, rows):
    nc, _, ch = dest.shape
    return pl.pallas_call(
        _invert_kernel,
        grid=(nc,),
        in_specs=[pl.BlockSpec((None, 1, ch), lambda c: (c, 0, 0), memory_space=pltpu.SMEM),
                  pl.BlockSpec(memory_space=pl.ANY)],
        out_specs=pl.BlockSpec(memory_space=pltpu.SMEM),
        out_shape=jax.ShapeDtypeStruct((rows,), jnp.int32),
        compiler_params=pltpu.CompilerParams(dimension_semantics=("arbitrary",)),
        name="moe_invert",
    )(dest, jnp.full((rows,), -1, jnp.int32))


def _stream_chunks(copy, n, consume):
    copy(0, 0).start()
    for c in range(n):
        if c + 1 < n:
            copy(c + 1, (c + 1) % 2).start()
        copy(c, c % 2).wait()
        consume(c, c % 2)


def _expert_kernel(te_ref, nu_ref, src_cur, src_nxt, orow_prv, orow_cur, h_hbm, g_ref, wgu_hbm,
                   wd_hbm, yt_hbm, xbuf, ybuf, wgu_ref, wd_ref, stage_gu, stage_d, gsem, ssem, wsem):
    i = pl.program_id(0)
    fin = jnp.minimum(nu_ref[0], pl.num_programs(0) - 1)
    tm = xbuf.shape[1]
    s = lax.rem(i, 2)
    o = 1 - s
    e = te_ref[i]

    def gather(tok, r, slot):
        return pltpu.make_async_copy(h_hbm.at[pl.ds(tok, 1)],
                                     xbuf.at[slot, pl.ds(r, 1)], gsem.at[slot])

    def scatter(row, r, slot):
        return pltpu.make_async_copy(ybuf.at[slot, pl.ds(r, 1)],
                                     yt_hbm.at[pl.ds(row, 1)], ssem.at[slot])

    def for_rows(fn):
        def body(r, x):
            fn(r)
            return x

        lax.fori_loop(0, tm, body, 0, unroll=8)

    def load_weights():
        wc = stage_gu.shape[2]
        rc = stage_d.shape[1]

        def copy_gu(c, slot):
            return pltpu.make_async_copy(wgu_hbm.at[e, :, pl.ds(c * wc, wc)], stage_gu.at[slot],
                                         wsem.at[slot])

        def store_gu(c, slot):
            wgu_ref[:, c * wc:(c + 1) * wc] = stage_gu[slot].astype(BF16)

        def copy_d(c, slot):
            return pltpu.make_async_copy(wd_hbm.at[e, pl.ds(c * rc, rc), :], stage_d.at[slot],
                                         wsem.at[slot])

        def store_d(c, slot):
            wd_ref[c * rc:(c + 1) * rc, :] = stage_d[slot].astype(BF16)

        _stream_chunks(copy_gu, wgu_ref.shape[1] // wc, store_gu)
        _stream_chunks(copy_d, wd_ref.shape[0] // rc, store_d)

    def spare_fill(k):
        rows0 = yt_hbm.shape[0] - (N_EXPERTS + 1 - k) * tm
        return pltpu.make_async_copy(ybuf.at[1], yt_hbm.at[pl.ds(rows0, tm)], ssem.at[0])

    @pl.when(i <= fin)
    def _():
        @pl.when((i == 0) | (e != te_ref[jnp.maximum(i - 1, 0)]))
        def _():
            load_weights()

        @pl.when(i == 0)
        def _():
            ybuf[1] = jnp.zeros(ybuf.shape[1:], ybuf.dtype)
            for k in range(1, N_EXPERTS + 1):
                spare_fill(k).start()
            for k in range(1, N_EXPERTS + 1):
                spare_fill(k).wait()
            for_rows(lambda r: gather(src_cur[0, r], r, 0).start())

        for_rows(lambda r: gather(src_cur[0, r], r, s).wait())
        xn = _rms(xbuf[s], g_ref[...]).astype(BF16)

        def between(c, n_chunks):
            per = -(-tm // (n_chunks // 2))
            for r in range(c * per, min((c + 1) * per, tm)):
                gather(src_nxt[0, r], r, o).start()
                scatter(orow_prv[0, r], r, o).start()

        ybuf[s] = _pack_halves(_swiglu_acc(xn, wgu_ref, wd_ref, wd_ref.shape[0], between))
        for_rows(lambda r: scatter(orow_prv[0, r], r, o).wait())

        @pl.when(i == fin)
        def _():
            for_rows(lambda r: gather(src_nxt[0, r], r, o).wait())
            for_rows(lambda r: scatter(orow_cur[0, r], r, s).start())
            for_rows(lambda r: scatter(orow_cur[0, r], r, s).wait())


def _experts(tile_expert, n_used, src_tok, out_row, h, g, wgu, wd, yt_rows):
    n, _, tm = src_tok.shape
    d = h.shape[1]
    d_ff = wd.shape[1]
    smem = functools.partial(pl.BlockSpec, (None, 1, tm), memory_space=pltpu.SMEM)
    grid_spec = pltpu.PrefetchScalarGridSpec(
        num_scalar_prefetch=2,
        grid=(n,),
        in_specs=[
            smem(lambda i, te, nu: (i, 0, 0)),
            smem(lambda i, te, nu: (jnp.minimum(i + 1, n - 1), 0, 0)),
            smem(lambda i, te, nu: (i, 0, 0)),
            smem(lambda i, te, nu: (i + 1, 0, 0)),
            pl.BlockSpec(memory_space=pl.ANY),
            pl.BlockSpec(g.shape, lambda i, te, nu: (0, 0), pipeline_mode=pl.Buffered(1)),
            pl.BlockSpec(memory_space=pl.ANY),
            pl.BlockSpec(memory_space=pl.ANY),
        ],
        out_specs=pl.BlockSpec(memory_space=pl.ANY),
        scratch_shapes=[pltpu.VMEM((2, tm, d), F32), pltpu.VMEM((2, tm, d // 2), jnp.uint32),
                        pltpu.VMEM((d, 2 * d_ff), BF16), pltpu.VMEM((d_ff, d), BF16),
                        pltpu.VMEM((2, d, WEIGHT_STAGE), F32), pltpu.VMEM((2, WEIGHT_STAGE, d), F32),
                        pltpu.SemaphoreType.DMA((2,)), pltpu.SemaphoreType.DMA((2,)),
                        pltpu.SemaphoreType.DMA((2,))],
    )
    return pl.pallas_call(
        _expert_kernel,
        grid_spec=grid_spec,
        out_shape=jax.ShapeDtypeStruct((yt_rows, d // 2), jnp.uint32),
        compiler_params=_params(("arbitrary",)),
        name="moe_experts",
    )(tile_expert, n_used, src_tok, src_tok, out_row, out_row, h, g, wgu, wd)


def _combine_kernel(h_ref, pk_ref, y1_ref, y2_ref, o_ref):
    pk = pk_ref[...]
    o_ref[...] = (h_ref[...] + pk[:, 2:3] * _unpack_halves(y1_ref[...])
                  + pk[:, 3:4] * _unpack_halves(y2_ref[...]))


def _combine(h, packed, yt, tm):
    t, d = h.shape
    nb = t // tm
    return pl.pallas_call(
        _combine_kernel,
        grid=(nb,),
        in_specs=[pl.BlockSpec((tm, d), lambda i: (i, 0)),
                  pl.BlockSpec((tm, LANES), lambda i: (i, 0)),
                  pl.BlockSpec((tm, d // 2), lambda i: (i, 0)),
                  pl.BlockSpec((tm, d // 2), lambda i: (nb + i, 0))],
        out_specs=pl.BlockSpec((tm, d), lambda i: (i, 0)),
        out_shape=jax.ShapeDtypeStruct((t, d), F32),
        compiler_params=_params(("parallel",)),
        name="moe_combine",
    )(h, packed, yt, yt)


def _rel_bucket(rel):
    nb = REL_BUCKETS // 2
    max_exact = nb // 2
    n = jnp.abs(rel)
    nf = jnp.maximum(n, 1).astype(F32)
    large = max_exact + (jnp.log(nf / max_exact) / math.log(REL_MAX_DIST / max_exact)
                         * (nb - max_exact)).astype(jnp.int32)
    large = jnp.minimum(large, nb - 1)
    return jnp.where(rel > 0, nb, 0) + jnp.where(n < max_exact, n, large)


def _near_bias(rel_bias):
    t = ATT_T
    qp = jnp.arange(t)[:, None]
    kp = jnp.arange(t)[None, :]

    def lookup(rel):
        onehot = jax.nn.one_hot(_rel_bucket(rel), REL_BUCKETS, dtype=F32)
        return jnp.einsum('...b,bm->...m', onehot, rel_bias, precision=lax.Precision.HIGHEST)

    far = lookup(jnp.full((), -(2 * t), jnp.int32))
    prev = (lookup(kp - t - qp) - far) * LOG2E
    diag = (lookup(kp - qp) - far) * LOG2E
    diag = jnp.where(((kp // CHUNK) <= (qp // CHUNK))[:, :, None], diag, NEG_BIG)
    both = jnp.stack([prev, diag], axis=0).reshape(2, t, t, DIFF_HEADS, 2)
    return both.transpose(3, 0, 2, 4, 1).reshape(DIFF_HEADS, 2, t, 2 * t).astype(F32)


def kernel(x, rel_bias, norm1_g, w_in, pool_w, pool_scale, conv_w, sgu_ln_g, sgu_w, sgu_b, q_norm_g, k_norm_g, diff_lambda, subln_g, w_branch_pool, w_branch_conv, w_branch_sgu, w_branch_attn, w_out, norm2_g, ffn_w_gate_up, ffn_w_down, router_w, moe_w_gate_up, moe_w_down):
    b, s, d = x.shape
    t = b * s
    depth = w_in.shape[0]
    pw = pool_scale.shape[1]
    cw = conv_w.shape[2]
    sw = sgu_ln_g.shape[1]
    aw = w_branch_attn.shape[1]
    mix_cols = pw + 3 * cw + 2 * sw + 3 * aw
    nb = REL_BUCKETS // 2
    assert nb // 2 + int(math.log((ATT_T + 1) / (nb // 2)) / math.log(REL_MAX_DIST / (nb // 2))
                         * (nb - nb // 2)) >= nb - 1
    qk_off = pw + 3 * cw + 2 * sw
    z_cols = qk_off + 2 * aw
    q_col = qk_off // LANES
    k_col = q_col + aw // LANES
    tm = min(512, t)
    td = min(DENSE_TM, s)
    ts = min(512, s)

    bias_near = _near_bias(rel_bias)
    tri = jnp.tril(jnp.ones((SGU_SEG, SGU_SEG), bool))
    gd = pw // POOL_GROUPS

    h = x.reshape(t, d)
    for layer in range(depth):
        lam_init = 0.8 - 0.6 * math.exp(-0.3 * layer)
        w_mix = w_in[layer, :, :z_cols].astype(BF16)
        w_vt = w_in[layer, :, z_cols:mix_cols].T.astype(BF16)
        w_gate = w_in[layer, :, mix_cols:].astype(BF16)
        poolw_bd = jnp.zeros((pw, pw), F32)
        for g in range(POOL_GROUPS):
            poolw_bd = poolw_bd.at[g * gd:(g + 1) * gd, g * gd:(g + 1) * gd].set(pool_w[layer, g])
        sguw_cat = jnp.where(tri[None], sgu_w[layer], 0.0).transpose(1, 0, 2).reshape(
            SGU_SEG, SGU_GROUPS * SGU_SEG).astype(BF16)
        sgub_full = jnp.repeat(sgu_b[layer].T, sw // SGU_GROUPS, axis=1)
        wb = jnp.concatenate([w_branch_pool[layer], w_branch_conv[layer], w_branch_sgu[layer],
                              w_branch_attn[layer]], axis=0).astype(BF16)

        z, vt = _in_proj(h, norm1_g[layer][None], w_mix, w_vt, td, s)
        z3 = z.reshape(b, s, z_cols)
        y_abc = _local_mix(z3, poolw_bd.astype(BF16), pool_scale[layer][None], conv_w[layer],
                           sgu_ln_g[layer][None], sguw_cat, sgub_full, ts, pw, cw, sw)
        y_d = _diff_attn(z3, vt, bias_near, jnp.tile(q_norm_g[layer], 2)[None],
                         jnp.tile(k_norm_g[layer], 2)[None], diff_lambda[layer],
                         subln_g[layer][None], lam_init, q_col, k_col)
        h = _merge(h, y_abc.reshape(t, -1), y_d.reshape(t, -1), norm1_g[layer][None], w_gate, wb,
                   w_out[layer].astype(BF16), td, (pw, cw, sw, aw))

        g2 = norm2_g[layer][None]
        if layer % 2 == 0:
            h = _ffn(h, g2, ffn_w_gate_up[layer // 2].astype(BF16),
                     ffn_w_down[layer // 2].astype(BF16), td)
        else:
            li = layer // 2
            rw_pad = jnp.zeros((d, LANES), F32).at[:, :N_EXPERTS].set(router_w[li])
            packed, totals = _router(h, g2, jnp.stack(_hi_lo(rw_pad)), tm)
            n_e = totals[0, :N_EXPERTS].astype(jnp.int32)
            n_pad = ((n_e + tm - 1) // tm) * tm
            ends = jnp.cumsum(n_pad)
            starts = ends - n_pad
            e1 = packed[:, 0].astype(jnp.int32)
            e2 = packed[:, 1].astype(jnp.int32)
            eids = jnp.arange(N_EXPERTS)[None, :]
            dest1 = (jnp.sum(jnp.where(e1[:, None] == eids, starts[None, :], 0), axis=1)
                     + packed[:, 4].astype(jnp.int32))
            dest2 = (jnp.sum(jnp.where(e2[:, None] == eids, starts[None, :], 0), axis=1)
                     + packed[:, 5].astype(jnp.int32))
            rows = 2 * t + N_EXPERTS * tm
            n_tiles = rows // tm
            tile_expert = jnp.minimum(
                jnp.sum((jnp.arange(n_tiles)[:, None] * tm) >= ends[None, :], axis=1),
                N_EXPERTS - 1).astype(jnp.int32)
            n_used = (ends[-1] // tm).astype(jnp.int32)[None]
            ch = min(4096, t)
            dest_a = jnp.concatenate([dest1, dest2]).reshape(2 * t // ch, 1, ch)
            src = _invert(dest_a, rows)
            is_pad = src < 0
            src_tok = jnp.where(is_pad, 0, jnp.where(src >= t, src - t, src))
            pad_rank = jnp.cumsum(is_pad.astype(jnp.int32)) - 1
            out_row = jnp.where(is_pad, 2 * t + tm + pad_rank, src)
            spare = 2 * t + jnp.arange(tm, dtype=jnp.int32)
            yt = _experts(tile_expert, n_used, src_tok.reshape(n_tiles, 1, tm),
                          jnp.concatenate([spare, out_row]).reshape(n_tiles + 1, 1, tm), h, g2,
                          moe_w_gate_up[li], moe_w_down[li], rows + tm)
            h = _combine(h, packed, yt, tm)
    return h.reshape(b, s, d)
```

```python
import functools
import math

import jax
import jax.numpy as jnp
from jax import lax
from jax.experimental import pallas as pl
from jax.experimental.pallas import tpu as pltpu

F32 = jnp.float32
BF16 = jnp.bfloat16

NORM_EPS = 1e-6
CHUNK = 64
POOL_GROUPS = 4
SGU_GROUPS = 4
SGU_SEG = 128
DIFF_HEADS = 4
DIFF_QK_DIM = 64
DIFF_V_DIM = 128
REL_BUCKETS = 32
REL_MAX_DIST = 128
N_EXPERTS = 8
LANES = 128
V7X_VMEM_BYTES = 64 * 1024 * 1024
VMEM_LIMIT = V7X_VMEM_BYTES - 8 * 1024 * 1024
NEG_BIG = -1e30
LOG2E = math.log2(math.e)

HALO = 16
ATT_T = 256
ATT_HEADS_PER_STEP = 2
FF_CHUNK = 256
WEIGHT_STAGE = 512
DENSE_TM = 1024


def _rms(x, g):
    return x * lax.rsqrt(jnp.mean(x * x, axis=-1, keepdims=True) + NORM_EPS) * g


def _resident(shape):
    nd = len(shape)
    return pl.BlockSpec(shape, lambda *_: (0,) * nd, pipeline_mode=pl.Buffered(1))


def _params(sem):
    return pltpu.CompilerParams(dimension_semantics=sem, vmem_limit_bytes=VMEM_LIMIT)


def _hi_lo(x):
    hi = x.astype(BF16)
    return hi, (x - hi.astype(F32)).astype(BF16)


def _pack_halves(x):
    n = x.shape[1] // 2
    bits = lax.bitcast_convert_type(x, jnp.uint32) + jnp.uint32(0x8000)
    return (bits[:, :n] & jnp.uint32(0xFFFF0000)) | (bits[:, n:] >> 16)


def _unpack_halves(p):
    hi = lax.bitcast_convert_type(p & jnp.uint32(0xFFFF0000), F32)
    lo = lax.bitcast_convert_type(p << 16, F32)
    return jnp.concatenate([hi, lo], axis=1)


def _in_proj_kernel(x_ref, g_ref, w_ref, wvt_ref, o_ref, vt_ref, *, n_chunk):
    xn = _rms(x_ref[...], g_ref[...]).astype(BF16)
    n = o_ref.shape[1]
    for j in range(n // n_chunk):
        sl = slice(j * n_chunk, (j + 1) * n_chunk)
        o_ref[:, sl] = jnp.dot(xn, w_ref[:, sl], preferred_element_type=F32).astype(o_ref.dtype)
    vt_ref[...] = lax.dot_general(wvt_ref[...], xn, (((1,), (1,)), ((), ())),
                                  preferred_element_type=F32).astype(vt_ref.dtype)


def _in_proj(h, g, w, w_vt, tm, seq):
    t, d = h.shape
    n = w.shape[1]
    nv = w_vt.shape[0]
    per_seq = seq // tm
    return pl.pallas_call(
        functools.partial(_in_proj_kernel, n_chunk=512),
        grid=(t // tm,),
        in_specs=[pl.BlockSpec((tm, d), lambda i: (i, 0)), _resident((1, d)), _resident((d, n)),
                  _resident(w_vt.shape)],
        out_specs=[pl.BlockSpec((tm, n), lambda i: (i, 0)),
                   pl.BlockSpec((None, nv, tm), lambda i: (i // per_seq, 0, i % per_seq))],
        out_shape=[jax.ShapeDtypeStruct((t, n), BF16),
                   jax.ShapeDtypeStruct((t // seq, nv, seq), BF16)],
        compiler_params=_params(("parallel",)),
        name="in_proj",
    )(h, g, w, w_vt)


def _local_mix_kernel(z_ref, halo_ref, poolw_ref, pscale_ref, convw_ref, lng_ref, sguw_ref,
                      sgub_ref, o_ref, *, pw, cw):
    ts = z_ref.shape[0]
    i = pl.program_id(1)
    z = z_ref[...].astype(F32)
    halo = halo_ref[...].astype(F32)
    halo = jnp.where(i > 0, halo, 0.0)
    ext = jnp.concatenate([halo[:, :pw + 3 * cw], z[:, :pw + 3 * cw]], axis=0)
    rows = ext.shape[0]

    def back(x, k):
        return pltpu.roll(x, k, axis=0)

    a = ext[:, :pw]
    s2 = a + back(a, 1)
    s4 = s2 + back(s2, 2)
    s8 = s4 + back(s4, 4)
    s16 = s8 + back(s8, 8)
    lane = lax.broadcasted_iota(jnp.int32, (rows, pw), 1)
    grp = lane // (pw // POOL_GROUPS)
    win_sum = jnp.where(grp == 0, s2, jnp.where(grp == 1, s4, jnp.where(grp == 2, s8, s16)))
    win = jnp.where(grp == 0, 2, jnp.where(grp == 1, 4, jnp.where(grp == 2, 8, 16)))
    pos = i * ts - HALO + lax.broadcasted_iota(jnp.int32, (rows, pw), 0)
    count = jnp.minimum(pos + 1, win).astype(F32)
    pooled = (win_sum / jnp.maximum(count, 1.0) - a)[HALO:]
    y_a = jnp.dot(pooled.astype(BF16), poolw_ref[...], preferred_element_type=F32) * pscale_ref[...]
    o_ref[:, 0:pw] = y_a.astype(o_ref.dtype)

    b_gate = z[:, pw:pw + cw]
    zc = ext[:, pw + cw:pw + 2 * cw] * ext[:, pw + 2 * cw:pw + 3 * cw]
    conv = (convw_ref[0:1, :] * back(zc, 2) + convw_ref[1:2, :] * back(zc, 1)
            + convw_ref[2:3, :] * zc)[HALO:]
    o_ref[:, pw:pw + cw] = (b_gate * conv).astype(o_ref.dtype)

    sw = (z.shape[1] - pw - 3 * cw) // 2
    zc_uv = z[:, pw + 3 * cw:]
    uv = 0.5 * zc_uv * (1.0 + lax.erf(zc_uv * math.sqrt(0.5)))
    u = uv[:, :sw]
    v = uv[:, sw:]
    mu = jnp.mean(v, axis=-1, keepdims=True)
    var = jnp.mean(jnp.square(v - mu), axis=-1, keepdims=True)
    vn = (v - mu) * lax.rsqrt(var + NORM_EPS) * lng_ref[...]
    glane = lax.broadcasted_iota(jnp.int32, (SGU_SEG, sw), 1) // (sw // SGU_GROUPS)
    wcat = sguw_ref[...]
    bias = sgub_ref[...]
    for n in range(ts // SGU_SEG):
        seg = vn[n * SGU_SEG:(n + 1) * SGU_SEG]
        rhs = jnp.concatenate(
            [jnp.where(glane == g, seg, 0.0) for g in range(SGU_GROUPS)], axis=0).astype(BF16)
        s = jnp.dot(wcat, rhs, preferred_element_type=F32) + bias
        o_ref[n * SGU_SEG:(n + 1) * SGU_SEG, pw + cw:pw + cw + sw] = (
            u[n * SGU_SEG:(n + 1) * SGU_SEG] * s).astype(o_ref.dtype)


def _local_mix(z3, poolw_bd, pscale, convw, lng, sguw_cat, sgub_full, ts, pw, cw, sw):
    b, s, _ = z3.shape
    cols = pw + 3 * cw + 2 * sw
    hb = ts // HALO
    return pl.pallas_call(
        functools.partial(_local_mix_kernel, pw=pw, cw=cw),
        grid=(b, s // ts),
        in_specs=[
            pl.BlockSpec((None, ts, cols), lambda bi, i: (bi, i, 0)),
            pl.BlockSpec((None, HALO, cols), lambda bi, i: (bi, jnp.maximum(i * hb - 1, 0), 0)),
            _resident(poolw_bd.shape), _resident(pscale.shape), _resident(convw.shape),
            _resident(lng.shape), _resident(sguw_cat.shape), _resident(sgub_full.shape),
        ],
        out_specs=pl.BlockSpec((None, ts, pw + cw + sw), lambda bi, i: (bi, i, 0)),
        out_shape=jax.ShapeDtypeStruct((b, s, pw + cw + sw), BF16),
        compiler_params=_params(("parallel", "parallel")),
        name="local_mix",
    )(z3, z3, poolw_bd, pscale, convw, lng, sguw_cat, sgub_full)


def _diff_attn_kernel(q_ref, k_ref, vt_ref, bias_ref, qg_ref, kg_ref, lam_ref, sg_ref, o_ref,
                      kn_ref, qs_ref, st_ref, m_ref, *, lam_init):
    tq = ATT_T
    hp = qs_ref.shape[0]
    nt = k_ref.shape[0] // tq
    hw = 2 * DIFF_QK_DIM
    half = lax.broadcasted_iota(jnp.int32, (1, hw), 1) < DIFF_QK_DIM
    same_map = (lax.broadcasted_iota(jnp.int32, (hw, hw), 0) // DIFF_QK_DIM
                == lax.broadcasted_iota(jnp.int32, (hw, hw), 1) // DIFF_QK_DIM)
    ones_map = jnp.where(same_map, 1.0, 0.0).astype(BF16)

    def qk_norm_mxu(x, g):
        sq_hi, sq_lo = _hi_lo(x * x)
        ss = (jnp.dot(sq_hi, ones_map, preferred_element_type=F32)
              + jnp.dot(sq_lo, ones_map, preferred_element_type=F32))
        return x * lax.rsqrt(ss * (1.0 / DIFF_QK_DIM) + NORM_EPS) * g

    def qk_norm(x, g):
        sq = x * x
        ss0 = jnp.sum(jnp.where(half, sq, 0.0), axis=-1, keepdims=True)
        ss1 = jnp.sum(jnp.where(half, 0.0, sq), axis=-1, keepdims=True)
        r0 = lax.rsqrt(ss0 * (1.0 / DIFF_QK_DIM) + NORM_EPS)
        r1 = lax.rsqrt(ss1 * (1.0 / DIFF_QK_DIM) + NORM_EPS)
        return x * jnp.where(half, r0, r1) * g

    for h in range(hp):
        cols = slice(h * hw, (h + 1) * hw)
        for j in range(nt):
            rows = slice(j * tq, (j + 1) * tq)
            kn_ref[h, rows, :] = qk_norm_mxu(k_ref[rows, cols].astype(F32),
                                             kg_ref[...]).astype(BF16)

    def step(c):
        cur, prv = c % 2, 1 - c % 2
        scoring = c < nt
        if scoring:
            for h in range(hp):
                qn = (qk_norm(q_ref[c * tq:(c + 1) * tq, h * hw:(h + 1) * hw].astype(F32),
                              qg_ref[...]) * (DIFF_QK_DIM ** -0.5 * LOG2E))
                qs_ref[h, 0:tq, :] = jnp.where(half, qn, 0.0).astype(BF16)
                qs_ref[h, tq:2 * tq, :] = jnp.where(half, 0.0, qn).astype(BF16)
        m_new = [jnp.full((1, 2 * tq), NEG_BIG, F32) for _ in range(hp)]
        m_old = [m_ref[prv, h] for h in range(hp)] if c >= 1 else None
        l = [jnp.zeros((1, 2 * tq), F32) for _ in range(hp)]
        acc = [jnp.zeros((DIFF_V_DIM, 2 * tq), F32) for _ in range(hp)]
        for j in range(c + 1):
            rows = slice(j * tq, (j + 1) * tq)
            for h in range(hp):
                if scoring:
                    st = lax.dot_general(kn_ref[h, rows, :], qs_ref[h],
                                         (((1,), (1,)), ((), ())), preferred_element_type=F32)
                    if j >= c - 1:
                        st = st + bias_ref[h, j - (c - 1)]
                    st_ref[cur, h, rows, :] = st
                    m_new[h] = jnp.maximum(m_new[h], jnp.max(st, axis=0, keepdims=True))
                if j < c:
                    p = jnp.exp2(st_ref[prv, h, rows, :] - m_old[h])
                    l[h] = l[h] + jnp.sum(p, axis=0, keepdims=True)
                    acc[h] = acc[h] + jnp.dot(vt_ref[h * DIFF_V_DIM:(h + 1) * DIFF_V_DIM, rows],
                                              p.astype(BF16), preferred_element_type=F32)
        if scoring:
            for h in range(hp):
                m_ref[cur, h] = m_new[h]
        if c >= 1:
            lp = lam_ref[...]
            lam = (jnp.exp(jnp.sum(lp[0:1] * lp[1:2], axis=-1, keepdims=True))
                   - jnp.exp(jnp.sum(lp[2:3] * lp[3:4], axis=-1, keepdims=True)) + lam_init)
            for h in range(hp):
                o = (acc[h][:, :tq] * (1.0 / l[h][:, :tq])
                     - acc[h][:, tq:] * (lam / l[h][:, tq:]))
                o = o * lax.rsqrt(jnp.mean(o * o, axis=0, keepdims=True) + NORM_EPS)
                o_ref[(c - 1) * tq:c * tq, h * DIFF_V_DIM:(h + 1) * DIFF_V_DIM] = (
                    o.T * (sg_ref[...] * (1.0 - lam_init))).astype(o_ref.dtype)

    for c in range(nt + 1):
        step(c)


def _diff_attn(z3, vt, bias_near, qg2, kg2, lam_p, subln_g, lam_init, q_col, k_col):
    b, s, _ = z3.shape
    tq = ATT_T
    hp = ATT_HEADS_PER_STEP
    nt = s // tq
    hw = 2 * DIFF_QK_DIM
    return pl.pallas_call(
        functools.partial(_diff_attn_kernel, lam_init=lam_init),
        grid=(b, DIFF_HEADS // hp),
        in_specs=[
            pl.BlockSpec((None, s, hp * hw), lambda bi, g: (bi, 0, q_col // hp + g)),
            pl.BlockSpec((None, s, hp * hw), lambda bi, g: (bi, 0, k_col // hp + g)),
            pl.BlockSpec((None, hp * DIFF_V_DIM, s), lambda bi, g: (bi, g, 0)),
            pl.BlockSpec((hp, 2, tq, 2 * tq), lambda bi, g: (g, 0, 0, 0)),
            _resident(qg2.shape), _resident(kg2.shape), _resident(lam_p.shape),
            _resident(subln_g.shape),
        ],
        out_specs=pl.BlockSpec((None, s, hp * DIFF_V_DIM), lambda bi, g: (bi, 0, g)),
        out_shape=jax.ShapeDtypeStruct((b, s, DIFF_HEADS * DIFF_V_DIM), BF16),
        scratch_shapes=[
            pltpu.VMEM((hp, s, hw), BF16),
            pltpu.VMEM((hp, 2 * tq, hw), BF16),
            pltpu.VMEM((2, hp, s, 2 * tq), F32),
            pltpu.VMEM((2, hp, 1, 2 * tq), F32),
        ],
        compiler_params=_params(("parallel", "parallel")),
        name="diff_attn",
    )(z3, z3, vt, bias_near, qg2, kg2, lam_p, subln_g)


def _merge_kernel(h_ref, yabc_ref, yd_ref, g_ref, wg_ref, wb_ref, wo_ref, o_ref, *, widths):
    h = h_ref[...]
    d = h.shape[1]
    xn = _rms(h, g_ref[...]).astype(BF16)
    merged = None
    off = 0
    yoff = 0
    for bi, w in enumerate(widths):
        gate = jax.nn.sigmoid(jnp.dot(xn, wg_ref[:, bi * d:(bi + 1) * d],
                                      preferred_element_type=F32))
        if bi < len(widths) - 1:
            y = yabc_ref[:, yoff:yoff + w]
            yoff += w
        else:
            y = yd_ref[...]
        proj = jnp.dot(y, wb_ref[off:off + w, :], preferred_element_type=F32)
        off += w
        merged = gate * proj if merged is None else merged + gate * proj
    o_ref[...] = h + jnp.dot(merged.astype(BF16), wo_ref[...], preferred_element_type=F32)


def _merge(h, y_abc, y_d, g, wg, wb, wo, tm, widths):
    t, d = h.shape
    return pl.pallas_call(
        functools.partial(_merge_kernel, widths=widths),
        grid=(t // tm,),
        in_specs=[
            pl.BlockSpec((tm, d), lambda i: (i, 0)),
            pl.BlockSpec((tm, y_abc.shape[1]), lambda i: (i, 0)),
            pl.BlockSpec((tm, y_d.shape[1]), lambda i: (i, 0)),
            _resident(g.shape), _resident(wg.shape), _resident(wb.shape), _resident(wo.shape),
        ],
        out_specs=pl.BlockSpec((tm, d), lambda i: (i, 0)),
        out_shape=jax.ShapeDtypeStruct((t, d), F32),
        compiler_params=_params(("parallel",)),
        name="merge",
    )(h, y_abc, y_d, g, wg, wb, wo)


def _swiglu_acc(xn, wgu_ref, wd_ref, d_ff, between=None):
    acc = None
    n_chunks = d_ff // FF_CHUNK
    for c in range(n_chunks):
        lo = c * FF_CHUNK
        g = jnp.dot(xn, wgu_ref[:, lo:lo + FF_CHUNK], preferred_element_type=F32)
        u = jnp.dot(xn, wgu_ref[:, d_ff + lo:d_ff + lo + FF_CHUNK], preferred_element_type=F32)
        act = (g * jax.nn.sigmoid(g) * u).astype(BF16)
        part = jnp.dot(act, wd_ref[lo:lo + FF_CHUNK, :], preferred_element_type=F32)
        acc = part if acc is None else acc + part
        if between is not None:
            between(c, n_chunks)
    return acc


def _ffn_kernel(h_ref, g_ref, wgu_ref, wd_ref, o_ref):
    h = h_ref[...]
    xn = _rms(h, g_ref[...]).astype(BF16)
    o_ref[...] = h + _swiglu_acc(xn, wgu_ref, wd_ref, wd_ref.shape[0])


def _ffn(h, g, wgu, wd, tm):
    t, d = h.shape
    return pl.pallas_call(
        _ffn_kernel,
        grid=(t // tm,),
        in_specs=[pl.BlockSpec((tm, d), lambda i: (i, 0)), _resident(g.shape),
                  _resident(wgu.shape), _resident(wd.shape)],
        out_specs=pl.BlockSpec((tm, d), lambda i: (i, 0)),
        out_shape=jax.ShapeDtypeStruct((t, d), F32),
        compiler_params=_params(("parallel",)),
        name="ffn",
    )(h, g, wgu, wd)


def _router_kernel(h_ref, g_ref, rw_ref, o_ref, tot_ref, carry_ref):
    tm = h_ref.shape[0]

    @pl.when(pl.program_id(0) == 0)
    def _():
        carry_ref[...] = jnp.zeros(carry_ref.shape, F32)

    hn = _rms(h_ref[...], g_ref[...])
    hn_hi, hn_lo = _hi_lo(hn)
    logits = (jnp.dot(hn_hi, rw_ref[0], preferred_element_type=F32)
              + jnp.dot(hn_lo, rw_ref[0], preferred_element_type=F32)
              + jnp.dot(hn_hi, rw_ref[1], preferred_element_type=F32))
    lane = lax.broadcasted_iota(jnp.int32, (tm, LANES), 1)
    logits = jnp.where(lane < N_EXPERTS, logits, NEG_BIG)
    v1 = jnp.max(logits, axis=-1, keepdims=True)
    i1 = jnp.min(jnp.where(logits == v1, lane, LANES), axis=-1, keepdims=True)
    rest = jnp.where(lane == i1, NEG_BIG, logits)
    v2 = jnp.max(rest, axis=-1, keepdims=True)
    i2 = jnp.min(jnp.where(rest == v2, lane, LANES), axis=-1, keepdims=True)
    e = jnp.exp(v2 - v1)
    w1 = 1.0 / (1.0 + e)
    w2 = e / (1.0 + e)
    cnt = jnp.where((lane == i1) | (lane == i2), 1.0, 0.0)
    r = lax.broadcasted_iota(jnp.int32, (tm, tm), 0)
    c = lax.broadcasted_iota(jnp.int32, (tm, tm), 1)
    tri = jnp.where(c < r, 1.0, 0.0).astype(BF16)
    excl = jnp.dot(tri, cnt.astype(BF16), preferred_element_type=F32) + carry_ref[...]
    rank1 = jnp.sum(jnp.where(lane == i1, excl, 0.0), axis=-1, keepdims=True)
    rank2 = jnp.sum(jnp.where(lane == i2, excl, 0.0), axis=-1, keepdims=True)
    carry_ref[...] = carry_ref[...] + jnp.sum(cnt, axis=0, keepdims=True)
    tot_ref[...] = carry_ref[...]
    packed = jnp.where(lane == 0, i1.astype(F32), jnp.where(lane == 1, i2.astype(F32),
             jnp.where(lane == 2, w1, jnp.where(lane == 3, w2,
             jnp.where(lane == 4, rank1, jnp.where(lane == 5, rank2, 0.0))))))
    o_ref[...] = packed


def _router(h, g, rw_pad, tm):
    t, d = h.shape
    return pl.pallas_call(
        _router_kernel,
        grid=(t // tm,),
        in_specs=[pl.BlockSpec((tm, d), lambda i: (i, 0)), _resident(g.shape),
                  _resident(rw_pad.shape)],
        out_specs=[pl.BlockSpec((tm, LANES), lambda i: (i, 0)),
                   pl.BlockSpec((1, LANES), lambda i: (0, 0))],
        out_shape=[jax.ShapeDtypeStruct((t, LANES), F32), jax.ShapeDtypeStruct((1, LANES), F32)],
        scratch_shapes=[pltpu.VMEM((1, LANES), F32)],
        compiler_params=_params(("arbitrary",)),
        name="router",
    )(h, g, rw_pad)


def _invert_kernel(dest_ref, init_hbm, src_ref):
    c = pl.program_id(0)
    ch = dest_ref.shape[1]

    @pl.when(c == 0)
    def _():
        pltpu.sync_copy(init_hbm, src_ref)

    base = c * ch

    def place(a, x):
        src_ref[dest_ref[0, a]] = base + a
        return x

    lax.fori_loop(0, ch, place, 0, unroll=16)


def _invert(dest, rows):
    nc, _, ch = dest.shape
    return pl.pallas_call(
        _invert_kernel,
        grid=(nc,),
        in_specs=[pl.BlockSpec((None, 1, ch), lambda c: (c, 0, 0), memory_space=pltpu.SMEM),
                  pl.BlockSpec(memory_space=pl.ANY)],
        out_specs=pl.BlockSpec(memory_space=pltpu.SMEM),
        out_shape=jax.ShapeDtypeStruct((rows,), jnp.int32),
        compiler_params=pltpu.CompilerParams(dimension_semantics=("arbitrary",)),
        name="moe_invert",
    )(dest, jnp.full((rows,), -1, jnp.int32))


def _stream_chunks(copy, n, consume):
    copy(0, 0).start()
    for c in range(n):
        if c + 1 < n:
            copy(c + 1, (c + 1) % 2).start()
        copy(c, c % 2).wait()
        consume(c, c % 2)


def _expert_kernel(te_ref, nu_ref, src_cur, src_nxt, orow_prv, orow_cur, h_hbm, g_ref, wgu_hbm,
                   wd_hbm, yt_hbm, xbuf, ybuf, wgu_ref, wd_ref, stage_gu, stage_d, gsem, ssem, wsem):
    i = pl.program_id(0)
    fin = jnp.minimum(nu_ref[0], pl.num_programs(0) - 1)
    tm = xbuf.shape[1]
    s = lax.rem(i, 2)
    o = 1 - s
    e = te_ref[i]

    def gather(tok, r, slot):
        return pltpu.make_async_copy(h_hbm.at[pl.ds(tok, 1)],
                                     xbuf.at[slot, pl.ds(r, 1)], gsem.at[slot])

    def scatter(row, r, slot):
        return pltpu.make_async_copy(ybuf.at[slot, pl.ds(r, 1)],
                                     yt_hbm.at[pl.ds(row, 1)], ssem.at[slot])

    def for_rows(fn):
        def body(r, x):
            fn(r)
            return x

        lax.fori_loop(0, tm, body, 0, unroll=8)

    def load_weights():
        wc = stage_gu.shape[2]
        rc = stage_d.shape[1]

        def copy_gu(c, slot):
            return pltpu.make_async_copy(wgu_hbm.at[e, :, pl.ds(c * wc, wc)], stage_gu.at[slot],
                                         wsem.at[slot])

        def store_gu(c, slot):
            wgu_ref[:, c * wc:(c + 1) * wc] = stage_gu[slot].astype(BF16)

        def copy_d(c, slot):
            return pltpu.make_async_copy(wd_hbm.at[e, pl.ds(c * rc, rc), :], stage_d.at[slot],
                                         wsem.at[slot])

        def store_d(c, slot):
            wd_ref[c * rc:(c + 1) * rc, :] = stage_d[slot].astype(BF16)

        _stream_chunks(copy_gu, wgu_ref.shape[1] // wc, store_gu)
        _stream_chunks(copy_d, wd_ref.shape[0] // rc, store_d)

    def spare_fill(k):
        rows0 = yt_hbm.shape[0] - (N_EXPERTS + 1 - k) * tm
        return pltpu.make_async_copy(ybuf.at[1], yt_hbm.at[pl.ds(rows0, tm)], ssem.at[0])

    @pl.when(i <= fin)
    def _():
        @pl.when((i == 0) | (e != te_ref[jnp.maximum(i - 1, 0)]))
        def _():
            load_weights()

        @pl.when(i == 0)
        def _():
            ybuf[1] = jnp.zeros(ybuf.shape[1:], ybuf.dtype)
            for k in range(1, N_EXPERTS + 1):
                spare_fill(k).start()
            for k in range(1, N_EXPERTS + 1):
                spare_fill(k).wait()
            for_rows(lambda r: gather(src_cur[0, r], r, 0).start())

        for_rows(lambda r: gather(src_cur[0, r], r, s).wait())
        xn = _rms(xbuf[s], g_ref[...]).astype(BF16)

        def between(c, n_chunks):
            per = -(-tm // (n_chunks // 2))
            for r in range(c * per, min((c + 1) * per, tm)):
                gather(src_nxt[0, r], r, o).start()
                scatter(orow_prv[0, r], r, o).start()

        ybuf[s] = _pack_halves(_swiglu_acc(xn, wgu_ref, wd_ref, wd_ref.shape[0], between))
        for_rows(lambda r: scatter(orow_prv[0, r], r, o).wait())

        @pl.when(i == fin)
        def _():
            for_rows(lambda r: gather(src_nxt[0, r], r, o).wait())
            for_rows(lambda r: scatter(orow_cur[0, r], r, s).start())
            for_rows(lambda r: scatter(orow_cur[0, r], r, s).wait())


def _experts(tile_expert, n_used, src_tok, out_row, h, g, wgu, wd, yt_rows):
    n, _, tm = src_tok.shape
    d = h.shape[1]
    d_ff = wd.shape[1]
    smem = functools.partial(pl.BlockSpec, (None, 1, tm), memory_space=pltpu.SMEM)
    grid_spec = pltpu.PrefetchScalarGridSpec(
        num_scalar_prefetch=2,
        grid=(n,),
        in_specs=[
            smem(lambda i, te, nu: (i, 0, 0)),
            smem(lambda i, te, nu: (jnp.minimum(i + 1, n - 1), 0, 0)),
            smem(lambda i, te, nu: (i, 0, 0)),
            smem(lambda i, te, nu: (i + 1, 0, 0)),
            pl.BlockSpec(memory_space=pl.ANY),
            pl.BlockSpec(g.shape, lambda i, te, nu: (0, 0), pipeline_mode=pl.Buffered(1)),
            pl.BlockSpec(memory_space=pl.ANY),
            pl.BlockSpec(memory_space=pl.ANY),
        ],
        out_specs=pl.BlockSpec(memory_space=pl.ANY),
        scratch_shapes=[pltpu.VMEM((2, tm, d), F32), pltpu.VMEM((2, tm, d // 2), jnp.uint32),
                        pltpu.VMEM((d, 2 * d_ff), BF16), pltpu.VMEM((d_ff, d), BF16),
                        pltpu.VMEM((2, d, WEIGHT_STAGE), F32), pltpu.VMEM((2, WEIGHT_STAGE, d), F32),
                        pltpu.SemaphoreType.DMA((2,)), pltpu.SemaphoreType.DMA((2,)),
                        pltpu.SemaphoreType.DMA((2,))],
    )
    return pl.pallas_call(
        _expert_kernel,
        grid_spec=grid_spec,
        out_shape=jax.ShapeDtypeStruct((yt_rows, d // 2), jnp.uint32),
        compiler_params=_params(("arbitrary",)),
        name="moe_experts",
    )(tile_expert, n_used, src_tok, src_tok, out_row, out_row, h, g, wgu, wd)


def _combine_kernel(h_ref, pk_ref, y1_ref, y2_ref, o_ref):
    pk = pk_ref[...]
    o_ref[...] = (h_ref[...] + pk[:, 2:3] * _unpack_halves(y1_ref[...])
                  + pk[:, 3:4] * _unpack_halves(y2_ref[...]))


def _combine(h, packed, yt, tm):
    t, d = h.shape
    nb = t // tm
    return pl.pallas_call(
        _combine_kernel,
        grid=(nb,),
        in_specs=[pl.BlockSpec((tm, d), lambda i: (i, 0)),
                  pl.BlockSpec((tm, LANES), lambda i: (i, 0)),
                  pl.BlockSpec((tm, d // 2), lambda i: (i, 0)),
                  pl.BlockSpec((tm, d // 2), lambda i: (nb + i, 0))],
        out_specs=pl.BlockSpec((tm, d), lambda i: (i, 0)),
        out_shape=jax.ShapeDtypeStruct((t, d), F32),
        compiler_params=_params(("parallel",)),
        name="moe_combine",
    )(h, packed, yt, yt)


def _rel_bucket(rel):
    nb = REL_BUCKETS // 2
    max_exact = nb // 2
    n = jnp.abs(rel)
    nf = jnp.maximum(n, 1).astype(F32)
    large = max_exact + (jnp.log(nf / max_exact) / math.log(REL_MAX_DIST / max_exact)
                         * (nb - max_exact)).astype(jnp.int32)
    large = jnp.minimum(large, nb - 1)
    return jnp.where(rel > 0, nb, 0) + jnp.where(n < max_exact, n, large)


def _near_bias(rel_bias):
    t = ATT_T
    qp = jnp.arange(t)[:, None]
    kp = jnp.arange(t)[None, :]

    def lookup(rel):
        onehot = jax.nn.one_hot(_rel_bucket(rel), REL_BUCKETS, dtype=F32)
        return jnp.einsum('...b,bm->...m', onehot, rel_bias, precision=lax.Precision.HIGHEST)

    far = lookup(jnp.full((), -(2 * t), jnp.int32))
    prev = (lookup(kp - t - qp) - far) * LOG2E
    diag = (lookup(kp - qp) - far) * LOG2E
    diag = jnp.where(((kp // CHUNK) <= (qp // CHUNK))[:, :, None], diag, NEG_BIG)
    both = jnp.stack([prev, diag], axis=0).reshape(2, t, t, DIFF_HEADS, 2)
    return both.transpose(3, 0, 2, 4, 1).reshape(DIFF_HEADS, 2, t, 2 * t).astype(F32)


def kernel(x, rel_bias, norm1_g, w_in, pool_w, pool_scale, conv_w, sgu_ln_g, sgu_w, sgu_b, q_norm_g, k_norm_g, diff_lambda, subln_g, w_branch_pool, w_branch_conv, w_branch_sgu, w_branch_attn, w_out, norm2_g, ffn_w_gate_up, ffn_w_down, router_w, moe_w_gate_up, moe_w_down):
    b, s, d = x.shape
    t = b * s
    depth = w_in.shape[0]
    pw = pool_scale.shape[1]
    cw = conv_w.shape[2]
    sw = sgu_ln_g.shape[1]
    aw = w_branch_attn.shape[1]
    mix_cols = pw + 3 * cw + 2 * sw + 3 * aw
    nb = REL_BUCKETS // 2
    assert nb // 2 + int(math.log((ATT_T + 1) / (nb // 2)) / math.log(REL_MAX_DIST / (nb // 2))
                         * (nb - nb // 2)) >= nb - 1
    qk_off = pw + 3 * cw + 2 * sw
    z_cols = qk_off + 2 * aw
    q_col = qk_off // LANES
    k_col = q_col + aw // LANES
    tm = min(512, t)
    td = min(DENSE_TM, s)
    ts = min(512, s)

    bias_near = _near_bias(rel_bias)
    tri = jnp.tril(jnp.ones((SGU_SEG, SGU_SEG), bool))
    gd = pw // POOL_GROUPS

    h = x.reshape(t, d)
    for layer in range(depth):
        lam_init = 0.8 - 0.6 * math.exp(-0.3 * layer)
        w_mix = w_in[layer, :, :z_cols].astype(BF16)
        w_vt = w_in[layer, :, z_cols:mix_cols].T.astype(BF16)
        w_gate = w_in[layer, :, mix_cols:].astype(BF16)
        poolw_bd = jnp.zeros((pw, pw), F32)
        for g in range(POOL_GROUPS):
            poolw_bd = poolw_bd.at[g * gd:(g + 1) * gd, g * gd:(g + 1) * gd].set(pool_w[layer, g])
        sguw_cat = jnp.where(tri[None], sgu_w[layer], 0.0).transpose(1, 0, 2).reshape(
            SGU_SEG, SGU_GROUPS * SGU_SEG).astype(BF16)
        sgub_full = jnp.repeat(sgu_b[layer].T, sw // SGU_GROUPS, axis=1)
        wb = jnp.concatenate([w_branch_pool[layer], w_branch_conv[layer], w_branch_sgu[layer],
                              w_branch_attn[layer]], axis=0).astype(BF16)

        z, vt = _in_proj(h, norm1_g[layer][None], w_mix, w_vt, td, s)
        z3 = z.reshape(b, s, z_cols)
        y_abc = _local_mix(z3, poolw_bd.astype(BF16), pool_scale[layer][None], conv_w[layer],
                           sgu_ln_g[layer][None], sguw_cat, sgub_full, ts, pw, cw, sw)
        y_d = _diff_attn(z3, vt, bias_near, jnp.tile(q_norm_g[layer], 2)[None],
                         jnp.tile(k_norm_g[layer], 2)[None], diff_lambda[layer],
                         subln_g[layer][None], lam_init, q_col, k_col)
        h = _merge(h, y_abc.reshape(t, -1), y_d.reshape(t, -1), norm1_g[layer][None], w_gate, wb,
                   w_out[layer].astype(BF16), td, (pw, cw, sw, aw))

        g2 = norm2_g[layer][None]
        if layer % 2 == 0:
            h = _ffn(h, g2, ffn_w_gate_up[layer // 2].astype(BF16),
                     ffn_w_down[layer // 2].astype(BF16), td)
        else:
            li = layer // 2
            rw_pad = jnp.zeros((d, LANES), F32).at[:, :N_EXPERTS].set(router_w[li])
            packed, totals = _router(h, g2, jnp.stack(_hi_lo(rw_pad)), tm)
            n_e = totals[0, :N_EXPERTS].astype(jnp.int32)
            n_pad = ((n_e + tm - 1) // tm) * tm
            ends = jnp.cumsum(n_pad)
            starts = ends - n_pad
            e1 = packed[:, 0].astype(jnp.int32)
            e2 = packed[:, 1].astype(jnp.int32)
            eids = jnp.arange(N_EXPERTS)[None, :]
            dest1 = (jnp.sum(jnp.where(e1[:, None] == eids, starts[None, :], 0), axis=1)
                     + packed[:, 4].astype(jnp.int32))
            dest2 = (jnp.sum(jnp.where(e2[:, None] == eids, starts[None, :], 0), axis=1)
                     + packed[:, 5].astype(jnp.int32))
            rows = 2 * t + N_EXPERTS * tm
            n_tiles = rows // tm
            tile_expert = jnp.minimum(
                jnp.sum((jnp.arange(n_tiles)[:, None] * tm) >= ends[None, :], axis=1),
                N_EXPERTS - 1).astype(jnp.int32)
            n_used = (ends[-1] // tm).astype(jnp.int32)[None]
            ch = min(4096, t)
            dest_a = jnp.concatenate([dest1, dest2]).reshape(2 * t // ch, 1, ch)
            src = _invert(dest_a, rows)
            is_pad = src < 0
            src_tok = jnp.where(is_pad, 0, jnp.where(src >= t, src - t, src))
            pad_rank = jnp.cumsum(is_pad.astype(jnp.int32)) - 1
            out_row = jnp.where(is_pad, 2 * t + tm + pad_rank, src)
            spare = 2 * t + jnp.arange(tm, dtype=jnp.int32)
            yt = _experts(tile_expert, n_used, src_tok.reshape(n_tiles, 1, tm),
                          jnp.concatenate([spare, out_row]).reshape(n_tiles + 1, 1, tm), h, g2,
                          moe_w_gate_up[li], moe_w_down[li], rows + tm)
            h = _combine(h, packed, yt, tm)
    return h.reshape(b, s, d)
```

```python
import functools
import math

import jax
import jax.numpy as jnp
from jax import lax
from jax.experimental import pallas as pl
from jax.experimental.pallas import tpu as pltpu

F32 = jnp.float32
BF16 = jnp.bfloat16

NORM_EPS = 1e-6
CHUNK = 64
POOL_GROUPS = 4
SGU_GROUPS = 4
SGU_SEG = 128
DIFF_HEADS = 4
DIFF_QK_DIM = 64
DIFF_V_DIM = 128
REL_BUCKETS = 32
REL_MAX_DIST = 128
N_EXPERTS = 8
LANES = 128
V7X_VMEM_BYTES = 64 * 1024 * 1024
VMEM_LIMIT = V7X_VMEM_BYTES - 8 * 1024 * 1024
NEG_BIG = -1e30
LOG2E = math.log2(math.e)

HALO = 16
ATT_T = 256
ATT_HEADS_PER_STEP = 2
FF_CHUNK = 256
WEIGHT_STAGE = 512
DENSE_TM = 1024


def _rms(x, g):
    return x * lax.rsqrt(jnp.mean(x * x, axis=-1, keepdims=True) + NORM_EPS) * g


def _resident(shape):
    nd = len(shape)
    return pl.BlockSpec(shape, lambda *_: (0,) * nd, pipeline_mode=pl.Buffered(1))


def _params(sem):
    return pltpu.CompilerParams(dimension_semantics=sem, vmem_limit_bytes=VMEM_LIMIT)


def _hi_lo(x):
    hi = x.astype(BF16)
    return hi, (x - hi.astype(F32)).astype(BF16)


def _pack_halves(x):
    n = x.shape[1] // 2
    bits = lax.bitcast_convert_type(x, jnp.uint32) + jnp.uint32(0x8000)
    return (bits[:, :n] & jnp.uint32(0xFFFF0000)) | (bits[:, n:] >> 16)


def _unpack_halves(p):
    hi = lax.bitcast_convert_type(p & jnp.uint32(0xFFFF0000), F32)
    lo = lax.bitcast_convert_type(p << 16, F32)
    return jnp.concatenate([hi, lo], axis=1)


def _in_proj_kernel(x_ref, g_ref, w_ref, wvt_ref, o_ref, vt_ref, *, n_chunk):
    xn = _rms(x_ref[...], g_ref[...]).astype(BF16)
    n = o_ref.shape[1]
    for j in range(n // n_chunk):
        sl = slice(j * n_chunk, (j + 1) * n_chunk)
        o_ref[:, sl] = jnp.dot(xn, w_ref[:, sl], preferred_element_type=F32).astype(o_ref.dtype)
    vt_ref[...] = lax.dot_general(wvt_ref[...], xn, (((1,), (1,)), ((), ())),
                                  preferred_element_type=F32).astype(vt_ref.dtype)


def _in_proj(h, g, w, w_vt, tm, seq):
    t, d = h.shape
    n = w.shape[1]
    nv = w_vt.shape[0]
    per_seq = seq // tm
    return pl.pallas_call(
        functools.partial(_in_proj_kernel, n_chunk=512),
        grid=(t // tm,),
        in_specs=[pl.BlockSpec((tm, d), lambda i: (i, 0)), _resident((1, d)), _resident((d, n)),
                  _resident(w_vt.shape)],
        out_specs=[pl.BlockSpec((tm, n), lambda i: (i, 0)),
                   pl.BlockSpec((None, nv, tm), lambda i: (i // per_seq, 0, i % per_seq))],
        out_shape=[jax.ShapeDtypeStruct((t, n), BF16),
                   jax.ShapeDtypeStruct((t // seq, nv, seq), BF16)],
        compiler_params=_params(("parallel",)),
        name="in_proj",
    )(h, g, w, w_vt)


def _local_mix_kernel(z_ref, halo_ref, poolw_ref, pscale_ref, convw_ref, lng_ref, sguw_ref,
                      sgub_ref, o_ref, *, pw, cw):
    ts = z_ref.shape[0]
    i = pl.program_id(1)
    z = z_ref[...].astype(F32)
    halo = halo_ref[...].astype(F32)
    halo = jnp.where(i > 0, halo, 0.0)
    ext = jnp.concatenate([halo[:, :pw + 3 * cw], z[:, :pw + 3 * cw]], axis=0)
    rows = ext.shape[0]

    def back(x, k):
        return pltpu.roll(x, k, axis=0)

    a = ext[:, :pw]
    s2 = a + back(a, 1)
    s4 = s2 + back(s2, 2)
    s8 = s4 + back(s4, 4)
    s16 = s8 + back(s8, 8)
    lane = lax.broadcasted_iota(jnp.int32, (rows, pw), 1)
    grp = lane // (pw // POOL_GROUPS)
    win_sum = jnp.where(grp == 0, s2, jnp.where(grp == 1, s4, jnp.where(grp == 2, s8, s16)))
    win = jnp.where(grp == 0, 2, jnp.where(grp == 1, 4, jnp.where(grp == 2, 8, 16)))
    pos = i * ts - HALO + lax.broadcasted_iota(jnp.int32, (rows, pw), 0)
    count = jnp.minimum(pos + 1, win).astype(F32)
    pooled = (win_sum / jnp.maximum(count, 1.0) - a)[HALO:]
    y_a = jnp.dot(pooled.astype(BF16), poolw_ref[...], preferred_element_type=F32) * pscale_ref[...]
    o_ref[:, 0:pw] = y_a.astype(o_ref.dtype)

    b_gate = z[:, pw:pw + cw]
    zc = ext[:, pw + cw:pw + 2 * cw] * ext[:, pw + 2 * cw:pw + 3 * cw]
    conv = (convw_ref[0:1, :] * back(zc, 2) + convw_ref[1:2, :] * back(zc, 1)
            + convw_ref[2:3, :] * zc)[HALO:]
    o_ref[:, pw:pw + cw] = (b_gate * conv).astype(o_ref.dtype)

    sw = (z.shape[1] - pw - 3 * cw) // 2
    zc_uv = z[:, pw + 3 * cw:]
    uv = 0.5 * zc_uv * (1.0 + lax.erf(zc_uv * math.sqrt(0.5)))
    u = uv[:, :sw]
    v = uv[:, sw:]
    mu = jnp.mean(v, axis=-1, keepdims=True)
    var = jnp.mean(jnp.square(v - mu), axis=-1, keepdims=True)
    vn = (v - mu) * lax.rsqrt(var + NORM_EPS) * lng_ref[...]
    glane = lax.broadcasted_iota(jnp.int32, (SGU_SEG, sw), 1) // (sw // SGU_GROUPS)
    wcat = sguw_ref[...]
    bias = sgub_ref[...]
    for n in range(ts // SGU_SEG):
        seg = vn[n * SGU_SEG:(n + 1) * SGU_SEG]
        rhs = jnp.concatenate(
            [jnp.where(glane == g, seg, 0.0) for g in range(SGU_GROUPS)], axis=0).astype(BF16)
        s = jnp.dot(wcat, rhs, preferred_element_type=F32) + bias
        o_ref[n * SGU_SEG:(n + 1) * SGU_SEG, pw + cw:pw + cw + sw] = (
            u[n * SGU_SEG:(n + 1) * SGU_SEG] * s).astype(o_ref.dtype)


def _local_mix(z3, poolw_bd, pscale, convw, lng, sguw_cat, sgub_full, ts, pw, cw, sw):
    b, s, _ = z3.shape
    cols = pw + 3 * cw + 2 * sw
    hb = ts // HALO
    return pl.pallas_call(
        functools.partial(_local_mix_kernel, pw=pw, cw=cw),
        grid=(b, s // ts),
        in_specs=[
            pl.BlockSpec((None, ts, cols), lambda bi, i: (bi, i, 0)),
            pl.BlockSpec((None, HALO, cols), lambda bi, i: (bi, jnp.maximum(i * hb - 1, 0), 0)),
            _resident(poolw_bd.shape), _resident(pscale.shape), _resident(convw.shape),
            _resident(lng.shape), _resident(sguw_cat.shape), _resident(sgub_full.shape),
        ],
        out_specs=pl.BlockSpec((None, ts, pw + cw + sw), lambda bi, i: (bi, i, 0)),
        out_shape=jax.ShapeDtypeStruct((b, s, pw + cw + sw), BF16),
        compiler_params=_params(("parallel", "parallel")),
        name="local_mix",
    )(z3, z3, poolw_bd, pscale, convw, lng, sguw_cat, sgub_full)


def _diff_attn_kernel(q_ref, k_ref, vt_ref, bias_ref, qg_ref, kg_ref, lam_ref, sg_ref, o_ref,
                      kn_ref, qs_ref, st_ref, m_ref, *, lam_init):
    tq = ATT_T
    hp = qs_ref.shape[0]
    nt = k_ref.shape[0] // tq
    hw = 2 * DIFF_QK_DIM
    half = lax.broadcasted_iota(jnp.int32, (1, hw), 1) < DIFF_QK_DIM
    same_map = (lax.broadcasted_iota(jnp.int32, (hw, hw), 0) // DIFF_QK_DIM
                == lax.broadcasted_iota(jnp.int32, (hw, hw), 1) // DIFF_QK_DIM)
    ones_map = jnp.where(same_map, 1.0, 0.0).astype(BF16)

    def qk_norm_mxu(x, g):
        sq_hi, sq_lo = _hi_lo(x * x)
        ss = (jnp.dot(sq_hi, ones_map, preferred_element_type=F32)
              + jnp.dot(sq_lo, ones_map, preferred_element_type=F32))
        return x * lax.rsqrt(ss * (1.0 / DIFF_QK_DIM) + NORM_EPS) * g

    def qk_norm(x, g):
        sq = x * x
        ss0 = jnp.sum(jnp.where(half, sq, 0.0), axis=-1, keepdims=True)
        ss1 = jnp.sum(jnp.where(half, 0.0, sq), axis=-1, keepdims=True)
        r0 = lax.rsqrt(ss0 * (1.0 / DIFF_QK_DIM) + NORM_EPS)
        r1 = lax.rsqrt(ss1 * (1.0 / DIFF_QK_DIM) + NORM_EPS)
        return x * jnp.where(half, r0, r1) * g

    for h in range(hp):
        cols = slice(h * hw, (h + 1) * hw)
        for j in range(nt):
            rows = slice(j * tq, (j + 1) * tq)
            kn_ref[h, rows, :] = qk_norm_mxu(k_ref[rows, cols].astype(F32),
                                             kg_ref[...]).astype(BF16)

    def step(c):
        cur, prv = c % 2, 1 - c % 2
        scoring = c < nt
        if scoring:
            for h in range(hp):
                qn = (qk_norm(q_ref[c * tq:(c + 1) * tq, h * hw:(h + 1) * hw].astype(F32),
                              qg_ref[...]) * (DIFF_QK_DIM ** -0.5 * LOG2E))
                qs_ref[h, 0:tq, :] = jnp.where(half, qn, 0.0).astype(BF16)
                qs_ref[h, tq:2 * tq, :] = jnp.where(half, 0.0, qn).astype(BF16)
        m_new = [jnp.full((1, 2 * tq), NEG_BIG, F32) for _ in range(hp)]
        m_old = [m_ref[prv, h] for h in range(hp)] if c >= 1 else None
        l = [jnp.zeros((1, 2 * tq), F32) for _ in range(hp)]
        acc = [jnp.zeros((DIFF_V_DIM, 2 * tq), F32) for _ in range(hp)]
        for j in range(c + 1):
            rows = slice(j * tq, (j + 1) * tq)
            for h in range(hp):
                if scoring:
                    st = lax.dot_general(kn_ref[h, rows, :], qs_ref[h],
                                         (((1,), (1,)), ((), ())), preferred_element_type=F32)
                    if j >= c - 1:
                        st = st + bias_ref[h, j - (c - 1)]
                    st_ref[cur, h, rows, :] = st
                    m_new[h] = jnp.maximum(m_new[h], jnp.max(st, axis=0, keepdims=True))
                if j < c:
                    p = jnp.exp2(st_ref[prv, h, rows, :] - m_old[h])
                    l[h] = l[h] + jnp.sum(p, axis=0, keepdims=True)
                    acc[h] = acc[h] + jnp.dot(vt_ref[h * DIFF_V_DIM:(h + 1) * DIFF_V_DIM, rows],
                                              p.astype(BF16), preferred_element_type=F32)
        if scoring:
            for h in range(hp):
                m_ref[cur, h] = m_new[h]
        if c >= 1:
            lp = lam_ref[...]
            lam = (jnp.exp(jnp.sum(lp[0:1] * lp[1:2], axis=-1, keepdims=True))
                   - jnp.exp(jnp.sum(lp[2:3] * lp[3:4], axis=-1, keepdims=True)) + lam_init)
            for h in range(hp):
                o = (acc[h][:, :tq] * (1.0 / l[h][:, :tq])
                     - acc[h][:, tq:] * (lam / l[h][:, tq:]))
                o = o * lax.rsqrt(jnp.mean(o * o, axis=0, keepdims=True) + NORM_EPS)
                o_ref[(c - 1) * tq:c * tq, h * DIFF_V_DIM:(h + 1) * DIFF_V_DIM] = (
                    o.T * (sg_ref[...] * (1.0 - lam_init))).astype(o_ref.dtype)

    for c in range(nt + 1):
        step(c)


def _diff_attn(z3, vt, bias_near, qg2, kg2, lam_p, subln_g, lam_init, q_col, k_col):
    b, s, _ = z3.shape
    tq = ATT_T
    hp = ATT_HEADS_PER_STEP
    nt = s // tq
    hw = 2 * DIFF_QK_DIM
    return pl.pallas_call(
        functools.partial(_diff_attn_kernel, lam_init=lam_init),
        grid=(b, DIFF_HEADS // hp),
        in_specs=[
            pl.BlockSpec((None, s, hp * hw), lambda bi, g: (bi, 0, q_col // hp + g)),
            pl.BlockSpec((None, s, hp * hw), lambda bi, g: (bi, 0, k_col // hp + g)),
            pl.BlockSpec((None, hp * DIFF_V_DIM, s), lambda bi, g: (bi, g, 0)),
            pl.BlockSpec((hp, 2, tq, 2 * tq), lambda bi, g: (g, 0, 0, 0)),
            _resident(qg2.shape), _resident(kg2.shape), _resident(lam_p.shape),
            _resident(subln_g.shape),
        ],
        out_specs=pl.BlockSpec((None, s, hp * DIFF_V_DIM), lambda bi, g: (bi, 0, g)),
        out_shape=jax.ShapeDtypeStruct((b, s, DIFF_HEADS * DIFF_V_DIM), BF16),
        scratch_shapes=[
            pltpu.VMEM((hp, s, hw), BF16),
            pltpu.VMEM((hp, 2 * tq, hw), BF16),
            pltpu.VMEM((2, hp, s, 2 * tq), F32),
            pltpu.VMEM((2, hp, 1, 2 * tq), F32),
        ],
        compiler_params=_params(("parallel", "parallel")),
        name="diff_attn",
    )(z3, z3, vt, bias_near, qg2, kg2, lam_p, subln_g)


def _merge_kernel(h_ref, yabc_ref, yd_ref, g_ref, wg_ref, wb_ref, wo_ref, o_ref, *, widths):
    h = h_ref[...]
    d = h.shape[1]
    xn = _rms(h, g_ref[...]).astype(BF16)
    merged = None
    off = 0
    yoff = 0
    for bi, w in enumerate(widths):
        gate = jax.nn.sigmoid(jnp.dot(xn, wg_ref[:, bi * d:(bi + 1) * d],
                                      preferred_element_type=F32))
        if bi < len(widths) - 1:
            y = yabc_ref[:, yoff:yoff + w]
            yoff += w
        else:
            y = yd_ref[...]
        proj = jnp.dot(y, wb_ref[off:off + w, :], preferred_element_type=F32)
        off += w
        merged = gate * proj if merged is None else merged + gate * proj
    o_ref[...] = h + jnp.dot(merged.astype(BF16), wo_ref[...], preferred_element_type=F32)


def _merge(h, y_abc, y_d, g, wg, wb, wo, tm, widths):
    t, d = h.shape
    return pl.pallas_call(
        functools.partial(_merge_kernel, widths=widths),
        grid=(t // tm,),
        in_specs=[
            pl.BlockSpec((tm, d), lambda i: (i, 0)),
            pl.BlockSpec((tm, y_abc.shape[1]), lambda i: (i, 0)),
            pl.BlockSpec((tm, y_d.shape[1]), lambda i: (i, 0)),
            _resident(g.shape), _resident(wg.shape), _resident(wb.shape), _resident(wo.shape),
        ],
        out_specs=pl.BlockSpec((tm, d), lambda i: (i, 0)),
        out_shape=jax.ShapeDtypeStruct((t, d), F32),
        compiler_params=_params(("parallel",)),
        name="merge",
    )(h, y_abc, y_d, g, wg, wb, wo)


def _swiglu_acc(xn, wgu_ref, wd_ref, d_ff, between=None):
    acc = None
    n_chunks = d_ff // FF_CHUNK
    for c in range(n_chunks):
        lo = c * FF_CHUNK
        g = jnp.dot(xn, wgu_ref[:, lo:lo + FF_CHUNK], preferred_element_type=F32)
        u = jnp.dot(xn, wgu_ref[:, d_ff + lo:d_ff + lo + FF_CHUNK], preferred_element_type=F32)
        act = (g * jax.nn.sigmoid(g) * u).astype(BF16)
        part = jnp.dot(act, wd_ref[lo:lo + FF_CHUNK, :], preferred_element_type=F32)
        acc = part if acc is None else acc + part
        if between is not None:
            between(c, n_chunks)
    return acc


def _ffn_kernel(h_ref, g_ref, wgu_ref, wd_ref, o_ref):
    h = h_ref[...]
    xn = _rms(h, g_ref[...]).astype(BF16)
    o_ref[...] = h + _swiglu_acc(xn, wgu_ref, wd_ref, wd_ref.shape[0])


def _ffn(h, g, wgu, wd, tm):
    t, d = h.shape
    return pl.pallas_call(
        _ffn_kernel,
        grid=(t // tm,),
        in_specs=[pl.BlockSpec((tm, d), lambda i: (i, 0)), _resident(g.shape),
                  _resident(wgu.shape), _resident(wd.shape)],
        out_specs=pl.BlockSpec((tm, d), lambda i: (i, 0)),
        out_shape=jax.ShapeDtypeStruct((t, d), F32),
        compiler_params=_params(("parallel",)),
        name="ffn",
    )(h, g, wgu, wd)


def _router_kernel(h_ref, g_ref, rw_ref, o_ref, tot_ref, carry_ref):
    tm = h_ref.shape[0]

    @pl.when(pl.program_id(0) == 0)
    def _():
        carry_ref[...] = jnp.zeros(carry_ref.shape, F32)

    hn = _rms(h_ref[...], g_ref[...])
    hn_hi, hn_lo = _hi_lo(hn)
    logits = (jnp.dot(hn_hi, rw_ref[0], preferred_element_type=F32)
              + jnp.dot(hn_lo, rw_ref[0], preferred_element_type=F32)
              + jnp.dot(hn_hi, rw_ref[1], preferred_element_type=F32))
    lane = lax.broadcasted_iota(jnp.int32, (tm, LANES), 1)
    logits = jnp.where(lane < N_EXPERTS, logits, NEG_BIG)
    v1 = jnp.max(logits, axis=-1, keepdims=True)
    i1 = jnp.min(jnp.where(logits == v1, lane, LANES), axis=-1, keepdims=True)
    rest = jnp.where(lane == i1, NEG_BIG, logits)
    v2 = jnp.max(rest, axis=-1, keepdims=True)
    i2 = jnp.min(jnp.where(rest == v2, lane, LANES), axis=-1, keepdims=True)
    e = jnp.exp(v2 - v1)
    w1 = 1.0 / (1.0 + e)
    w2 = e / (1.0 + e)
    cnt = jnp.where((lane == i1) | (lane == i2), 1.0, 0.0)
    r = lax.broadcasted_iota(jnp.int32, (tm, tm), 0)
    c = lax.broadcasted_iota(jnp.int32, (tm, tm), 1)
    tri = jnp.where(c < r, 1.0, 0.0).astype(BF16)
    excl = jnp.dot(tri, cnt.astype(BF16), preferred_element_type=F32) + carry_ref[...]
    rank1 = jnp.sum(jnp.where(lane == i1, excl, 0.0), axis=-1, keepdims=True)
    rank2 = jnp.sum(jnp.where(lane == i2, excl, 0.0), axis=-1, keepdims=True)
    carry_ref[...] = carry_ref[...] + jnp.sum(cnt, axis=0, keepdims=True)
    tot_ref[...] = carry_ref[...]
    packed = jnp.where(lane == 0, i1.astype(F32), jnp.where(lane == 1, i2.astype(F32),
             jnp.where(lane == 2, w1, jnp.where(lane == 3, w2,
             jnp.where(lane == 4, rank1, jnp.where(lane == 5, rank2, 0.0))))))
    o_ref[...] = packed


def _router(h, g, rw_pad, tm):
    t, d = h.shape
    return pl.pallas_call(
        _router_kernel,
        grid=(t // tm,),
        in_specs=[pl.BlockSpec((tm, d), lambda i: (i, 0)), _resident(g.shape),
                  _resident(rw_pad.shape)],
        out_specs=[pl.BlockSpec((tm, LANES), lambda i: (i, 0)),
                   pl.BlockSpec((1, LANES), lambda i: (0, 0))],
        out_shape=[jax.ShapeDtypeStruct((t, LANES), F32), jax.ShapeDtypeStruct((1, LANES), F32)],
        scratch_shapes=[pltpu.VMEM((1, LANES), F32)],
        compiler_params=_params(("arbitrary",)),
        name="router",
    )(h, g, rw_pad)


def _invert_kernel(dest_ref, init_hbm, src_ref):
    c = pl.program_id(0)
    ch = dest_ref.shape[1]

    @pl.when(c == 0)
    def _():
        pltpu.sync_copy(init_hbm, src_ref)

    base = c * ch

    def place(a, x):
        src_ref[dest_ref[0, a]] = base + a
        return x

    lax.fori_loop(0, ch, place, 0, unroll=16)


def _invert(dest, rows):
    nc, _, ch = dest.shape
    return pl.pallas_call(
        _invert_kernel,
        grid=(nc,),
        in_specs=[pl.BlockSpec((None, 1, ch), lambda c: (c, 0, 0), memory_space=pltpu.SMEM),
                  pl.BlockSpec(memory_space=pl.ANY)],
        out_specs=pl.BlockSpec(memory_space=pltpu.SMEM),
        out_shape=jax.ShapeDtypeStruct((rows,), jnp.int32),
        compiler_params=pltpu.CompilerParams(dimension_semantics=("arbitrary",)),
        name="moe_invert",
    )(dest, jnp.full((rows,), -1, jnp.int32))


def _stream_chunks(copy, n, consume):
    copy(0, 0).start()
    for c in range(n):
        if c + 1 < n:
            copy(c + 1, (c + 1) % 2).start()
        copy(c, c % 2).wait()
        consume(c, c % 2)


def _expert_kernel(te_ref, nu_ref, src_cur, src_nxt, orow_prv, orow_cur, h_hbm, g_ref, wgu_hbm,
                   wd_hbm, yt_hbm, xbuf, ybuf, wgu_ref, wd_ref, stage_gu, stage_d, gsem, ssem, wsem):
    i = pl.program_id(0)
    fin = nu_ref[0] - 1
    tm = xbuf.shape[1]
    s = lax.rem(i, 2)
    o = 1 - s
    e = te_ref[i]

    def gather(tok, r, slot):
        return pltpu.make_async_copy(h_hbm.at[pl.ds(tok, 1)],
                                     xbuf.at[slot, pl.ds(r, 1)], gsem.at[slot])

    def scatter(row, r, slot):
        return pltpu.make_async_copy(ybuf.at[slot, pl.ds(r, 1)],
                                     yt_hbm.at[pl.ds(row, 1)], ssem.at[slot])

    def for_rows(fn):
        def body(r, x):
            fn(r)
            return x

        lax.fori_loop(0, tm, body, 0, unroll=8)

    def load_weights():
        wc = stage_gu.shape[2]
        rc = stage_d.shape[1]

        def copy_gu(c, slot):
            return pltpu.make_async_copy(wgu_hbm.at[e, :, pl.ds(c * wc, wc)], stage_gu.at[slot],
                                         wsem.at[slot])

        def store_gu(c, slot):
            wgu_ref[:, c * wc:(c + 1) * wc] = stage_gu[slot].astype(BF16)

        def copy_d(c, slot):
            return pltpu.make_async_copy(wd_hbm.at[e, pl.ds(c * rc, rc), :], stage_d.at[slot],
                                         wsem.at[slot])

        def store_d(c, slot):
            wd_ref[c * rc:(c + 1) * rc, :] = stage_d[slot].astype(BF16)

        _stream_chunks(copy_gu, wgu_ref.shape[1] // wc, store_gu)
        _stream_chunks(copy_d, wd_ref.shape[0] // rc, store_d)

    def spare_fill(k):
        rows0 = yt_hbm.shape[0] - (N_EXPERTS + 1 - k) * tm
        return pltpu.make_async_copy(ybuf.at[1], yt_hbm.at[pl.ds(rows0, tm)], ssem.at[0])

    @pl.when(i <= fin)
    def _():
        @pl.when((i == 0) | (e != te_ref[jnp.maximum(i - 1, 0)]))
        def _():
            load_weights()

        @pl.when(i == 0)
        def _():
            ybuf[1] = jnp.zeros(ybuf.shape[1:], ybuf.dtype)
            for k in range(1, N_EXPERTS + 1):
                spare_fill(k).start()
            for k in range(1, N_EXPERTS + 1):
                spare_fill(k).wait()
            for_rows(lambda r: gather(src_cur[0, r], r, 0).start())

        for_rows(lambda r: gather(src_cur[0, r], r, s).wait())
        xn = _rms(xbuf[s], g_ref[...]).astype(BF16)

        def between(c, n_chunks):
            per = -(-tm // (n_chunks // 2))
            for r in range(c * per, min((c + 1) * per, tm)):
                gather(src_nxt[0, r], r, o).start()
                scatter(orow_prv[0, r], r, o).start()

        ybuf[s] = _pack_halves(_swiglu_acc(xn, wgu_ref, wd_ref, wd_ref.shape[0], between))
        for_rows(lambda r: scatter(orow_prv[0, r], r, o).wait())

        @pl.when(i == fin)
        def _():
            for_rows(lambda r: gather(src_nxt[0, r], r, o).wait())
            for_rows(lambda r: scatter(orow_cur[0, r], r, s).start())
            for_rows(lambda r: scatter(orow_cur[0, r], r, s).wait())


def _experts(tile_expert, n_used, src_tok, out_row, h, g, wgu, wd, yt_rows):
    n, _, tm = src_tok.shape
    d = h.shape[1]
    d_ff = wd.shape[1]
    smem = functools.partial(pl.BlockSpec, (None, 1, tm), memory_space=pltpu.SMEM)
    grid_spec = pltpu.PrefetchScalarGridSpec(
        num_scalar_prefetch=2,
        grid=(n,),
        in_specs=[
            smem(lambda i, te, nu: (i, 0, 0)),
            smem(lambda i, te, nu: (jnp.minimum(i + 1, n - 1), 0, 0)),
            smem(lambda i, te, nu: (i, 0, 0)),
            smem(lambda i, te, nu: (i + 1, 0, 0)),
            pl.BlockSpec(memory_space=pl.ANY),
            pl.BlockSpec(g.shape, lambda i, te, nu: (0, 0), pipeline_mode=pl.Buffered(1)),
            pl.BlockSpec(memory_space=pl.ANY),
            pl.BlockSpec(memory_space=pl.ANY),
        ],
        out_specs=pl.BlockSpec(memory_space=pl.ANY),
        scratch_shapes=[pltpu.VMEM((2, tm, d), F32), pltpu.VMEM((2, tm, d // 2), jnp.uint32),
                        pltpu.VMEM((d, 2 * d_ff), BF16), pltpu.VMEM((d_ff, d), BF16),
                        pltpu.VMEM((2, d, WEIGHT_STAGE), F32), pltpu.VMEM((2, WEIGHT_STAGE, d), F32),
                        pltpu.SemaphoreType.DMA((2,)), pltpu.SemaphoreType.DMA((2,)),
                        pltpu.SemaphoreType.DMA((2,))],
    )
    return pl.pallas_call(
        _expert_kernel,
        grid_spec=grid_spec,
        out_shape=jax.ShapeDtypeStruct((yt_rows, d // 2), jnp.uint32),
        compiler_params=_params(("arbitrary",)),
        name="moe_experts",
    )(tile_expert, n_used, src_tok, src_tok, out_row, out_row, h, g, wgu, wd)


def _combine_kernel(h_ref, pk_ref, y1_ref, y2_ref, o_ref):
    pk = pk_ref[...]
    o_ref[...] = (h_ref[...] + pk[:, 2:3] * _unpack_halves(y1_ref[...])
                  + pk[:, 3:4] * _unpack_halves(y2_ref[...]))


def _combine(h, packed, yt, tm):
    t, d = h.shape
    nb = t // tm
    return pl.pallas_call(
        _combine_kernel,
        grid=(nb,),
        in_specs=[pl.BlockSpec((tm, d), lambda i: (i, 0)),
                  pl.BlockSpec((tm, LANES), lambda i: (i, 0)),
                  pl.BlockSpec((tm, d // 2), lambda i: (i, 0)),
                  pl.BlockSpec((tm, d // 2), lambda i: (nb + i, 0))],
        out_specs=pl.BlockSpec((tm, d), lambda i: (i, 0)),
        out_shape=jax.ShapeDtypeStruct((t, d), F32),
        compiler_params=_params(("parallel",)),
        name="moe_combine",
    )(h, packed, yt, yt)


def _rel_bucket(rel):
    nb = REL_BUCKETS // 2
    max_exact = nb // 2
    n = jnp.abs(rel)
    nf = jnp.maximum(n, 1).astype(F32)
    large = max_exact + (jnp.log(nf / max_exact) / math.log(REL_MAX_DIST / max_exact)
                         * (nb - max_exact)).astype(jnp.int32)
    large = jnp.minimum(large, nb - 1)
    return jnp.where(rel > 0, nb, 0) + jnp.where(n < max_exact, n, large)


def _near_bias(rel_bias):
    t = ATT_T
    qp = jnp.arange(t)[:, None]
    kp = jnp.arange(t)[None, :]

    def lookup(rel):
        onehot = jax.nn.one_hot(_rel_bucket(rel), REL_BUCKETS, dtype=F32)
        return jnp.einsum('...b,bm->...m', onehot, rel_bias, precision=lax.Precision.HIGHEST)

    far = lookup(jnp.full((), -(2 * t), jnp.int32))
    prev = (lookup(kp - t - qp) - far) * LOG2E
    diag = (lookup(kp - qp) - far) * LOG2E
    diag = jnp.where(((kp // CHUNK) <= (qp // CHUNK))[:, :, None], diag, NEG_BIG)
    both = jnp.stack([prev, diag], axis=0).reshape(2, t, t, DIFF_HEADS, 2)
    return both.transpose(3, 0, 2, 4, 1).reshape(DIFF_HEADS, 2, t, 2 * t).astype(F32)


def kernel(x, rel_bias, norm1_g, w_in, pool_w, pool_scale, conv_w, sgu_ln_g, sgu_w, sgu_b, q_norm_g, k_norm_g, diff_lambda, subln_g, w_branch_pool, w_branch_conv, w_branch_sgu, w_branch_attn, w_out, norm2_g, ffn_w_gate_up, ffn_w_down, router_w, moe_w_gate_up, moe_w_down):
    b, s, d = x.shape
    t = b * s
    depth = w_in.shape[0]
    pw = pool_scale.shape[1]
    cw = conv_w.shape[2]
    sw = sgu_ln_g.shape[1]
    aw = w_branch_attn.shape[1]
    mix_cols = pw + 3 * cw + 2 * sw + 3 * aw
    nb = REL_BUCKETS // 2
    assert nb // 2 + int(math.log((ATT_T + 1) / (nb // 2)) / math.log(REL_MAX_DIST / (nb // 2))
                         * (nb - nb // 2)) >= nb - 1
    qk_off = pw + 3 * cw + 2 * sw
    z_cols = qk_off + 2 * aw
    q_col = qk_off // LANES
    k_col = q_col + aw // LANES
    tm = min(512, t)
    td = min(DENSE_TM, s)
    ts = min(512, s)

    bias_near = _near_bias(rel_bias)
    tri = jnp.tril(jnp.ones((SGU_SEG, SGU_SEG), bool))
    gd = pw // POOL_GROUPS

    h = x.reshape(t, d)
    for layer in range(depth):
        lam_init = 0.8 - 0.6 * math.exp(-0.3 * layer)
        w_mix = w_in[layer, :, :z_cols].astype(BF16)
        w_vt = w_in[layer, :, z_cols:mix_cols].T.astype(BF16)
        w_gate = w_in[layer, :, mix_cols:].astype(BF16)
        poolw_bd = jnp.zeros((pw, pw), F32)
        for g in range(POOL_GROUPS):
            poolw_bd = poolw_bd.at[g * gd:(g + 1) * gd, g * gd:(g + 1) * gd].set(pool_w[layer, g])
        sguw_cat = jnp.where(tri[None], sgu_w[layer], 0.0).transpose(1, 0, 2).reshape(
            SGU_SEG, SGU_GROUPS * SGU_SEG).astype(BF16)
        sgub_full = jnp.repeat(sgu_b[layer].T, sw // SGU_GROUPS, axis=1)
        wb = jnp.concatenate([w_branch_pool[layer], w_branch_conv[layer], w_branch_sgu[layer],
                              w_branch_attn[layer]], axis=0).astype(BF16)

        z, vt = _in_proj(h, norm1_g[layer][None], w_mix, w_vt, td, s)
        z3 = z.reshape(b, s, z_cols)
        y_abc = _local_mix(z3, poolw_bd.astype(BF16), pool_scale[layer][None], conv_w[layer],
                           sgu_ln_g[layer][None], sguw_cat, sgub_full, ts, pw, cw, sw)
        y_d = _diff_attn(z3, vt, bias_near, jnp.tile(q_norm_g[layer], 2)[None],
                         jnp.tile(k_norm_g[layer], 2)[None], diff_lambda[layer],
                         subln_g[layer][None], lam_init, q_col, k_col)
        h = _merge(h, y_abc.reshape(t, -1), y_d.reshape(t, -1), norm1_g[layer][None], w_gate, wb,
                   w_out[layer].astype(BF16), td, (pw, cw, sw, aw))

        g2 = norm2_g[layer][None]
        if layer % 2 == 0:
            h = _ffn(h, g2, ffn_w_gate_up[layer // 2].astype(BF16),
                     ffn_w_down[layer // 2].astype(BF16), td)
        else:
            li = layer // 2
            rw_pad = jnp.zeros((d, LANES), F32).at[:, :N_EXPERTS].set(router_w[li])
            packed, totals = _router(h, g2, jnp.stack(_hi_lo(rw_pad)), tm)
            n_e = totals[0, :N_EXPERTS].astype(jnp.int32)
            n_pad = ((n_e + tm - 1) // tm) * tm
            ends = jnp.cumsum(n_pad)
            starts = ends - n_pad
            e1 = packed[:, 0].astype(jnp.int32)
            e2 = packed[:, 1].astype(jnp.int32)
            eids = jnp.arange(N_EXPERTS)[None, :]
            dest1 = (jnp.sum(jnp.where(e1[:, None] == eids, starts[None, :], 0), axis=1)
                     + packed[:, 4].astype(jnp.int32))
            dest2 = (jnp.sum(jnp.where(e2[:, None] == eids, starts[None, :], 0), axis=1)
                     + packed[:, 5].astype(jnp.int32))
            rows = 2 * t + N_EXPERTS * tm
            n_tiles = rows // tm
            tile_expert = jnp.minimum(
                jnp.sum((jnp.arange(n_tiles)[:, None] * tm) >= ends[None, :], axis=1),
                N_EXPERTS - 1).astype(jnp.int32)
            n_used = (ends[-1] // tm).astype(jnp.int32)[None]
            ch = min(4096, t)
            dest_a = jnp.concatenate([dest1, dest2]).reshape(2 * t // ch, 1, ch)
            src = _invert(dest_a, rows)
            is_pad = src < 0
            src_tok = jnp.where(is_pad, 0, jnp.where(src >= t, src - t, src))
            pad_rank = jnp.cumsum(is_pad.astype(jnp.int32)) - 1
            out_row = jnp.where(is_pad, 2 * t + tm + pad_rank, src)
            spare = 2 * t + jnp.arange(tm, dtype=jnp.int32)
            yt = _experts(tile_expert, n_used, src_tok.reshape(n_tiles, 1, tm),
                          jnp.concatenate([spare, out_row]).reshape(n_tiles + 1, 1, tm), h, g2,
                          moe_w_gate_up[li], moe_w_down[li], rows + tm)
            h = _combine(h, packed, yt, tm)
    return h.reshape(b, s, d)
```

```python
import functools
import math

import jax
import jax.numpy as jnp
from jax import lax
from jax.experimental import pallas as pl
from jax.experimental.pallas import tpu as pltpu

F32 = jnp.float32
BF16 = jnp.bfloat16

NORM_EPS = 1e-6
CHUNK = 64
POOL_GROUPS = 4
SGU_GROUPS = 4
SGU_SEG = 128
DIFF_HEADS = 4
DIFF_QK_DIM = 64
DIFF_V_DIM = 128
REL_BUCKETS = 32
REL_MAX_DIST = 128
N_EXPERTS = 8
LANES = 128
V7X_VMEM_BYTES = 64 * 1024 * 1024
VMEM_LIMIT = V7X_VMEM_BYTES - 8 * 1024 * 1024
NEG_BIG = -1e30
LOG2E = math.log2(math.e)

HALO = 16
ATT_T = 256
ATT_HEADS_PER_STEP = 2
FF_CHUNK = 256
WEIGHT_STAGE = 512
DENSE_TM = 1024


def _rms(x, g):
    return x * lax.rsqrt(jnp.mean(x * x, axis=-1, keepdims=True) + NORM_EPS) * g


def _resident(shape):
    nd = len(shape)
    return pl.BlockSpec(shape, lambda *_: (0,) * nd, pipeline_mode=pl.Buffered(1))


def _params(sem):
    return pltpu.CompilerParams(dimension_semantics=sem, vmem_limit_bytes=VMEM_LIMIT)


def _hi_lo(x):
    hi = x.astype(BF16)
    return hi, (x - hi.astype(F32)).astype(BF16)


def _pack_halves(x):
    n = x.shape[1] // 2
    bits = lax.bitcast_convert_type(x, jnp.uint32) + jnp.uint32(0x8000)
    return (bits[:, :n] & jnp.uint32(0xFFFF0000)) | (bits[:, n:] >> 16)


def _unpack_halves(p):
    hi = lax.bitcast_convert_type(p & jnp.uint32(0xFFFF0000), F32)
    lo = lax.bitcast_convert_type(p << 16, F32)
    return jnp.concatenate([hi, lo], axis=1)


def _in_proj_kernel(x_ref, g_ref, w_ref, wvt_ref, o_ref, vt_ref, *, n_chunk):
    xn = _rms(x_ref[...], g_ref[...]).astype(BF16)
    n = o_ref.shape[1]
    for j in range(n // n_chunk):
        sl = slice(j * n_chunk, (j + 1) * n_chunk)
        o_ref[:, sl] = jnp.dot(xn, w_ref[:, sl], preferred_element_type=F32).astype(o_ref.dtype)
    vt_ref[...] = lax.dot_general(wvt_ref[...], xn, (((1,), (1,)), ((), ())),
                                  preferred_element_type=F32).astype(vt_ref.dtype)


def _in_proj(h, g, w, w_vt, tm, seq):
    t, d = h.shape
    n = w.shape[1]
    nv = w_vt.shape[0]
    per_seq = seq // tm
    return pl.pallas_call(
        functools.partial(_in_proj_kernel, n_chunk=512),
        grid=(t // tm,),
        in_specs=[pl.BlockSpec((tm, d), lambda i: (i, 0)), _resident((1, d)), _resident((d, n)),
                  _resident(w_vt.shape)],
        out_specs=[pl.BlockSpec((tm, n), lambda i: (i, 0)),
                   pl.BlockSpec((None, nv, tm), lambda i: (i // per_seq, 0, i % per_seq))],
        out_shape=[jax.ShapeDtypeStruct((t, n), BF16),
                   jax.ShapeDtypeStruct((t // seq, nv, seq), BF16)],
        compiler_params=_params(("parallel",)),
        name="in_proj",
    )(h, g, w, w_vt)


def _local_mix_kernel(z_ref, halo_ref, poolw_ref, pscale_ref, convw_ref, lng_ref, sguw_ref,
                      sgub_ref, o_ref, *, pw, cw):
    ts = z_ref.shape[0]
    i = pl.program_id(1)
    z = z_ref[...].astype(F32)
    halo = halo_ref[...].astype(F32)
    halo = jnp.where(i > 0, halo, 0.0)
    ext = jnp.concatenate([halo[:, :pw + 3 * cw], z[:, :pw + 3 * cw]], axis=0)
    rows = ext.shape[0]

    def back(x, k):
        return pltpu.roll(x, k, axis=0)

    a = ext[:, :pw]
    s2 = a + back(a, 1)
    s4 = s2 + back(s2, 2)
    s8 = s4 + back(s4, 4)
    s16 = s8 + back(s8, 8)
    lane = lax.broadcasted_iota(jnp.int32, (rows, pw), 1)
    grp = lane // (pw // POOL_GROUPS)
    win_sum = jnp.where(grp == 0, s2, jnp.where(grp == 1, s4, jnp.where(grp == 2, s8, s16)))
    win = jnp.where(grp == 0, 2, jnp.where(grp == 1, 4, jnp.where(grp == 2, 8, 16)))
    pos = i * ts - HALO + lax.broadcasted_iota(jnp.int32, (rows, pw), 0)
    count = jnp.minimum(pos + 1, win).astype(F32)
    pooled = (win_sum / jnp.maximum(count, 1.0) - a)[HALO:]
    y_a = jnp.dot(pooled.astype(BF16), poolw_ref[...], preferred_element_type=F32) * pscale_ref[...]
    o_ref[:, 0:pw] = y_a.astype(o_ref.dtype)

    b_gate = z[:, pw:pw + cw]
    zc = ext[:, pw + cw:pw + 2 * cw] * ext[:, pw + 2 * cw:pw + 3 * cw]
    conv = (convw_ref[0:1, :] * back(zc, 2) + convw_ref[1:2, :] * back(zc, 1)
            + convw_ref[2:3, :] * zc)[HALO:]
    o_ref[:, pw:pw + cw] = (b_gate * conv).astype(o_ref.dtype)

    sw = (z.shape[1] - pw - 3 * cw) // 2
    zc_uv = z[:, pw + 3 * cw:]
    uv = 0.5 * zc_uv * (1.0 + lax.erf(zc_uv * math.sqrt(0.5)))
    u = uv[:, :sw]
    v = uv[:, sw:]
    mu = jnp.mean(v, axis=-1, keepdims=True)
    var = jnp.mean(jnp.square(v - mu), axis=-1, keepdims=True)
    vn = (v - mu) * lax.rsqrt(var + NORM_EPS) * lng_ref[...]
    glane = lax.broadcasted_iota(jnp.int32, (SGU_SEG, sw), 1) // (sw // SGU_GROUPS)
    wcat = sguw_ref[...]
    bias = sgub_ref[...]
    for n in range(ts // SGU_SEG):
        seg = vn[n * SGU_SEG:(n + 1) * SGU_SEG]
        rhs = jnp.concatenate(
            [jnp.where(glane == g, seg, 0.0) for g in range(SGU_GROUPS)], axis=0).astype(BF16)
        s = jnp.dot(wcat, rhs, preferred_element_type=F32) + bias
        o_ref[n * SGU_SEG:(n + 1) * SGU_SEG, pw + cw:pw + cw + sw] = (
            u[n * SGU_SEG:(n + 1) * SGU_SEG] * s).astype(o_ref.dtype)


def _local_mix(z3, poolw_bd, pscale, convw, lng, sguw_cat, sgub_full, ts, pw, cw, sw):
    b, s, _ = z3.shape
    cols = pw + 3 * cw + 2 * sw
    hb = ts // HALO
    return pl.pallas_call(
        functools.partial(_local_mix_kernel, pw=pw, cw=cw),
        grid=(b, s // ts),
        in_specs=[
            pl.BlockSpec((None, ts, cols), lambda bi, i: (bi, i, 0)),
            pl.BlockSpec((None, HALO, cols), lambda bi, i: (bi, jnp.maximum(i * hb - 1, 0), 0)),
            _resident(poolw_bd.shape), _resident(pscale.shape), _resident(convw.shape),
            _resident(lng.shape), _resident(sguw_cat.shape), _resident(sgub_full.shape),
        ],
        out_specs=pl.BlockSpec((None, ts, pw + cw + sw), lambda bi, i: (bi, i, 0)),
        out_shape=jax.ShapeDtypeStruct((b, s, pw + cw + sw), BF16),
        compiler_params=_params(("parallel", "parallel")),
        name="local_mix",
    )(z3, z3, poolw_bd, pscale, convw, lng, sguw_cat, sgub_full)


def _diff_attn_kernel(q_ref, k_ref, vt_ref, bias_ref, qg_ref, kg_ref, lam_ref, sg_ref, o_ref,
                      kn_ref, qs_ref, st_ref, m_ref, *, lam_init):
    tq = ATT_T
    hp = qs_ref.shape[0]
    nt = k_ref.shape[0] // tq
    hw = 2 * DIFF_QK_DIM
    half = lax.broadcasted_iota(jnp.int32, (1, hw), 1) < DIFF_QK_DIM
    same_map = (lax.broadcasted_iota(jnp.int32, (hw, hw), 0) // DIFF_QK_DIM
                == lax.broadcasted_iota(jnp.int32, (hw, hw), 1) // DIFF_QK_DIM)
    ones_map = jnp.where(same_map, 1.0, 0.0).astype(BF16)

    def qk_norm_mxu(x, g):
        sq_hi, sq_lo = _hi_lo(x * x)
        ss = (jnp.dot(sq_hi, ones_map, preferred_element_type=F32)
              + jnp.dot(sq_lo, ones_map, preferred_element_type=F32))
        return x * lax.rsqrt(ss * (1.0 / DIFF_QK_DIM) + NORM_EPS) * g

    def qk_norm(x, g):
        sq = x * x
        ss0 = jnp.sum(jnp.where(half, sq, 0.0), axis=-1, keepdims=True)
        ss1 = jnp.sum(jnp.where(half, 0.0, sq), axis=-1, keepdims=True)
        r0 = lax.rsqrt(ss0 * (1.0 / DIFF_QK_DIM) + NORM_EPS)
        r1 = lax.rsqrt(ss1 * (1.0 / DIFF_QK_DIM) + NORM_EPS)
        return x * jnp.where(half, r0, r1) * g

    for h in range(hp):
        cols = slice(h * hw, (h + 1) * hw)
        for j in range(nt):
            rows = slice(j * tq, (j + 1) * tq)
            kn_ref[h, rows, :] = qk_norm_mxu(k_ref[rows, cols].astype(F32),
                                             kg_ref[...]).astype(BF16)

    def step(c):
        cur, prv = c % 2, 1 - c % 2
        scoring = c < nt
        if scoring:
            for h in range(hp):
                qn = (qk_norm(q_ref[c * tq:(c + 1) * tq, h * hw:(h + 1) * hw].astype(F32),
                              qg_ref[...]) * (DIFF_QK_DIM ** -0.5 * LOG2E))
                qs_ref[h, 0:tq, :] = jnp.where(half, qn, 0.0).astype(BF16)
                qs_ref[h, tq:2 * tq, :] = jnp.where(half, 0.0, qn).astype(BF16)
        m_new = [jnp.full((1, 2 * tq), NEG_BIG, F32) for _ in range(hp)]
        m_old = [m_ref[prv, h] for h in range(hp)] if c >= 1 else None
        l = [jnp.zeros((1, 2 * tq), F32) for _ in range(hp)]
        acc = [jnp.zeros((DIFF_V_DIM, 2 * tq), F32) for _ in range(hp)]
        for j in range(c + 1):
            rows = slice(j * tq, (j + 1) * tq)
            for h in range(hp):
                if scoring:
                    st = lax.dot_general(kn_ref[h, rows, :], qs_ref[h],
                                         (((1,), (1,)), ((), ())), preferred_element_type=F32)
                    if j >= c - 1:
                        st = st + bias_ref[h, j - (c - 1)]
                    st_ref[cur, h, rows, :] = st
                    m_new[h] = jnp.maximum(m_new[h], jnp.max(st, axis=0, keepdims=True))
                if j < c:
                    p = jnp.exp2(st_ref[prv, h, rows, :] - m_old[h])
                    l[h] = l[h] + jnp.sum(p, axis=0, keepdims=True)
                    acc[h] = acc[h] + jnp.dot(vt_ref[h * DIFF_V_DIM:(h + 1) * DIFF_V_DIM, rows],
                                              p.astype(BF16), preferred_element_type=F32)
        if scoring:
            for h in range(hp):
                m_ref[cur, h] = m_new[h]
        if c >= 1:
            lp = lam_ref[...]
            lam = (jnp.exp(jnp.sum(lp[0:1] * lp[1:2], axis=-1, keepdims=True))
                   - jnp.exp(jnp.sum(lp[2:3] * lp[3:4], axis=-1, keepdims=True)) + lam_init)
            for h in range(hp):
                o = (acc[h][:, :tq] * (1.0 / l[h][:, :tq])
                     - acc[h][:, tq:] * (lam / l[h][:, tq:]))
                o = o * lax.rsqrt(jnp.mean(o * o, axis=0, keepdims=True) + NORM_EPS)
                o_ref[(c - 1) * tq:c * tq, h * DIFF_V_DIM:(h + 1) * DIFF_V_DIM] = (
                    o.T * (sg_ref[...] * (1.0 - lam_init))).astype(o_ref.dtype)

    for c in range(nt + 1):
        step(c)


def _diff_attn(z3, vt, bias_near, qg2, kg2, lam_p, subln_g, lam_init, q_col, k_col):
    b, s, _ = z3.shape
    tq = ATT_T
    hp = ATT_HEADS_PER_STEP
    nt = s // tq
    hw = 2 * DIFF_QK_DIM
    return pl.pallas_call(
        functools.partial(_diff_attn_kernel, lam_init=lam_init),
        grid=(b, DIFF_HEADS // hp),
        in_specs=[
            pl.BlockSpec((None, s, hp * hw), lambda bi, g: (bi, 0, q_col // hp + g)),
            pl.BlockSpec((None, s, hp * hw), lambda bi, g: (bi, 0, k_col // hp + g)),
            pl.BlockSpec((None, hp * DIFF_V_DIM, s), lambda bi, g: (bi, g, 0)),
            pl.BlockSpec((hp, 2, tq, 2 * tq), lambda bi, g: (g, 0, 0, 0)),
            _resident(qg2.shape), _resident(kg2.shape), _resident(lam_p.shape),
            _resident(subln_g.shape),
        ],
        out_specs=pl.BlockSpec((None, s, hp * DIFF_V_DIM), lambda bi, g: (bi, 0, g)),
        out_shape=jax.ShapeDtypeStruct((b, s, DIFF_HEADS * DIFF_V_DIM), BF16),
        scratch_shapes=[
            pltpu.VMEM((hp, s, hw), BF16),
            pltpu.VMEM((hp, 2 * tq, hw), BF16),
            pltpu.VMEM((2, hp, s, 2 * tq), F32),
            pltpu.VMEM((2, hp, 1, 2 * tq), F32),
        ],
        compiler_params=_params(("parallel", "parallel")),
        name="diff_attn",
    )(z3, z3, vt, bias_near, qg2, kg2, lam_p, subln_g)


def _merge_kernel(h_ref, yabc_ref, yd_ref, g_ref, wg_ref, wb_ref, wo_ref, o_ref, *, widths):
    h = h_ref[...]
    d = h.shape[1]
    xn = _rms(h, g_ref[...]).astype(BF16)
    merged = None
    off = 0
    yoff = 0
    for bi, w in enumerate(widths):
        gate = jax.nn.sigmoid(jnp.dot(xn, wg_ref[:, bi * d:(bi + 1) * d],
                                      preferred_element_type=F32))
        if bi < len(widths) - 1:
            y = yabc_ref[:, yoff:yoff + w]
            yoff += w
        else:
            y = yd_ref[...]
        proj = jnp.dot(y, wb_ref[off:off + w, :], preferred_element_type=F32)
        off += w
        merged = gate * proj if merged is None else merged + gate * proj
    o_ref[...] = h + jnp.dot(merged.astype(BF16), wo_ref[...], preferred_element_type=F32)


def _merge(h, y_abc, y_d, g, wg, wb, wo, tm, widths):
    t, d = h.shape
    return pl.pallas_call(
        functools.partial(_merge_kernel, widths=widths),
        grid=(t // tm,),
        in_specs=[
            pl.BlockSpec((tm, d), lambda i: (i, 0)),
            pl.BlockSpec((tm, y_abc.shape[1]), lambda i: (i, 0)),
            pl.BlockSpec((tm, y_d.shape[1]), lambda i: (i, 0)),
            _resident(g.shape), _resident(wg.shape), _resident(wb.shape), _resident(wo.shape),
        ],
        out_specs=pl.BlockSpec((tm, d), lambda i: (i, 0)),
        out_shape=jax.ShapeDtypeStruct((t, d), F32),
        compiler_params=_params(("parallel",)),
        name="merge",
    )(h, y_abc, y_d, g, wg, wb, wo)


def _swiglu_acc(xn, wgu_ref, wd_ref, d_ff, between=None):
    acc = None
    n_chunks = d_ff // FF_CHUNK
    for c in range(n_chunks):
        lo = c * FF_CHUNK
        g = jnp.dot(xn, wgu_ref[:, lo:lo + FF_CHUNK], preferred_element_type=F32)
        u = jnp.dot(xn, wgu_ref[:, d_ff + lo:d_ff + lo + FF_CHUNK], preferred_element_type=F32)
        act = (g * jax.nn.sigmoid(g) * u).astype(BF16)
        part = jnp.dot(act, wd_ref[lo:lo + FF_CHUNK, :], preferred_element_type=F32)
        acc = part if acc is None else acc + part
        if between is not None:
            between(c, n_chunks)
    return acc


def _ffn_kernel(h_ref, g_ref, wgu_ref, wd_ref, o_ref):
    h = h_ref[...]
    xn = _rms(h, g_ref[...]).astype(BF16)
    o_ref[...] = h + _swiglu_acc(xn, wgu_ref, wd_ref, wd_ref.shape[0])


def _ffn(h, g, wgu, wd, tm):
    t, d = h.shape
    return pl.pallas_call(
        _ffn_kernel,
        grid=(t // tm,),
        in_specs=[pl.BlockSpec((tm, d), lambda i: (i, 0)), _resident(g.shape),
                  _resident(wgu.shape), _resident(wd.shape)],
        out_specs=pl.BlockSpec((tm, d), lambda i: (i, 0)),
        out_shape=jax.ShapeDtypeStruct((t, d), F32),
        compiler_params=_params(("parallel",)),
        name="ffn",
    )(h, g, wgu, wd)


def _router_kernel(h_ref, g_ref, rw_ref, o_ref, tot_ref, carry_ref):
    tm = h_ref.shape[0]

    @pl.when(pl.program_id(0) == 0)
    def _():
        carry_ref[...] = jnp.zeros(carry_ref.shape, F32)

    hn = _rms(h_ref[...], g_ref[...])
    hn_hi, hn_lo = _hi_lo(hn)
    logits = (jnp.dot(hn_hi, rw_ref[0], preferred_element_type=F32)
              + jnp.dot(hn_lo, rw_ref[0], preferred_element_type=F32)
              + jnp.dot(hn_hi, rw_ref[1], preferred_element_type=F32))
    lane = lax.broadcasted_iota(jnp.int32, (tm, LANES), 1)
    logits = jnp.where(lane < N_EXPERTS, logits, NEG_BIG)
    v1 = jnp.max(logits, axis=-1, keepdims=True)
    i1 = jnp.min(jnp.where(logits == v1, lane, LANES), axis=-1, keepdims=True)
    rest = jnp.where(lane == i1, NEG_BIG, logits)
    v2 = jnp.max(rest, axis=-1, keepdims=True)
    i2 = jnp.min(jnp.where(rest == v2, lane, LANES), axis=-1, keepdims=True)
    e = jnp.exp(v2 - v1)
    w1 = 1.0 / (1.0 + e)
    w2 = e / (1.0 + e)
    cnt = jnp.where((lane == i1) | (lane == i2), 1.0, 0.0)
    r = lax.broadcasted_iota(jnp.int32, (tm, tm), 0)
    c = lax.broadcasted_iota(jnp.int32, (tm, tm), 1)
    tri = jnp.where(c < r, 1.0, 0.0).astype(BF16)
    excl = jnp.dot(tri, cnt.astype(BF16), preferred_element_type=F32) + carry_ref[...]
    rank1 = jnp.sum(jnp.where(lane == i1, excl, 0.0), axis=-1, keepdims=True)
    rank2 = jnp.sum(jnp.where(lane == i2, excl, 0.0), axis=-1, keepdims=True)
    carry_ref[...] = carry_ref[...] + jnp.sum(cnt, axis=0, keepdims=True)
    tot_ref[...] = carry_ref[...]
    packed = jnp.where(lane == 0, i1.astype(F32), jnp.where(lane == 1, i2.astype(F32),
             jnp.where(lane == 2, w1, jnp.where(lane == 3, w2,
             jnp.where(lane == 4, rank1, jnp.where(lane == 5, rank2, 0.0))))))
    o_ref[...] = packed


def _router(h, g, rw_pad, tm):
    t, d = h.shape
    return pl.pallas_call(
        _router_kernel,
        grid=(t // tm,),
        in_specs=[pl.BlockSpec((tm, d), lambda i: (i, 0)), _resident(g.shape),
                  _resident(rw_pad.shape)],
        out_specs=[pl.BlockSpec((tm, LANES), lambda i: (i, 0)),
                   pl.BlockSpec((1, LANES), lambda i: (0, 0))],
        out_shape=[jax.ShapeDtypeStruct((t, LANES), F32), jax.ShapeDtypeStruct((1, LANES), F32)],
        scratch_shapes=[pltpu.VMEM((1, LANES), F32)],
        compiler_params=_params(("arbitrary",)),
        name="router",
    )(h, g, rw_pad)


def _invert_kernel(dest_ref, init_hbm, src_ref):
    c = pl.program_id(0)
    ch = dest_ref.shape[1]

    @pl.when(c == 0)
    def _():
        pltpu.sync_copy(init_hbm, src_ref)

    base = c * ch

    def place(a, x):
        src_ref[dest_ref[0, a]] = base + a
        return x

    lax.fori_loop(0, ch, place, 0, unroll=16)


def _invert(dest, rows):
    nc, _, ch = dest.shape
    return pl.pallas_call(
        _invert_kernel,
        grid=(nc,),
        in_specs=[pl.BlockSpec((None, 1, ch), lambda c: (c, 0, 0), memory_space=pltpu.SMEM),
                  pl.BlockSpec(memory_space=pl.ANY)],
        out_specs=pl.BlockSpec(memory_space=pltpu.SMEM),
        out_shape=jax.ShapeDtypeStruct((rows,), jnp.int32),
        compiler_params=pltpu.CompilerParams(dimension_semantics=("arbitrary",)),
        name="moe_invert",
    )(dest, jnp.full((rows,), -1, jnp.int32))


def _stream_chunks(copy, n, consume):
    copy(0, 0).start()
    for c in range(n):
        if c + 1 < n:
            copy(c + 1, (c + 1) % 2).start()
        copy(c, c % 2).wait()
        consume(c, c % 2)


def _expert_kernel(te_ref, nu_ref, src_cur, src_nxt, orow_prv, orow_cur, h_hbm, g_ref, wgu_hbm,
                   wd_hbm, yt_hbm, xbuf, ybuf, wgu_ref, wd_ref, stage_gu, stage_d, gsem, ssem, wsem):
    i = pl.program_id(0)
    fin = nu_ref[0] - 1
    tm = xbuf.shape[1]
    s = lax.rem(i, 2)
    o = 1 - s
    e = te_ref[i]

    def gather(tok, r, slot):
        return pltpu.make_async_copy(h_hbm.at[pl.ds(tok, 1)],
                                     xbuf.at[slot, pl.ds(r, 1)], gsem.at[slot])

    def scatter(row, r, slot):
        return pltpu.make_async_copy(ybuf.at[slot, pl.ds(r, 1)],
                                     yt_hbm.at[pl.ds(row, 1)], ssem.at[slot])

    def for_rows(fn):
        def body(r, x):
            fn(r)
            return x

        lax.fori_loop(0, tm, body, 0, unroll=8)

    def load_weights():
        wc = stage_gu.shape[2]
        rc = stage_d.shape[1]

        def copy_gu(c, slot):
            return pltpu.make_async_copy(wgu_hbm.at[e, :, pl.ds(c * wc, wc)], stage_gu.at[slot],
                                         wsem.at[slot])

        def store_gu(c, slot):
            wgu_ref[:, c * wc:(c + 1) * wc] = stage_gu[slot].astype(BF16)

        def copy_d(c, slot):
            return pltpu.make_async_copy(wd_hbm.at[e, pl.ds(c * rc, rc), :], stage_d.at[slot],
                                         wsem.at[slot])

        def store_d(c, slot):
            wd_ref[c * rc:(c + 1) * rc, :] = stage_d[slot].astype(BF16)

        _stream_chunks(copy_gu, wgu_ref.shape[1] // wc, store_gu)
        _stream_chunks(copy_d, wd_ref.shape[0] // rc, store_d)

    def spare_fill(k):
        rows0 = yt_hbm.shape[0] - (N_EXPERTS + 1 - k) * tm
        return pltpu.make_async_copy(ybuf.at[1], yt_hbm.at[pl.ds(rows0, tm)], ssem.at[0])

    @pl.when(i <= fin)
    def _():
        @pl.when((i == 0) | (e != te_ref[jnp.maximum(i - 1, 0)]))
        def _():
            load_weights()

        @pl.when(i == 0)
        def _():
            ybuf[1] = jnp.zeros(ybuf.shape[1:], ybuf.dtype)
            for k in range(1, N_EXPERTS + 1):
                spare_fill(k).start()
            for k in range(1, N_EXPERTS + 1):
                spare_fill(k).wait()
            for_rows(lambda r: gather(src_cur[0, r], r, 0).start())

        for_rows(lambda r: gather(src_cur[0, r], r, s).wait())
        xn = _rms(xbuf[s], g_ref[...]).astype(BF16)

        def between(c, n_chunks):
            per = -(-tm // (n_chunks // 2))
            for r in range(c * per, min((c + 1) * per, tm)):
                gather(src_nxt[0, r], r, o).start()
                scatter(orow_prv[0, r], r, o).start()

        ybuf[s] = _pack_halves(_swiglu_acc(xn, wgu_ref, wd_ref, wd_ref.shape[0], between))
        for_rows(lambda r: scatter(orow_prv[0, r], r, o).wait())

        @pl.when(i == fin)
        def _():
            for_rows(lambda r: gather(src_nxt[0, r], r, o).wait())
            for_rows(lambda r: scatter(orow_cur[0, r], r, s).start())
            for_rows(lambda r: scatter(orow_cur[0, r], r, s).wait())


def _experts(tile_expert, n_used, src_tok, out_row, h, g, wgu, wd, yt_rows):
    n, _, tm = src_tok.shape
    d = h.shape[1]
    d_ff = wd.shape[1]
    smem = functools.partial(pl.BlockSpec, (None, 1, tm), memory_space=pltpu.SMEM)
    grid_spec = pltpu.PrefetchScalarGridSpec(
        num_scalar_prefetch=2,
        grid=(n,),
        in_specs=[
            smem(lambda i, te, nu: (i, 0, 0)),
            smem(lambda i, te, nu: (jnp.minimum(i + 1, n - 1), 0, 0)),
            smem(lambda i, te, nu: (i, 0, 0)),
            smem(lambda i, te, nu: (i + 1, 0, 0)),
            pl.BlockSpec(memory_space=pl.ANY),
            pl.BlockSpec(g.shape, lambda i, te, nu: (0, 0), pipeline_mode=pl.Buffered(1)),
            pl.BlockSpec(memory_space=pl.ANY),
            pl.BlockSpec(memory_space=pl.ANY),
        ],
        out_specs=pl.BlockSpec(memory_space=pl.ANY),
        scratch_shapes=[pltpu.VMEM((2, tm, d), F32), pltpu.VMEM((2, tm, d // 2), jnp.uint32),
                        pltpu.VMEM((d, 2 * d_ff), BF16), pltpu.VMEM((d_ff, d), BF16),
                        pltpu.VMEM((2, d, WEIGHT_STAGE), F32), pltpu.VMEM((2, WEIGHT_STAGE, d), F32),
                        pltpu.SemaphoreType.DMA((2,)), pltpu.SemaphoreType.DMA((2,)),
                        pltpu.SemaphoreType.DMA((2,))],
    )
    return pl.pallas_call(
        _expert_kernel,
        grid_spec=grid_spec,
        out_shape=jax.ShapeDtypeStruct((yt_rows, d // 2), jnp.uint32),
        compiler_params=_params(("arbitrary",)),
        name="moe_experts",
    )(tile_expert, n_used, src_tok, src_tok, out_row, out_row, h, g, wgu, wd)


def _combine_kernel(h_ref, pk_ref, y1_ref, y2_ref, o_ref):
    pk = pk_ref[...]
    o_ref[...] = (h_ref[...] + pk[:, 2:3] * _unpack_halves(y1_ref[...])
                  + pk[:, 3:4] * _unpack_halves(y2_ref[...]))


def _combine(h, packed, yt, tm):
    t, d = h.shape
    nb = t // tm
    return pl.pallas_call(
        _combine_kernel,
        grid=(nb,),
        in_specs=[pl.BlockSpec((tm, d), lambda i: (i, 0)),
                  pl.BlockSpec((tm, LANES), lambda i: (i, 0)),
                  pl.BlockSpec((tm, d // 2), lambda i: (i, 0)),
                  pl.BlockSpec((tm, d // 2), lambda i: (nb + i, 0))],
        out_specs=pl.BlockSpec((tm, d), lambda i: (i, 0)),
        out_shape=jax.ShapeDtypeStruct((t, d), F32),
        compiler_params=_params(("parallel",)),
        name="moe_combine",
    )(h, packed, yt, yt)


def _rel_bucket(rel):
    nb = REL_BUCKETS // 2
    max_exact = nb // 2
    n = jnp.abs(rel)
    nf = jnp.maximum(n, 1).astype(F32)
    large = max_exact + (jnp.log(nf / max_exact) / math.log(REL_MAX_DIST / max_exact)
                         * (nb - max_exact)).astype(jnp.int32)
    large = jnp.minimum(large, nb - 1)
    return jnp.where(rel > 0, nb, 0) + jnp.where(n < max_exact, n, large)


def _near_bias(rel_bias):
    t = ATT_T
    qp = jnp.arange(t)[:, None]
    kp = jnp.arange(t)[None, :]

    def lookup(rel):
        onehot = jax.nn.one_hot(_rel_bucket(rel), REL_BUCKETS, dtype=F32)
        return jnp.einsum('...b,bm->...m', onehot, rel_bias, precision=lax.Precision.HIGHEST)

    far = lookup(jnp.full((), -(2 * t), jnp.int32))
    period = 3 * t
    m = jnp.arange(period)
    rel_of_m = jnp.where(m < 2 * t, m, m - period) - t
    table = (lookup(rel_of_m) - far) * LOG2E
    toeplitz = jnp.tile(table, (t, 1))[:t * (period - 1)].reshape(t, period - 1, -1)
    prev = toeplitz[:, :t]
    diag = toeplitz[:, t:2 * t]
    diag = jnp.where(((kp // CHUNK) <= (qp // CHUNK))[:, :, None], diag, NEG_BIG)
    both = jnp.stack([prev, diag], axis=0).reshape(2, t, t, DIFF_HEADS, 2)
    return both.transpose(3, 0, 2, 4, 1).reshape(DIFF_HEADS, 2, t, 2 * t).astype(F32)


def kernel(x, rel_bias, norm1_g, w_in, pool_w, pool_scale, conv_w, sgu_ln_g, sgu_w, sgu_b, q_norm_g, k_norm_g, diff_lambda, subln_g, w_branch_pool, w_branch_conv, w_branch_sgu, w_branch_attn, w_out, norm2_g, ffn_w_gate_up, ffn_w_down, router_w, moe_w_gate_up, moe_w_down):
    b, s, d = x.shape
    t = b * s
    depth = w_in.shape[0]
    pw = pool_scale.shape[1]
    cw = conv_w.shape[2]
    sw = sgu_ln_g.shape[1]
    aw = w_branch_attn.shape[1]
    mix_cols = pw + 3 * cw + 2 * sw + 3 * aw
    nb = REL_BUCKETS // 2
    assert nb // 2 + int(math.log((ATT_T + 1) / (nb // 2)) / math.log(REL_MAX_DIST / (nb // 2))
                         * (nb - nb // 2)) >= nb - 1
    qk_off = pw + 3 * cw + 2 * sw
    z_cols = qk_off + 2 * aw
    q_col = qk_off // LANES
    k_col = q_col + aw // LANES
    tm = min(512, t)
    td = min(DENSE_TM, s)
    ts = min(512, s)

    bias_near = _near_bias(rel_bias)
    tri = jnp.tril(jnp.ones((SGU_SEG, SGU_SEG), bool))
    gd = pw // POOL_GROUPS

    h = x.reshape(t, d)
    for layer in range(depth):
        lam_init = 0.8 - 0.6 * math.exp(-0.3 * layer)
        w_mix = w_in[layer, :, :z_cols].astype(BF16)
        w_vt = w_in[layer, :, z_cols:mix_cols].T.astype(BF16)
        w_gate = w_in[layer, :, mix_cols:].astype(BF16)
        poolw_bd = jnp.zeros((pw, pw), F32)
        for g in range(POOL_GROUPS):
            poolw_bd = poolw_bd.at[g * gd:(g + 1) * gd, g * gd:(g + 1) * gd].set(pool_w[layer, g])
        sguw_cat = jnp.where(tri[None], sgu_w[layer], 0.0).transpose(1, 0, 2).reshape(
            SGU_SEG, SGU_GROUPS * SGU_SEG).astype(BF16)
        sgub_full = jnp.repeat(sgu_b[layer].T, sw // SGU_GROUPS, axis=1)
        wb = jnp.concatenate([w_branch_pool[layer], w_branch_conv[layer], w_branch_sgu[layer],
                              w_branch_attn[layer]], axis=0).astype(BF16)

        z, vt = _in_proj(h, norm1_g[layer][None], w_mix, w_vt, td, s)
        z3 = z.reshape(b, s, z_cols)
        y_abc = _local_mix(z3, poolw_bd.astype(BF16), pool_scale[layer][None], conv_w[layer],
                           sgu_ln_g[layer][None], sguw_cat, sgub_full, ts, pw, cw, sw)
        y_d = _diff_attn(z3, vt, bias_near, jnp.tile(q_norm_g[layer], 2)[None],
                         jnp.tile(k_norm_g[layer], 2)[None], diff_lambda[layer],
                         subln_g[layer][None], lam_init, q_col, k_col)
        h = _merge(h, y_abc.reshape(t, -1), y_d.reshape(t, -1), norm1_g[layer][None], w_gate, wb,
                   w_out[layer].astype(BF16), td, (pw, cw, sw, aw))

        g2 = norm2_g[layer][None]
        if layer % 2 == 0:
            h = _ffn(h, g2, ffn_w_gate_up[layer // 2].astype(BF16),
                     ffn_w_down[layer // 2].astype(BF16), td)
        else:
            li = layer // 2
            rw_pad = jnp.zeros((d, LANES), F32).at[:, :N_EXPERTS].set(router_w[li])
            packed, totals = _router(h, g2, jnp.stack(_hi_lo(rw_pad)), tm)
            n_e = totals[0, :N_EXPERTS].astype(jnp.int32)
            n_pad = ((n_e + tm - 1) // tm) * tm
            ends = jnp.cumsum(n_pad)
            starts = ends - n_pad
            e1 = packed[:, 0].astype(jnp.int32)
            e2 = packed[:, 1].astype(jnp.int32)
            eids = jnp.arange(N_EXPERTS)[None, :]
            dest1 = (jnp.sum(jnp.where(e1[:, None] == eids, starts[None, :], 0), axis=1)
                     + packed[:, 4].astype(jnp.int32))
            dest2 = (jnp.sum(jnp.where(e2[:, None] == eids, starts[None, :], 0), axis=1)
                     + packed[:, 5].astype(jnp.int32))
            rows = 2 * t + N_EXPERTS * tm
            n_tiles = rows // tm
            tile_expert = jnp.minimum(
                jnp.sum((jnp.arange(n_tiles)[:, None] * tm) >= ends[None, :], axis=1),
                N_EXPERTS - 1).astype(jnp.int32)
            n_used = (ends[-1] // tm).astype(jnp.int32)[None]
            ch = min(4096, t)
            dest_a = jnp.concatenate([dest1, dest2]).reshape(2 * t // ch, 1, ch)
            src = _invert(dest_a, rows)
            is_pad = src < 0
            src_tok = jnp.where(is_pad, 0, jnp.where(src >= t, src - t, src))
            pad_rank = jnp.cumsum(is_pad.astype(jnp.int32)) - 1
            out_row = jnp.where(is_pad, 2 * t + tm + pad_rank, src)
            spare = 2 * t + jnp.arange(tm, dtype=jnp.int32)
            yt = _experts(tile_expert, n_used, src_tok.reshape(n_tiles, 1, tm),
                          jnp.concatenate([spare, out_row]).reshape(n_tiles + 1, 1, tm), h, g2,
                          moe_w_gate_up[li], moe_w_down[li], rows + tm)
            h = _combine(h, packed, yt, tm)
    return h.reshape(b, s, d)
```

```python
import functools
import math

import jax
import jax.numpy as jnp
from jax import lax
from jax.experimental import pallas as pl
from jax.experimental.pallas import tpu as pltpu

F32 = jnp.float32
BF16 = jnp.bfloat16

NORM_EPS = 1e-6
CHUNK = 64
POOL_GROUPS = 4
SGU_GROUPS = 4
SGU_SEG = 128
DIFF_HEADS = 4
DIFF_QK_DIM = 64
DIFF_V_DIM = 128
REL_BUCKETS = 32
REL_MAX_DIST = 128
N_EXPERTS = 8
LANES = 128
V7X_VMEM_BYTES = 64 * 1024 * 1024
VMEM_LIMIT = V7X_VMEM_BYTES - 8 * 1024 * 1024
NEG_BIG = -1e30
LOG2E = math.log2(math.e)

HALO = 16
ATT_T = 256
ATT_HEADS_PER_STEP = 2
FF_CHUNK = 256
WEIGHT_STAGE = 512
DENSE_TM = 1024


def _rms(x, g):
    return x * lax.rsqrt(jnp.mean(x * x, axis=-1, keepdims=True) + NORM_EPS) * g


def _resident(shape):
    nd = len(shape)
    return pl.BlockSpec(shape, lambda *_: (0,) * nd, pipeline_mode=pl.Buffered(1))


def _params(sem):
    return pltpu.CompilerParams(dimension_semantics=sem, vmem_limit_bytes=VMEM_LIMIT)


def _hi_lo(x):
    hi = x.astype(BF16)
    return hi, (x - hi.astype(F32)).astype(BF16)


def _pack_halves(x):
    n = x.shape[1] // 2
    bits = lax.bitcast_convert_type(x, jnp.uint32) + jnp.uint32(0x8000)
    return (bits[:, :n] & jnp.uint32(0xFFFF0000)) | (bits[:, n:] >> 16)


def _unpack_halves(p):
    hi = lax.bitcast_convert_type(p & jnp.uint32(0xFFFF0000), F32)
    lo = lax.bitcast_convert_type(p << 16, F32)
    return jnp.concatenate([hi, lo], axis=1)


def _in_proj_kernel(x_ref, g_ref, w_ref, wvt_ref, o_ref, vt_ref, *, n_chunk):
    xn = _rms(x_ref[...], g_ref[...]).astype(BF16)
    n = o_ref.shape[1]
    for j in range(n // n_chunk):
        sl = slice(j * n_chunk, (j + 1) * n_chunk)
        o_ref[:, sl] = jnp.dot(xn, w_ref[:, sl], preferred_element_type=F32).astype(o_ref.dtype)
    vt_ref[...] = lax.dot_general(wvt_ref[...], xn, (((1,), (1,)), ((), ())),
                                  preferred_element_type=F32).astype(vt_ref.dtype)


def _in_proj(h, g, w, w_vt, tm, seq):
    t, d = h.shape
    n = w.shape[1]
    nv = w_vt.shape[0]
    per_seq = seq // tm
    return pl.pallas_call(
        functools.partial(_in_proj_kernel, n_chunk=512),
        grid=(t // tm,),
        in_specs=[pl.BlockSpec((tm, d), lambda i: (i, 0)), _resident((1, d)), _resident((d, n)),
                  _resident(w_vt.shape)],
        out_specs=[pl.BlockSpec((tm, n), lambda i: (i, 0)),
                   pl.BlockSpec((None, nv, tm), lambda i: (i // per_seq, 0, i % per_seq))],
        out_shape=[jax.ShapeDtypeStruct((t, n), BF16),
                   jax.ShapeDtypeStruct((t // seq, nv, seq), BF16)],
        compiler_params=_params(("parallel",)),
        name="in_proj",
    )(h, g, w, w_vt)


def _local_mix_kernel(z_ref, halo_ref, poolw_ref, pscale_ref, convw_ref, lng_ref, sguw_ref,
                      sgub_ref, o_ref, *, pw, cw):
    ts = z_ref.shape[0]
    i = pl.program_id(1)
    z = z_ref[...].astype(F32)
    halo = halo_ref[...].astype(F32)
    halo = jnp.where(i > 0, halo, 0.0)
    ext = jnp.concatenate([halo[:, :pw + 3 * cw], z[:, :pw + 3 * cw]], axis=0)
    rows = ext.shape[0]

    def back(x, k):
        return pltpu.roll(x, k, axis=0)

    a = ext[:, :pw]
    s2 = a + back(a, 1)
    s4 = s2 + back(s2, 2)
    s8 = s4 + back(s4, 4)
    s16 = s8 + back(s8, 8)
    lane = lax.broadcasted_iota(jnp.int32, (rows, pw), 1)
    grp = lane // (pw // POOL_GROUPS)
    win_sum = jnp.where(grp == 0, s2, jnp.where(grp == 1, s4, jnp.where(grp == 2, s8, s16)))
    win = jnp.where(grp == 0, 2, jnp.where(grp == 1, 4, jnp.where(grp == 2, 8, 16)))
    pos = i * ts - HALO + lax.broadcasted_iota(jnp.int32, (rows, pw), 0)
    count = jnp.minimum(pos + 1, win).astype(F32)
    pooled = (win_sum / jnp.maximum(count, 1.0) - a)[HALO:]
    y_a = jnp.dot(pooled.astype(BF16), poolw_ref[...], preferred_element_type=F32) * pscale_ref[...]
    o_ref[:, 0:pw] = y_a.astype(o_ref.dtype)

    b_gate = z[:, pw:pw + cw]
    zc = ext[:, pw + cw:pw + 2 * cw] * ext[:, pw + 2 * cw:pw + 3 * cw]
    conv = (convw_ref[0:1, :] * back(zc, 2) + convw_ref[1:2, :] * back(zc, 1)
            + convw_ref[2:3, :] * zc)[HALO:]
    o_ref[:, pw:pw + cw] = (b_gate * conv).astype(o_ref.dtype)

    sw = (z.shape[1] - pw - 3 * cw) // 2
    zc_uv = z[:, pw + 3 * cw:]
    uv = 0.5 * zc_uv * (1.0 + lax.erf(zc_uv * math.sqrt(0.5)))
    u = uv[:, :sw]
    v = uv[:, sw:]
    mu = jnp.mean(v, axis=-1, keepdims=True)
    var = jnp.mean(jnp.square(v - mu), axis=-1, keepdims=True)
    vn = (v - mu) * lax.rsqrt(var + NORM_EPS) * lng_ref[...]
    glane = lax.broadcasted_iota(jnp.int32, (SGU_SEG, sw), 1) // (sw // SGU_GROUPS)
    wcat = sguw_ref[...]
    bias = sgub_ref[...]
    for n in range(ts // SGU_SEG):
        seg = vn[n * SGU_SEG:(n + 1) * SGU_SEG]
        rhs = jnp.concatenate(
            [jnp.where(glane == g, seg, 0.0) for g in range(SGU_GROUPS)], axis=0).astype(BF16)
        s = jnp.dot(wcat, rhs, preferred_element_type=F32) + bias
        o_ref[n * SGU_SEG:(n + 1) * SGU_SEG, pw + cw:pw + cw + sw] = (
            u[n * SGU_SEG:(n + 1) * SGU_SEG] * s).astype(o_ref.dtype)


def _local_mix(z3, poolw_bd, pscale, convw, lng, sguw_cat, sgub_full, ts, pw, cw, sw):
    b, s, _ = z3.shape
    cols = pw + 3 * cw + 2 * sw
    hb = ts // HALO
    return pl.pallas_call(
        functools.partial(_local_mix_kernel, pw=pw, cw=cw),
        grid=(b, s // ts),
        in_specs=[
            pl.BlockSpec((None, ts, cols), lambda bi, i: (bi, i, 0)),
            pl.BlockSpec((None, HALO, cols), lambda bi, i: (bi, jnp.maximum(i * hb - 1, 0), 0)),
            _resident(poolw_bd.shape), _resident(pscale.shape), _resident(convw.shape),
            _resident(lng.shape), _resident(sguw_cat.shape), _resident(sgub_full.shape),
        ],
        out_specs=pl.BlockSpec((None, ts, pw + cw + sw), lambda bi, i: (bi, i, 0)),
        out_shape=jax.ShapeDtypeStruct((b, s, pw + cw + sw), BF16),
        compiler_params=_params(("parallel", "parallel")),
        name="local_mix",
    )(z3, z3, poolw_bd, pscale, convw, lng, sguw_cat, sgub_full)


def _diff_attn_kernel(q_ref, k_ref, vt_ref, bias_ref, qg_ref, kg_ref, lam_ref, sg_ref, o_ref,
                      kn_ref, qs_ref, st_ref, m_ref, *, lam_init):
    tq = ATT_T
    hp = qs_ref.shape[0]
    nt = k_ref.shape[0] // tq
    hw = 2 * DIFF_QK_DIM
    half = lax.broadcasted_iota(jnp.int32, (1, hw), 1) < DIFF_QK_DIM
    same_map = (lax.broadcasted_iota(jnp.int32, (hw, hw), 0) // DIFF_QK_DIM
                == lax.broadcasted_iota(jnp.int32, (hw, hw), 1) // DIFF_QK_DIM)
    ones_map = jnp.where(same_map, 1.0, 0.0).astype(BF16)

    def qk_norm_mxu(x, g):
        sq_hi, sq_lo = _hi_lo(x * x)
        ss = (jnp.dot(sq_hi, ones_map, preferred_element_type=F32)
              + jnp.dot(sq_lo, ones_map, preferred_element_type=F32))
        return x * lax.rsqrt(ss * (1.0 / DIFF_QK_DIM) + NORM_EPS) * g

    def qk_norm(x, g):
        sq = x * x
        ss0 = jnp.sum(jnp.where(half, sq, 0.0), axis=-1, keepdims=True)
        ss1 = jnp.sum(jnp.where(half, 0.0, sq), axis=-1, keepdims=True)
        r0 = lax.rsqrt(ss0 * (1.0 / DIFF_QK_DIM) + NORM_EPS)
        r1 = lax.rsqrt(ss1 * (1.0 / DIFF_QK_DIM) + NORM_EPS)
        return x * jnp.where(half, r0, r1) * g

    for h in range(hp):
        cols = slice(h * hw, (h + 1) * hw)
        for j in range(nt):
            rows = slice(j * tq, (j + 1) * tq)
            kn_ref[h, rows, :] = qk_norm_mxu(k_ref[rows, cols].astype(F32),
                                             kg_ref[...]).astype(BF16)

    def step(c):
        cur, prv = c % 2, 1 - c % 2
        scoring = c < nt
        if scoring:
            for h in range(hp):
                qn = (qk_norm(q_ref[c * tq:(c + 1) * tq, h * hw:(h + 1) * hw].astype(F32),
                              qg_ref[...]) * (DIFF_QK_DIM ** -0.5 * LOG2E))
                qs_ref[h, 0:tq, :] = jnp.where(half, qn, 0.0).astype(BF16)
                qs_ref[h, tq:2 * tq, :] = jnp.where(half, 0.0, qn).astype(BF16)
        m_new = [jnp.full((1, 2 * tq), NEG_BIG, F32) for _ in range(hp)]
        m_old = [m_ref[prv, h] for h in range(hp)] if c >= 1 else None
        l = [jnp.zeros((1, 2 * tq), F32) for _ in range(hp)]
        acc = [jnp.zeros((DIFF_V_DIM, 2 * tq), F32) for _ in range(hp)]
        for j in range(c + 1):
            rows = slice(j * tq, (j + 1) * tq)
            for h in range(hp):
                if scoring:
                    st = lax.dot_general(kn_ref[h, rows, :], qs_ref[h],
                                         (((1,), (1,)), ((), ())), preferred_element_type=F32)
                    if j >= c - 1:
                        st = st + bias_ref[h, j - (c - 1)]
                    st_ref[cur, h, rows, :] = st
                    m_new[h] = jnp.maximum(m_new[h], jnp.max(st, axis=0, keepdims=True))
                if j < c:
                    p = jnp.exp2(st_ref[prv, h, rows, :] - m_old[h])
                    l[h] = l[h] + jnp.sum(p, axis=0, keepdims=True)
                    acc[h] = acc[h] + jnp.dot(vt_ref[h * DIFF_V_DIM:(h + 1) * DIFF_V_DIM, rows],
                                              p.astype(BF16), preferred_element_type=F32)
        if scoring:
            for h in range(hp):
                m_ref[cur, h] = m_new[h]
        if c >= 1:
            lp = lam_ref[...]
            lam = (jnp.exp(jnp.sum(lp[0:1] * lp[1:2], axis=-1, keepdims=True))
                   - jnp.exp(jnp.sum(lp[2:3] * lp[3:4], axis=-1, keepdims=True)) + lam_init)
            for h in range(hp):
                o = (acc[h][:, :tq] * (1.0 / l[h][:, :tq])
                     - acc[h][:, tq:] * (lam / l[h][:, tq:]))
                o = o * lax.rsqrt(jnp.mean(o * o, axis=0, keepdims=True) + NORM_EPS)
                o_ref[(c - 1) * tq:c * tq, h * DIFF_V_DIM:(h + 1) * DIFF_V_DIM] = (
                    o.T * (sg_ref[...] * (1.0 - lam_init))).astype(o_ref.dtype)

    for c in range(nt + 1):
        step(c)


def _diff_attn(z3, vt, bias_near, qg2, kg2, lam_p, subln_g, lam_init, q_col, k_col):
    b, s, _ = z3.shape
    tq = ATT_T
    hp = ATT_HEADS_PER_STEP
    nt = s // tq
    hw = 2 * DIFF_QK_DIM
    return pl.pallas_call(
        functools.partial(_diff_attn_kernel, lam_init=lam_init),
        grid=(b, DIFF_HEADS // hp),
        in_specs=[
            pl.BlockSpec((None, s, hp * hw), lambda bi, g: (bi, 0, q_col // hp + g)),
            pl.BlockSpec((None, s, hp * hw), lambda bi, g: (bi, 0, k_col // hp + g)),
            pl.BlockSpec((None, hp * DIFF_V_DIM, s), lambda bi, g: (bi, g, 0)),
            pl.BlockSpec((hp, 2, tq, 2 * tq), lambda bi, g: (g, 0, 0, 0)),
            _resident(qg2.shape), _resident(kg2.shape), _resident(lam_p.shape),
            _resident(subln_g.shape),
        ],
        out_specs=pl.BlockSpec((None, s, hp * DIFF_V_DIM), lambda bi, g: (bi, 0, g)),
        out_shape=jax.ShapeDtypeStruct((b, s, DIFF_HEADS * DIFF_V_DIM), BF16),
        scratch_shapes=[
            pltpu.VMEM((hp, s, hw), BF16),
            pltpu.VMEM((hp, 2 * tq, hw), BF16),
            pltpu.VMEM((2, hp, s, 2 * tq), F32),
            pltpu.VMEM((2, hp, 1, 2 * tq), F32),
        ],
        compiler_params=_params(("parallel", "parallel")),
        name="diff_attn",
    )(z3, z3, vt, bias_near, qg2, kg2, lam_p, subln_g)


def _merge_kernel(h_ref, yabc_ref, yd_ref, g_ref, wg_ref, wb_ref, wo_ref, o_ref, *, widths):
    h = h_ref[...]
    d = h.shape[1]
    xn = _rms(h, g_ref[...]).astype(BF16)
    merged = None
    off = 0
    yoff = 0
    for bi, w in enumerate(widths):
        gate = jax.nn.sigmoid(jnp.dot(xn, wg_ref[:, bi * d:(bi + 1) * d],
                                      preferred_element_type=F32))
        if bi < len(widths) - 1:
            y = yabc_ref[:, yoff:yoff + w]
            yoff += w
        else:
            y = yd_ref[...]
        proj = jnp.dot(y, wb_ref[off:off + w, :], preferred_element_type=F32)
        off += w
        merged = gate * proj if merged is None else merged + gate * proj
    o_ref[...] = h + jnp.dot(merged.astype(BF16), wo_ref[...], preferred_element_type=F32)


def _merge(h, y_abc, y_d, g, wg, wb, wo, tm, widths):
    t, d = h.shape
    return pl.pallas_call(
        functools.partial(_merge_kernel, widths=widths),
        grid=(t // tm,),
        in_specs=[
            pl.BlockSpec((tm, d), lambda i: (i, 0)),
            pl.BlockSpec((tm, y_abc.shape[1]), lambda i: (i, 0)),
            pl.BlockSpec((tm, y_d.shape[1]), lambda i: (i, 0)),
            _resident(g.shape), _resident(wg.shape), _resident(wb.shape), _resident(wo.shape),
        ],
        out_specs=pl.BlockSpec((tm, d), lambda i: (i, 0)),
        out_shape=jax.ShapeDtypeStruct((t, d), F32),
        compiler_params=_params(("parallel",)),
        name="merge",
    )(h, y_abc, y_d, g, wg, wb, wo)


def _swiglu_acc(xn, wgu_ref, wd_ref, d_ff, between=None):
    acc = None
    n_chunks = d_ff // FF_CHUNK
    for c in range(n_chunks):
        lo = c * FF_CHUNK
        g = jnp.dot(xn, wgu_ref[:, lo:lo + FF_CHUNK], preferred_element_type=F32)
        u = jnp.dot(xn, wgu_ref[:, d_ff + lo:d_ff + lo + FF_CHUNK], preferred_element_type=F32)
        act = (g * jax.nn.sigmoid(g) * u).astype(BF16)
        part = jnp.dot(act, wd_ref[lo:lo + FF_CHUNK, :], preferred_element_type=F32)
        acc = part if acc is None else acc + part
        if between is not None:
            between(c, n_chunks)
    return acc


def _ffn_kernel(h_ref, g_ref, wgu_ref, wd_ref, o_ref):
    h = h_ref[...]
    xn = _rms(h, g_ref[...]).astype(BF16)
    o_ref[...] = h + _swiglu_acc(xn, wgu_ref, wd_ref, wd_ref.shape[0])


def _ffn(h, g, wgu, wd, tm):
    t, d = h.shape
    return pl.pallas_call(
        _ffn_kernel,
        grid=(t // tm,),
        in_specs=[pl.BlockSpec((tm, d), lambda i: (i, 0)), _resident(g.shape),
                  _resident(wgu.shape), _resident(wd.shape)],
        out_specs=pl.BlockSpec((tm, d), lambda i: (i, 0)),
        out_shape=jax.ShapeDtypeStruct((t, d), F32),
        compiler_params=_params(("parallel",)),
        name="ffn",
    )(h, g, wgu, wd)


def _router_kernel(h_ref, g_ref, rw_ref, o_ref, ot_ref, tot_ref, carry_ref):
    tm = h_ref.shape[0]

    @pl.when(pl.program_id(0) == 0)
    def _():
        carry_ref[...] = jnp.zeros(carry_ref.shape, F32)

    hn = _rms(h_ref[...], g_ref[...])
    hn_hi, hn_lo = _hi_lo(hn)
    logits = (jnp.dot(hn_hi, rw_ref[0], preferred_element_type=F32)
              + jnp.dot(hn_lo, rw_ref[0], preferred_element_type=F32)
              + jnp.dot(hn_hi, rw_ref[1], preferred_element_type=F32))
    lane = lax.broadcasted_iota(jnp.int32, (tm, LANES), 1)
    logits = jnp.where(lane < N_EXPERTS, logits, NEG_BIG)
    v1 = jnp.max(logits, axis=-1, keepdims=True)
    i1 = jnp.min(jnp.where(logits == v1, lane, LANES), axis=-1, keepdims=True)
    rest = jnp.where(lane == i1, NEG_BIG, logits)
    v2 = jnp.max(rest, axis=-1, keepdims=True)
    i2 = jnp.min(jnp.where(rest == v2, lane, LANES), axis=-1, keepdims=True)
    e = jnp.exp(v2 - v1)
    w1 = 1.0 / (1.0 + e)
    w2 = e / (1.0 + e)
    cnt = jnp.where((lane == i1) | (lane == i2), 1.0, 0.0)
    r = lax.broadcasted_iota(jnp.int32, (tm, tm), 0)
    c = lax.broadcasted_iota(jnp.int32, (tm, tm), 1)
    tri = jnp.where(c < r, 1.0, 0.0).astype(BF16)
    excl = jnp.dot(tri, cnt.astype(BF16), preferred_element_type=F32) + carry_ref[...]
    rank1 = jnp.sum(jnp.where(lane == i1, excl, 0.0), axis=-1, keepdims=True)
    rank2 = jnp.sum(jnp.where(lane == i2, excl, 0.0), axis=-1, keepdims=True)
    carry_ref[...] = carry_ref[...] + jnp.sum(cnt, axis=0, keepdims=True)
    tot_ref[...] = carry_ref[...]
    packed = jnp.where(lane == 0, i1.astype(F32), jnp.where(lane == 1, i2.astype(F32),
             jnp.where(lane == 2, w1, jnp.where(lane == 3, w2,
             jnp.where(lane == 4, rank1, jnp.where(lane == 5, rank2, 0.0))))))
    o_ref[...] = packed
    ot_ref[...] = packed.T[0:8, :]


def _router(h, g, rw_pad, tm):
    t, d = h.shape
    return pl.pallas_call(
        _router_kernel,
        grid=(t // tm,),
        in_specs=[pl.BlockSpec((tm, d), lambda i: (i, 0)), _resident(g.shape),
                  _resident(rw_pad.shape)],
        out_specs=[pl.BlockSpec((tm, LANES), lambda i: (i, 0)),
                   pl.BlockSpec((8, tm), lambda i: (0, i)),
                   pl.BlockSpec((1, LANES), lambda i: (0, 0))],
        out_shape=[jax.ShapeDtypeStruct((t, LANES), F32), jax.ShapeDtypeStruct((8, t), F32),
                   jax.ShapeDtypeStruct((1, LANES), F32)],
        scratch_shapes=[pltpu.VMEM((1, LANES), F32)],
        compiler_params=_params(("arbitrary",)),
        name="router",
    )(h, g, rw_pad)


def _invert_kernel(dest_ref, init_hbm, src_ref):
    c = pl.program_id(0)
    ch = dest_ref.shape[1]

    @pl.when(c == 0)
    def _():
        pltpu.sync_copy(init_hbm, src_ref)

    base = c * ch

    def place(a, x):
        src_ref[dest_ref[0, a]] = base + a
        return x

    lax.fori_loop(0, ch, place, 0, unroll=16)


def _invert(dest, rows):
    nc, _, ch = dest.shape
    return pl.pallas_call(
        _invert_kernel,
        grid=(nc,),
        in_specs=[pl.BlockSpec((None, 1, ch), lambda c: (c, 0, 0), memory_space=pltpu.SMEM),
                  pl.BlockSpec(memory_space=pl.ANY)],
        out_specs=pl.BlockSpec(memory_space=pltpu.SMEM),
        out_shape=jax.ShapeDtypeStruct((rows,), jnp.int32),
        compiler_params=pltpu.CompilerParams(dimension_semantics=("arbitrary",)),
        name="moe_invert",
    )(dest, jnp.full((rows,), -1, jnp.int32))


def _stream_chunks(copy, n, consume):
    copy(0, 0).start()
    for c in range(n):
        if c + 1 < n:
            copy(c + 1, (c + 1) % 2).start()
        copy(c, c % 2).wait()
        consume(c, c % 2)


def _expert_kernel(te_ref, nu_ref, src_cur, src_nxt, orow_prv, orow_cur, h_hbm, g_ref, wgu_hbm,
                   wd_hbm, yt_hbm, xbuf, ybuf, wgu_ref, wd_ref, stage_gu, stage_d, gsem, ssem, wsem):
    i = pl.program_id(0)
    fin = nu_ref[0] - 1
    tm = xbuf.shape[1]
    s = lax.rem(i, 2)
    o = 1 - s
    e = te_ref[i]

    def gather(tok, r, slot):
        return pltpu.make_async_copy(h_hbm.at[pl.ds(tok, 1)],
                                     xbuf.at[slot, pl.ds(r, 1)], gsem.at[slot])

    def scatter(row, r, slot):
        return pltpu.make_async_copy(ybuf.at[slot, pl.ds(r, 1)],
                                     yt_hbm.at[pl.ds(row, 1)], ssem.at[slot])

    def for_rows(fn):
        def body(r, x):
            fn(r)
            return x

        lax.fori_loop(0, tm, body, 0, unroll=8)

    def load_weights():
        wc = stage_gu.shape[2]
        rc = stage_d.shape[1]

        def copy_gu(c, slot):
            return pltpu.make_async_copy(wgu_hbm.at[e, :, pl.ds(c * wc, wc)], stage_gu.at[slot],
                                         wsem.at[slot])

        def store_gu(c, slot):
            wgu_ref[:, c * wc:(c + 1) * wc] = stage_gu[slot].astype(BF16)

        def copy_d(c, slot):
            return pltpu.make_async_copy(wd_hbm.at[e, pl.ds(c * rc, rc), :], stage_d.at[slot],
                                         wsem.at[slot])

        def store_d(c, slot):
            wd_ref[c * rc:(c + 1) * rc, :] = stage_d[slot].astype(BF16)

        _stream_chunks(copy_gu, wgu_ref.shape[1] // wc, store_gu)
        _stream_chunks(copy_d, wd_ref.shape[0] // rc, store_d)

    def spare_fill(k):
        rows0 = yt_hbm.shape[0] - (N_EXPERTS + 1 - k) * tm
        return pltpu.make_async_copy(ybuf.at[1], yt_hbm.at[pl.ds(rows0, tm)], ssem.at[0])

    @pl.when(i <= fin)
    def _():
        @pl.when((i == 0) | (e != te_ref[jnp.maximum(i - 1, 0)]))
        def _():
            load_weights()

        @pl.when(i == 0)
        def _():
            ybuf[1] = jnp.zeros(ybuf.shape[1:], ybuf.dtype)
            for k in range(1, N_EXPERTS + 1):
                spare_fill(k).start()
            for k in range(1, N_EXPERTS + 1):
                spare_fill(k).wait()
            for_rows(lambda r: gather(src_cur[0, r], r, 0).start())

        for_rows(lambda r: gather(src_cur[0, r], r, s).wait())
        xn = _rms(xbuf[s], g_ref[...]).astype(BF16)

        def between(c, n_chunks):
            per = -(-tm // (n_chunks // 2))
            for r in range(c * per, min((c + 1) * per, tm)):
                gather(src_nxt[0, r], r, o).start()
                scatter(orow_prv[0, r], r, o).start()

        ybuf[s] = _pack_halves(_swiglu_acc(xn, wgu_ref, wd_ref, wd_ref.shape[0], between))
        for_rows(lambda r: scatter(orow_prv[0, r], r, o).wait())

        @pl.when(i == fin)
        def _():
            for_rows(lambda r: gather(src_nxt[0, r], r, o).wait())
            for_rows(lambda r: scatter(orow_cur[0, r], r, s).start())
            for_rows(lambda r: scatter(orow_cur[0, r], r, s).wait())


def _experts(tile_expert, n_used, src_tok, out_row, h, g, wgu, wd, yt_rows):
    n, _, tm = src_tok.shape
    d = h.shape[1]
    d_ff = wd.shape[1]
    smem = functools.partial(pl.BlockSpec, (None, 1, tm), memory_space=pltpu.SMEM)
    grid_spec = pltpu.PrefetchScalarGridSpec(
        num_scalar_prefetch=2,
        grid=(n,),
        in_specs=[
            smem(lambda i, te, nu: (i, 0, 0)),
            smem(lambda i, te, nu: (jnp.minimum(i + 1, n - 1), 0, 0)),
            smem(lambda i, te, nu: (i, 0, 0)),
            smem(lambda i, te, nu: (i + 1, 0, 0)),
            pl.BlockSpec(memory_space=pl.ANY),
            pl.BlockSpec(g.shape, lambda i, te, nu: (0, 0), pipeline_mode=pl.Buffered(1)),
            pl.BlockSpec(memory_space=pl.ANY),
            pl.BlockSpec(memory_space=pl.ANY),
        ],
        out_specs=pl.BlockSpec(memory_space=pl.ANY),
        scratch_shapes=[pltpu.VMEM((2, tm, d), F32), pltpu.VMEM((2, tm, d // 2), jnp.uint32),
                        pltpu.VMEM((d, 2 * d_ff), BF16), pltpu.VMEM((d_ff, d), BF16),
                        pltpu.VMEM((2, d, WEIGHT_STAGE), F32), pltpu.VMEM((2, WEIGHT_STAGE, d), F32),
                        pltpu.SemaphoreType.DMA((2,)), pltpu.SemaphoreType.DMA((2,)),
                        pltpu.SemaphoreType.DMA((2,))],
    )
    return pl.pallas_call(
        _expert_kernel,
        grid_spec=grid_spec,
        out_shape=jax.ShapeDtypeStruct((yt_rows, d // 2), jnp.uint32),
        compiler_params=_params(("arbitrary",)),
        name="moe_experts",
    )(tile_expert, n_used, src_tok, src_tok, out_row, out_row, h, g, wgu, wd)


def _combine_kernel(h_ref, pk_ref, y1_ref, y2_ref, o_ref):
    pk = pk_ref[...]
    o_ref[...] = (h_ref[...] + pk[:, 2:3] * _unpack_halves(y1_ref[...])
                  + pk[:, 3:4] * _unpack_halves(y2_ref[...]))


def _combine(h, packed, yt, tm):
    t, d = h.shape
    nb = t // tm
    return pl.pallas_call(
        _combine_kernel,
        grid=(nb,),
        in_specs=[pl.BlockSpec((tm, d), lambda i: (i, 0)),
                  pl.BlockSpec((tm, LANES), lambda i: (i, 0)),
                  pl.BlockSpec((tm, d // 2), lambda i: (i, 0)),
                  pl.BlockSpec((tm, d // 2), lambda i: (nb + i, 0))],
        out_specs=pl.BlockSpec((tm, d), lambda i: (i, 0)),
        out_shape=jax.ShapeDtypeStruct((t, d), F32),
        compiler_params=_params(("parallel",)),
        name="moe_combine",
    )(h, packed, yt, yt)


def _rel_bucket(rel):
    nb = REL_BUCKETS // 2
    max_exact = nb // 2
    n = jnp.abs(rel)
    nf = jnp.maximum(n, 1).astype(F32)
    large = max_exact + (jnp.log(nf / max_exact) / math.log(REL_MAX_DIST / max_exact)
                         * (nb - max_exact)).astype(jnp.int32)
    large = jnp.minimum(large, nb - 1)
    return jnp.where(rel > 0, nb, 0) + jnp.where(n < max_exact, n, large)


def _near_bias(rel_bias):
    t = ATT_T
    qp = jnp.arange(t)[:, None]
    kp = jnp.arange(t)[None, :]

    def lookup(rel):
        onehot = jax.nn.one_hot(_rel_bucket(rel), REL_BUCKETS, dtype=F32)
        return jnp.einsum('...b,bm->...m', onehot, rel_bias, precision=lax.Precision.HIGHEST)

    far = lookup(jnp.full((), -(2 * t), jnp.int32))
    period = 3 * t
    m = jnp.arange(period)
    rel_of_m = jnp.where(m < 2 * t, m, m - period) - t
    table = (lookup(rel_of_m) - far) * LOG2E
    toeplitz = jnp.tile(table, (t, 1))[:t * (period - 1)].reshape(t, period - 1, -1)
    prev = toeplitz[:, :t]
    diag = toeplitz[:, t:2 * t]
    diag = jnp.where(((kp // CHUNK) <= (qp // CHUNK))[:, :, None], diag, NEG_BIG)
    both = jnp.stack([prev, diag], axis=0).reshape(2, t, t, DIFF_HEADS, 2)
    return both.transpose(3, 0, 2, 4, 1).reshape(DIFF_HEADS, 2, t, 2 * t).astype(F32)


def kernel(x, rel_bias, norm1_g, w_in, pool_w, pool_scale, conv_w, sgu_ln_g, sgu_w, sgu_b, q_norm_g, k_norm_g, diff_lambda, subln_g, w_branch_pool, w_branch_conv, w_branch_sgu, w_branch_attn, w_out, norm2_g, ffn_w_gate_up, ffn_w_down, router_w, moe_w_gate_up, moe_w_down):
    b, s, d = x.shape
    t = b * s
    depth = w_in.shape[0]
    pw = pool_scale.shape[1]
    cw = conv_w.shape[2]
    sw = sgu_ln_g.shape[1]
    aw = w_branch_attn.shape[1]
    mix_cols = pw + 3 * cw + 2 * sw + 3 * aw
    nb = REL_BUCKETS // 2
    assert nb // 2 + int(math.log((ATT_T + 1) / (nb // 2)) / math.log(REL_MAX_DIST / (nb // 2))
                         * (nb - nb // 2)) >= nb - 1
    qk_off = pw + 3 * cw + 2 * sw
    z_cols = qk_off + 2 * aw
    q_col = qk_off // LANES
    k_col = q_col + aw // LANES
    tm = min(512, t)
    td = min(DENSE_TM, s)
    ts = min(512, s)

    bias_near = _near_bias(rel_bias)
    tri = jnp.tril(jnp.ones((SGU_SEG, SGU_SEG), bool))
    gd = pw // POOL_GROUPS

    h = x.reshape(t, d)
    for layer in range(depth):
        lam_init = 0.8 - 0.6 * math.exp(-0.3 * layer)
        w_mix = w_in[layer, :, :z_cols].astype(BF16)
        w_vt = w_in[layer, :, z_cols:mix_cols].T.astype(BF16)
        w_gate = w_in[layer, :, mix_cols:].astype(BF16)
        poolw_bd = jnp.zeros((pw, pw), F32)
        for g in range(POOL_GROUPS):
            poolw_bd = poolw_bd.at[g * gd:(g + 1) * gd, g * gd:(g + 1) * gd].set(pool_w[layer, g])
        sguw_cat = jnp.where(tri[None], sgu_w[layer], 0.0).transpose(1, 0, 2).reshape(
            SGU_SEG, SGU_GROUPS * SGU_SEG).astype(BF16)
        sgub_full = jnp.repeat(sgu_b[layer].T, sw // SGU_GROUPS, axis=1)
        wb = jnp.concatenate([w_branch_pool[layer], w_branch_conv[layer], w_branch_sgu[layer],
                              w_branch_attn[layer]], axis=0).astype(BF16)

        z, vt = _in_proj(h, norm1_g[layer][None], w_mix, w_vt, td, s)
        z3 = z.reshape(b, s, z_cols)
        y_abc = _local_mix(z3, poolw_bd.astype(BF16), pool_scale[layer][None], conv_w[layer],
                           sgu_ln_g[layer][None], sguw_cat, sgub_full, ts, pw, cw, sw)
        y_d = _diff_attn(z3, vt, bias_near, jnp.tile(q_norm_g[layer], 2)[None],
                         jnp.tile(k_norm_g[layer], 2)[None], diff_lambda[layer],
                         subln_g[layer][None], lam_init, q_col, k_col)
        h = _merge(h, y_abc.reshape(t, -1), y_d.reshape(t, -1), norm1_g[layer][None], w_gate, wb,
                   w_out[layer].astype(BF16), td, (pw, cw, sw, aw))

        g2 = norm2_g[layer][None]
        if layer % 2 == 0:
            h = _ffn(h, g2, ffn_w_gate_up[layer // 2].astype(BF16),
                     ffn_w_down[layer // 2].astype(BF16), td)
        else:
            li = layer // 2
            rw_pad = jnp.zeros((d, LANES), F32).at[:, :N_EXPERTS].set(router_w[li])
            packed, routed, totals = _router(h, g2, jnp.stack(_hi_lo(rw_pad)), tm)
            n_e = totals[0, :N_EXPERTS].astype(jnp.int32)
            n_pad = ((n_e + tm - 1) // tm) * tm
            ends = jnp.cumsum(n_pad)
            starts = ends - n_pad
            routed = routed.astype(jnp.int32)
            eids = jnp.arange(N_EXPERTS)[:, None]
            dest1 = jnp.sum(jnp.where(routed[0][None] == eids, starts[:, None], 0), axis=0) + routed[4]
            dest2 = jnp.sum(jnp.where(routed[1][None] == eids, starts[:, None], 0), axis=0) + routed[5]
            rows = 2 * t + N_EXPERTS * tm
            n_tiles = rows // tm
            tile_expert = jnp.minimum(
                jnp.sum((jnp.arange(n_tiles)[:, None] * tm) >= ends[None, :], axis=1),
                N_EXPERTS - 1).astype(jnp.int32)
            n_used = (ends[-1] // tm).astype(jnp.int32)[None]
            ch = min(4096, t)
            dest_a = jnp.concatenate([dest1, dest2]).reshape(2 * t // ch, 1, ch)
            src = _invert(dest_a, rows)
            is_pad = src < 0
            src_tok = jnp.where(is_pad, 0, jnp.where(src >= t, src - t, src))
            pad_rank = jnp.cumsum(is_pad.astype(jnp.int32)) - 1
            out_row = jnp.where(is_pad, 2 * t + tm + pad_rank, src)
            spare = 2 * t + jnp.arange(tm, dtype=jnp.int32)
            yt = _experts(tile_expert, n_used, src_tok.reshape(n_tiles, 1, tm),
                          jnp.concatenate([spare, out_row]).reshape(n_tiles + 1, 1, tm), h, g2,
                          moe_w_gate_up[li], moe_w_down[li], rows + tm)
            h = _combine(h, packed, yt, tm)
    return h.reshape(b, s, d)
```

```python
import functools
import math

import jax
import jax.numpy as jnp
from jax import lax
from jax.experimental import pallas as pl
from jax.experimental.pallas import tpu as pltpu

F32 = jnp.float32
BF16 = jnp.bfloat16

NORM_EPS = 1e-6
CHUNK = 64
POOL_GROUPS = 4
SGU_GROUPS = 4
SGU_SEG = 128
DIFF_HEADS = 4
DIFF_QK_DIM = 64
DIFF_V_DIM = 128
REL_BUCKETS = 32
REL_MAX_DIST = 128
N_EXPERTS = 8
LANES = 128
V7X_VMEM_BYTES = 64 * 1024 * 1024
VMEM_LIMIT = V7X_VMEM_BYTES - 8 * 1024 * 1024
NEG_BIG = -1e30
LOG2E = math.log2(math.e)

HALO = 16
ATT_T = 256
ATT_HEADS_PER_STEP = 2
FF_CHUNK = 256
WEIGHT_STAGE = 512
DENSE_TM = 1024


def _rms(x, g):
    return x * lax.rsqrt(jnp.mean(x * x, axis=-1, keepdims=True) + NORM_EPS) * g


def _resident(shape):
    nd = len(shape)
    return pl.BlockSpec(shape, lambda *_: (0,) * nd, pipeline_mode=pl.Buffered(1))


def _params(sem):
    return pltpu.CompilerParams(dimension_semantics=sem, vmem_limit_bytes=VMEM_LIMIT)


def _hi_lo(x):
    hi = x.astype(BF16)
    return hi, (x - hi.astype(F32)).astype(BF16)


def _pack_halves(x):
    n = x.shape[1] // 2
    bits = lax.bitcast_convert_type(x, jnp.uint32) + jnp.uint32(0x8000)
    return (bits[:, :n] & jnp.uint32(0xFFFF0000)) | (bits[:, n:] >> 16)


def _unpack_halves(p):
    hi = lax.bitcast_convert_type(p & jnp.uint32(0xFFFF0000), F32)
    lo = lax.bitcast_convert_type(p << 16, F32)
    return jnp.concatenate([hi, lo], axis=1)


def _in_proj_kernel(x_ref, g_ref, w_ref, wvt_ref, o_ref, vt_ref, *, n_chunk):
    xn = _rms(x_ref[...], g_ref[...]).astype(BF16)
    n = o_ref.shape[1]
    for j in range(n // n_chunk):
        sl = slice(j * n_chunk, (j + 1) * n_chunk)
        o_ref[:, sl] = jnp.dot(xn, w_ref[:, sl], preferred_element_type=F32).astype(o_ref.dtype)
    vt_ref[...] = lax.dot_general(wvt_ref[...], xn, (((1,), (1,)), ((), ())),
                                  preferred_element_type=F32).astype(vt_ref.dtype)


def _in_proj(h, g, w, w_vt, tm, seq):
    t, d = h.shape
    n = w.shape[1]
    nv = w_vt.shape[0]
    per_seq = seq // tm
    return pl.pallas_call(
        functools.partial(_in_proj_kernel, n_chunk=512),
        grid=(t // tm,),
        in_specs=[pl.BlockSpec((tm, d), lambda i: (i, 0)), _resident((1, d)), _resident((d, n)),
                  _resident(w_vt.shape)],
        out_specs=[pl.BlockSpec((tm, n), lambda i: (i, 0)),
                   pl.BlockSpec((None, nv, tm), lambda i: (i // per_seq, 0, i % per_seq))],
        out_shape=[jax.ShapeDtypeStruct((t, n), BF16),
                   jax.ShapeDtypeStruct((t // seq, nv, seq), BF16)],
        compiler_params=_params(("parallel",)),
        name="in_proj",
    )(h, g, w, w_vt)


def _local_mix_kernel(z_ref, halo_ref, poolw_ref, pscale_ref, convw_ref, lng_ref, sguw_ref,
                      sgub_ref, o_ref, *, pw, cw):
    ts = z_ref.shape[0]
    i = pl.program_id(1)
    z = z_ref[...].astype(F32)
    halo = halo_ref[...].astype(F32)
    halo = jnp.where(i > 0, halo, 0.0)
    ext = jnp.concatenate([halo[:, :pw + 3 * cw], z[:, :pw + 3 * cw]], axis=0)
    rows = ext.shape[0]

    def back(x, k):
        return pltpu.roll(x, k, axis=0)

    a = ext[:, :pw]
    s2 = a + back(a, 1)
    s4 = s2 + back(s2, 2)
    s8 = s4 + back(s4, 4)
    s16 = s8 + back(s8, 8)
    lane = lax.broadcasted_iota(jnp.int32, (rows, pw), 1)
    grp = lane // (pw // POOL_GROUPS)
    win_sum = jnp.where(grp == 0, s2, jnp.where(grp == 1, s4, jnp.where(grp == 2, s8, s16)))
    win = jnp.where(grp == 0, 2, jnp.where(grp == 1, 4, jnp.where(grp == 2, 8, 16)))
    pos = i * ts - HALO + lax.broadcasted_iota(jnp.int32, (rows, pw), 0)
    count = jnp.minimum(pos + 1, win).astype(F32)
    pooled = (win_sum / jnp.maximum(count, 1.0) - a)[HALO:]
    y_a = jnp.dot(pooled.astype(BF16), poolw_ref[...], preferred_element_type=F32) * pscale_ref[...]
    o_ref[:, 0:pw] = y_a.astype(o_ref.dtype)

    b_gate = z[:, pw:pw + cw]
    zc = ext[:, pw + cw:pw + 2 * cw] * ext[:, pw + 2 * cw:pw + 3 * cw]
    conv = (convw_ref[0:1, :] * back(zc, 2) + convw_ref[1:2, :] * back(zc, 1)
            + convw_ref[2:3, :] * zc)[HALO:]
    o_ref[:, pw:pw + cw] = (b_gate * conv).astype(o_ref.dtype)

    sw = (z.shape[1] - pw - 3 * cw) // 2
    zc_uv = z[:, pw + 3 * cw:]
    uv = 0.5 * zc_uv * (1.0 + lax.erf(zc_uv * math.sqrt(0.5)))
    u = uv[:, :sw]
    v = uv[:, sw:]
    mu = jnp.mean(v, axis=-1, keepdims=True)
    var = jnp.mean(jnp.square(v - mu), axis=-1, keepdims=True)
    vn = (v - mu) * lax.rsqrt(var + NORM_EPS) * lng_ref[...]
    glane = lax.broadcasted_iota(jnp.int32, (SGU_SEG, sw), 1) // (sw // SGU_GROUPS)
    wcat = sguw_ref[...]
    bias = sgub_ref[...]
    for n in range(ts // SGU_SEG):
        seg = vn[n * SGU_SEG:(n + 1) * SGU_SEG]
        rhs = jnp.concatenate(
            [jnp.where(glane == g, seg, 0.0) for g in range(SGU_GROUPS)], axis=0).astype(BF16)
        s = jnp.dot(wcat, rhs, preferred_element_type=F32) + bias
        o_ref[n * SGU_SEG:(n + 1) * SGU_SEG, pw + cw:pw + cw + sw] = (
            u[n * SGU_SEG:(n + 1) * SGU_SEG] * s).astype(o_ref.dtype)


def _local_mix(z3, poolw_bd, pscale, convw, lng, sguw_cat, sgub_full, ts, pw, cw, sw):
    b, s, _ = z3.shape
    cols = pw + 3 * cw + 2 * sw
    hb = ts // HALO
    return pl.pallas_call(
        functools.partial(_local_mix_kernel, pw=pw, cw=cw),
        grid=(b, s // ts),
        in_specs=[
            pl.BlockSpec((None, ts, cols), lambda bi, i: (bi, i, 0)),
            pl.BlockSpec((None, HALO, cols), lambda bi, i: (bi, jnp.maximum(i * hb - 1, 0), 0)),
            _resident(poolw_bd.shape), _resident(pscale.shape), _resident(convw.shape),
            _resident(lng.shape), _resident(sguw_cat.shape), _resident(sgub_full.shape),
        ],
        out_specs=pl.BlockSpec((None, ts, pw + cw + sw), lambda bi, i: (bi, i, 0)),
        out_shape=jax.ShapeDtypeStruct((b, s, pw + cw + sw), BF16),
        compiler_params=_params(("parallel", "parallel")),
        name="local_mix",
    )(z3, z3, poolw_bd, pscale, convw, lng, sguw_cat, sgub_full)


def _diff_attn_kernel(q_ref, k_ref, vt_ref, bias_ref, qg_ref, kg_ref, lam_ref, sg_ref, o_ref,
                      kn_ref, qs_ref, st_ref, m_ref, *, lam_init):
    tq = ATT_T
    hp = qs_ref.shape[0]
    nt = k_ref.shape[0] // tq
    hw = 2 * DIFF_QK_DIM
    half = lax.broadcasted_iota(jnp.int32, (1, hw), 1) < DIFF_QK_DIM
    same_map = (lax.broadcasted_iota(jnp.int32, (hw, hw), 0) // DIFF_QK_DIM
                == lax.broadcasted_iota(jnp.int32, (hw, hw), 1) // DIFF_QK_DIM)
    ones_map = jnp.where(same_map, 1.0, 0.0).astype(BF16)

    def qk_norm_mxu(x, g):
        sq_hi, sq_lo = _hi_lo(x * x)
        ss = (jnp.dot(sq_hi, ones_map, preferred_element_type=F32)
              + jnp.dot(sq_lo, ones_map, preferred_element_type=F32))
        return x * lax.rsqrt(ss * (1.0 / DIFF_QK_DIM) + NORM_EPS) * g

    def qk_norm(x, g):
        sq = x * x
        ss0 = jnp.sum(jnp.where(half, sq, 0.0), axis=-1, keepdims=True)
        ss1 = jnp.sum(jnp.where(half, 0.0, sq), axis=-1, keepdims=True)
        r0 = lax.rsqrt(ss0 * (1.0 / DIFF_QK_DIM) + NORM_EPS)
        r1 = lax.rsqrt(ss1 * (1.0 / DIFF_QK_DIM) + NORM_EPS)
        return x * jnp.where(half, r0, r1) * g

    for h in range(hp):
        cols = slice(h * hw, (h + 1) * hw)
        for j in range(nt):
            rows = slice(j * tq, (j + 1) * tq)
            kn_ref[h, rows, :] = qk_norm_mxu(k_ref[rows, cols].astype(F32),
                                             kg_ref[...]).astype(BF16)

    def step(c):
        cur, prv = c % 2, 1 - c % 2
        scoring = c < nt
        if scoring:
            for h in range(hp):
                qn = (qk_norm(q_ref[c * tq:(c + 1) * tq, h * hw:(h + 1) * hw].astype(F32),
                              qg_ref[...]) * (DIFF_QK_DIM ** -0.5 * LOG2E))
                qs_ref[h, 0:tq, :] = jnp.where(half, qn, 0.0).astype(BF16)
                qs_ref[h, tq:2 * tq, :] = jnp.where(half, 0.0, qn).astype(BF16)
        m_new = [jnp.full((1, 2 * tq), NEG_BIG, F32) for _ in range(hp)]
        m_old = [m_ref[prv, h] for h in range(hp)] if c >= 1 else None
        l = [jnp.zeros((1, 2 * tq), F32) for _ in range(hp)]
        acc = [jnp.zeros((DIFF_V_DIM, 2 * tq), F32) for _ in range(hp)]
        for j in range(c + 1):
            rows = slice(j * tq, (j + 1) * tq)
            for h in range(hp):
                if scoring:
                    st = lax.dot_general(kn_ref[h, rows, :], qs_ref[h],
                                         (((1,), (1,)), ((), ())), preferred_element_type=F32)
                    if j >= c - 1:
                        st = st + bias_ref[h, j - (c - 1)]
                    st_ref[cur, h, rows, :] = st
                    m_new[h] = jnp.maximum(m_new[h], jnp.max(st, axis=0, keepdims=True))
                if j < c:
                    p = jnp.exp2(st_ref[prv, h, rows, :] - m_old[h])
                    l[h] = l[h] + jnp.sum(p, axis=0, keepdims=True)
                    acc[h] = acc[h] + jnp.dot(vt_ref[h * DIFF_V_DIM:(h + 1) * DIFF_V_DIM, rows],
                                              p.astype(BF16), preferred_element_type=F32)
        if scoring:
            for h in range(hp):
                m_ref[cur, h] = m_new[h]
        if c >= 1:
            lp = lam_ref[...]
            lam = (jnp.exp(jnp.sum(lp[0:1] * lp[1:2], axis=-1, keepdims=True))
                   - jnp.exp(jnp.sum(lp[2:3] * lp[3:4], axis=-1, keepdims=True)) + lam_init)
            for h in range(hp):
                o = (acc[h][:, :tq] * (1.0 / l[h][:, :tq])
                     - acc[h][:, tq:] * (lam / l[h][:, tq:]))
                o = o * lax.rsqrt(jnp.mean(o * o, axis=0, keepdims=True) + NORM_EPS)
                o_ref[(c - 1) * tq:c * tq, h * DIFF_V_DIM:(h + 1) * DIFF_V_DIM] = (
                    o.T * (sg_ref[...] * (1.0 - lam_init))).astype(o_ref.dtype)

    for c in range(nt + 1):
        step(c)


def _diff_attn(z3, vt, bias_near, qg2, kg2, lam_p, subln_g, lam_init, q_col, k_col):
    b, s, _ = z3.shape
    tq = ATT_T
    hp = ATT_HEADS_PER_STEP
    nt = s // tq
    hw = 2 * DIFF_QK_DIM
    return pl.pallas_call(
        functools.partial(_diff_attn_kernel, lam_init=lam_init),
        grid=(b, DIFF_HEADS // hp),
        in_specs=[
            pl.BlockSpec((None, s, hp * hw), lambda bi, g: (bi, 0, q_col // hp + g)),
            pl.BlockSpec((None, s, hp * hw), lambda bi, g: (bi, 0, k_col // hp + g)),
            pl.BlockSpec((None, hp * DIFF_V_DIM, s), lambda bi, g: (bi, g, 0)),
            pl.BlockSpec((hp, 2, tq, 2 * tq), lambda bi, g: (g, 0, 0, 0)),
            _resident(qg2.shape), _resident(kg2.shape), _resident(lam_p.shape),
            _resident(subln_g.shape),
        ],
        out_specs=pl.BlockSpec((None, s, hp * DIFF_V_DIM), lambda bi, g: (bi, 0, g)),
        out_shape=jax.ShapeDtypeStruct((b, s, DIFF_HEADS * DIFF_V_DIM), BF16),
        scratch_shapes=[
            pltpu.VMEM((hp, s, hw), BF16),
            pltpu.VMEM((hp, 2 * tq, hw), BF16),
            pltpu.VMEM((2, hp, s, 2 * tq), F32),
            pltpu.VMEM((2, hp, 1, 2 * tq), F32),
        ],
        compiler_params=_params(("parallel", "parallel")),
        name="diff_attn",
    )(z3, z3, vt, bias_near, qg2, kg2, lam_p, subln_g)


def _merge_kernel(h_ref, yabc_ref, yd_ref, g_ref, wg_ref, wb_ref, wo_ref, o_ref, *, widths):
    h = h_ref[...]
    d = h.shape[1]
    xn = _rms(h, g_ref[...]).astype(BF16)
    merged = None
    off = 0
    yoff = 0
    for bi, w in enumerate(widths):
        gate = jax.nn.sigmoid(jnp.dot(xn, wg_ref[:, bi * d:(bi + 1) * d],
                                      preferred_element_type=F32))
        if bi < len(widths) - 1:
            y = yabc_ref[:, yoff:yoff + w]
            yoff += w
        else:
            y = yd_ref[...]
        proj = jnp.dot(y, wb_ref[off:off + w, :], preferred_element_type=F32)
        off += w
        merged = gate * proj if merged is None else merged + gate * proj
    o_ref[...] = h + jnp.dot(merged.astype(BF16), wo_ref[...], preferred_element_type=F32)


def _merge(h, y_abc, y_d, g, wg, wb, wo, tm, widths):
    t, d = h.shape
    return pl.pallas_call(
        functools.partial(_merge_kernel, widths=widths),
        grid=(t // tm,),
        in_specs=[
            pl.BlockSpec((tm, d), lambda i: (i, 0)),
            pl.BlockSpec((tm, y_abc.shape[1]), lambda i: (i, 0)),
            pl.BlockSpec((tm, y_d.shape[1]), lambda i: (i, 0)),
            _resident(g.shape), _resident(wg.shape), _resident(wb.shape), _resident(wo.shape),
        ],
        out_specs=pl.BlockSpec((tm, d), lambda i: (i, 0)),
        out_shape=jax.ShapeDtypeStruct((t, d), F32),
        compiler_params=_params(("parallel",)),
        name="merge",
    )(h, y_abc, y_d, g, wg, wb, wo)


def _swiglu_acc(xn, wgu_ref, wd_ref, d_ff, between=None):
    acc = None
    n_chunks = d_ff // FF_CHUNK
    for c in range(n_chunks):
        lo = c * FF_CHUNK
        g = jnp.dot(xn, wgu_ref[:, lo:lo + FF_CHUNK], preferred_element_type=F32)
        u = jnp.dot(xn, wgu_ref[:, d_ff + lo:d_ff + lo + FF_CHUNK], preferred_element_type=F32)
        act = (g * jax.nn.sigmoid(g) * u).astype(BF16)
        part = jnp.dot(act, wd_ref[lo:lo + FF_CHUNK, :], preferred_element_type=F32)
        acc = part if acc is None else acc + part
        if between is not None:
            between(c, n_chunks)
    return acc


def _ffn_kernel(h_ref, g_ref, wgu_ref, wd_ref, o_ref):
    h = h_ref[...]
    xn = _rms(h, g_ref[...]).astype(BF16)
    o_ref[...] = h + _swiglu_acc(xn, wgu_ref, wd_ref, wd_ref.shape[0])


def _ffn(h, g, wgu, wd, tm):
    t, d = h.shape
    return pl.pallas_call(
        _ffn_kernel,
        grid=(t // tm,),
        in_specs=[pl.BlockSpec((tm, d), lambda i: (i, 0)), _resident(g.shape),
                  _resident(wgu.shape), _resident(wd.shape)],
        out_specs=pl.BlockSpec((tm, d), lambda i: (i, 0)),
        out_shape=jax.ShapeDtypeStruct((t, d), F32),
        compiler_params=_params(("parallel",)),
        name="ffn",
    )(h, g, wgu, wd)


def _router_kernel(h_ref, g_ref, rw_ref, o_ref, tot_ref, carry_ref):
    tm = h_ref.shape[0]

    @pl.when(pl.program_id(0) == 0)
    def _():
        carry_ref[...] = jnp.zeros(carry_ref.shape, F32)

    hn = _rms(h_ref[...], g_ref[...])
    hn_hi, hn_lo = _hi_lo(hn)
    logits = (jnp.dot(hn_hi, rw_ref[0], preferred_element_type=F32)
              + jnp.dot(hn_lo, rw_ref[0], preferred_element_type=F32)
              + jnp.dot(hn_hi, rw_ref[1], preferred_element_type=F32))
    lane = lax.broadcasted_iota(jnp.int32, (tm, LANES), 1)
    logits = jnp.where(lane < N_EXPERTS, logits, NEG_BIG)
    v1 = jnp.max(logits, axis=-1, keepdims=True)
    i1 = jnp.min(jnp.where(logits == v1, lane, LANES), axis=-1, keepdims=True)
    rest = jnp.where(lane == i1, NEG_BIG, logits)
    v2 = jnp.max(rest, axis=-1, keepdims=True)
    i2 = jnp.min(jnp.where(rest == v2, lane, LANES), axis=-1, keepdims=True)
    e = jnp.exp(v2 - v1)
    w1 = 1.0 / (1.0 + e)
    w2 = e / (1.0 + e)
    cnt = jnp.where((lane == i1) | (lane == i2), 1.0, 0.0)
    r = lax.broadcasted_iota(jnp.int32, (tm, tm), 0)
    c = lax.broadcasted_iota(jnp.int32, (tm, tm), 1)
    tri = jnp.where(c < r, 1.0, 0.0).astype(BF16)
    excl = jnp.dot(tri, cnt.astype(BF16), preferred_element_type=F32) + carry_ref[...]
    rank1 = jnp.sum(jnp.where(lane == i1, excl, 0.0), axis=-1, keepdims=True)
    rank2 = jnp.sum(jnp.where(lane == i2, excl, 0.0), axis=-1, keepdims=True)
    carry_ref[...] = carry_ref[...] + jnp.sum(cnt, axis=0, keepdims=True)
    tot_ref[...] = carry_ref[...]
    packed = jnp.where(lane == 0, i1.astype(F32), jnp.where(lane == 1, i2.astype(F32),
             jnp.where(lane == 2, w1, jnp.where(lane == 3, w2,
             jnp.where(lane == 4, rank1, jnp.where(lane == 5, rank2, 0.0))))))
    o_ref[...] = packed


def _router(h, g, rw_pad, tm):
    t, d = h.shape
    return pl.pallas_call(
        _router_kernel,
        grid=(t // tm,),
        in_specs=[pl.BlockSpec((tm, d), lambda i: (i, 0)), _resident(g.shape),
                  _resident(rw_pad.shape)],
        out_specs=[pl.BlockSpec((tm, LANES), lambda i: (i, 0)),
                   pl.BlockSpec((1, LANES), lambda i: (0, 0))],
        out_shape=[jax.ShapeDtypeStruct((t, LANES), F32), jax.ShapeDtypeStruct((1, LANES), F32)],
        scratch_shapes=[pltpu.VMEM((1, LANES), F32)],
        compiler_params=_params(("arbitrary",)),
        name="router",
    )(h, g, rw_pad)


def _invert_kernel(dest_ref, init_hbm, src_ref):
    c = pl.program_id(0)
    ch = dest_ref.shape[1]

    @pl.when(c == 0)
    def _():
        pltpu.sync_copy(init_hbm, src_ref)

    base = c * ch

    def place(a, x):
        src_ref[dest_ref[0, a]] = base + a
        return x

    lax.fori_loop(0, ch, place, 0, unroll=16)


def _invert(dest, rows):
    nc, _, ch = dest.shape
    return pl.pallas_call(
        _invert_kernel,
        grid=(nc,),
        in_specs=[pl.BlockSpec((None, 1, ch), lambda c: (c, 0, 0), memory_space=pltpu.SMEM),
                  pl.BlockSpec(memory_space=pl.ANY)],
        out_specs=pl.BlockSpec(memory_space=pltpu.SMEM),
        out_shape=jax.ShapeDtypeStruct((rows,), jnp.int32),
        compiler_params=pltpu.CompilerParams(dimension_semantics=("arbitrary",)),
        name="moe_invert",
    )(dest, jnp.full((rows,), -1, jnp.int32))


def _stream_chunks(copy, n, consume):
    copy(0, 0).start()
    for c in range(n):
        if c + 1 < n:
            copy(c + 1, (c + 1) % 2).start()
        copy(c, c % 2).wait()
        consume(c, c % 2)


def _expert_kernel(te_ref, nu_ref, src_cur, src_nxt, orow_prv, orow_cur, h_hbm, g_ref, wgu_hbm,
                   wd_hbm, yt_hbm, xbuf, ybuf, wgu_ref, wd_ref, stage_gu, stage_d, gsem, ssem, wsem):
    i = pl.program_id(0)
    fin = nu_ref[0] - 1
    tm = xbuf.shape[1]
    s = lax.rem(i, 2)
    o = 1 - s
    e = te_ref[i]

    def gather(tok, r, slot):
        return pltpu.make_async_copy(h_hbm.at[pl.ds(tok, 1)],
                                     xbuf.at[slot, pl.ds(r, 1)], gsem.at[slot])

    def scatter(row, r, slot):
        return pltpu.make_async_copy(ybuf.at[slot, pl.ds(r, 1)],
                                     yt_hbm.at[pl.ds(row, 1)], ssem.at[slot])

    def for_rows(fn):
        def body(r, x):
            fn(r)
            return x

        lax.fori_loop(0, tm, body, 0, unroll=64)

    def load_weights():
        wc = stage_gu.shape[2]
        rc = stage_d.shape[1]

        def copy_gu(c, slot):
            return pltpu.make_async_copy(wgu_hbm.at[e, :, pl.ds(c * wc, wc)], stage_gu.at[slot],
                                         wsem.at[slot])

        def store_gu(c, slot):
            wgu_ref[:, c * wc:(c + 1) * wc] = stage_gu[slot].astype(BF16)

        def copy_d(c, slot):
            return pltpu.make_async_copy(wd_hbm.at[e, pl.ds(c * rc, rc), :], stage_d.at[slot],
                                         wsem.at[slot])

        def store_d(c, slot):
            wd_ref[c * rc:(c + 1) * rc, :] = stage_d[slot].astype(BF16)

        _stream_chunks(copy_gu, wgu_ref.shape[1] // wc, store_gu)
        _stream_chunks(copy_d, wd_ref.shape[0] // rc, store_d)

    def spare_fill(k):
        rows0 = yt_hbm.shape[0] - (N_EXPERTS + 1 - k) * tm
        return pltpu.make_async_copy(ybuf.at[1], yt_hbm.at[pl.ds(rows0, tm)], ssem.at[0])

    @pl.when(i <= fin)
    def _():
        @pl.when((i == 0) | (e != te_ref[jnp.maximum(i - 1, 0)]))
        def _():
            load_weights()

        @pl.when(i == 0)
        def _():
            ybuf[1] = jnp.zeros(ybuf.shape[1:], ybuf.dtype)
            for k in range(1, N_EXPERTS + 1):
                spare_fill(k).start()
            for k in range(1, N_EXPERTS + 1):
                spare_fill(k).wait()
            for_rows(lambda r: gather(src_cur[0, r], r, 0).start())

        for_rows(lambda r: gather(src_cur[0, r], r, s).wait())
        xn = _rms(xbuf[s], g_ref[...]).astype(BF16)

        def between(c, n_chunks):
            per = -(-tm // (n_chunks // 2))
            for r in range(c * per, min((c + 1) * per, tm)):
                gather(src_nxt[0, r], r, o).start()
                scatter(orow_prv[0, r], r, o).start()

        ybuf[s] = _pack_halves(_swiglu_acc(xn, wgu_ref, wd_ref, wd_ref.shape[0], between))
        for_rows(lambda r: scatter(orow_prv[0, r], r, o).wait())

        @pl.when(i == fin)
        def _():
            for_rows(lambda r: gather(src_nxt[0, r], r, o).wait())
            for_rows(lambda r: scatter(orow_cur[0, r], r, s).start())
            for_rows(lambda r: scatter(orow_cur[0, r], r, s).wait())


def _experts(tile_expert, n_used, src_tok, out_row, h, g, wgu, wd, yt_rows):
    n, _, tm = src_tok.shape
    d = h.shape[1]
    d_ff = wd.shape[1]
    smem = functools.partial(pl.BlockSpec, (None, 1, tm), memory_space=pltpu.SMEM)
    grid_spec = pltpu.PrefetchScalarGridSpec(
        num_scalar_prefetch=2,
        grid=(n,),
        in_specs=[
            smem(lambda i, te, nu: (i, 0, 0)),
            smem(lambda i, te, nu: (jnp.minimum(i + 1, n - 1), 0, 0)),
            smem(lambda i, te, nu: (i, 0, 0)),
            smem(lambda i, te, nu: (i + 1, 0, 0)),
            pl.BlockSpec(memory_space=pl.ANY),
            pl.BlockSpec(g.shape, lambda i, te, nu: (0, 0), pipeline_mode=pl.Buffered(1)),
            pl.BlockSpec(memory_space=pl.ANY),
            pl.BlockSpec(memory_space=pl.ANY),
        ],
        out_specs=pl.BlockSpec(memory_space=pl.ANY),
        scratch_shapes=[pltpu.VMEM((2, tm, d), F32), pltpu.VMEM((2, tm, d // 2), jnp.uint32),
                        pltpu.VMEM((d, 2 * d_ff), BF16), pltpu.VMEM((d_ff, d), BF16),
                        pltpu.VMEM((2, d, WEIGHT_STAGE), F32), pltpu.VMEM((2, WEIGHT_STAGE, d), F32),
                        pltpu.SemaphoreType.DMA((2,)), pltpu.SemaphoreType.DMA((2,)),
                        pltpu.SemaphoreType.DMA((2,))],
    )
    return pl.pallas_call(
        _expert_kernel,
        grid_spec=grid_spec,
        out_shape=jax.ShapeDtypeStruct((yt_rows, d // 2), jnp.uint32),
        compiler_params=_params(("arbitrary",)),
        name="moe_experts",
    )(tile_expert, n_used, src_tok, src_tok, out_row, out_row, h, g, wgu, wd)


def _combine_kernel(h_ref, pk_ref, y1_ref, y2_ref, o_ref):
    pk = pk_ref[...]
    o_ref[...] = (h_ref[...] + pk[:, 2:3] * _unpack_halves(y1_ref[...])
                  + pk[:, 3:4] * _unpack_halves(y2_ref[...]))


def _combine(h, packed, yt, tm):
    t, d = h.shape
    nb = t // tm
    return pl.pallas_call(
        _combine_kernel,
        grid=(nb,),
        in_specs=[pl.BlockSpec((tm, d), lambda i: (i, 0)),
                  pl.BlockSpec((tm, LANES), lambda i: (i, 0)),
                  pl.BlockSpec((tm, d // 2), lambda i: (i, 0)),
                  pl.BlockSpec((tm, d // 2), lambda i: (nb + i, 0))],
        out_specs=pl.BlockSpec((tm, d), lambda i: (i, 0)),
        out_shape=jax.ShapeDtypeStruct((t, d), F32),
        compiler_params=_params(("parallel",)),
        name="moe_combine",
    )(h, packed, yt, yt)


def _rel_bucket(rel):
    nb = REL_BUCKETS // 2
    max_exact = nb // 2
    n = jnp.abs(rel)
    nf = jnp.maximum(n, 1).astype(F32)
    large = max_exact + (jnp.log(nf / max_exact) / math.log(REL_MAX_DIST / max_exact)
                         * (nb - max_exact)).astype(jnp.int32)
    large = jnp.minimum(large, nb - 1)
    return jnp.where(rel > 0, nb, 0) + jnp.where(n < max_exact, n, large)


def _near_bias(rel_bias):
    t = ATT_T
    qp = jnp.arange(t)[:, None]
    kp = jnp.arange(t)[None, :]

    def lookup(rel):
        onehot = jax.nn.one_hot(_rel_bucket(rel), REL_BUCKETS, dtype=F32)
        return jnp.einsum('...b,bm->...m', onehot, rel_bias, precision=lax.Precision.HIGHEST)

    far = lookup(jnp.full((), -(2 * t), jnp.int32))
    period = 3 * t
    m = jnp.arange(period)
    rel_of_m = jnp.where(m < 2 * t, m, m - period) - t
    table = (lookup(rel_of_m) - far) * LOG2E
    toeplitz = jnp.tile(table, (t, 1))[:t * (period - 1)].reshape(t, period - 1, -1)
    prev = toeplitz[:, :t]
    diag = toeplitz[:, t:2 * t]
    diag = jnp.where(((kp // CHUNK) <= (qp // CHUNK))[:, :, None], diag, NEG_BIG)
    both = jnp.stack([prev, diag], axis=0).reshape(2, t, t, DIFF_HEADS, 2)
    return both.transpose(3, 0, 2, 4, 1).reshape(DIFF_HEADS, 2, t, 2 * t).astype(F32)


def kernel(x, rel_bias, norm1_g, w_in, pool_w, pool_scale, conv_w, sgu_ln_g, sgu_w, sgu_b, q_norm_g, k_norm_g, diff_lambda, subln_g, w_branch_pool, w_branch_conv, w_branch_sgu, w_branch_attn, w_out, norm2_g, ffn_w_gate_up, ffn_w_down, router_w, moe_w_gate_up, moe_w_down):
    b, s, d = x.shape
    t = b * s
    depth = w_in.shape[0]
    pw = pool_scale.shape[1]
    cw = conv_w.shape[2]
    sw = sgu_ln_g.shape[1]
    aw = w_branch_attn.shape[1]
    mix_cols = pw + 3 * cw + 2 * sw + 3 * aw
    nb = REL_BUCKETS // 2
    assert nb // 2 + int(math.log((ATT_T + 1) / (nb // 2)) / math.log(REL_MAX_DIST / (nb // 2))
                         * (nb - nb // 2)) >= nb - 1
    qk_off = pw + 3 * cw + 2 * sw
    z_cols = qk_off + 2 * aw
    q_col = qk_off // LANES
    k_col = q_col + aw // LANES
    tm = min(512, t)
    td = min(DENSE_TM, s)
    ts = min(512, s)

    bias_near = _near_bias(rel_bias)
    tri = jnp.tril(jnp.ones((SGU_SEG, SGU_SEG), bool))
    gd = pw // POOL_GROUPS

    h = x.reshape(t, d)
    for layer in range(depth):
        lam_init = 0.8 - 0.6 * math.exp(-0.3 * layer)
        w_mix = w_in[layer, :, :z_cols].astype(BF16)
        w_vt = w_in[layer, :, z_cols:mix_cols].T.astype(BF16)
        w_gate = w_in[layer, :, mix_cols:].astype(BF16)
        poolw_bd = jnp.zeros((pw, pw), F32)
        for g in range(POOL_GROUPS):
            poolw_bd = poolw_bd.at[g * gd:(g + 1) * gd, g * gd:(g + 1) * gd].set(pool_w[layer, g])
        sguw_cat = jnp.where(tri[None], sgu_w[layer], 0.0).transpose(1, 0, 2).reshape(
            SGU_SEG, SGU_GROUPS * SGU_SEG).astype(BF16)
        sgub_full = jnp.repeat(sgu_b[layer].T, sw // SGU_GROUPS, axis=1)
        wb = jnp.concatenate([w_branch_pool[layer], w_branch_conv[layer], w_branch_sgu[layer],
                              w_branch_attn[layer]], axis=0).astype(BF16)

        z, vt = _in_proj(h, norm1_g[layer][None], w_mix, w_vt, td, s)
        z3 = z.reshape(b, s, z_cols)
        y_abc = _local_mix(z3, poolw_bd.astype(BF16), pool_scale[layer][None], conv_w[layer],
                           sgu_ln_g[layer][None], sguw_cat, sgub_full, ts, pw, cw, sw)
        y_d = _diff_attn(z3, vt, bias_near, jnp.tile(q_norm_g[layer], 2)[None],
                         jnp.tile(k_norm_g[layer], 2)[None], diff_lambda[layer],
                         subln_g[layer][None], lam_init, q_col, k_col)
        h = _merge(h, y_abc.reshape(t, -1), y_d.reshape(t, -1), norm1_g[layer][None], w_gate, wb,
                   w_out[layer].astype(BF16), td, (pw, cw, sw, aw))

        g2 = norm2_g[layer][None]
        if layer % 2 == 0:
            h = _ffn(h, g2, ffn_w_gate_up[layer // 2].astype(BF16),
                     ffn_w_down[layer // 2].astype(BF16), td)
        else:
            li = layer // 2
            rw_pad = jnp.zeros((d, LANES), F32).at[:, :N_EXPERTS].set(router_w[li])
            packed, totals = _router(h, g2, jnp.stack(_hi_lo(rw_pad)), tm)
            n_e = totals[0, :N_EXPERTS].astype(jnp.int32)
            n_pad = ((n_e + tm - 1) // tm) * tm
            ends = jnp.cumsum(n_pad)
            starts = ends - n_pad
            e1 = packed[:, 0].astype(jnp.int32)
            e2 = packed[:, 1].astype(jnp.int32)
            eids = jnp.arange(N_EXPERTS)[None, :]
            dest1 = (jnp.sum(jnp.where(e1[:, None] == eids, starts[None, :], 0), axis=1)
                     + packed[:, 4].astype(jnp.int32))
            dest2 = (jnp.sum(jnp.where(e2[:, None] == eids, starts[None, :], 0), axis=1)
                     + packed[:, 5].astype(jnp.int32))
            rows = 2 * t + N_EXPERTS * tm
            n_tiles = rows // tm
            tile_expert = jnp.minimum(
                jnp.sum((jnp.arange(n_tiles)[:, None] * tm) >= ends[None, :], axis=1),
                N_EXPERTS - 1).astype(jnp.int32)
            n_used = (ends[-1] // tm).astype(jnp.int32)[None]
            ch = min(4096, t)
            dest_a = jnp.concatenate([dest1, dest2]).reshape(2 * t // ch, 1, ch)
            src = _invert(dest_a, rows)
            is_pad = src < 0
            src_tok = jnp.where(is_pad, 0, jnp.where(src >= t, src - t, src))
            pad_rank = jnp.cumsum(is_pad.astype(jnp.int32)) - 1
            out_row = jnp.where(is_pad, 2 * t + tm + pad_rank, src)
            spare = 2 * t + jnp.arange(tm, dtype=jnp.int32)
            yt = _experts(tile_expert, n_used, src_tok.reshape(n_tiles, 1, tm),
                          jnp.concatenate([spare, out_row]).reshape(n_tiles + 1, 1, tm), h, g2,
                          moe_w_gate_up[li], moe_w_down[li], rows + tm)
            h = _combine(h, packed, yt, tm)
    return h.reshape(b, s, d)
```
